```python
import math
import jax, jax.numpy as jnp
from jax import lax
import numpy as np

D_MODEL = 1024
BATCH = 32
SEQ = 256
DEPTH = 2
DEC_BATCH = 8
DEC_SEQ = 2048
PAST_LEN = 256

GRID_W = 64
N_EVEN = (DEPTH + 1) // 2
N_ODD = DEPTH // 2
MIX_W = D_MODEL
ATTN_W = MIX_W // 2
CONV_W = MIX_W - ATTN_W
N_HEADS_A = 4
V_HEAD = ATTN_W // N_HEADS_A
HALF_HEAD = V_HEAD // 2
QK_HEAD = 2 * HALF_HEAD
ROPE_AXIS = HALF_HEAD // 2
ROPE_BASE = 10000.0
Q_BLOCK = 128
EVEN_IN_W = 3 * ATTN_W + 3 * CONV_W
SPLIT_EVEN = (ATTN_W, 2 * ATTN_W, 3 * ATTN_W, 3 * ATTN_W + CONV_W, 3 * ATTN_W + 2 * CONV_W)
POOL_W = MIX_W // 2
FOURIER_W = MIX_W - POOL_W
POOL_WINDOWS = (2, 4, 8, 16)
N_POOL_GROUPS = len(POOL_WINDOWS)
POOL_GROUP = POOL_W // N_POOL_GROUPS
N_FOURIER_GROUPS = 4
FOURIER_GROUP = FOURIER_W // N_FOURIER_GROUPS
D_FF = -(-8 * D_MODEL // (3 * 256)) * 256
EPS = 1e-6

kernel_name = 'hybrid_diffattn_shortconv_pool_fourier_prefix_step'


def rmsnorm(x, g):
    xf = x.astype(jnp.float32)
    y = xf * lax.rsqrt(jnp.mean(xf * xf, axis=-1, keepdims=True) + EPS)
    return (y * g.astype(jnp.float32)).astype(x.dtype)


def modulate(x, g, shift, scale):
    return rmsnorm(x, g) * (1 + scale) + shift


def gated_residual(x, y, g, gate):
    return x + gate * rmsnorm(y, g)


def swiglu(h, w_gate, w_up, w_down):
    return (jax.nn.silu(h @ w_gate) * (h @ w_up)) @ w_down


def split_mod(mod):
    return jnp.split(mod, 6, axis=-1)


def axial_rope_tables(n_tok):
    rows = n_tok // GRID_W
    row = jnp.repeat(jnp.arange(rows), GRID_W).astype(jnp.float32)
    col = jnp.tile(jnp.arange(GRID_W), rows).astype(jnp.float32)
    inv = ROPE_BASE ** (-jnp.arange(0, ROPE_AXIS, 2, dtype=jnp.float32) / ROPE_AXIS)
    ang_r = row[:, None] * inv[None, :]
    ang_c = col[:, None] * inv[None, :]
    ang = jnp.concatenate([ang_r, ang_r, ang_c, ang_c], axis=-1)
    return jnp.cos(ang), jnp.sin(ang)


def rotate_axial(x):
    half = ROPE_AXIS // 2
    xr, xc = x[..., :ROPE_AXIS], x[..., ROPE_AXIS:]
    return jnp.concatenate([-xr[..., half:], xr[..., :half], -xc[..., half:], xc[..., :half]], axis=-1)


def apply_rope(x, cos, sin):
    xf = x.astype(jnp.float32)
    cb = cos[:, None, None, :]
    sb = sin[:, None, None, :]
    return (xf * cb + rotate_axial(xf) * sb).astype(x.dtype)


def diff_lambda(lp, lam_init):
    lp = lp.astype(jnp.float32)
    return jnp.exp(jnp.sum(lp[0] * lp[1])) - jnp.exp(jnp.sum(lp[2] * lp[3])) + lam_init


def diff_attention(q, k, v, lam):
    bsz, nh, lq = q.shape[0], q.shape[1], q.shape[2]
    nb = lq // Q_BLOCK
    scale = HALF_HEAD ** -0.5
    qb = q.reshape(bsz, nh, nb, Q_BLOCK, 2, HALF_HEAD).transpose(2, 0, 1, 3, 4, 5)

    def one_block(qblk):
        s = jnp.einsum('bhqcd,bhkcd->cbhqk', qblk, k).astype(jnp.float32) * scale
        p = jax.nn.softmax(s, axis=-1)
        w = (p[0] - lam * p[1]).astype(v.dtype)
        return jnp.einsum('bhqk,bhkd->bhqd', w, v)

    o = lax.map(one_block, qb)
    return o.transpose(1, 2, 0, 3, 4).reshape(bsz, nh, lq, V_HEAD)


def even_mixer(h, w_in, conv_w, w_out, sub_g, lam, lam_init, rope, ctx_k, ctx_v):
    bsz, n, _ = h.shape
    q, k, v, gate_b, gate_c, xin = jnp.split(h @ w_in, SPLIT_EVEN, axis=-1)
    q = q.reshape(bsz, n, N_HEADS_A, 2, HALF_HEAD)
    k = k.reshape(bsz, n, N_HEADS_A, 2, HALF_HEAD)
    if rope is not None:
        cos, sin = rope
        q = apply_rope(q, cos, sin)
        k = apply_rope(k, cos, sin)
    q = q.transpose(0, 2, 1, 3, 4)
    k = k.transpose(0, 2, 1, 3, 4)
    v = v.reshape(bsz, n, N_HEADS_A, V_HEAD).transpose(0, 2, 1, 3)
    if ctx_k is None:
        k_all, v_all = k, v
    else:
        ck = ctx_k.reshape(bsz, N_HEADS_A, ctx_k.shape[2], 2, HALF_HEAD)
        k_all = jnp.concatenate([ck, k], axis=2)
        v_all = jnp.concatenate([ctx_v, v], axis=2)
    o = diff_attention(q, k_all, v_all, lam)
    o = rmsnorm(o, sub_g) * (1.0 - lam_init)
    o = o.transpose(0, 2, 1, 3).reshape(bsz, n, ATTN_W)
    z = gate_c * xin
    zp = jnp.pad(z, ((0, 0), (1, 1), (0, 0)))
    conv = conv_w[0] * zp[:, :-2] + conv_w[1] * zp[:, 1:-1] + conv_w[2] * zp[:, 2:]
    y = jnp.concatenate([o, gate_b * conv], axis=-1) @ w_out
    return y, k.reshape(bsz, N_HEADS_A, n, QK_HEAD), v


def multi_scale_pool(x):
    bsz, n, ch = x.shape
    xf = x.astype(jnp.float32)
    csum = jnp.concatenate([jnp.zeros((bsz, 1, ch), jnp.float32), jnp.cumsum(xf, axis=1)], axis=1)
    t = jnp.arange(n)
    outs = []
    for g, win in enumerate(POOL_WINDOWS):
        lo = jnp.clip(t - win // 2, 0, n)
        hi = jnp.clip(t - win // 2 + win, 0, n)
        sg = csum[..., g * POOL_GROUP:(g + 1) * POOL_GROUP]
        cnt = (hi - lo).astype(jnp.float32)[:, None]
        outs.append((sg[:, hi] - sg[:, lo]) / cnt)
    return jnp.concatenate(outs, axis=-1).astype(x.dtype)


def odd_mixer(h, w_in, w_pool, pool_scale, w_four, w_out):
    bsz, n, _ = h.shape
    u = h @ w_in
    up, uf = u[..., :POOL_W], u[..., POOL_W:]
    pooled = (multi_scale_pool(up) - up).reshape(bsz, n, N_POOL_GROUPS, POOL_GROUP)
    pc = jnp.einsum('blgc,gcd->blgd', pooled, w_pool).reshape(bsz, n, POOL_W) * pool_scale
    fg = uf.reshape(bsz, n, N_FOURIER_GROUPS, FOURIER_GROUP).astype(jnp.float32)
    four = jnp.fft.fft2(fg, axes=(1, 3), norm='ortho').real.astype(h.dtype)
    fc = jnp.einsum('blgc,gcd->blgd', four, w_four).reshape(bsz, n, FOURIER_W)
    return jnp.concatenate([pc, fc], axis=-1) @ w_out


def setup_inputs(seed: int = 0) -> dict:
    key = jax.random.key(seed)
    ks = jax.random.split(key, 24)
    f32 = jnp.float32
    D = D_MODEL

    def nrm(k, shape, scale):
        return jax.random.normal(k, shape, f32) * scale

    return {
        'x_prompt': nrm(ks[0], (BATCH, SEQ, D), 1.0),
        'x_sample': nrm(ks[1], (DEC_BATCH, DEC_SEQ, D), 1.0),
        'cache_k': nrm(ks[2], (DEC_BATCH, N_EVEN, N_HEADS_A, PAST_LEN, QK_HEAD), 1.0),
        'cache_v': nrm(ks[3], (DEC_BATCH, N_EVEN, N_HEADS_A, PAST_LEN, V_HEAD), 1.0),
        'c': nrm(ks[4], (DEC_BATCH, D), 1.0),
        'c_ctx': nrm(ks[5], (D,), 1.0),
        'w_mod': nrm(ks[6], (DEPTH, D, 6 * D), 0.5 * D ** -0.5),
        'b_mod': nrm(ks[7], (DEPTH, 6 * D), 0.02),
        'norm_g': 1.0 + nrm(ks[8], (DEPTH, 4, D), 0.02),
        'w_in_even': nrm(ks[9], (N_EVEN, D, EVEN_IN_W), D ** -0.5),
        'lam_params': nrm(ks[10], (N_EVEN, 4, HALF_HEAD), 0.1),
        'subln_g': 1.0 + nrm(ks[11], (N_EVEN, V_HEAD), 0.02),
        'conv_w': nrm(ks[12], (N_EVEN, 3, CONV_W), 3 ** -0.5),
        'w_out_even': nrm(ks[13], (N_EVEN, MIX_W, D), MIX_W ** -0.5),
        'w_in_odd': nrm(ks[14], (N_ODD, D, MIX_W), D ** -0.5),
        'w_pool': nrm(ks[15], (N_ODD, N_POOL_GROUPS, POOL_GROUP, POOL_GROUP), POOL_GROUP ** -0.5),
        'pool_scale': 1.0 + nrm(ks[16], (N_ODD, POOL_W), 0.02),
        'w_fourier': nrm(ks[17], (N_ODD, N_FOURIER_GROUPS, FOURIER_GROUP, FOURIER_GROUP), FOURIER_GROUP ** -0.5),
        'w_out_odd': nrm(ks[18], (N_ODD, MIX_W, D), MIX_W ** -0.5),
        'w_gate': nrm(ks[19], (DEPTH, D, D_FF), D ** -0.5),
        'w_up': nrm(ks[20], (DEPTH, D, D_FF), D ** -0.5),
        'w_down': nrm(ks[21], (DEPTH, D_FF, D), D_FF ** -0.5),
    }


def reference(x_prompt, x_sample, cache_k, cache_v, c, c_ctx, w_mod, b_mod, norm_g,
              w_in_even, lam_params, subln_g, conv_w, w_out_even,
              w_in_odd, w_pool, pool_scale, w_fourier, w_out_odd,
              w_gate, w_up, w_down):
    xp, xs = x_prompt, x_sample
    rope = axial_rope_tables(xs.shape[1])
    silu_c = jax.nn.silu(c)
    silu_ctx = jax.nn.silu(c_ctx)
    new_k, new_v = [], []
    for l in range(DEPTH):
        mp = split_mod(silu_ctx @ w_mod[l] + b_mod[l])
        ms = split_mod((silu_c @ w_mod[l] + b_mod[l])[:, None, :])
        g = norm_g[l]
        hp = modulate(xp, g[0], mp[0], mp[1])
        hs = modulate(xs, g[0], ms[0], ms[1])
        if l % 2 == 0:
            i = l // 2
            lam_init = 0.8 - 0.6 * math.exp(-0.3 * l)
            lam = diff_lambda(lam_params[i], lam_init)
            yp, kp, vp = even_mixer(hp, w_in_even[i], conv_w[i], w_out_even[i], subln_g[i],
                                    lam, lam_init, None, None, None)
            ys, _, _ = even_mixer(hs, w_in_even[i], conv_w[i], w_out_even[i], subln_g[i],
                                  lam, lam_init, rope, cache_k[:, i], cache_v[:, i])
            new_k.append(kp)
            new_v.append(vp)
        else:
            i = l // 2
            yp = odd_mixer(hp, w_in_odd[i], w_pool[i], pool_scale[i], w_fourier[i], w_out_odd[i])
            ys = odd_mixer(hs, w_in_odd[i], w_pool[i], pool_scale[i], w_fourier[i], w_out_odd[i])
        xp = gated_residual(xp, yp, g[1], mp[2])
        xs = gated_residual(xs, ys, g[1], ms[2])
        hp = modulate(xp, g[2], mp[3], mp[4])
        hs = modulate(xs, g[2], ms[3], ms[4])
        xp = gated_residual(xp, swiglu(hp, w_gate[l], w_up[l], w_down[l]), g[3], mp[5])
        xs = gated_residual(xs, swiglu(hs, w_gate[l], w_up[l], w_down[l]), g[3], ms[5])
    new_k_arr = jnp.stack(new_k, axis=1)
    new_v_arr = jnp.stack(new_v, axis=1)
    return (xp, xs, new_k_arr, new_v_arr)
```

```python
import functools
import math

import numpy as np
import jax
import jax.numpy as jnp
from jax import lax
from jax.experimental import pallas as pl
from jax.experimental.pallas import tpu as pltpu

F32 = jnp.float32
BF16 = jnp.bfloat16

D_MODEL = 1024
GRID_W = 64
N_HEADS = 4
HEAD = 128
HALF_HEAD = 64
ROPE_AXIS = 32
ROPE_BASE = 10000.0
ATTN_W = 512
CONV_W = 512
POOL_W = 512
FOURIER_W = 512
GROUP = 128
POOL_WINDOWS = (2, 4, 8, 16)
D_FF = 2816
EPS = 1e-6
HALO = 8
VMEM_LIMIT = 56 * 1024 * 1024


def _params(n_axes):
    return pltpu.CompilerParams(dimension_semantics=("arbitrary",) * n_axes,
                                vmem_limit_bytes=VMEM_LIMIT)


def _resident(shape):
    return pl.BlockSpec(shape, lambda *_: (0,) * len(shape), pipeline_mode=pl.Buffered(1))


def _rms(x, g):
    ms = jnp.mean(x * x, axis=-1, keepdims=True)
    return x * lax.rsqrt(ms + EPS) * g


def _modulate(x, g, shift, scale):
    return _rms(x, g) * (1.0 + scale) + shift


def _dot(a, b):
    return jnp.dot(a, b, preferred_element_type=F32)


def _silu(x):
    return x / (1.0 + jnp.exp(-x))


def _mod_kernel(cc_ref, w_ref, b_ref, o_ref):
    s = _silu(cc_ref[...]).astype(BF16)
    o_ref[...] = _dot(s, w_ref[...].astype(BF16)) + b_ref[...]


def _mod_call(cc, w_mod, b_mod):
    depth, d, n6 = w_mod.shape
    rows = cc.shape[0]
    tn = 2048
    return pl.pallas_call(
        _mod_kernel,
        out_shape=jax.ShapeDtypeStruct((depth, rows, n6), F32),
        grid=(depth, n6 // tn),
        in_specs=[
            pl.BlockSpec((rows, d), lambda l, j: (0, 0)),
            pl.BlockSpec((None, d, tn), lambda l, j: (l, 0, j)),
            pl.BlockSpec((None, 1, tn), lambda l, j: (l, 0, j)),
        ],
        out_specs=pl.BlockSpec((None, rows, tn), lambda l, j: (l, 0, j)),
        compiler_params=_params(2),
        name="mod",
    )(cc, w_mod, b_mod.reshape(depth, 1, n6))


def _tile_specs(n, tm, mod_base, mod_stride):
    nb8 = n // HALO
    t8 = tm // HALO
    x_spec = pl.BlockSpec((None, tm, D_MODEL), lambda b, j: (b, j, 0))
    prev_spec = pl.BlockSpec((None, HALO, D_MODEL),
                             lambda b, j: (b, jnp.maximum(j * t8 - 1, 0), 0))
    next_spec = pl.BlockSpec((None, HALO, D_MODEL),
                             lambda b, j: (b, jnp.minimum((j + 1) * t8, nb8 - 1), 0))
    mod_spec = pl.BlockSpec((None, 6, D_MODEL),
                            lambda b, j: (mod_base + mod_stride * b, 0, 0))
    return x_spec, prev_spec, next_spec, mod_spec


def _rope(t, cos, sin_signed, first_half):
    outs = []
    for hh in range(N_HEADS):
        th = t[:, HEAD * hh:HEAD * (hh + 1)]
        swapped = jnp.where(first_half, pltpu.roll(th, HEAD - 16, 1), pltpu.roll(th, 16, 1))
        outs.append(th * cos + swapped * sin_signed)
    return outs


def _even_in_kernel(*refs, use_rope):
    if use_rope:
        (x_ref, xp_ref, xn_ref, mod_ref, g_ref, w_ref, cw_ref, cos_ref, sin_ref,
         q_ref, k_ref, v_ref, cb_ref, zs_ref) = refs
    else:
        (x_ref, xp_ref, xn_ref, mod_ref, g_ref, w_ref, cw_ref,
         q_ref, k_ref, v_ref, cb_ref, zs_ref) = refs
    j = pl.program_id(1)
    nt = pl.num_programs(1)
    tm = x_ref.shape[0]
    g = g_ref[0:1, :]
    shift = mod_ref[0:1, :]
    scale = mod_ref[1:2, :]
    h = _modulate(x_ref[...], g, shift, scale).astype(BF16)

    def proj(hh, lo):
        return _dot(hh, w_ref[:, lo:lo + 512])

    q = proj(h, 0) * (HALF_HEAD ** -0.5)
    k = proj(h, 512)
    v = proj(h, 1024)
    if use_rope:
        lane = lax.broadcasted_iota(jnp.int32, (1, HEAD), 1)
        first_half = (lane % 32) < 16
        cos = cos_ref[...]
        sin_signed = sin_ref[...]
        qs = _rope(q, cos, sin_signed, first_half)
        ks = _rope(k, cos, sin_signed, first_half)
    else:
        qs = [q[:, HEAD * hh:HEAD * (hh + 1)] for hh in range(N_HEADS)]
        ks = [k[:, HEAD * hh:HEAD * (hh + 1)] for hh in range(N_HEADS)]
    for hh in range(N_HEADS):
        q_ref[hh] = qs[hh].astype(q_ref.dtype)
        k_ref[hh] = ks[hh].astype(k_ref.dtype)
        v_ref[hh] = v[:, HEAD * hh:HEAD * (hh + 1)].astype(v_ref.dtype)

    gate_b = proj(h, 1536)
    z = proj(h, 2048) * proj(h, 2560)
    xh = jnp.concatenate([xp_ref[...], xn_ref[...]], axis=0)
    hh_ = _modulate(xh, g, shift, scale).astype(BF16)
    zh = proj(hh_, 2048) * proj(hh_, 2560)
    zs_ref[0:HALO, :] = jnp.where(j > 0, zh[0:HALO], 0.0)
    zs_ref[HALO:HALO + tm, :] = z
    zs_ref[HALO + tm:, :] = jnp.where(j < nt - 1, zh[HALO:], 0.0)
    conv = (cw_ref[0:1, :] * zs_ref[HALO - 1:HALO - 1 + tm, :] + cw_ref[1:2, :] * z
            + cw_ref[2:3, :] * zs_ref[HALO + 1:HALO + 1 + tm, :])
    cb_ref[...] = (gate_b * conv).astype(BF16)


def _even_in_call(x, mod, g, w_in, conv_w, rope, tm, mod_base, mod_stride, kv_dtype):
    bsz, n, d = x.shape
    use_rope = rope is not None
    x_spec, prev_spec, next_spec, mod_spec = _tile_specs(n, tm, mod_base, mod_stride)
    in_specs = [x_spec, prev_spec, next_spec, mod_spec,
                _resident(g.shape), _resident(w_in.shape), _resident(conv_w.shape)]
    args = [x, x, x, mod, g, w_in, conv_w]
    if use_rope:
        tab = pl.BlockSpec((tm, HEAD), lambda b, j: (j, 0))
        in_specs += [tab, tab]
        args += list(rope)
    head_spec = pl.BlockSpec((None, N_HEADS, tm, HEAD), lambda b, j: (b, 0, j, 0))
    head_shape = (bsz, N_HEADS, n, HEAD)
    return pl.pallas_call(
        functools.partial(_even_in_kernel, use_rope=use_rope),
        out_shape=(jax.ShapeDtypeStruct(head_shape, BF16),
                   jax.ShapeDtypeStruct(head_shape, kv_dtype),
                   jax.ShapeDtypeStruct(head_shape, kv_dtype),
                   jax.ShapeDtypeStruct((bsz, n, CONV_W), BF16)),
        grid=(bsz, n // tm),
        in_specs=in_specs,
        out_specs=(head_spec, head_spec, head_spec,
                   pl.BlockSpec((None, tm, CONV_W), lambda b, j: (b, j, 0))),
        scratch_shapes=[pltpu.VMEM((tm + 2 * HALO, CONV_W), F32)],
        compiler_params=_params(2),
        name="even_in_rope" if use_rope else "even_in",
    )(*args)


def _attn_kernel(*refs, has_cache, lam_init):
    if has_cache:
        lam_ref, sg_ref, q_ref, k_ref, v_ref, ck_ref, cv_ref, o_ref = refs
    else:
        lam_ref, sg_ref, q_ref, k_ref, v_ref, o_ref = refs
    lp = lam_ref[...]
    lam = (jnp.exp(jnp.sum(lp[0:1] * lp[1:2], axis=-1, keepdims=True))
           - jnp.exp(jnp.sum(lp[2:3] * lp[3:4], axis=-1, keepdims=True)) + lam_init)
    q = q_ref[...]
    lane = lax.broadcasted_iota(jnp.int32, (1, HEAD), 1)
    kn = k_ref[...].astype(BF16)
    vn = v_ref[...].astype(BF16)
    if has_cache:
        kc = ck_ref[...].astype(BF16)
        vc = cv_ref[...].astype(BF16)

    def qk(qc, kk):
        return lax.dot_general(qc, kk, (((1,), (1,)), ((), ())), preferred_element_type=F32)

    e_new, e_cache, inv_l = [], [], []
    for comp in range(2):
        in_comp = (lane < HALF_HEAD) if comp == 0 else (lane >= HALF_HEAD)
        qc = jnp.where(in_comp, q, jnp.zeros_like(q))
        s_n = qk(qc, kn)
        m = jnp.max(s_n, axis=-1, keepdims=True)
        if has_cache:
            s_c = qk(qc, kc)
            m = jnp.maximum(m, jnp.max(s_c, axis=-1, keepdims=True))
            e_c = jnp.exp(s_c - m)
        e_n = jnp.exp(s_n - m)
        l = jnp.sum(e_n, axis=-1, keepdims=True)
        if has_cache:
            l = l + jnp.sum(e_c, axis=-1, keepdims=True)
            e_cache.append(e_c)
        e_new.append(e_n)
        inv_l.append(1.0 / l)
    r0 = inv_l[0]
    r1 = lam * inv_l[1]
    o = _dot((e_new[0] * r0 - e_new[1] * r1).astype(BF16), vn)
    if has_cache:
        o = o + _dot((e_cache[0] * r0 - e_cache[1] * r1).astype(BF16), vc)
    o_ref[...] = (_rms(o, sg_ref[...]) * (1.0 - lam_init)).astype(BF16)


def _attn_call(q, k, v, lam_params, subln_g, cache, tq, lam_init):
    bsz, nh, n, hd = q.shape
    has_cache = cache is not None
    kv_spec = pl.BlockSpec((None, None, n, hd), lambda b, h, j: (b, h, 0, 0))
    in_specs = [_resident(lam_params.shape), _resident(subln_g.shape),
                pl.BlockSpec((None, None, tq, hd), lambda b, h, j: (b, h, j, 0)),
                kv_spec, kv_spec]
    args = [lam_params, subln_g, q, k, v]
    if has_cache:
        ck, cv, layer = cache
        past = ck.shape[3]
        c_spec = pl.BlockSpec((None, None, None, past, hd), lambda b, h, j: (b, layer, h, 0, 0))
        in_specs += [c_spec, c_spec]
        args += [ck, cv]
    return pl.pallas_call(
        functools.partial(_attn_kernel, has_cache=has_cache, lam_init=lam_init),
        out_shape=jax.ShapeDtypeStruct((bsz, n, nh * hd), BF16),
        grid=(bsz, nh, n // tq),
        in_specs=in_specs,
        out_specs=pl.BlockSpec((None, tq, hd), lambda b, h, j: (b, j, h)),
        compiler_params=_params(3),
        name="attn_cache" if has_cache else "attn",
    )(*args)


def _post_kernel(x_ref, a_ref, b_ref, mod_ref, g_ref, wo_ref, wg_ref, wu_ref, wd_ref, o_ref):
    half = a_ref.shape[1]
    y = _dot(a_ref[...], wo_ref[0:half, :]) + _dot(b_ref[...], wo_ref[half:, :])
    x1 = x_ref[...] + mod_ref[2:3, :] * _rms(y, g_ref[1:2, :])
    h = _modulate(x1, g_ref[2:3, :], mod_ref[3:4, :], mod_ref[4:5, :]).astype(BF16)
    act = (_silu(_dot(h, wg_ref[...])) * _dot(h, wu_ref[...])).astype(BF16)
    f = _dot(act, wd_ref[...])
    o_ref[...] = x1 + mod_ref[5:6, :] * _rms(f, g_ref[3:4, :])


def _post_call(x, a, b, mod, g, w_out, w_gate, w_up, w_down, tm, mod_base, mod_stride):
    bsz, n, d = x.shape
    x_spec, _, _, mod_spec = _tile_specs(n, tm, mod_base, mod_stride)
    half_spec = pl.BlockSpec((None, tm, a.shape[2]), lambda b_, j: (b_, j, 0))
    return pl.pallas_call(
        _post_kernel,
        out_shape=jax.ShapeDtypeStruct(x.shape, F32),
        grid=(bsz, n // tm),
        in_specs=[x_spec, half_spec, half_spec, mod_spec, _resident(g.shape),
                  _resident(w_out.shape), _resident(w_gate.shape), _resident(w_up.shape),
                  _resident(w_down.shape)],
        out_specs=x_spec,
        compiler_params=_params(2),
        name="post",
    )(x, a, b, mod, g, w_out, w_gate, w_up, w_down)


def _odd_in_kernel(x_ref, xp_ref, xn_ref, mod_ref, g_ref, w_ref, wp_ref, ps_ref, cs_ref,
                   pc_ref, xc_ref, xs_ref, us_ref, *, n_seq):
    j = pl.program_id(1)
    nt = pl.num_programs(1)
    tm = x_ref.shape[0]
    g = g_ref[0:1, :]
    shift = mod_ref[0:1, :]
    scale = mod_ref[1:2, :]
    h = _modulate(x_ref[...], g, shift, scale).astype(BF16)
    up = _dot(h, w_ref[:, 0:POOL_W])
    uf = _dot(h, w_ref[:, POOL_W:])
    xh = jnp.concatenate([xp_ref[...], xn_ref[...]], axis=0)
    hh = _modulate(xh, g, shift, scale).astype(BF16)
    uph = _dot(hh, w_ref[:, 0:POOL_W])
    us_ref[0:HALO, :] = jnp.where(j > 0, uph[0:HALO], 0.0)
    us_ref[HALO:HALO + tm, :] = up
    us_ref[HALO + tm:, :] = jnp.where(j < nt - 1, uph[HALO:], 0.0)

    t = j * tm + lax.broadcasted_iota(jnp.int32, (tm, 1), 0)
    for gi, win in enumerate(POOL_WINDOWS):
        lanes = slice(GROUP * gi, GROUP * (gi + 1))
        start = HALO - win // 2
        acc = us_ref[start:start + tm, lanes]
        for dlt in range(1, win):
            acc = acc + us_ref[start + dlt:start + dlt + tm, lanes]
        lo = jnp.clip(t - win // 2, 0, n_seq)
        hi = jnp.clip(t - win // 2 + win, 0, n_seq)
        cnt = (hi - lo).astype(F32)
        diff = (acc / cnt - up[:, lanes]).astype(BF16)
        pc_ref[:, lanes] = (_dot(diff, wp_ref[gi]) * ps_ref[0:1, lanes]).astype(BF16)
        cs = _dot(uf[:, lanes].astype(BF16), cs_ref[...])
        xc_ref[:, lanes] = cs[:, 0:GROUP].astype(BF16)
        xs_ref[:, lanes] = cs[:, GROUP:].astype(BF16)


def _odd_in_call(x, mod, g, w_in, w_pool, pool_scale, cs_mat, tm, mod_base, mod_stride):
    bsz, n, d = x.shape
    x_spec, prev_spec, next_spec, mod_spec = _tile_specs(n, tm, mod_base, mod_stride)
    out_spec = pl.BlockSpec((None, tm, POOL_W), lambda b, j: (b, j, 0))
    out_sds = jax.ShapeDtypeStruct((bsz, n, POOL_W), BF16)
    return pl.pallas_call(
        functools.partial(_odd_in_kernel, n_seq=n),
        out_shape=(out_sds, out_sds, out_sds),
        grid=(bsz, n // tm),
        in_specs=[x_spec, prev_spec, next_spec, mod_spec, _resident(g.shape), _resident(w_in.shape),
                  _resident(w_pool.shape), _resident(pool_scale.shape), _resident(cs_mat.shape)],
        out_specs=(out_spec, out_spec, out_spec),
        scratch_shapes=[pltpu.VMEM((tm + 2 * HALO, POOL_W), F32)],
        compiler_params=_params(2),
        name="odd_in",
    )(x, x, x, mod, g, w_in, w_pool, pool_scale, cs_mat)


def _four_kernel(c_ref, s_ref, xc_ref, xs_ref, wf_ref, o_ref, *, scale):
    y = _dot(c_ref[...], xc_ref[...]) - _dot(s_ref[...], xs_ref[...])
    four = (y * scale).astype(BF16)
    for gi in range(FOURIER_W // GROUP):
        lanes = slice(GROUP * gi, GROUP * (gi + 1))
        o_ref[:, lanes] = _dot(four[:, lanes], wf_ref[gi]).astype(BF16)


def _four_call(cn, sn, xc, xs, w_four, tm):
    bsz, n, w = xc.shape
    mat_spec = pl.BlockSpec((tm, n), lambda b, j: (j, 0))
    seq_spec = pl.BlockSpec((None, n, w), lambda b, j: (b, 0, 0))
    return pl.pallas_call(
        functools.partial(_four_kernel, scale=float(1.0 / math.sqrt(n * GROUP))),
        out_shape=jax.ShapeDtypeStruct((bsz, n, w), BF16),
        grid=(bsz, n // tm),
        in_specs=[mat_spec, mat_spec, seq_spec, seq_spec, _resident(w_four.shape)],
        out_specs=pl.BlockSpec((None, tm, w), lambda b, j: (b, j, 0)),
        compiler_params=_params(2),
        name="fourier",
    )(cn, sn, xc, xs, w_four)


def _rope_tables(n_tok):
    rows = n_tok // GRID_W
    row = np.repeat(np.arange(rows), GRID_W).astype(np.float64)
    col = np.tile(np.arange(GRID_W), rows).astype(np.float64)
    inv = ROPE_BASE ** (-np.arange(0, ROPE_AXIS, 2, dtype=np.float64) / ROPE_AXIS)
    ang_r = row[:, None] * inv[None, :]
    ang_c = col[:, None] * inv[None, :]
    ang = np.concatenate([ang_r, ang_r, ang_c, ang_c], axis=-1)
    cos = np.concatenate([np.cos(ang)] * 2, axis=-1)
    sin = np.concatenate([np.sin(ang)] * 2, axis=-1)
    first_half = (np.arange(HEAD) % 32) < 16
    sin_signed = np.where(first_half[None, :], -sin, sin)
    return jnp.asarray(cos, F32), jnp.asarray(sin_signed, F32)


def _dft_mats(n):
    idx = np.arange(n, dtype=np.int64)
    ang = 2.0 * np.pi * ((idx[:, None] * idx[None, :]) % n).astype(np.float64) / n
    return np.cos(ang), np.sin(ang)


def kernel(x_prompt, x_sample, cache_k, cache_v, c, c_ctx, w_mod, b_mod, norm_g,
           w_in_even, lam_params, subln_g, conv_w, w_out_even,
           w_in_odd, w_pool, pool_scale, w_fourier, w_out_odd,
           w_gate, w_up, w_down):
    depth = w_mod.shape[0]
    n_dec = x_sample.shape[0]
    n_p, n_s = x_prompt.shape[1], x_sample.shape[1]
    tm_p, tm_s = n_p, 512

    pad_rows = 16 - 1 - n_dec
    cc = jnp.concatenate([c_ctx[None, :], c, jnp.zeros((pad_rows, D_MODEL), F32)], axis=0)
    mod_all = _mod_call(cc, w_mod, b_mod)[:, :1 + n_dec].reshape(depth, 1 + n_dec, 6, D_MODEL)

    rope = _rope_tables(n_s)
    cc_g, sc_g = _dft_mats(GROUP)
    cs_mat = jnp.asarray(np.concatenate([cc_g, sc_g], axis=1), F32).astype(BF16)
    dft = {n: tuple(jnp.asarray(m, F32).astype(BF16) for m in _dft_mats(n)) for n in (n_p, n_s)}

    xp, xs = x_prompt, x_sample
    new_k, new_v = [], []
    for l in range(depth):
        mod = mod_all[l]
        g = norm_g[l]
        i = l // 2
        wg, wu, wd = w_gate[l].astype(BF16), w_up[l].astype(BF16), w_down[l].astype(BF16)
        streams = []
        if l % 2 == 0:
            lam_init = 0.8 - 0.6 * math.exp(-0.3 * l)
            w_in = w_in_even[i].astype(BF16)
            w_out = w_out_even[i].astype(BF16)
            sg = subln_g[i][None, :]
            qp, kp, vp, cbp = _even_in_call(xp, mod, g, w_in, conv_w[i], None, tm_p, 0, 0, F32)
            ap = _attn_call(qp, kp, vp, lam_params[i], sg, None, n_p, lam_init)
            new_k.append(kp)
            new_v.append(vp)
            qs, ks, vs, cbs = _even_in_call(xs, mod, g, w_in, conv_w[i], rope, tm_s, 1, 1, BF16)
            a_s = _attn_call(qs, ks, vs, lam_params[i], sg, (cache_k, cache_v, i), 256, lam_init)
            streams = [(ap, cbp), (a_s, cbs)]
        else:
            w_in = w_in_odd[i].astype(BF16)
            w_out = w_out_odd[i].astype(BF16)
            wp = w_pool[i].astype(BF16)
            wf = w_fourier[i].astype(BF16)
            ps = pool_scale[i][None, :]
            pcp, xcp, xsp = _odd_in_call(xp, mod, g, w_in, wp, ps, cs_mat, tm_p, 0, 0)
            fcp = _four_call(*dft[n_p], xcp, xsp, wf, n_p)
            pcs, xcs, xss = _odd_in_call(xs, mod, g, w_in, wp, ps, cs_mat, tm_s, 1, 1)
            fcs = _four_call(*dft[n_s], xcs, xss, wf, 512)
            streams = [(pcp, fcp), (pcs, fcs)]
        xp = _post_call(xp, streams[0][0], streams[0][1], mod, g, w_out, wg, wu, wd, tm_p, 0, 0)
        xs = _post_call(xs, streams[1][0], streams[1][1], mod, g, w_out, wg, wu, wd, 256, 1, 1)
    return xp, xs, jnp.stack(new_k, axis=1), jnp.stack(new_v, axis=1)
```

```python
import functools
import math

import numpy as np
import jax
import jax.numpy as jnp
from jax import lax
from jax.experimental import pallas as pl
from jax.experimental.pallas import tpu as pltpu

F32 = jnp.float32
BF16 = jnp.bfloat16

D_MODEL = 1024
GRID_W = 64
N_HEADS = 4
HEAD = 128
HALF_HEAD = 64
ROPE_AXIS = 32
ROPE_BASE = 10000.0
ATTN_W = 512
CONV_W = 512
POOL_W = 512
FOURIER_W = 512
GROUP = 128
POOL_WINDOWS = (2, 4, 8, 16)
D_FF = 2816
EPS = 1e-6
LOG2E = math.log2(math.e)
HALO = 8
MXU_N = 256
ATTN_SUB = 128
VMEM_LIMIT = 56 * 1024 * 1024


def _params(n_axes):
    return pltpu.CompilerParams(dimension_semantics=("arbitrary",) * n_axes,
                                vmem_limit_bytes=VMEM_LIMIT)


def _resident(shape):
    return pl.BlockSpec(shape, lambda *_: (0,) * len(shape), pipeline_mode=pl.Buffered(1))


def _rms(x, g):
    ms = jnp.mean(x * x, axis=-1, keepdims=True)
    return x * lax.rsqrt(ms + EPS) * g


def _modulate(x, g, shift, scale):
    return _rms(x, g) * (1.0 + scale) + shift


def _dot(a, b):
    return jnp.dot(a, b, preferred_element_type=F32)


def _silu(x):
    return x / (1.0 + jnp.exp(-x))


def _mod_kernel(cc_ref, w_ref, b_ref, o_ref):
    s = _silu(cc_ref[...]).astype(BF16)
    o_ref[...] = _dot(s, w_ref[...].astype(BF16)) + b_ref[...]


def _mod_call(cc, w_mod, b_mod):
    depth, d, n6 = w_mod.shape
    rows = cc.shape[0]
    tn = 2048
    return pl.pallas_call(
        _mod_kernel,
        out_shape=jax.ShapeDtypeStruct((depth, rows, n6), F32),
        grid=(depth, n6 // tn),
        in_specs=[
            pl.BlockSpec((rows, d), lambda l, j: (0, 0)),
            pl.BlockSpec((None, d, tn), lambda l, j: (l, 0, j)),
            pl.BlockSpec((None, 1, tn), lambda l, j: (l, 0, j)),
        ],
        out_specs=pl.BlockSpec((None, rows, tn), lambda l, j: (l, 0, j)),
        compiler_params=_params(2),
        name="mod",
    )(cc, w_mod, b_mod.reshape(depth, 1, n6))


def _tile_specs(n, tm, mod_base, mod_stride):
    nb8 = n // HALO
    t8 = tm // HALO
    x_spec = pl.BlockSpec((None, tm, D_MODEL), lambda b, j: (b, j, 0))
    prev_spec = pl.BlockSpec((None, HALO, D_MODEL),
                             lambda b, j: (b, jnp.maximum(j * t8 - 1, 0), 0))
    next_spec = pl.BlockSpec((None, HALO, D_MODEL),
                             lambda b, j: (b, jnp.minimum((j + 1) * t8, nb8 - 1), 0))
    mod_spec = pl.BlockSpec((None, 6, D_MODEL),
                            lambda b, j: (mod_base + mod_stride * b, 0, 0))
    return x_spec, prev_spec, next_spec, mod_spec


def _rope(t, cos, sin_signed, first_half):
    outs = []
    for hh in range(N_HEADS):
        th = t[:, HEAD * hh:HEAD * (hh + 1)]
        swapped = jnp.where(first_half, pltpu.roll(th, HEAD - 16, 1), pltpu.roll(th, 16, 1))
        outs.append(th * cos + swapped * sin_signed)
    return outs


def _even_in_kernel(*refs, use_rope):
    if use_rope:
        (x_ref, xp_ref, xn_ref, mod_ref, g_ref, w_ref, cw_ref, cos_ref, sin_ref,
         q_ref, k_ref, v_ref, cb_ref, zs_ref) = refs
    else:
        (x_ref, xp_ref, xn_ref, mod_ref, g_ref, w_ref, cw_ref,
         q_ref, k_ref, v_ref, cb_ref, zs_ref) = refs
    j = pl.program_id(1)
    nt = pl.num_programs(1)
    tm = x_ref.shape[0]
    g = g_ref[0:1, :]
    shift = mod_ref[0:1, :]
    scale = mod_ref[1:2, :]
    h = _modulate(x_ref[...], g, shift, scale).astype(BF16)

    def proj(hh, lo):
        return _dot(hh, w_ref[:, lo:lo + 512])

    q = proj(h, 0) * (HALF_HEAD ** -0.5 * LOG2E)
    k = proj(h, 512)
    v = proj(h, 1024)
    if use_rope:
        lane = lax.broadcasted_iota(jnp.int32, (1, HEAD), 1)
        first_half = (lane % 32) < 16
        cos = cos_ref[...]
        sin_signed = sin_ref[...]
        qs = _rope(q, cos, sin_signed, first_half)
        ks = [kh.T for kh in _rope(k, cos, sin_signed, first_half)]
    else:
        qs = [q[:, HEAD * hh:HEAD * (hh + 1)] for hh in range(N_HEADS)]
        ks = [k[:, HEAD * hh:HEAD * (hh + 1)] for hh in range(N_HEADS)]
    for hh in range(N_HEADS):
        q_ref[hh] = qs[hh].astype(q_ref.dtype)
        k_ref[hh] = ks[hh].astype(k_ref.dtype)
        v_ref[hh] = v[:, HEAD * hh:HEAD * (hh + 1)].astype(v_ref.dtype)

    gate_b = proj(h, 1536)
    z = proj(h, 2048) * proj(h, 2560)
    xh = jnp.concatenate([xp_ref[...], xn_ref[...]], axis=0)
    hh_ = _modulate(xh, g, shift, scale).astype(BF16)
    zh = proj(hh_, 2048) * proj(hh_, 2560)
    zs_ref[0:HALO, :] = jnp.where(j > 0, zh[0:HALO], 0.0)
    zs_ref[HALO:HALO + tm, :] = z
    zs_ref[HALO + tm:, :] = jnp.where(j < nt - 1, zh[HALO:], 0.0)
    conv = (cw_ref[0:1, :] * zs_ref[HALO - 1:HALO - 1 + tm, :] + cw_ref[1:2, :] * z
            + cw_ref[2:3, :] * zs_ref[HALO + 1:HALO + 1 + tm, :])
    cb_ref[...] = (gate_b * conv).astype(BF16)


def _even_in_call(x, mod, g, w_in, conv_w, rope, tm, mod_base, mod_stride):
    bsz, n, d = x.shape
    use_rope = rope is not None
    x_spec, prev_spec, next_spec, mod_spec = _tile_specs(n, tm, mod_base, mod_stride)
    in_specs = [x_spec, prev_spec, next_spec, mod_spec,
                _resident(g.shape), _resident(w_in.shape), _resident(conv_w.shape)]
    args = [x, x, x, mod, g, w_in, conv_w]
    if use_rope:
        tab = pl.BlockSpec((tm, HEAD), lambda b, j: (j, 0))
        in_specs += [tab, tab]
        args += list(rope)
    head_spec = pl.BlockSpec((None, N_HEADS, tm, HEAD), lambda b, j: (b, 0, j, 0))
    head_shape = (bsz, N_HEADS, n, HEAD)
    if use_rope:
        k_spec = pl.BlockSpec((None, N_HEADS, HEAD, tm), lambda b, j: (b, 0, 0, j))
        k_sds = jax.ShapeDtypeStruct((bsz, N_HEADS, HEAD, n), BF16)
        v_sds = jax.ShapeDtypeStruct(head_shape, BF16)
    else:
        k_spec = head_spec
        k_sds = v_sds = jax.ShapeDtypeStruct(head_shape, F32)
    return pl.pallas_call(
        functools.partial(_even_in_kernel, use_rope=use_rope),
        out_shape=(jax.ShapeDtypeStruct(head_shape, BF16), k_sds, v_sds,
                   jax.ShapeDtypeStruct((bsz, n, CONV_W), BF16)),
        grid=(bsz, n // tm),
        in_specs=in_specs,
        out_specs=(head_spec, k_spec, head_spec,
                   pl.BlockSpec((None, tm, CONV_W), lambda b, j: (b, j, 0))),
        scratch_shapes=[pltpu.VMEM((tm + 2 * HALO, CONV_W), F32)],
        compiler_params=_params(2),
        name="even_in_rope" if use_rope else "even_in",
    )(*args)


def _diff_lambda(lam_ref, lam_init):
    lp = lam_ref[...]
    return (jnp.exp(jnp.sum(lp[0:1] * lp[1:2], axis=-1, keepdims=True))
            - jnp.exp(jnp.sum(lp[2:3] * lp[3:4], axis=-1, keepdims=True)) + lam_init)


def _stack_components(q):
    lane = lax.broadcasted_iota(jnp.int32, (1, HEAD), 1)
    zero = jnp.zeros_like(q)
    return jnp.concatenate([jnp.where(lane < HALF_HEAD, q, zero),
                            jnp.where(lane >= HALF_HEAD, q, zero)], axis=0)


def _softmax_pv(s, v_ext):
    e = jnp.exp2(s - jnp.max(s, axis=-1, keepdims=True)).astype(BF16)
    return _dot(e, v_ext)


def _normalise(ov, lam, sg, lam_init):
    t = ov.shape[0] // 2
    o = ov[:t, :HEAD] / ov[:t, HEAD:] - lam * (ov[t:, :HEAD] / ov[t:, HEAD:])
    return (_rms(o, sg) * (1.0 - lam_init)).astype(BF16)


def _chain_pipeline(n_groups, width, scores_fn, pv_fn, finish_fn, s_ref, ov_ref):
    assert n_groups % 2 == 0 and n_groups >= 2

    def scores(g, par):
        for u in range(width):
            s_ref[par * width + u] = scores_fn(g, u)

    def values(g, par):
        for u in range(width):
            ov_ref[par * width + u] = pv_fn(g, u, s_ref[par * width + u])

    def finish(g, par):
        for u in range(width):
            finish_fn(g, u, ov_ref[par * width + u])

    scores(0, 0)
    scores(1, 1)
    values(0, 0)

    def body(t, carry):
        g = 2 * t
        scores(g, 0)
        values(g - 1, 1)
        finish(g - 2, 0)
        scores(g + 1, 1)
        values(g, 0)
        finish(g - 1, 1)
        return carry

    lax.fori_loop(1, n_groups // 2, body, 0)
    values(n_groups - 1, 1)
    finish(n_groups - 2, 0)
    finish(n_groups - 1, 1)


def _attn_prompt_kernel(lam_ref, sg_ref, q_ref, k_ref, v_ref, o_ref, s_ref, ov_ref, *, lam_init):
    lam = _diff_lambda(lam_ref, lam_init)
    sg = sg_ref[...]
    n = k_ref.shape[2]
    ones = jnp.ones((n, MXU_N - HEAD), BF16)

    def scores_fn(b, hh):
        return lax.dot_general(_stack_components(q_ref[b, hh]), k_ref[b, hh].astype(BF16),
                               (((1,), (1,)), ((), ())), preferred_element_type=F32)

    def pv_fn(b, hh, s):
        return _softmax_pv(s, jnp.concatenate([v_ref[b, hh].astype(BF16), ones], axis=1))

    def finish_fn(b, hh, ov):
        o_ref[b, :, HEAD * hh:HEAD * (hh + 1)] = _normalise(ov, lam, sg, lam_init)

    _chain_pipeline(q_ref.shape[0], N_HEADS, scores_fn, pv_fn, finish_fn, s_ref, ov_ref)


def _attn_prompt_call(q, k, v, lam_params, subln_g, lam_init, nb):
    bsz, nh, n, hd = q.shape
    spec = pl.BlockSpec((nb, nh, n, hd), lambda b: (b, 0, 0, 0))
    return pl.pallas_call(
        functools.partial(_attn_prompt_kernel, lam_init=lam_init),
        out_shape=jax.ShapeDtypeStruct((bsz, n, nh * hd), BF16),
        grid=(bsz // nb,),
        in_specs=[_resident(lam_params.shape), _resident(subln_g.shape), spec, spec, spec],
        out_specs=pl.BlockSpec((nb, n, nh * hd), lambda b: (b, 0, 0)),
        scratch_shapes=[pltpu.VMEM((2 * nh, 2 * n, n), F32), pltpu.VMEM((2 * nh, 2 * n, MXU_N), F32)],
        compiler_params=_params(1),
        name="attn",
    )(lam_params, subln_g, q, k, v)


def _attn_cache_kernel(lam_ref, sg_ref, q_ref, kt_ref, v_ref, ck_ref, cv_ref, o_ref,
                       ktbuf_ref, vbuf_ref, s_ref, ov_ref, *, lam_init):
    past = ck_ref.shape[0]
    ktbuf_ref[:, 0:past] = ck_ref[...].T.astype(BF16)
    ktbuf_ref[:, past:] = kt_ref[...]
    vbuf_ref[0:past, 0:HEAD] = cv_ref[...].astype(BF16)
    vbuf_ref[past:, 0:HEAD] = v_ref[...]
    vbuf_ref[:, HEAD:] = jnp.ones((vbuf_ref.shape[0], MXU_N - HEAD), BF16)
    lam = _diff_lambda(lam_ref, lam_init)
    sg = sg_ref[...]

    def rows(g):
        return pl.ds(pl.multiple_of(g * ATTN_SUB, ATTN_SUB), ATTN_SUB)

    def scores_fn(g, u):
        return _dot(_stack_components(q_ref[rows(g), :]), ktbuf_ref[...])

    def pv_fn(g, u, s):
        return _softmax_pv(s, vbuf_ref[...])

    def finish_fn(g, u, ov):
        o_ref[rows(g), :] = _normalise(ov, lam, sg, lam_init)

    _chain_pipeline(q_ref.shape[0] // ATTN_SUB, 1, scores_fn, pv_fn, finish_fn, s_ref, ov_ref)


def _attn_cache_call(q, kt, v, lam_params, subln_g, cache_k, cache_v, layer, lam_init):
    bsz, nh, n, hd = q.shape
    past = cache_k.shape[3]
    seq_spec = pl.BlockSpec((None, None, n, hd), lambda b, h: (b, h, 0, 0))
    kt_spec = pl.BlockSpec((None, None, hd, n), lambda b, h: (b, h, 0, 0))
    c_spec = pl.BlockSpec((None, None, None, past, hd), lambda b, h: (b, layer, h, 0, 0))
    return pl.pallas_call(
        functools.partial(_attn_cache_kernel, lam_init=lam_init),
        out_shape=jax.ShapeDtypeStruct((bsz, n, nh * hd), BF16),
        grid=(bsz, nh),
        in_specs=[_resident(lam_params.shape), _resident(subln_g.shape),
                  seq_spec, kt_spec, seq_spec, c_spec, c_spec],
        out_specs=pl.BlockSpec((None, n, hd), lambda b, h: (b, 0, h)),
        scratch_shapes=[pltpu.VMEM((hd, past + n), BF16), pltpu.VMEM((past + n, MXU_N), BF16),
                        pltpu.VMEM((2, 2 * ATTN_SUB, past + n), F32),
                        pltpu.VMEM((2, 2 * ATTN_SUB, MXU_N), F32)],
        compiler_params=_params(2),
        name="attn_cache",
    )(lam_params, subln_g, q, kt, v, cache_k, cache_v)


def _post_kernel(x_ref, a_ref, b_ref, mod_ref, g_ref, wo_ref, wg_ref, wu_ref, wd_ref, o_ref):
    half = a_ref.shape[1]
    y = _dot(a_ref[...], wo_ref[0:half, :]) + _dot(b_ref[...], wo_ref[half:, :])
    x1 = x_ref[...] + mod_ref[2:3, :] * _rms(y, g_ref[1:2, :])
    h = _modulate(x1, g_ref[2:3, :], mod_ref[3:4, :], mod_ref[4:5, :]).astype(BF16)
    act = (_silu(_dot(h, wg_ref[...])) * _dot(h, wu_ref[...])).astype(BF16)
    f = _dot(act, wd_ref[...])
    o_ref[...] = x1 + mod_ref[5:6, :] * _rms(f, g_ref[3:4, :])


def _post_call(x, a, b, mod, g, w_out, w_gate, w_up, w_down, tm, mod_base, mod_stride):
    bsz, n, d = x.shape
    x_spec, _, _, mod_spec = _tile_specs(n, tm, mod_base, mod_stride)
    half_spec = pl.BlockSpec((None, tm, a.shape[2]), lambda b_, j: (b_, j, 0))
    return pl.pallas_call(
        _post_kernel,
        out_shape=jax.ShapeDtypeStruct(x.shape, F32),
        grid=(bsz, n // tm),
        in_specs=[x_spec, half_spec, half_spec, mod_spec, _resident(g.shape),
                  _resident(w_out.shape), _resident(w_gate.shape), _resident(w_up.shape),
                  _resident(w_down.shape)],
        out_specs=x_spec,
        compiler_params=_params(2),
        name="post",
    )(x, a, b, mod, g, w_out, w_gate, w_up, w_down)


def _odd_in_kernel(x_ref, xp_ref, xn_ref, mod_ref, g_ref, w_ref, wp_ref, ps_ref, cs_ref,
                   pc_ref, xc_ref, xs_ref, us_ref, *, n_seq):
    j = pl.program_id(1)
    nt = pl.num_programs(1)
    tm = x_ref.shape[0]
    g = g_ref[0:1, :]
    shift = mod_ref[0:1, :]
    scale = mod_ref[1:2, :]
    h = _modulate(x_ref[...], g, shift, scale).astype(BF16)
    up = _dot(h, w_ref[:, 0:POOL_W])
    uf = _dot(h, w_ref[:, POOL_W:])
    xh = jnp.concatenate([xp_ref[...], xn_ref[...]], axis=0)
    hh = _modulate(xh, g, shift, scale).astype(BF16)
    uph = _dot(hh, w_ref[:, 0:POOL_W])
    us_ref[0:HALO, :] = jnp.where(j > 0, uph[0:HALO], 0.0)
    us_ref[HALO:HALO + tm, :] = up
    us_ref[HALO + tm:, :] = jnp.where(j < nt - 1, uph[HALO:], 0.0)

    t = j * tm + lax.broadcasted_iota(jnp.int32, (tm, 1), 0)
    for gi, win in enumerate(POOL_WINDOWS):
        lanes = slice(GROUP * gi, GROUP * (gi + 1))
        start = HALO - win // 2
        acc = us_ref[start:start + tm, lanes]
        for dlt in range(1, win):
            acc = acc + us_ref[start + dlt:start + dlt + tm, lanes]
        lo = jnp.clip(t - win // 2, 0, n_seq)
        hi = jnp.clip(t - win // 2 + win, 0, n_seq)
        cnt = (hi - lo).astype(F32)
        diff = (acc / cnt - up[:, lanes]).astype(BF16)
        pc_ref[:, lanes] = (_dot(diff, wp_ref[gi]) * ps_ref[0:1, lanes]).astype(BF16)
        cs = _dot(uf[:, lanes].astype(BF16), cs_ref[...])
        xc_ref[:, lanes] = cs[:, 0:GROUP].astype(BF16)
        xs_ref[:, lanes] = cs[:, GROUP:].astype(BF16)


def _odd_in_call(x, mod, g, w_in, w_pool, pool_scale, cs_mat, tm, mod_base, mod_stride):
    bsz, n, d = x.shape
    x_spec, prev_spec, next_spec, mod_spec = _tile_specs(n, tm, mod_base, mod_stride)
    out_spec = pl.BlockSpec((None, tm, POOL_W), lambda b, j: (b, j, 0))
    out_sds = jax.ShapeDtypeStruct((bsz, n, POOL_W), BF16)
    return pl.pallas_call(
        functools.partial(_odd_in_kernel, n_seq=n),
        out_shape=(out_sds, out_sds, out_sds),
        grid=(bsz, n // tm),
        in_specs=[x_spec, prev_spec, next_spec, mod_spec, _resident(g.shape), _resident(w_in.shape),
                  _resident(w_pool.shape), _resident(pool_scale.shape), _resident(cs_mat.shape)],
        out_specs=(out_spec, out_spec, out_spec),
        scratch_shapes=[pltpu.VMEM((tm + 2 * HALO, POOL_W), F32)],
        compiler_params=_params(2),
        name="odd_in",
    )(x, x, x, mod, g, w_in, w_pool, pool_scale, cs_mat)


def _four_kernel(c_ref, s_ref, xc_ref, xs_ref, wf_ref, o_ref, *, scale):
    y = _dot(c_ref[...], xc_ref[...]) - _dot(s_ref[...], xs_ref[...])
    four = (y * scale).astype(BF16)
    for gi in range(FOURIER_W // GROUP):
        lanes = slice(GROUP * gi, GROUP * (gi + 1))
        o_ref[:, lanes] = _dot(four[:, lanes], wf_ref[gi]).astype(BF16)


def _four_call(cn, sn, xc, xs, w_four, tm):
    bsz, n, w = xc.shape
    mat_spec = pl.BlockSpec((tm, n), lambda b, j: (j, 0))
    seq_spec = pl.BlockSpec((None, n, w), lambda b, j: (b, 0, 0))
    return pl.pallas_call(
        functools.partial(_four_kernel, scale=float(1.0 / math.sqrt(n * GROUP))),
        out_shape=jax.ShapeDtypeStruct((bsz, n, w), BF16),
        grid=(bsz, n // tm),
        in_specs=[mat_spec, mat_spec, seq_spec, seq_spec, _resident(w_four.shape)],
        out_specs=pl.BlockSpec((None, tm, w), lambda b, j: (b, j, 0)),
        compiler_params=_params(2),
        name="fourier",
    )(cn, sn, xc, xs, w_four)


def _rope_tables(n_tok):
    rows = n_tok // GRID_W
    row = np.repeat(np.arange(rows), GRID_W).astype(np.float64)
    col = np.tile(np.arange(GRID_W), rows).astype(np.float64)
    inv = ROPE_BASE ** (-np.arange(0, ROPE_AXIS, 2, dtype=np.float64) / ROPE_AXIS)
    ang_r = row[:, None] * inv[None, :]
    ang_c = col[:, None] * inv[None, :]
    ang = np.concatenate([ang_r, ang_r, ang_c, ang_c], axis=-1)
    cos = np.concatenate([np.cos(ang)] * 2, axis=-1)
    sin = np.concatenate([np.sin(ang)] * 2, axis=-1)
    first_half = (np.arange(HEAD) % 32) < 16
    sin_signed = np.where(first_half[None, :], -sin, sin)
    return jnp.asarray(cos, F32), jnp.asarray(sin_signed, F32)


def _dft_mats(n):
    idx = np.arange(n, dtype=np.int64)
    ang = 2.0 * np.pi * ((idx[:, None] * idx[None, :]) % n).astype(np.float64) / n
    return np.cos(ang), np.sin(ang)


def kernel(x_prompt, x_sample, cache_k, cache_v, c, c_ctx, w_mod, b_mod, norm_g,
           w_in_even, lam_params, subln_g, conv_w, w_out_even,
           w_in_odd, w_pool, pool_scale, w_fourier, w_out_odd,
           w_gate, w_up, w_down):
    depth = w_mod.shape[0]
    n_dec = x_sample.shape[0]
    n_p, n_s = x_prompt.shape[1], x_sample.shape[1]
    tm_p, tm_s = n_p, 512

    pad_rows = 16 - 1 - n_dec
    cc = jnp.concatenate([c_ctx[None, :], c, jnp.zeros((pad_rows, D_MODEL), F32)], axis=0)
    mod_all = _mod_call(cc, w_mod, b_mod)[:, :1 + n_dec].reshape(depth, 1 + n_dec, 6, D_MODEL)

    rope = _rope_tables(n_s)
    cc_g, sc_g = _dft_mats(GROUP)
    cs_mat = jnp.asarray(np.concatenate([cc_g, sc_g], axis=1), F32).astype(BF16)
    dft = {n: tuple(jnp.asarray(m, F32).astype(BF16) for m in _dft_mats(n)) for n in (n_p, n_s)}

    xp, xs = x_prompt, x_sample
    new_k, new_v = [], []
    for l in range(depth):
        mod = mod_all[l]
        g = norm_g[l]
        i = l // 2
        wg, wu, wd = w_gate[l].astype(BF16), w_up[l].astype(BF16), w_down[l].astype(BF16)
        streams = []
        if l % 2 == 0:
            lam_init = 0.8 - 0.6 * math.exp(-0.3 * l)
            w_in = w_in_even[i].astype(BF16)
            w_out = w_out_even[i].astype(BF16)
            sg = subln_g[i][None, :]
            qp, kp, vp, cbp = _even_in_call(xp, mod, g, w_in, conv_w[i], None, tm_p, 0, 0)
            ap = _attn_prompt_call(qp, kp, vp, lam_params[i], sg, lam_init, 8)
            new_k.append(kp)
            new_v.append(vp)
            qs, kts, vs, cbs = _even_in_call(xs, mod, g, w_in, conv_w[i], rope, tm_s, 1, 1)
            a_s = _attn_cache_call(qs, kts, vs, lam_params[i], sg, cache_k, cache_v, i, lam_init)
            streams = [(ap, cbp), (a_s, cbs)]
        else:
            w_in = w_in_odd[i].astype(BF16)
            w_out = w_out_odd[i].astype(BF16)
            wp = w_pool[i].astype(BF16)
            wf = w_fourier[i].astype(BF16)
            ps = pool_scale[i][None, :]
            pcp, xcp, xsp = _odd_in_call(xp, mod, g, w_in, wp, ps, cs_mat, tm_p, 0, 0)
            fcp = _four_call(*dft[n_p], xcp, xsp, wf, n_p)
            pcs, xcs, xss = _odd_in_call(xs, mod, g, w_in, wp, ps, cs_mat, tm_s, 1, 1)
            fcs = _four_call(*dft[n_s], xcs, xss, wf, 512)
            streams = [(pcp, fcp), (pcs, fcs)]
        xp = _post_call(xp, streams[0][0], streams[0][1], mod, g, w_out, wg, wu, wd, tm_p, 0, 0)
        xs = _post_call(xs, streams[1][0], streams[1][1], mod, g, w_out, wg, wu, wd, 512, 1, 1)
    return xp, xs, jnp.stack(new_k, axis=1), jnp.stack(new_v, axis=1)
```

```python
import functools
import math

import numpy as np
import jax
import jax.numpy as jnp
from jax import lax
from jax.experimental import pallas as pl
from jax.experimental.pallas import tpu as pltpu

F32 = jnp.float32
BF16 = jnp.bfloat16

D_MODEL = 1024
GRID_W = 64
N_HEADS = 4
HEAD = 128
HALF_HEAD = 64
ROPE_AXIS = 32
ROPE_BASE = 10000.0
ATTN_W = 512
CONV_W = 512
POOL_W = 512
FOURIER_W = 512
GROUP = 128
POOL_WINDOWS = (2, 4, 8, 16)
D_FF = 2816
EPS = 1e-6
LOG2E = math.log2(math.e)
HALO = 8
MXU_N = 256
ATTN_SUB = 128
VMEM_LIMIT = 56 * 1024 * 1024


def _params(n_axes):
    return pltpu.CompilerParams(dimension_semantics=("arbitrary",) * n_axes,
                                vmem_limit_bytes=VMEM_LIMIT)


def _resident(shape):
    return pl.BlockSpec(shape, lambda *_: (0,) * len(shape), pipeline_mode=pl.Buffered(1))


def _rms(x, g):
    ms = jnp.mean(x * x, axis=-1, keepdims=True)
    return x * lax.rsqrt(ms + EPS) * g


def _modulate(x, g, shift, scale):
    return _rms(x, g) * (1.0 + scale) + shift


def _dot(a, b):
    return jnp.dot(a, b, preferred_element_type=F32)


def _silu(x):
    return x / (1.0 + jnp.exp(-x))


def _mod_kernel(cc_ref, w_ref, b_ref, o_ref):
    s = _silu(cc_ref[...]).astype(BF16)
    o_ref[...] = _dot(s, w_ref[...].astype(BF16)) + b_ref[...]


def _mod_call(cc, w_mod, b_mod):
    depth, d, n6 = w_mod.shape
    rows = cc.shape[0]
    tn = 2048
    return pl.pallas_call(
        _mod_kernel,
        out_shape=jax.ShapeDtypeStruct((depth, rows, n6), F32),
        grid=(depth, n6 // tn),
        in_specs=[
            pl.BlockSpec((rows, d), lambda l, j: (0, 0)),
            pl.BlockSpec((None, d, tn), lambda l, j: (l, 0, j)),
            pl.BlockSpec((None, 1, tn), lambda l, j: (l, 0, j)),
        ],
        out_specs=pl.BlockSpec((None, rows, tn), lambda l, j: (l, 0, j)),
        compiler_params=_params(2),
        name="mod",
    )(cc, w_mod, b_mod.reshape(depth, 1, n6))


def _tile_specs(n, tm, mod_base, mod_stride):
    nb8 = n // HALO
    t8 = tm // HALO
    x_spec = pl.BlockSpec((None, tm, D_MODEL), lambda b, j: (b, j, 0))
    prev_spec = pl.BlockSpec((None, HALO, D_MODEL),
                             lambda b, j: (b, jnp.maximum(j * t8 - 1, 0), 0))
    next_spec = pl.BlockSpec((None, HALO, D_MODEL),
                             lambda b, j: (b, jnp.minimum((j + 1) * t8, nb8 - 1), 0))
    mod_spec = pl.BlockSpec((None, 6, D_MODEL),
                            lambda b, j: (mod_base + mod_stride * b, 0, 0))
    return x_spec, prev_spec, next_spec, mod_spec


def _rope(t, cos, sin_signed, first_half):
    outs = []
    for hh in range(N_HEADS):
        th = t[:, HEAD * hh:HEAD * (hh + 1)]
        swapped = jnp.where(first_half, pltpu.roll(th, HEAD - 16, 1), pltpu.roll(th, 16, 1))
        outs.append(th * cos + swapped * sin_signed)
    return outs


def _even_in_kernel(*refs, use_rope):
    if use_rope:
        (x_ref, xp_ref, xn_ref, mod_ref, g_ref, w_ref, cw_ref, cos_ref, sin_ref,
         q_ref, k_ref, v_ref, cb_ref, zs_ref) = refs
    else:
        (x_ref, xp_ref, xn_ref, mod_ref, g_ref, w_ref, cw_ref,
         q_ref, k_ref, v_ref, cb_ref, zs_ref) = refs
    j = pl.program_id(1)
    nt = pl.num_programs(1)
    tm = x_ref.shape[0]
    g = g_ref[0:1, :]
    shift = mod_ref[0:1, :]
    scale = mod_ref[1:2, :]
    h = _modulate(x_ref[...], g, shift, scale).astype(BF16)

    def proj(hh, lo):
        return _dot(hh, w_ref[:, lo:lo + 512])

    q = proj(h, 0) * (HALF_HEAD ** -0.5 * LOG2E)
    k = proj(h, 512)
    v = proj(h, 1024)
    if use_rope:
        lane = lax.broadcasted_iota(jnp.int32, (1, HEAD), 1)
        first_half = (lane % 32) < 16
        cos = cos_ref[...]
        sin_signed = sin_ref[...]
        qs = _rope(q, cos, sin_signed, first_half)
        ks = [kh.T for kh in _rope(k, cos, sin_signed, first_half)]
    else:
        qs = [q[:, HEAD * hh:HEAD * (hh + 1)] for hh in range(N_HEADS)]
        ks = [k[:, HEAD * hh:HEAD * (hh + 1)] for hh in range(N_HEADS)]
    for hh in range(N_HEADS):
        q_ref[hh] = qs[hh].astype(q_ref.dtype)
        k_ref[hh] = ks[hh].astype(k_ref.dtype)
        v_ref[hh] = v[:, HEAD * hh:HEAD * (hh + 1)].astype(v_ref.dtype)

    gate_b = proj(h, 1536)
    z = proj(h, 2048) * proj(h, 2560)
    xh = jnp.concatenate([xp_ref[...], xn_ref[...]], axis=0)
    hh_ = _modulate(xh, g, shift, scale).astype(BF16)
    zh = proj(hh_, 2048) * proj(hh_, 2560)
    zs_ref[0:HALO, :] = jnp.where(j > 0, zh[0:HALO], 0.0)
    zs_ref[HALO:HALO + tm, :] = z
    zs_ref[HALO + tm:, :] = jnp.where(j < nt - 1, zh[HALO:], 0.0)
    conv = (cw_ref[0:1, :] * zs_ref[HALO - 1:HALO - 1 + tm, :] + cw_ref[1:2, :] * z
            + cw_ref[2:3, :] * zs_ref[HALO + 1:HALO + 1 + tm, :])
    cb_ref[...] = (gate_b * conv).astype(BF16)


def _even_in_call(x, mod, g, w_in, conv_w, rope, tm, mod_base, mod_stride):
    bsz, n, d = x.shape
    use_rope = rope is not None
    x_spec, prev_spec, next_spec, mod_spec = _tile_specs(n, tm, mod_base, mod_stride)
    in_specs = [x_spec, prev_spec, next_spec, mod_spec,
                _resident(g.shape), _resident(w_in.shape), _resident(conv_w.shape)]
    args = [x, x, x, mod, g, w_in, conv_w]
    if use_rope:
        tab = pl.BlockSpec((tm, HEAD), lambda b, j: (j, 0))
        in_specs += [tab, tab]
        args += list(rope)
    head_spec = pl.BlockSpec((None, N_HEADS, tm, HEAD), lambda b, j: (b, 0, j, 0))
    head_shape = (bsz, N_HEADS, n, HEAD)
    if use_rope:
        k_spec = pl.BlockSpec((None, N_HEADS, HEAD, tm), lambda b, j: (b, 0, 0, j))
        k_sds = jax.ShapeDtypeStruct((bsz, N_HEADS, HEAD, n), BF16)
        v_sds = jax.ShapeDtypeStruct(head_shape, BF16)
    else:
        k_spec = head_spec
        k_sds = v_sds = jax.ShapeDtypeStruct(head_shape, F32)
    return pl.pallas_call(
        functools.partial(_even_in_kernel, use_rope=use_rope),
        out_shape=(jax.ShapeDtypeStruct(head_shape, BF16), k_sds, v_sds,
                   jax.ShapeDtypeStruct((bsz, n, CONV_W), BF16)),
        grid=(bsz, n // tm),
        in_specs=in_specs,
        out_specs=(head_spec, k_spec, head_spec,
                   pl.BlockSpec((None, tm, CONV_W), lambda b, j: (b, j, 0))),
        scratch_shapes=[pltpu.VMEM((tm + 2 * HALO, CONV_W), F32)],
        compiler_params=_params(2),
        name="even_in_rope" if use_rope else "even_in",
    )(*args)


def _diff_lambda(lam_ref, lam_init):
    lp = lam_ref[...]
    return (jnp.exp(jnp.sum(lp[0:1] * lp[1:2], axis=-1, keepdims=True))
            - jnp.exp(jnp.sum(lp[2:3] * lp[3:4], axis=-1, keepdims=True)) + lam_init)


def _stack_components(q):
    lane = lax.broadcasted_iota(jnp.int32, (1, HEAD), 1)
    zero = jnp.zeros_like(q)
    return jnp.concatenate([jnp.where(lane < HALF_HEAD, q, zero),
                            jnp.where(lane >= HALF_HEAD, q, zero)], axis=0)


def _softmax_pv(s, v_ext):
    e = jnp.exp2(s - jnp.max(s, axis=-1, keepdims=True)).astype(BF16)
    return _dot(e, v_ext)


def _normalise(ov, lam, sg, lam_init):
    t = ov.shape[0] // 2
    o = ov[:t, :HEAD] / ov[:t, HEAD:] - lam * (ov[t:, :HEAD] / ov[t:, HEAD:])
    return (_rms(o, sg) * (1.0 - lam_init)).astype(BF16)


def _chain_pipeline(n_groups, width, scores_fn, pv_fn, finish_fn, s_ref, ov_ref):
    assert n_groups % 2 == 0 and n_groups >= 2

    def scores(g, par):
        for u in range(width):
            s_ref[par * width + u] = scores_fn(g, u)

    def values(g, par):
        for u in range(width):
            ov_ref[par * width + u] = pv_fn(g, u, s_ref[par * width + u])

    def finish(g, par):
        for u in range(width):
            finish_fn(g, u, ov_ref[par * width + u])

    scores(0, 0)
    scores(1, 1)
    values(0, 0)

    def body(t, carry):
        g = 2 * t
        scores(g, 0)
        values(g - 1, 1)
        finish(g - 2, 0)
        scores(g + 1, 1)
        values(g, 0)
        finish(g - 1, 1)
        return carry

    lax.fori_loop(1, n_groups // 2, body, 0)
    values(n_groups - 1, 1)
    finish(n_groups - 2, 0)
    finish(n_groups - 1, 1)


def _attn_prompt_kernel(lam_ref, sg_ref, q_ref, k_ref, v_ref, o_ref, s_ref, ov_ref, *, lam_init):
    lam = _diff_lambda(lam_ref, lam_init)
    sg = sg_ref[...]
    n = k_ref.shape[2]
    ones = jnp.ones((n, MXU_N - HEAD), BF16)

    def scores_fn(b, hh):
        return lax.dot_general(_stack_components(q_ref[b, hh]), k_ref[b, hh].astype(BF16),
                               (((1,), (1,)), ((), ())), preferred_element_type=F32)

    def pv_fn(b, hh, s):
        return _softmax_pv(s, jnp.concatenate([v_ref[b, hh].astype(BF16), ones], axis=1))

    def finish_fn(b, hh, ov):
        o_ref[b, :, HEAD * hh:HEAD * (hh + 1)] = _normalise(ov, lam, sg, lam_init)

    _chain_pipeline(q_ref.shape[0], N_HEADS, scores_fn, pv_fn, finish_fn, s_ref, ov_ref)


def _attn_prompt_call(q, k, v, lam_params, subln_g, lam_init, nb):
    bsz, nh, n, hd = q.shape
    spec = pl.BlockSpec((nb, nh, n, hd), lambda b: (b, 0, 0, 0))
    return pl.pallas_call(
        functools.partial(_attn_prompt_kernel, lam_init=lam_init),
        out_shape=jax.ShapeDtypeStruct((bsz, n, nh * hd), BF16),
        grid=(bsz // nb,),
        in_specs=[_resident(lam_params.shape), _resident(subln_g.shape), spec, spec, spec],
        out_specs=pl.BlockSpec((nb, n, nh * hd), lambda b: (b, 0, 0)),
        scratch_shapes=[pltpu.VMEM((2 * nh, 2 * n, n), F32), pltpu.VMEM((2 * nh, 2 * n, MXU_N), F32)],
        compiler_params=_params(1),
        name="attn",
    )(lam_params, subln_g, q, k, v)


def _attn_cache_kernel(lam_ref, sg_ref, q_ref, kt_ref, v_ref, ck_ref, cv_ref, o_ref,
                       ktbuf_ref, vbuf_ref, s_ref, ov_ref, *, lam_init):
    past = ck_ref.shape[0]
    ktbuf_ref[:, 0:past] = ck_ref[...].T.astype(BF16)
    ktbuf_ref[:, past:] = kt_ref[...]
    vbuf_ref[0:past, 0:HEAD] = cv_ref[...].astype(BF16)
    vbuf_ref[past:, 0:HEAD] = v_ref[...]
    vbuf_ref[:, HEAD:] = jnp.ones((vbuf_ref.shape[0], MXU_N - HEAD), BF16)
    lam = _diff_lambda(lam_ref, lam_init)
    sg = sg_ref[...]

    def rows(g):
        return pl.ds(pl.multiple_of(g * ATTN_SUB, ATTN_SUB), ATTN_SUB)

    def scores_fn(g, u):
        return _dot(_stack_components(q_ref[rows(g), :]), ktbuf_ref[...])

    def pv_fn(g, u, s):
        return _softmax_pv(s, vbuf_ref[...])

    def finish_fn(g, u, ov):
        o_ref[rows(g), :] = _normalise(ov, lam, sg, lam_init)

    _chain_pipeline(q_ref.shape[0] // ATTN_SUB, 1, scores_fn, pv_fn, finish_fn, s_ref, ov_ref)


def _attn_cache_call(q, kt, v, lam_params, subln_g, cache_k, cache_v, layer, lam_init):
    bsz, nh, n, hd = q.shape
    past = cache_k.shape[3]
    seq_spec = pl.BlockSpec((None, None, n, hd), lambda b, h: (b, h, 0, 0))
    kt_spec = pl.BlockSpec((None, None, hd, n), lambda b, h: (b, h, 0, 0))
    c_spec = pl.BlockSpec((None, None, None, past, hd), lambda b, h: (b, layer, h, 0, 0))
    return pl.pallas_call(
        functools.partial(_attn_cache_kernel, lam_init=lam_init),
        out_shape=jax.ShapeDtypeStruct((bsz, n, nh * hd), BF16),
        grid=(bsz, nh),
        in_specs=[_resident(lam_params.shape), _resident(subln_g.shape),
                  seq_spec, kt_spec, seq_spec, c_spec, c_spec],
        out_specs=pl.BlockSpec((None, n, hd), lambda b, h: (b, 0, h)),
        scratch_shapes=[pltpu.VMEM((hd, past + n), BF16), pltpu.VMEM((past + n, MXU_N), BF16),
                        pltpu.VMEM((2, 2 * ATTN_SUB, past + n), F32),
                        pltpu.VMEM((2, 2 * ATTN_SUB, MXU_N), F32)],
        compiler_params=_params(2),
        name="attn_cache",
    )(lam_params, subln_g, q, kt, v, cache_k, cache_v)


def _post_kernel(x_ref, a_ref, b_ref, mod_ref, g_ref, wo_ref, wg_ref, wu_ref, wd_ref, o_ref):
    half = a_ref.shape[1]
    y = _dot(a_ref[...], wo_ref[0:half, :]) + _dot(b_ref[...], wo_ref[half:, :])
    x1 = x_ref[...] + mod_ref[2:3, :] * _rms(y, g_ref[1:2, :])
    h = _modulate(x1, g_ref[2:3, :], mod_ref[3:4, :], mod_ref[4:5, :]).astype(BF16)
    act = (_silu(_dot(h, wg_ref[...])) * _dot(h, wu_ref[...])).astype(BF16)
    f = _dot(act, wd_ref[...])
    o_ref[...] = x1 + mod_ref[5:6, :] * _rms(f, g_ref[3:4, :])


def _post_call(x, a, b, mod, g, w_out, w_gate, w_up, w_down, tm, mod_base, mod_stride):
    bsz, n, d = x.shape
    x_spec, _, _, mod_spec = _tile_specs(n, tm, mod_base, mod_stride)
    half_spec = pl.BlockSpec((None, tm, a.shape[2]), lambda b_, j: (b_, j, 0))
    return pl.pallas_call(
        _post_kernel,
        out_shape=jax.ShapeDtypeStruct(x.shape, F32),
        grid=(bsz, n // tm),
        in_specs=[x_spec, half_spec, half_spec, mod_spec, _resident(g.shape),
                  _resident(w_out.shape), _resident(w_gate.shape), _resident(w_up.shape),
                  _resident(w_down.shape)],
        out_specs=x_spec,
        compiler_params=_params(2),
        name="post",
    )(x, a, b, mod, g, w_out, w_gate, w_up, w_down)


def _odd_in_kernel(x_ref, xp_ref, xn_ref, mod_ref, g_ref, w_ref, wp_ref, ps_ref, cs_ref,
                   pc_ref, xc_ref, xs_ref, us_ref, f2_ref, f4_ref, *, n_seq):
    j = pl.program_id(1)
    nt = pl.num_programs(1)
    tm = x_ref.shape[0]
    g = g_ref[0:1, :]
    shift = mod_ref[0:1, :]
    scale = mod_ref[1:2, :]
    h = _modulate(x_ref[...], g, shift, scale).astype(BF16)
    up = _dot(h, w_ref[:, 0:POOL_W])
    uf = _dot(h, w_ref[:, POOL_W:])
    xh = jnp.concatenate([xp_ref[...], xn_ref[...]], axis=0)
    hh = _modulate(xh, g, shift, scale).astype(BF16)
    uph = _dot(hh, w_ref[:, 0:POOL_W])
    rows = tm + 2 * HALO
    us_ref[0:HALO, :] = jnp.where(j > 0, uph[0:HALO], 0.0)
    us_ref[HALO:HALO + tm, :] = up
    us_ref[HALO + tm:rows, :] = jnp.where(j < nt - 1, uph[HALO:], 0.0)
    us_ref[rows:, :] = jnp.zeros((HALO, POOL_W), F32)
    f2_ref[0:rows, :] = us_ref[0:rows, GROUP:] + us_ref[1:rows + 1, GROUP:]
    f2_ref[rows:, :] = jnp.zeros((HALO, POOL_W - GROUP), F32)
    f4_ref[0:rows, :] = f2_ref[0:rows, GROUP:] + f2_ref[2:rows + 2, GROUP:]
    f4_ref[rows:, :] = jnp.zeros((HALO, POOL_W - 2 * GROUP), F32)
    f8 = f4_ref[0:rows, GROUP:] + f4_ref[4:rows + 4, GROUP:]
    sums = (us_ref[HALO - 1:HALO - 1 + tm, 0:GROUP] + up[:, 0:GROUP],
            f2_ref[HALO - 2:HALO - 2 + tm, 0:GROUP] + f2_ref[HALO:HALO + tm, 0:GROUP],
            f4_ref[HALO - 4:HALO - 4 + tm, 0:GROUP] + f4_ref[HALO:HALO + tm, 0:GROUP],
            f8[0:tm] + f8[HALO:HALO + tm])

    t = (j * tm + lax.broadcasted_iota(jnp.int32, (tm, 1), 0)).astype(F32)
    for gi, win in enumerate(POOL_WINDOWS):
        lanes = slice(GROUP * gi, GROUP * (gi + 1))
        cnt = jnp.minimum(t + float(win // 2), float(n_seq)) - jnp.maximum(t - float(win // 2), 0.0)
        diff = (sums[gi] / cnt - up[:, lanes]).astype(BF16)
        pc_ref[:, lanes] = (_dot(diff, wp_ref[gi]) * ps_ref[0:1, lanes]).astype(BF16)
        cs = _dot(uf[:, lanes].astype(BF16), cs_ref[...])
        xc_ref[:, lanes] = cs[:, 0:GROUP].astype(BF16)
        xs_ref[:, lanes] = cs[:, GROUP:].astype(BF16)


def _odd_in_call(x, mod, g, w_in, w_pool, pool_scale, cs_mat, tm, mod_base, mod_stride):
    bsz, n, d = x.shape
    x_spec, prev_spec, next_spec, mod_spec = _tile_specs(n, tm, mod_base, mod_stride)
    out_spec = pl.BlockSpec((None, tm, POOL_W), lambda b, j: (b, j, 0))
    out_sds = jax.ShapeDtypeStruct((bsz, n, POOL_W), BF16)
    return pl.pallas_call(
        functools.partial(_odd_in_kernel, n_seq=n),
        out_shape=(out_sds, out_sds, out_sds),
        grid=(bsz, n // tm),
        in_specs=[x_spec, prev_spec, next_spec, mod_spec, _resident(g.shape), _resident(w_in.shape),
                  _resident(w_pool.shape), _resident(pool_scale.shape), _resident(cs_mat.shape)],
        out_specs=(out_spec, out_spec, out_spec),
        scratch_shapes=[pltpu.VMEM((tm + 3 * HALO, POOL_W), F32),
                        pltpu.VMEM((tm + 3 * HALO, POOL_W - GROUP), F32),
                        pltpu.VMEM((tm + 3 * HALO, POOL_W - 2 * GROUP), F32)],
        compiler_params=_params(2),
        name="odd_in",
    )(x, x, x, mod, g, w_in, w_pool, pool_scale, cs_mat)


def _four_kernel(c_ref, s_ref, xc_ref, xs_ref, wf_ref, o_ref, *, scale):
    y = _dot(c_ref[...], xc_ref[...]) - _dot(s_ref[...], xs_ref[...])
    four = (y * scale).astype(BF16)
    for gi in range(FOURIER_W // GROUP):
        lanes = slice(GROUP * gi, GROUP * (gi + 1))
        o_ref[:, lanes] = _dot(four[:, lanes], wf_ref[gi]).astype(BF16)


def _four_call(cn, sn, xc, xs, w_four, tm):
    bsz, n, w = xc.shape
    mat_spec = pl.BlockSpec((tm, n), lambda b, j: (j, 0))
    seq_spec = pl.BlockSpec((None, n, w), lambda b, j: (b, 0, 0))
    return pl.pallas_call(
        functools.partial(_four_kernel, scale=float(1.0 / math.sqrt(n * GROUP))),
        out_shape=jax.ShapeDtypeStruct((bsz, n, w), BF16),
        grid=(bsz, n // tm),
        in_specs=[mat_spec, mat_spec, seq_spec, seq_spec, _resident(w_four.shape)],
        out_specs=pl.BlockSpec((None, tm, w), lambda b, j: (b, j, 0)),
        compiler_params=_params(2),
        name="fourier",
    )(cn, sn, xc, xs, w_four)


def _rope_tables(n_tok):
    rows = n_tok // GRID_W
    row = np.repeat(np.arange(rows), GRID_W).astype(np.float64)
    col = np.tile(np.arange(GRID_W), rows).astype(np.float64)
    inv = ROPE_BASE ** (-np.arange(0, ROPE_AXIS, 2, dtype=np.float64) / ROPE_AXIS)
    ang_r = row[:, None] * inv[None, :]
    ang_c = col[:, None] * inv[None, :]
    ang = np.concatenate([ang_r, ang_r, ang_c, ang_c], axis=-1)
    cos = np.concatenate([np.cos(ang)] * 2, axis=-1)
    sin = np.concatenate([np.sin(ang)] * 2, axis=-1)
    first_half = (np.arange(HEAD) % 32) < 16
    sin_signed = np.where(first_half[None, :], -sin, sin)
    return jnp.asarray(cos, F32), jnp.asarray(sin_signed, F32)


def _dft_mats(n):
    idx = np.arange(n, dtype=np.int64)
    ang = 2.0 * np.pi * ((idx[:, None] * idx[None, :]) % n).astype(np.float64) / n
    return np.cos(ang), np.sin(ang)


def kernel(x_prompt, x_sample, cache_k, cache_v, c, c_ctx, w_mod, b_mod, norm_g,
           w_in_even, lam_params, subln_g, conv_w, w_out_even,
           w_in_odd, w_pool, pool_scale, w_fourier, w_out_odd,
           w_gate, w_up, w_down):
    depth = w_mod.shape[0]
    n_dec = x_sample.shape[0]
    n_p, n_s = x_prompt.shape[1], x_sample.shape[1]
    tm_p, tm_s = n_p, 512

    pad_rows = 16 - 1 - n_dec
    cc = jnp.concatenate([c_ctx[None, :], c, jnp.zeros((pad_rows, D_MODEL), F32)], axis=0)
    mod_all = _mod_call(cc, w_mod, b_mod)[:, :1 + n_dec].reshape(depth, 1 + n_dec, 6, D_MODEL)

    rope = _rope_tables(n_s)
    cc_g, sc_g = _dft_mats(GROUP)
    cs_mat = jnp.asarray(np.concatenate([cc_g, sc_g], axis=1), F32).astype(BF16)
    dft = {n: tuple(jnp.asarray(m, F32).astype(BF16) for m in _dft_mats(n)) for n in (n_p, n_s)}

    xp, xs = x_prompt, x_sample
    new_k, new_v = [], []
    for l in range(depth):
        mod = mod_all[l]
        g = norm_g[l]
        i = l // 2
        wg, wu, wd = w_gate[l].astype(BF16), w_up[l].astype(BF16), w_down[l].astype(BF16)
        streams = []
        if l % 2 == 0:
            lam_init = 0.8 - 0.6 * math.exp(-0.3 * l)
            w_in = w_in_even[i].astype(BF16)
            w_out = w_out_even[i].astype(BF16)
            sg = subln_g[i][None, :]
            qp, kp, vp, cbp = _even_in_call(xp, mod, g, w_in, conv_w[i], None, tm_p, 0, 0)
            ap = _attn_prompt_call(qp, kp, vp, lam_params[i], sg, lam_init, 8)
            new_k.append(kp)
            new_v.append(vp)
            qs, kts, vs, cbs = _even_in_call(xs, mod, g, w_in, conv_w[i], rope, tm_s, 1, 1)
            a_s = _attn_cache_call(qs, kts, vs, lam_params[i], sg, cache_k, cache_v, i, lam_init)
            streams = [(ap, cbp), (a_s, cbs)]
        else:
            w_in = w_in_odd[i].astype(BF16)
            w_out = w_out_odd[i].astype(BF16)
            wp = w_pool[i].astype(BF16)
            wf = w_fourier[i].astype(BF16)
            ps = pool_scale[i][None, :]
            pcp, xcp, xsp = _odd_in_call(xp, mod, g, w_in, wp, ps, cs_mat, tm_p, 0, 0)
            fcp = _four_call(*dft[n_p], xcp, xsp, wf, n_p)
            pcs, xcs, xss = _odd_in_call(xs, mod, g, w_in, wp, ps, cs_mat, tm_s, 1, 1)
            fcs = _four_call(*dft[n_s], xcs, xss, wf, 512)
            streams = [(pcp, fcp), (pcs, fcs)]
        xp = _post_call(*(t.reshape(1, -1, t.shape[-1]) for t in (xp,) + streams[0]),
                        mod, g, w_out, wg, wu, wd, 512, 0, 0).reshape(x_prompt.shape)
        xs = _post_call(xs, streams[1][0], streams[1][1], mod, g, w_out, wg, wu, wd, 512, 1, 1)
    def stack_layers(parts):
        if len(parts) == 1:
            return parts[0][:, None]
        return jnp.stack(parts, axis=1)

    return xp, xs, stack_layers(new_k), stack_layers(new_v)
```

```python
import functools
import math

import numpy as np
import jax
import jax.numpy as jnp
from jax import lax
from jax.experimental import pallas as pl
from jax.experimental.pallas import tpu as pltpu

F32 = jnp.float32
BF16 = jnp.bfloat16

D_MODEL = 1024
GRID_W = 64
N_HEADS = 4
HEAD = 128
HALF_HEAD = 64
ROPE_AXIS = 32
ROPE_BASE = 10000.0
ATTN_W = 512
CONV_W = 512
POOL_W = 512
FOURIER_W = 512
GROUP = 128
POOL_WINDOWS = (2, 4, 8, 16)
D_FF = 2816
EPS = 1e-6
LOG2E = math.log2(math.e)
HALO = 8
MXU_N = 256
ATTN_SUB = 128
VMEM_LIMIT = 56 * 1024 * 1024


def _params(n_axes):
    return pltpu.CompilerParams(dimension_semantics=("arbitrary",) * n_axes,
                                vmem_limit_bytes=VMEM_LIMIT)


def _resident(shape):
    return pl.BlockSpec(shape, lambda *_: (0,) * len(shape), pipeline_mode=pl.Buffered(1))


def _rms(x, g):
    ms = jnp.mean(x * x, axis=-1, keepdims=True)
    return x * lax.rsqrt(ms + EPS) * g


def _modulate(x, g, shift, scale):
    return _rms(x, g) * (1.0 + scale) + shift


def _dot(a, b):
    return jnp.dot(a, b, preferred_element_type=F32)


def _silu(x):
    return x / (1.0 + jnp.exp(-x))


def _mod_kernel(cc_ref, w_ref, b_ref, o_ref):
    s = _silu(cc_ref[...]).astype(BF16)
    o_ref[...] = _dot(s, w_ref[...].astype(BF16)) + b_ref[...]


def _mod_call(cc, w_mod, b_mod):
    depth, d, n6 = w_mod.shape
    rows = cc.shape[0]
    tn = 2048
    return pl.pallas_call(
        _mod_kernel,
        out_shape=jax.ShapeDtypeStruct((depth, rows, n6), F32),
        grid=(depth, n6 // tn),
        in_specs=[
            pl.BlockSpec((rows, d), lambda l, j: (0, 0)),
            pl.BlockSpec((None, d, tn), lambda l, j: (l, 0, j)),
            pl.BlockSpec((None, 1, tn), lambda l, j: (l, 0, j)),
        ],
        out_specs=pl.BlockSpec((None, rows, tn), lambda l, j: (l, 0, j)),
        compiler_params=_params(2),
        name="mod",
    )(cc, w_mod, b_mod.reshape(depth, 1, n6))


def _tile_specs(n, tm, mod_base, mod_stride):
    nb8 = n // HALO
    t8 = tm // HALO
    x_spec = pl.BlockSpec((None, tm, D_MODEL), lambda b, j: (b, j, 0))
    prev_spec = pl.BlockSpec((None, HALO, D_MODEL),
                             lambda b, j: (b, jnp.maximum(j * t8 - 1, 0), 0))
    next_spec = pl.BlockSpec((None, HALO, D_MODEL),
                             lambda b, j: (b, jnp.minimum((j + 1) * t8, nb8 - 1), 0))
    mod_spec = pl.BlockSpec((None, 6, D_MODEL),
                            lambda b, j: (mod_base + mod_stride * b, 0, 0))
    return x_spec, prev_spec, next_spec, mod_spec


def _rope(t, cos, sin_signed, first_half):
    outs = []
    for hh in range(N_HEADS):
        th = t[:, HEAD * hh:HEAD * (hh + 1)]
        swapped = jnp.where(first_half, pltpu.roll(th, HEAD - 16, 1), pltpu.roll(th, 16, 1))
        outs.append(th * cos + swapped * sin_signed)
    return outs


def _even_in_kernel(*refs, use_rope):
    if use_rope:
        (x_ref, xp_ref, xn_ref, mod_ref, g_ref, w_ref, cw_ref, cos_ref, sin_ref,
         q_ref, k_ref, v_ref, cb_ref, zs_ref) = refs
    else:
        (x_ref, xp_ref, xn_ref, mod_ref, g_ref, w_ref, cw_ref,
         q_ref, k_ref, v_ref, cb_ref, zs_ref) = refs
    j = pl.program_id(1)
    nt = pl.num_programs(1)
    tm = x_ref.shape[0]
    g = g_ref[0:1, :]
    shift = mod_ref[0:1, :]
    scale = mod_ref[1:2, :]
    h = _modulate(x_ref[...], g, shift, scale).astype(BF16)

    def proj(hh, lo):
        return _dot(hh, w_ref[:, lo:lo + 512])

    q = proj(h, 0) * (HALF_HEAD ** -0.5 * LOG2E)
    k = proj(h, 512)
    v = proj(h, 1024)
    if use_rope:
        lane = lax.broadcasted_iota(jnp.int32, (1, HEAD), 1)
        first_half = (lane % 32) < 16
        cos = cos_ref[...]
        sin_signed = sin_ref[...]
        qs = _rope(q, cos, sin_signed, first_half)
        ks = _rope(k, cos, sin_signed, first_half)
    else:
        qs = [q[:, HEAD * hh:HEAD * (hh + 1)] for hh in range(N_HEADS)]
        ks = [k[:, HEAD * hh:HEAD * (hh + 1)] for hh in range(N_HEADS)]
    for hh in range(N_HEADS):
        vh = v[:, HEAD * hh:HEAD * (hh + 1)]
        q_ref[hh] = qs[hh].astype(q_ref.dtype)
        k_ref[hh] = ks[hh].astype(k_ref.dtype)
        v_ref[hh] = (vh.T if use_rope else vh).astype(v_ref.dtype)

    gate_b = proj(h, 1536)
    z = proj(h, 2048) * proj(h, 2560)
    xh = jnp.concatenate([xp_ref[...], xn_ref[...]], axis=0)
    hh_ = _modulate(xh, g, shift, scale).astype(BF16)
    zh = proj(hh_, 2048) * proj(hh_, 2560)
    zs_ref[0:HALO, :] = jnp.where(j > 0, zh[0:HALO], 0.0)
    zs_ref[HALO:HALO + tm, :] = z
    zs_ref[HALO + tm:, :] = jnp.where(j < nt - 1, zh[HALO:], 0.0)
    conv = (cw_ref[0:1, :] * zs_ref[HALO - 1:HALO - 1 + tm, :] + cw_ref[1:2, :] * z
            + cw_ref[2:3, :] * zs_ref[HALO + 1:HALO + 1 + tm, :])
    cb_ref[...] = (gate_b * conv).astype(BF16)


def _even_in_call(x, mod, g, w_in, conv_w, rope, tm, mod_base, mod_stride):
    bsz, n, d = x.shape
    use_rope = rope is not None
    x_spec, prev_spec, next_spec, mod_spec = _tile_specs(n, tm, mod_base, mod_stride)
    in_specs = [x_spec, prev_spec, next_spec, mod_spec,
                _resident(g.shape), _resident(w_in.shape), _resident(conv_w.shape)]
    args = [x, x, x, mod, g, w_in, conv_w]
    if use_rope:
        tab = pl.BlockSpec((tm, HEAD), lambda b, j: (j, 0))
        in_specs += [tab, tab]
        args += list(rope)
    head_spec = pl.BlockSpec((None, N_HEADS, tm, HEAD), lambda b, j: (b, 0, j, 0))
    head_shape = (bsz, N_HEADS, n, HEAD)
    if use_rope:
        v_spec = pl.BlockSpec((None, N_HEADS, HEAD, tm), lambda b, j: (b, 0, 0, j))
        k_sds = jax.ShapeDtypeStruct(head_shape, BF16)
        v_sds = jax.ShapeDtypeStruct((bsz, N_HEADS, HEAD, n), BF16)
    else:
        v_spec = head_spec
        k_sds = v_sds = jax.ShapeDtypeStruct(head_shape, F32)
    return pl.pallas_call(
        functools.partial(_even_in_kernel, use_rope=use_rope),
        out_shape=(jax.ShapeDtypeStruct(head_shape, BF16), k_sds, v_sds,
                   jax.ShapeDtypeStruct((bsz, n, CONV_W), BF16)),
        grid=(bsz, n // tm),
        in_specs=in_specs,
        out_specs=(head_spec, head_spec, v_spec,
                   pl.BlockSpec((None, tm, CONV_W), lambda b, j: (b, j, 0))),
        scratch_shapes=[pltpu.VMEM((tm + 2 * HALO, CONV_W), F32)],
        compiler_params=_params(2),
        name="even_in_rope" if use_rope else "even_in",
    )(*args)


def _diff_lambda(lam_ref, lam_init):
    lp = lam_ref[...]
    return (jnp.exp(jnp.sum(lp[0:1] * lp[1:2], axis=-1, keepdims=True))
            - jnp.exp(jnp.sum(lp[2:3] * lp[3:4], axis=-1, keepdims=True)) + lam_init)


def _stack_components(q):
    lane = lax.broadcasted_iota(jnp.int32, (1, HEAD), 1)
    zero = jnp.zeros_like(q)
    return jnp.concatenate([jnp.where(lane < HALF_HEAD, q, zero),
                            jnp.where(lane >= HALF_HEAD, q, zero)], axis=0)


def _softmax_pv(s, v_ext):
    e = jnp.exp2(s - jnp.max(s, axis=-1, keepdims=True)).astype(BF16)
    return _dot(e, v_ext)


def _normalise(ov, lam, sg, lam_init):
    t = ov.shape[0] // 2
    o = ov[:t, :HEAD] / ov[:t, HEAD:] - lam * (ov[t:, :HEAD] / ov[t:, HEAD:])
    return (_rms(o, sg) * (1.0 - lam_init)).astype(BF16)


def _chain_pipeline(n_groups, width, scores_fn, pv_fn, finish_fn, s_ref, ov_ref):
    assert n_groups % 2 == 0 and n_groups >= 2

    def scores(g, par):
        for u in range(width):
            s_ref[par * width + u] = scores_fn(g, u)

    def values(g, par):
        for u in range(width):
            ov_ref[par * width + u] = pv_fn(g, u, s_ref[par * width + u])

    def finish(g, par):
        for u in range(width):
            finish_fn(g, u, ov_ref[par * width + u])

    scores(0, 0)
    scores(1, 1)
    values(0, 0)

    def body(t, carry):
        g = 2 * t
        scores(g, 0)
        values(g - 1, 1)
        finish(g - 2, 0)
        scores(g + 1, 1)
        values(g, 0)
        finish(g - 1, 1)
        return carry

    lax.fori_loop(1, n_groups // 2, body, 0)
    values(n_groups - 1, 1)
    finish(n_groups - 2, 0)
    finish(n_groups - 1, 1)


def _attn_prompt_kernel(lam_ref, sg_ref, q_ref, k_ref, v_ref, o_ref, s_ref, ov_ref, *, lam_init):
    lam = _diff_lambda(lam_ref, lam_init)
    sg = sg_ref[...]
    n = k_ref.shape[2]
    ones = jnp.ones((n, MXU_N - HEAD), BF16)

    def scores_fn(b, hh):
        return lax.dot_general(_stack_components(q_ref[b, hh]), k_ref[b, hh].astype(BF16),
                               (((1,), (1,)), ((), ())), preferred_element_type=F32)

    def pv_fn(b, hh, s):
        return _softmax_pv(s, jnp.concatenate([v_ref[b, hh].astype(BF16), ones], axis=1))

    def finish_fn(b, hh, ov):
        o_ref[b, :, HEAD * hh:HEAD * (hh + 1)] = _normalise(ov, lam, sg, lam_init)

    _chain_pipeline(q_ref.shape[0], N_HEADS, scores_fn, pv_fn, finish_fn, s_ref, ov_ref)


def _attn_prompt_call(q, k, v, lam_params, subln_g, lam_init, nb):
    bsz, nh, n, hd = q.shape
    spec = pl.BlockSpec((nb, nh, n, hd), lambda b: (b, 0, 0, 0))
    return pl.pallas_call(
        functools.partial(_attn_prompt_kernel, lam_init=lam_init),
        out_shape=jax.ShapeDtypeStruct((bsz, n, nh * hd), BF16),
        grid=(bsz // nb,),
        in_specs=[_resident(lam_params.shape), _resident(subln_g.shape), spec, spec, spec],
        out_specs=pl.BlockSpec((nb, n, nh * hd), lambda b: (b, 0, 0)),
        scratch_shapes=[pltpu.VMEM((2 * nh, 2 * n, n), F32), pltpu.VMEM((2 * nh, 2 * n, MXU_N), F32)],
        compiler_params=_params(1),
        name="attn",
    )(lam_params, subln_g, q, k, v)


ONES_ROWS = 16
ATTN_WIDTH = 2


def _attn_cache_kernel(lam_ref, sg_ref, q_ref, k_ref, vt_ref, ck_ref, cv_ref, o_ref,
                       kbuf_ref, vtbuf_ref, s_ref, ov_ref, *, lam_init):
    past = ck_ref.shape[0]
    lk = kbuf_ref.shape[0]
    kbuf_ref[0:past, :] = ck_ref[...].astype(BF16)
    kbuf_ref[past:, :] = k_ref[...]
    vtbuf_ref[0:HEAD, 0:past] = cv_ref[...].T.astype(BF16)
    vtbuf_ref[0:HEAD, past:] = vt_ref[...]
    vtbuf_ref[HEAD:, :] = jnp.ones((ONES_ROWS, lk), BF16)
    lam = _diff_lambda(lam_ref, lam_init)
    sg = sg_ref[...] * (1.0 - lam_init)

    def rows(g, u):
        start = (g * ATTN_WIDTH + u) * ATTN_SUB
        return pl.ds(pl.multiple_of(start, ATTN_SUB), ATTN_SUB)

    def scores_fn(g, u):
        return lax.dot_general(kbuf_ref[...], _stack_components(q_ref[rows(g, u), :]),
                               (((1,), (1,)), ((), ())), preferred_element_type=F32)

    def pv_fn(g, u, s):
        e = jnp.exp2(s - jnp.max(s, axis=0, keepdims=True)).astype(BF16)
        return _dot(vtbuf_ref[...], e)

    def finish_fn(g, u, ov):
        o_t = (ov[0:HEAD, 0:ATTN_SUB] / ov[HEAD:HEAD + 1, 0:ATTN_SUB]
               - lam * (ov[0:HEAD, ATTN_SUB:] / ov[HEAD:HEAD + 1, ATTN_SUB:]))
        ms = jnp.mean(o_t * o_t, axis=0, keepdims=True)
        o_ref[rows(g, u), :] = ((o_t * lax.rsqrt(ms + EPS)).T * sg).astype(BF16)

    _chain_pipeline(q_ref.shape[0] // (ATTN_SUB * ATTN_WIDTH), ATTN_WIDTH,
                    scores_fn, pv_fn, finish_fn, s_ref, ov_ref)


def _attn_cache_call(q, k, vt, lam_params, subln_g, cache_k, cache_v, layer, lam_init):
    bsz, nh, n, hd = q.shape
    past = cache_k.shape[3]
    seq_spec = pl.BlockSpec((None, None, n, hd), lambda b, h: (b, h, 0, 0))
    vt_spec = pl.BlockSpec((None, None, hd, n), lambda b, h: (b, h, 0, 0))
    c_spec = pl.BlockSpec((None, None, None, past, hd), lambda b, h: (b, layer, h, 0, 0))
    return pl.pallas_call(
        functools.partial(_attn_cache_kernel, lam_init=lam_init),
        out_shape=jax.ShapeDtypeStruct((bsz, n, nh * hd), BF16),
        grid=(bsz, nh),
        in_specs=[_resident(lam_params.shape), _resident(subln_g.shape),
                  seq_spec, seq_spec, vt_spec, c_spec, c_spec],
        out_specs=pl.BlockSpec((None, n, hd), lambda b, h: (b, 0, h)),
        scratch_shapes=[pltpu.VMEM((past + n, hd), BF16), pltpu.VMEM((hd + ONES_ROWS, past + n), BF16),
                        pltpu.VMEM((2 * ATTN_WIDTH, past + n, 2 * ATTN_SUB), F32),
                        pltpu.VMEM((2 * ATTN_WIDTH, hd + ONES_ROWS, 2 * ATTN_SUB), F32)],
        compiler_params=_params(2),
        name="attn_cache",
    )(lam_params, subln_g, q, k, vt, cache_k, cache_v)


def _post_kernel(x_ref, a_ref, b_ref, mod_ref, g_ref, wo_ref, wg_ref, wu_ref, wd_ref, o_ref):
    half = a_ref.shape[1]
    y = _dot(a_ref[...], wo_ref[0:half, :]) + _dot(b_ref[...], wo_ref[half:, :])
    x1 = x_ref[...] + mod_ref[2:3, :] * _rms(y, g_ref[1:2, :])
    h = _modulate(x1, g_ref[2:3, :], mod_ref[3:4, :], mod_ref[4:5, :]).astype(BF16)
    act = (_silu(_dot(h, wg_ref[...])) * _dot(h, wu_ref[...])).astype(BF16)
    f = _dot(act, wd_ref[...])
    o_ref[...] = x1 + mod_ref[5:6, :] * _rms(f, g_ref[3:4, :])


def _post_call(x, a, b, mod, g, w_out, w_gate, w_up, w_down, layer, tm, mod_base, mod_stride):
    bsz, n, d = x.shape
    x_spec, _, _, mod_spec = _tile_specs(n, tm, mod_base, mod_stride)
    half_spec = pl.BlockSpec((None, tm, a.shape[2]), lambda b_, j: (b_, j, 0))

    def layer_resident(w):
        return pl.BlockSpec((None,) + w.shape[1:], lambda *_: (layer, 0, 0), pipeline_mode=pl.Buffered(1))

    return pl.pallas_call(
        _post_kernel,
        out_shape=jax.ShapeDtypeStruct(x.shape, F32),
        grid=(bsz, n // tm),
        in_specs=[x_spec, half_spec, half_spec, mod_spec, _resident(g.shape),
                  _resident(w_out.shape), layer_resident(w_gate), layer_resident(w_up),
                  layer_resident(w_down)],
        out_specs=x_spec,
        compiler_params=_params(2),
        name="post",
    )(x, a, b, mod, g, w_out, w_gate, w_up, w_down)


def _odd_in_kernel(x_ref, xp_ref, xn_ref, mod_ref, g_ref, w_ref, wp_ref, ps_ref, cs_ref,
                   pc_ref, xc_ref, xs_ref, us_ref, f2_ref, f4_ref, *, n_seq):
    j = pl.program_id(1)
    nt = pl.num_programs(1)
    tm = x_ref.shape[0]
    g = g_ref[0:1, :]
    shift = mod_ref[0:1, :]
    scale = mod_ref[1:2, :]
    h = _modulate(x_ref[...], g, shift, scale).astype(BF16)
    up = _dot(h, w_ref[:, 0:POOL_W])
    uf = _dot(h, w_ref[:, POOL_W:])
    xh = jnp.concatenate([xp_ref[...], xn_ref[...]], axis=0)
    hh = _modulate(xh, g, shift, scale).astype(BF16)
    uph = _dot(hh, w_ref[:, 0:POOL_W])
    rows = tm + 2 * HALO
    us_ref[0:HALO, :] = jnp.where(j > 0, uph[0:HALO], 0.0)
    us_ref[HALO:HALO + tm, :] = up
    us_ref[HALO + tm:rows, :] = jnp.where(j < nt - 1, uph[HALO:], 0.0)
    us_ref[rows:, :] = jnp.zeros((HALO, POOL_W), F32)
    f2_ref[0:rows, :] = us_ref[0:rows, GROUP:] + us_ref[1:rows + 1, GROUP:]
    f2_ref[rows:, :] = jnp.zeros((HALO, POOL_W - GROUP), F32)
    f4_ref[0:rows, :] = f2_ref[0:rows, GROUP:] + f2_ref[2:rows + 2, GROUP:]
    f4_ref[rows:, :] = jnp.zeros((HALO, POOL_W - 2 * GROUP), F32)
    f8 = f4_ref[0:rows, GROUP:] + f4_ref[4:rows + 4, GROUP:]
    sums = (us_ref[HALO - 1:HALO - 1 + tm, 0:GROUP] + up[:, 0:GROUP],
            f2_ref[HALO - 2:HALO - 2 + tm, 0:GROUP] + f2_ref[HALO:HALO + tm, 0:GROUP],
            f4_ref[HALO - 4:HALO - 4 + tm, 0:GROUP] + f4_ref[HALO:HALO + tm, 0:GROUP],
            f8[0:tm] + f8[HALO:HALO + tm])

    t = (j * tm + lax.broadcasted_iota(jnp.int32, (tm, 1), 0)).astype(F32)
    for gi, win in enumerate(POOL_WINDOWS):
        lanes = slice(GROUP * gi, GROUP * (gi + 1))
        cnt = jnp.minimum(t + float(win // 2), float(n_seq)) - jnp.maximum(t - float(win // 2), 0.0)
        diff = (sums[gi] / cnt - up[:, lanes]).astype(BF16)
        pc_ref[:, lanes] = (_dot(diff, wp_ref[gi]) * ps_ref[0:1, lanes]).astype(BF16)
        cs = _dot(uf[:, lanes].astype(BF16), cs_ref[...])
        xc_ref[:, lanes] = cs[:, 0:GROUP].astype(BF16)
        xs_ref[:, lanes] = cs[:, GROUP:].astype(BF16)


def _odd_in_call(x, mod, g, w_in, w_pool, pool_scale, cs_mat, tm, mod_base, mod_stride):
    bsz, n, d = x.shape
    x_spec, prev_spec, next_spec, mod_spec = _tile_specs(n, tm, mod_base, mod_stride)
    out_spec = pl.BlockSpec((None, tm, POOL_W), lambda b, j: (b, j, 0))
    out_sds = jax.ShapeDtypeStruct((bsz, n, POOL_W), BF16)
    return pl.pallas_call(
        functools.partial(_odd_in_kernel, n_seq=n),
        out_shape=(out_sds, out_sds, out_sds),
        grid=(bsz, n // tm),
        in_specs=[x_spec, prev_spec, next_spec, mod_spec, _resident(g.shape), _resident(w_in.shape),
                  _resident(w_pool.shape), _resident(pool_scale.shape), _resident(cs_mat.shape)],
        out_specs=(out_spec, out_spec, out_spec),
        scratch_shapes=[pltpu.VMEM((tm + 3 * HALO, POOL_W), F32),
                        pltpu.VMEM((tm + 3 * HALO, POOL_W - GROUP), F32),
                        pltpu.VMEM((tm + 3 * HALO, POOL_W - 2 * GROUP), F32)],
        compiler_params=_params(2),
        name="odd_in",
    )(x, x, x, mod, g, w_in, w_pool, pool_scale, cs_mat)


def _four_kernel(c_ref, s_ref, xc_ref, xs_ref, wf_ref, o_ref, *, scale):
    y = _dot(c_ref[...], xc_ref[...]) - _dot(s_ref[...], xs_ref[...])
    four = (y * scale).astype(BF16)
    for gi in range(FOURIER_W // GROUP):
        lanes = slice(GROUP * gi, GROUP * (gi + 1))
        o_ref[:, lanes] = _dot(four[:, lanes], wf_ref[gi]).astype(BF16)


def _four_call(cn, sn, xc, xs, w_four, tm):
    bsz, n, w = xc.shape
    mat_spec = pl.BlockSpec((tm, n), lambda b, j: (j, 0))
    seq_spec = pl.BlockSpec((None, n, w), lambda b, j: (b, 0, 0))
    return pl.pallas_call(
        functools.partial(_four_kernel, scale=float(1.0 / math.sqrt(n * GROUP))),
        out_shape=jax.ShapeDtypeStruct((bsz, n, w), BF16),
        grid=(bsz, n // tm),
        in_specs=[mat_spec, mat_spec, seq_spec, seq_spec, _resident(w_four.shape)],
        out_specs=pl.BlockSpec((None, tm, w), lambda b, j: (b, j, 0)),
        compiler_params=_params(2),
        name="fourier",
    )(cn, sn, xc, xs, w_four)


def _rope_tables(n_tok):
    rows = n_tok // GRID_W
    row = np.repeat(np.arange(rows), GRID_W).astype(np.float64)
    col = np.tile(np.arange(GRID_W), rows).astype(np.float64)
    inv = ROPE_BASE ** (-np.arange(0, ROPE_AXIS, 2, dtype=np.float64) / ROPE_AXIS)
    ang_r = row[:, None] * inv[None, :]
    ang_c = col[:, None] * inv[None, :]
    ang = np.concatenate([ang_r, ang_r, ang_c, ang_c], axis=-1)
    cos = np.concatenate([np.cos(ang)] * 2, axis=-1)
    sin = np.concatenate([np.sin(ang)] * 2, axis=-1)
    first_half = (np.arange(HEAD) % 32) < 16
    sin_signed = np.where(first_half[None, :], -sin, sin)
    return jnp.asarray(cos, F32), jnp.asarray(sin_signed, F32)


def _dft_mats(n):
    idx = np.arange(n, dtype=np.int64)
    ang = 2.0 * np.pi * ((idx[:, None] * idx[None, :]) % n).astype(np.float64) / n
    return np.cos(ang), np.sin(ang)


def kernel(x_prompt, x_sample, cache_k, cache_v, c, c_ctx, w_mod, b_mod, norm_g,
           w_in_even, lam_params, subln_g, conv_w, w_out_even,
           w_in_odd, w_pool, pool_scale, w_fourier, w_out_odd,
           w_gate, w_up, w_down):
    depth = w_mod.shape[0]
    n_dec = x_sample.shape[0]
    n_p, n_s = x_prompt.shape[1], x_sample.shape[1]
    tm_p, tm_s = n_p, 512

    pad_rows = 16 - 1 - n_dec
    cc = jnp.concatenate([c_ctx[None, :], c, jnp.zeros((pad_rows, D_MODEL), F32)], axis=0)
    mod_all = _mod_call(cc, w_mod, b_mod)[:, :1 + n_dec].reshape(depth, 1 + n_dec, 6, D_MODEL)

    rope = _rope_tables(n_s)
    cc_g, sc_g = _dft_mats(GROUP)
    cs_mat = jnp.asarray(np.concatenate([cc_g, sc_g], axis=1), F32).astype(BF16)
    dft = {n: tuple(jnp.asarray(m, F32).astype(BF16) for m in _dft_mats(n)) for n in (n_p, n_s)}

    wg, wu, wd = w_gate.astype(BF16), w_up.astype(BF16), w_down.astype(BF16)
    xp, xs = x_prompt, x_sample
    new_k, new_v = [], []
    for l in range(depth):
        mod = mod_all[l]
        g = norm_g[l]
        i = l // 2
        streams = []
        if l % 2 == 0:
            lam_init = 0.8 - 0.6 * math.exp(-0.3 * l)
            w_in = w_in_even[i].astype(BF16)
            w_out = w_out_even[i].astype(BF16)
            sg = subln_g[i][None, :]
            qp, kp, vp, cbp = _even_in_call(xp, mod, g, w_in, conv_w[i], None, tm_p, 0, 0)
            ap = _attn_prompt_call(qp, kp, vp, lam_params[i], sg, lam_init, 8)
            new_k.append(kp)
            new_v.append(vp)
            qs, ks, vts, cbs = _even_in_call(xs, mod, g, w_in, conv_w[i], rope, tm_s, 1, 1)
            a_s = _attn_cache_call(qs, ks, vts, lam_params[i], sg, cache_k, cache_v, i, lam_init)
            streams = [(ap, cbp), (a_s, cbs)]
        else:
            w_in = w_in_odd[i].astype(BF16)
            w_out = w_out_odd[i].astype(BF16)
            wp = w_pool[i].astype(BF16)
            wf = w_fourier[i].astype(BF16)
            ps = pool_scale[i][None, :]
            pcp, xcp, xsp = _odd_in_call(xp, mod, g, w_in, wp, ps, cs_mat, tm_p, 0, 0)
            fcp = _four_call(*dft[n_p], xcp, xsp, wf, n_p)
            pcs, xcs, xss = _odd_in_call(xs, mod, g, w_in, wp, ps, cs_mat, tm_s, 1, 1)
            fcs = _four_call(*dft[n_s], xcs, xss, wf, 512)
            streams = [(pcp, fcp), (pcs, fcs)]
        xp = _post_call(*(t.reshape(1, -1, t.shape[-1]) for t in (xp,) + streams[0]),
                        mod, g, w_out, wg, wu, wd, l, 512, 0, 0).reshape(x_prompt.shape)
        xs = _post_call(xs, streams[1][0], streams[1][1], mod, g, w_out, wg, wu, wd, l, 512, 1, 1)
    def stack_layers(parts):
        if len(parts) == 1:
            return parts[0][:, None]
        return jnp.stack(parts, axis=1)

    return xp, xs, stack_layers(new_k), stack_layers(new_v)
```

```python
import functools
import math

import numpy as np
import jax
import jax.numpy as jnp
from jax import lax
from jax.experimental import pallas as pl
from jax.experimental.pallas import tpu as pltpu

F32 = jnp.float32
BF16 = jnp.bfloat16

D_MODEL = 1024
GRID_W = 64
N_HEADS = 4
HEAD = 128
HALF_HEAD = 64
ROPE_AXIS = 32
ROPE_BASE = 10000.0
ATTN_W = 512
CONV_W = 512
POOL_W = 512
FOURIER_W = 512
GROUP = 128
POOL_WINDOWS = (2, 4, 8, 16)
D_FF = 2816
EPS = 1e-6
LOG2E = math.log2(math.e)
HALO = 8
MXU_N = 256
ATTN_SUB = 128
VMEM_LIMIT = 56 * 1024 * 1024


def _params(n_axes):
    return pltpu.CompilerParams(dimension_semantics=("arbitrary",) * n_axes,
                                vmem_limit_bytes=VMEM_LIMIT)


def _resident(shape):
    return pl.BlockSpec(shape, lambda *_: (0,) * len(shape), pipeline_mode=pl.Buffered(1))


def _rms(x, g):
    ms = jnp.mean(x * x, axis=-1, keepdims=True)
    return x * lax.rsqrt(ms + EPS) * g


def _modulate(x, g, shift, scale):
    return _rms(x, g) * (1.0 + scale) + shift


def _dot(a, b):
    return jnp.dot(a, b, preferred_element_type=F32)


def _silu(x):
    return x / (1.0 + jnp.exp(-x))


def _mod_kernel(cc_ref, w_ref, b_ref, o_ref):
    s = _silu(cc_ref[...]).astype(BF16)
    o_ref[...] = _dot(s, w_ref[...].astype(BF16)) + b_ref[...]


def _mod_call(cc, w_mod, b_mod):
    depth, d, n6 = w_mod.shape
    rows = cc.shape[0]
    tn = 2048
    return pl.pallas_call(
        _mod_kernel,
        out_shape=jax.ShapeDtypeStruct((depth, rows, n6), F32),
        grid=(depth, n6 // tn),
        in_specs=[
            pl.BlockSpec((rows, d), lambda l, j: (0, 0)),
            pl.BlockSpec((None, d, tn), lambda l, j: (l, 0, j)),
            pl.BlockSpec((None, 1, tn), lambda l, j: (l, 0, j)),
        ],
        out_specs=pl.BlockSpec((None, rows, tn), lambda l, j: (l, 0, j)),
        compiler_params=_params(2),
        name="mod",
    )(cc, w_mod, b_mod.reshape(depth, 1, n6))


def _tile_specs(n, tm, mod_base, mod_stride):
    nb8 = n // HALO
    t8 = tm // HALO
    x_spec = pl.BlockSpec((None, tm, D_MODEL), lambda b, j: (b, j, 0))
    prev_spec = pl.BlockSpec((None, HALO, D_MODEL),
                             lambda b, j: (b, jnp.maximum(j * t8 - 1, 0), 0))
    next_spec = pl.BlockSpec((None, HALO, D_MODEL),
                             lambda b, j: (b, jnp.minimum((j + 1) * t8, nb8 - 1), 0))
    mod_spec = pl.BlockSpec((None, 6, D_MODEL),
                            lambda b, j: (mod_base + mod_stride * b, 0, 0))
    return x_spec, prev_spec, next_spec, mod_spec


def _rope(t, cos, sin_signed, first_half):
    outs = []
    for hh in range(N_HEADS):
        th = t[:, HEAD * hh:HEAD * (hh + 1)]
        swapped = jnp.where(first_half, pltpu.roll(th, HEAD - 16, 1), pltpu.roll(th, 16, 1))
        outs.append(th * cos + swapped * sin_signed)
    return outs


def _even_in_kernel(*refs, use_rope):
    if use_rope:
        (x_ref, xp_ref, xn_ref, mod_ref, g_ref, w_ref, cw_ref, cos_ref, sin_ref,
         q_ref, k_ref, v_ref, cb_ref, zs_ref) = refs
    else:
        (x_ref, xp_ref, xn_ref, mod_ref, g_ref, w_ref, cw_ref,
         q_ref, k_ref, v_ref, cb_ref, zs_ref) = refs
    j = pl.program_id(1)
    nt = pl.num_programs(1)
    tm = x_ref.shape[0]
    g = g_ref[0:1, :]
    shift = mod_ref[0:1, :]
    scale = mod_ref[1:2, :]
    h = _modulate(x_ref[...], g, shift, scale).astype(BF16)

    def proj(hh, lo):
        return _dot(hh, w_ref[:, lo:lo + 512])

    q = proj(h, 0) * (HALF_HEAD ** -0.5 * LOG2E)
    k = proj(h, 512)
    v = proj(h, 1024)
    if use_rope:
        lane = lax.broadcasted_iota(jnp.int32, (1, HEAD), 1)
        first_half = (lane % 32) < 16
        cos = cos_ref[...]
        sin_signed = sin_ref[...]
        qs = _rope(q, cos, sin_signed, first_half)
        ks = _rope(k, cos, sin_signed, first_half)
    else:
        qs = [q[:, HEAD * hh:HEAD * (hh + 1)] for hh in range(N_HEADS)]
        ks = [k[:, HEAD * hh:HEAD * (hh + 1)] for hh in range(N_HEADS)]
    for hh in range(N_HEADS):
        vh = v[:, HEAD * hh:HEAD * (hh + 1)]
        q_ref[hh] = qs[hh].astype(q_ref.dtype)
        k_ref[hh] = ks[hh].astype(k_ref.dtype)
        v_ref[hh] = (vh.T if use_rope else vh).astype(v_ref.dtype)

    gate_b = proj(h, 1536)
    z = proj(h, 2048) * proj(h, 2560)
    xh = jnp.concatenate([xp_ref[...], xn_ref[...]], axis=0)
    hh_ = _modulate(xh, g, shift, scale).astype(BF16)
    zh = proj(hh_, 2048) * proj(hh_, 2560)
    zs_ref[0:HALO, :] = jnp.where(j > 0, zh[0:HALO], 0.0)
    zs_ref[HALO:HALO + tm, :] = z
    zs_ref[HALO + tm:, :] = jnp.where(j < nt - 1, zh[HALO:], 0.0)
    conv = (cw_ref[0:1, :] * zs_ref[HALO - 1:HALO - 1 + tm, :] + cw_ref[1:2, :] * z
            + cw_ref[2:3, :] * zs_ref[HALO + 1:HALO + 1 + tm, :])
    cb_ref[...] = (gate_b * conv).astype(BF16)


def _even_in_call(x, mod, g, w_in, conv_w, rope, tm, mod_base, mod_stride):
    bsz, n, d = x.shape
    use_rope = rope is not None
    x_spec, prev_spec, next_spec, mod_spec = _tile_specs(n, tm, mod_base, mod_stride)
    in_specs = [x_spec, prev_spec, next_spec, mod_spec,
                _resident(g.shape), _resident(w_in.shape), _resident(conv_w.shape)]
    args = [x, x, x, mod, g, w_in, conv_w]
    if use_rope:
        tab = pl.BlockSpec((tm, HEAD), lambda b, j: (j, 0))
        in_specs += [tab, tab]
        args += list(rope)
    head_spec = pl.BlockSpec((None, N_HEADS, tm, HEAD), lambda b, j: (b, 0, j, 0))
    head_shape = (bsz, N_HEADS, n, HEAD)
    if use_rope:
        v_spec = pl.BlockSpec((None, N_HEADS, HEAD, tm), lambda b, j: (b, 0, 0, j))
        k_sds = jax.ShapeDtypeStruct(head_shape, BF16)
        v_sds = jax.ShapeDtypeStruct((bsz, N_HEADS, HEAD, n), BF16)
    else:
        v_spec = head_spec
        k_sds = v_sds = jax.ShapeDtypeStruct(head_shape, F32)
    return pl.pallas_call(
        functools.partial(_even_in_kernel, use_rope=use_rope),
        out_shape=(jax.ShapeDtypeStruct(head_shape, BF16), k_sds, v_sds,
                   jax.ShapeDtypeStruct((bsz, n, CONV_W), BF16)),
        grid=(bsz, n // tm),
        in_specs=in_specs,
        out_specs=(head_spec, head_spec, v_spec,
                   pl.BlockSpec((None, tm, CONV_W), lambda b, j: (b, j, 0))),
        scratch_shapes=[pltpu.VMEM((tm + 2 * HALO, CONV_W), F32)],
        compiler_params=_params(2),
        name="even_in_rope" if use_rope else "even_in",
    )(*args)


def _diff_lambda(lam_ref, lam_init):
    lp = lam_ref[...]
    return (jnp.exp(jnp.sum(lp[0:1] * lp[1:2], axis=-1, keepdims=True))
            - jnp.exp(jnp.sum(lp[2:3] * lp[3:4], axis=-1, keepdims=True)) + lam_init)


def _stack_components(q):
    lane = lax.broadcasted_iota(jnp.int32, (1, HEAD), 1)
    zero = jnp.zeros_like(q)
    return jnp.concatenate([jnp.where(lane < HALF_HEAD, q, zero),
                            jnp.where(lane >= HALF_HEAD, q, zero)], axis=0)


def _softmax_pv(s, v_ext):
    e = jnp.exp2(s - jnp.max(s, axis=-1, keepdims=True)).astype(BF16)
    return _dot(e, v_ext)


def _normalise(ov, lam, sg, lam_init):
    t = ov.shape[0] // 2
    o = ov[:t, :HEAD] / ov[:t, HEAD:] - lam * (ov[t:, :HEAD] / ov[t:, HEAD:])
    return (_rms(o, sg) * (1.0 - lam_init)).astype(BF16)


def _chain_pipeline(n_groups, width, scores_fn, pv_fn, finish_fn, s_ref, ov_ref):
    assert n_groups % 2 == 0 and n_groups >= 2

    def scores(g, par):
        for u in range(width):
            s_ref[par * width + u] = scores_fn(g, u)

    def values(g, par):
        for u in range(width):
            ov_ref[par * width + u] = pv_fn(g, u, s_ref[par * width + u])

    def finish(g, par):
        for u in range(width):
            finish_fn(g, u, ov_ref[par * width + u])

    scores(0, 0)
    scores(1, 1)
    values(0, 0)

    def body(t, carry):
        g = 2 * t
        scores(g, 0)
        values(g - 1, 1)
        finish(g - 2, 0)
        scores(g + 1, 1)
        values(g, 0)
        finish(g - 1, 1)
        return carry

    lax.fori_loop(1, n_groups // 2, body, 0)
    values(n_groups - 1, 1)
    finish(n_groups - 2, 0)
    finish(n_groups - 1, 1)


def _attn_prompt_kernel(lam_ref, sg_ref, q_ref, k_ref, v_ref, o_ref, s_ref, ov_ref, *, lam_init):
    lam = _diff_lambda(lam_ref, lam_init)
    sg = sg_ref[...]
    n = k_ref.shape[2]
    ones = jnp.ones((n, MXU_N - HEAD), BF16)

    def scores_fn(b, hh):
        return lax.dot_general(_stack_components(q_ref[b, hh]), k_ref[b, hh].astype(BF16),
                               (((1,), (1,)), ((), ())), preferred_element_type=F32)

    def pv_fn(b, hh, s):
        return _softmax_pv(s, jnp.concatenate([v_ref[b, hh].astype(BF16), ones], axis=1))

    def finish_fn(b, hh, ov):
        o_ref[b, :, HEAD * hh:HEAD * (hh + 1)] = _normalise(ov, lam, sg, lam_init)

    _chain_pipeline(q_ref.shape[0], N_HEADS, scores_fn, pv_fn, finish_fn, s_ref, ov_ref)


def _attn_prompt_call(q, k, v, lam_params, subln_g, lam_init, nb):
    bsz, nh, n, hd = q.shape
    spec = pl.BlockSpec((nb, nh, n, hd), lambda b: (b, 0, 0, 0))
    return pl.pallas_call(
        functools.partial(_attn_prompt_kernel, lam_init=lam_init),
        out_shape=jax.ShapeDtypeStruct((bsz, n, nh * hd), BF16),
        grid=(bsz // nb,),
        in_specs=[_resident(lam_params.shape), _resident(subln_g.shape), spec, spec, spec],
        out_specs=pl.BlockSpec((nb, n, nh * hd), lambda b: (b, 0, 0)),
        scratch_shapes=[pltpu.VMEM((2 * nh, 2 * n, n), F32), pltpu.VMEM((2 * nh, 2 * n, MXU_N), F32)],
        compiler_params=_params(1),
        name="attn",
    )(lam_params, subln_g, q, k, v)


ONES_ROWS = 16
ATTN_WIDTH = 2


def _attn_cache_kernel(lam_ref, sg_ref, q_ref, k_ref, vt_ref, ck_ref, cv_ref, o_ref,
                       kbuf_ref, vtbuf_ref, s_ref, ov_ref, *, lam_init):
    past = ck_ref.shape[0]
    lk = kbuf_ref.shape[0]
    kbuf_ref[0:past, :] = ck_ref[...].astype(BF16)
    kbuf_ref[past:, :] = k_ref[...]
    vtbuf_ref[0:HEAD, 0:past] = cv_ref[...].T.astype(BF16)
    vtbuf_ref[0:HEAD, past:] = vt_ref[...]
    vtbuf_ref[HEAD:, :] = jnp.ones((ONES_ROWS, lk), BF16)
    lam = _diff_lambda(lam_ref, lam_init)
    sg = sg_ref[...] * (1.0 - lam_init)

    def rows(g, u):
        start = (g * ATTN_WIDTH + u) * ATTN_SUB
        return pl.ds(pl.multiple_of(start, ATTN_SUB), ATTN_SUB)

    def scores_fn(g, u):
        return lax.dot_general(kbuf_ref[...], _stack_components(q_ref[rows(g, u), :]),
                               (((1,), (1,)), ((), ())), preferred_element_type=F32)

    def pv_fn(g, u, s):
        e = jnp.exp2(s - jnp.max(s, axis=0, keepdims=True)).astype(BF16)
        return _dot(vtbuf_ref[...], e)

    def finish_fn(g, u, ov):
        o_t = (ov[0:HEAD, 0:ATTN_SUB] / ov[HEAD:HEAD + 1, 0:ATTN_SUB]
               - lam * (ov[0:HEAD, ATTN_SUB:] / ov[HEAD:HEAD + 1, ATTN_SUB:]))
        ms = jnp.mean(o_t * o_t, axis=0, keepdims=True)
        o_ref[rows(g, u), :] = ((o_t * lax.rsqrt(ms + EPS)).T * sg).astype(BF16)

    _chain_pipeline(q_ref.shape[0] // (ATTN_SUB * ATTN_WIDTH), ATTN_WIDTH,
                    scores_fn, pv_fn, finish_fn, s_ref, ov_ref)


def _attn_cache_call(q, k, vt, lam_params, subln_g, cache_k, cache_v, layer, lam_init):
    bsz, nh, n, hd = q.shape
    past = cache_k.shape[3]
    seq_spec = pl.BlockSpec((None, None, n, hd), lambda b, h: (b, h, 0, 0))
    vt_spec = pl.BlockSpec((None, None, hd, n), lambda b, h: (b, h, 0, 0))
    c_spec = pl.BlockSpec((None, None, None, past, hd), lambda b, h: (b, layer, h, 0, 0))
    return pl.pallas_call(
        functools.partial(_attn_cache_kernel, lam_init=lam_init),
        out_shape=jax.ShapeDtypeStruct((bsz, n, nh * hd), BF16),
        grid=(bsz, nh),
        in_specs=[_resident(lam_params.shape), _resident(subln_g.shape),
                  seq_spec, seq_spec, vt_spec, c_spec, c_spec],
        out_specs=pl.BlockSpec((None, n, hd), lambda b, h: (b, 0, h)),
        scratch_shapes=[pltpu.VMEM((past + n, hd), BF16), pltpu.VMEM((hd + ONES_ROWS, past + n), BF16),
                        pltpu.VMEM((2 * ATTN_WIDTH, past + n, 2 * ATTN_SUB), F32),
                        pltpu.VMEM((2 * ATTN_WIDTH, hd + ONES_ROWS, 2 * ATTN_SUB), F32)],
        compiler_params=_params(2),
        name="attn_cache",
    )(lam_params, subln_g, q, k, vt, cache_k, cache_v)


def _post_kernel(x_ref, a_ref, b_ref, mod_ref, g_ref, wo_ref, wg_ref, wu_ref, wd_ref, o_ref):
    half = a_ref.shape[1]
    sub = x_ref.shape[0] // 2
    blocks = (slice(0, sub), slice(sub, 2 * sub))

    def out_proj(rows):
        return _dot(a_ref[rows, :], wo_ref[0:half, :]) + _dot(b_ref[rows, :], wo_ref[half:, :])

    def norms(rows, y):
        x1 = x_ref[rows, :] + mod_ref[2:3, :] * _rms(y, g_ref[1:2, :])
        return x1, _modulate(x1, g_ref[2:3, :], mod_ref[3:4, :], mod_ref[4:5, :]).astype(BF16)

    def finish(rows, x1, f):
        o_ref[rows, :] = x1 + mod_ref[5:6, :] * _rms(f, g_ref[3:4, :])

    y0, y1 = out_proj(blocks[0]), out_proj(blocks[1])
    x1_0, h0 = norms(blocks[0], y0)
    g0, u0 = _dot(h0, wg_ref[...]), _dot(h0, wu_ref[...])
    x1_1, h1 = norms(blocks[1], y1)
    f0 = _dot((_silu(g0) * u0).astype(BF16), wd_ref[...])
    g1, u1 = _dot(h1, wg_ref[...]), _dot(h1, wu_ref[...])
    finish(blocks[0], x1_0, f0)
    f1 = _dot((_silu(g1) * u1).astype(BF16), wd_ref[...])
    finish(blocks[1], x1_1, f1)


def _post_call(x, a, b, mod, g, w_out, w_gate, w_up, w_down, layer, tm, mod_base, mod_stride):
    bsz, n, d = x.shape
    x_spec, _, _, mod_spec = _tile_specs(n, tm, mod_base, mod_stride)
    half_spec = pl.BlockSpec((None, tm, a.shape[2]), lambda b_, j: (b_, j, 0))

    def layer_resident(w):
        return pl.BlockSpec((None,) + w.shape[1:], lambda *_: (layer, 0, 0), pipeline_mode=pl.Buffered(1))

    return pl.pallas_call(
        _post_kernel,
        out_shape=jax.ShapeDtypeStruct(x.shape, F32),
        grid=(bsz, n // tm),
        in_specs=[x_spec, half_spec, half_spec, mod_spec, _resident(g.shape),
                  _resident(w_out.shape), layer_resident(w_gate), layer_resident(w_up),
                  layer_resident(w_down)],
        out_specs=x_spec,
        compiler_params=_params(2),
        name="post",
    )(x, a, b, mod, g, w_out, w_gate, w_up, w_down)


def _odd_in_kernel(x_ref, xp_ref, xn_ref, mod_ref, g_ref, w_ref, wp_ref, ps_ref, cs_ref,
                   pc_ref, xc_ref, xs_ref, us_ref, f2_ref, f4_ref, *, n_seq):
    j = pl.program_id(1)
    nt = pl.num_programs(1)
    tm = x_ref.shape[0]
    g = g_ref[0:1, :]
    shift = mod_ref[0:1, :]
    scale = mod_ref[1:2, :]
    h = _modulate(x_ref[...], g, shift, scale).astype(BF16)
    up = _dot(h, w_ref[:, 0:POOL_W])
    uf = _dot(h, w_ref[:, POOL_W:])
    xh = jnp.concatenate([xp_ref[...], xn_ref[...]], axis=0)
    hh = _modulate(xh, g, shift, scale).astype(BF16)
    uph = _dot(hh, w_ref[:, 0:POOL_W])
    rows = tm + 2 * HALO
    us_ref[0:HALO, :] = jnp.where(j > 0, uph[0:HALO], 0.0)
    us_ref[HALO:HALO + tm, :] = up
    us_ref[HALO + tm:rows, :] = jnp.where(j < nt - 1, uph[HALO:], 0.0)
    us_ref[rows:, :] = jnp.zeros((HALO, POOL_W), F32)
    f2_ref[0:rows, :] = us_ref[0:rows, GROUP:] + us_ref[1:rows + 1, GROUP:]
    f2_ref[rows:, :] = jnp.zeros((HALO, POOL_W - GROUP), F32)
    f4_ref[0:rows, :] = f2_ref[0:rows, GROUP:] + f2_ref[2:rows + 2, GROUP:]
    f4_ref[rows:, :] = jnp.zeros((HALO, POOL_W - 2 * GROUP), F32)
    f8 = f4_ref[0:rows, GROUP:] + f4_ref[4:rows + 4, GROUP:]
    sums = (us_ref[HALO - 1:HALO - 1 + tm, 0:GROUP] + up[:, 0:GROUP],
            f2_ref[HALO - 2:HALO - 2 + tm, 0:GROUP] + f2_ref[HALO:HALO + tm, 0:GROUP],
            f4_ref[HALO - 4:HALO - 4 + tm, 0:GROUP] + f4_ref[HALO:HALO + tm, 0:GROUP],
            f8[0:tm] + f8[HALO:HALO + tm])

    t = (j * tm + lax.broadcasted_iota(jnp.int32, (tm, 1), 0)).astype(F32)
    for gi, win in enumerate(POOL_WINDOWS):
        lanes = slice(GROUP * gi, GROUP * (gi + 1))
        cnt = jnp.minimum(t + float(win // 2), float(n_seq)) - jnp.maximum(t - float(win // 2), 0.0)
        diff = (sums[gi] / cnt - up[:, lanes]).astype(BF16)
        pc_ref[:, lanes] = (_dot(diff, wp_ref[gi]) * ps_ref[0:1, lanes]).astype(BF16)
        cs = _dot(uf[:, lanes].astype(BF16), cs_ref[...])
        xc_ref[:, lanes] = cs[:, 0:GROUP].astype(BF16)
        xs_ref[:, lanes] = cs[:, GROUP:].astype(BF16)


def _odd_in_call(x, mod, g, w_in, w_pool, pool_scale, cs_mat, tm, mod_base, mod_stride):
    bsz, n, d = x.shape
    x_spec, prev_spec, next_spec, mod_spec = _tile_specs(n, tm, mod_base, mod_stride)
    out_spec = pl.BlockSpec((None, tm, POOL_W), lambda b, j: (b, j, 0))
    out_sds = jax.ShapeDtypeStruct((bsz, n, POOL_W), BF16)
    return pl.pallas_call(
        functools.partial(_odd_in_kernel, n_seq=n),
        out_shape=(out_sds, out_sds, out_sds),
        grid=(bsz, n // tm),
        in_specs=[x_spec, prev_spec, next_spec, mod_spec, _resident(g.shape), _resident(w_in.shape),
                  _resident(w_pool.shape), _resident(pool_scale.shape), _resident(cs_mat.shape)],
        out_specs=(out_spec, out_spec, out_spec),
        scratch_shapes=[pltpu.VMEM((tm + 3 * HALO, POOL_W), F32),
                        pltpu.VMEM((tm + 3 * HALO, POOL_W - GROUP), F32),
                        pltpu.VMEM((tm + 3 * HALO, POOL_W - 2 * GROUP), F32)],
        compiler_params=_params(2),
        name="odd_in",
    )(x, x, x, mod, g, w_in, w_pool, pool_scale, cs_mat)


def _four_kernel(c_ref, s_ref, xc_ref, xs_ref, wf_ref, o_ref, *, scale):
    y = _dot(c_ref[...], xc_ref[...]) - _dot(s_ref[...], xs_ref[...])
    four = (y * scale).astype(BF16)
    for gi in range(FOURIER_W // GROUP):
        lanes = slice(GROUP * gi, GROUP * (gi + 1))
        o_ref[:, lanes] = _dot(four[:, lanes], wf_ref[gi]).astype(BF16)


def _four_call(cn, sn, xc, xs, w_four, tm):
    bsz, n, w = xc.shape
    mat_spec = pl.BlockSpec((tm, n), lambda b, j: (j, 0))
    seq_spec = pl.BlockSpec((None, n, w), lambda b, j: (b, 0, 0))
    return pl.pallas_call(
        functools.partial(_four_kernel, scale=float(1.0 / math.sqrt(n * GROUP))),
        out_shape=jax.ShapeDtypeStruct((bsz, n, w), BF16),
        grid=(bsz, n // tm),
        in_specs=[mat_spec, mat_spec, seq_spec, seq_spec, _resident(w_four.shape)],
        out_specs=pl.BlockSpec((None, tm, w), lambda b, j: (b, j, 0)),
        compiler_params=_params(2),
        name="fourier",
    )(cn, sn, xc, xs, w_four)


def _rope_tables(n_tok):
    rows = n_tok // GRID_W
    row = np.repeat(np.arange(rows), GRID_W).astype(np.float64)
    col = np.tile(np.arange(GRID_W), rows).astype(np.float64)
    inv = ROPE_BASE ** (-np.arange(0, ROPE_AXIS, 2, dtype=np.float64) / ROPE_AXIS)
    ang_r = row[:, None] * inv[None, :]
    ang_c = col[:, None] * inv[None, :]
    ang = np.concatenate([ang_r, ang_r, ang_c, ang_c], axis=-1)
    cos = np.concatenate([np.cos(ang)] * 2, axis=-1)
    sin = np.concatenate([np.sin(ang)] * 2, axis=-1)
    first_half = (np.arange(HEAD) % 32) < 16
    sin_signed = np.where(first_half[None, :], -sin, sin)
    return jnp.asarray(cos, F32), jnp.asarray(sin_signed, F32)


def _dft_mats(n):
    idx = np.arange(n, dtype=np.int64)
    ang = 2.0 * np.pi * ((idx[:, None] * idx[None, :]) % n).astype(np.float64) / n
    return np.cos(ang), np.sin(ang)


def kernel(x_prompt, x_sample, cache_k, cache_v, c, c_ctx, w_mod, b_mod, norm_g,
           w_in_even, lam_params, subln_g, conv_w, w_out_even,
           w_in_odd, w_pool, pool_scale, w_fourier, w_out_odd,
           w_gate, w_up, w_down):
    depth = w_mod.shape[0]
    n_dec = x_sample.shape[0]
    n_p, n_s = x_prompt.shape[1], x_sample.shape[1]
    tm_p, tm_s = n_p, 512

    pad_rows = 16 - 1 - n_dec
    cc = jnp.concatenate([c_ctx[None, :], c, jnp.zeros((pad_rows, D_MODEL), F32)], axis=0)
    mod_all = _mod_call(cc, w_mod, b_mod)[:, :1 + n_dec].reshape(depth, 1 + n_dec, 6, D_MODEL)

    rope = _rope_tables(n_s)
    cc_g, sc_g = _dft_mats(GROUP)
    cs_mat = jnp.asarray(np.concatenate([cc_g, sc_g], axis=1), F32).astype(BF16)
    dft = {n: tuple(jnp.asarray(m, F32).astype(BF16) for m in _dft_mats(n)) for n in (n_p, n_s)}

    wg, wu, wd = w_gate.astype(BF16), w_up.astype(BF16), w_down.astype(BF16)
    xp, xs = x_prompt, x_sample
    new_k, new_v = [], []
    for l in range(depth):
        mod = mod_all[l]
        g = norm_g[l]
        i = l // 2
        streams = []
        if l % 2 == 0:
            lam_init = 0.8 - 0.6 * math.exp(-0.3 * l)
            w_in = w_in_even[i].astype(BF16)
            w_out = w_out_even[i].astype(BF16)
            sg = subln_g[i][None, :]
            qp, kp, vp, cbp = _even_in_call(xp, mod, g, w_in, conv_w[i], None, tm_p, 0, 0)
            ap = _attn_prompt_call(qp, kp, vp, lam_params[i], sg, lam_init, 8)
            new_k.append(kp)
            new_v.append(vp)
            qs, ks, vts, cbs = _even_in_call(xs, mod, g, w_in, conv_w[i], rope, tm_s, 1, 1)
            a_s = _attn_cache_call(qs, ks, vts, lam_params[i], sg, cache_k, cache_v, i, lam_init)
            streams = [(ap, cbp), (a_s, cbs)]
        else:
            w_in = w_in_odd[i].astype(BF16)
            w_out = w_out_odd[i].astype(BF16)
            wp = w_pool[i].astype(BF16)
            wf = w_fourier[i].astype(BF16)
            ps = pool_scale[i][None, :]
            pcp, xcp, xsp = _odd_in_call(xp, mod, g, w_in, wp, ps, cs_mat, tm_p, 0, 0)
            fcp = _four_call(*dft[n_p], xcp, xsp, wf, n_p)
            pcs, xcs, xss = _odd_in_call(xs, mod, g, w_in, wp, ps, cs_mat, tm_s, 1, 1)
            fcs = _four_call(*dft[n_s], xcs, xss, wf, 512)
            streams = [(pcp, fcp), (pcs, fcs)]
        xp = _post_call(*(t.reshape(1, -1, t.shape[-1]) for t in (xp,) + streams[0]),
                        mod, g, w_out, wg, wu, wd, l, 512, 0, 0).reshape(x_prompt.shape)
        xs = _post_call(xs, streams[1][0], streams[1][1], mod, g, w_out, wg, wu, wd, l, 512, 1, 1)
    def stack_layers(parts):
        if len(parts) == 1:
            return parts[0][:, None]
        return jnp.stack(parts, axis=1)

    return xp, xs, stack_layers(new_k), stack_layers(new_v)
```

```python
import functools
import math

import numpy as np
import jax
import jax.numpy as jnp
from jax import lax
from jax.experimental import pallas as pl
from jax.experimental.pallas import tpu as pltpu

F32 = jnp.float32
BF16 = jnp.bfloat16

D_MODEL = 1024
GRID_W = 64
N_HEADS = 4
HEAD = 128
HALF_HEAD = 64
ROPE_AXIS = 32
ROPE_BASE = 10000.0
ATTN_W = 512
CONV_W = 512
POOL_W = 512
FOURIER_W = 512
GROUP = 128
POOL_WINDOWS = (2, 4, 8, 16)
D_FF = 2816
EPS = 1e-6
LOG2E = math.log2(math.e)
HALO = 8
MXU_N = 256
ATTN_SUB = 128
VMEM_LIMIT = 56 * 1024 * 1024


def _params(n_axes):
    return pltpu.CompilerParams(dimension_semantics=("arbitrary",) * n_axes,
                                vmem_limit_bytes=VMEM_LIMIT)


def _resident(shape):
    return pl.BlockSpec(shape, lambda *_: (0,) * len(shape), pipeline_mode=pl.Buffered(1))


def _rms(x, g):
    ms = jnp.mean(x * x, axis=-1, keepdims=True)
    return x * lax.rsqrt(ms + EPS) * g


def _modulate(x, g, shift, scale):
    return _rms(x, g) * (1.0 + scale) + shift


def _dot(a, b):
    return jnp.dot(a, b, preferred_element_type=F32)


def _silu(x):
    return x / (1.0 + jnp.exp(-x))


def _mod_kernel(cc_ref, w_ref, b_ref, o_ref):
    s = _silu(cc_ref[...]).astype(BF16)
    o_ref[...] = _dot(s, w_ref[...].astype(BF16)) + b_ref[...]


def _mod_call(cc, w_mod, b_mod):
    depth, d, n6 = w_mod.shape
    rows = cc.shape[0]
    tn = 2048
    return pl.pallas_call(
        _mod_kernel,
        out_shape=jax.ShapeDtypeStruct((depth, rows, n6), F32),
        grid=(depth, n6 // tn),
        in_specs=[
            pl.BlockSpec((rows, d), lambda l, j: (0, 0)),
            pl.BlockSpec((None, d, tn), lambda l, j: (l, 0, j)),
            pl.BlockSpec((None, 1, tn), lambda l, j: (l, 0, j)),
        ],
        out_specs=pl.BlockSpec((None, rows, tn), lambda l, j: (l, 0, j)),
        compiler_params=_params(2),
        name="mod",
    )(cc, w_mod, b_mod.reshape(depth, 1, n6))


def _tile_specs(n, tm, mod_base, mod_stride):
    nb8 = n // HALO
    t8 = tm // HALO
    x_spec = pl.BlockSpec((None, tm, D_MODEL), lambda b, j: (b, j, 0))
    prev_spec = pl.BlockSpec((None, HALO, D_MODEL),
                             lambda b, j: (b, jnp.maximum(j * t8 - 1, 0), 0))
    next_spec = pl.BlockSpec((None, HALO, D_MODEL),
                             lambda b, j: (b, jnp.minimum((j + 1) * t8, nb8 - 1), 0))
    mod_spec = pl.BlockSpec((None, 6, D_MODEL),
                            lambda b, j: (mod_base + mod_stride * b, 0, 0))
    return x_spec, prev_spec, next_spec, mod_spec


def _rope(t, cos, sin_signed, first_half):
    outs = []
    for hh in range(N_HEADS):
        th = t[:, HEAD * hh:HEAD * (hh + 1)]
        swapped = jnp.where(first_half, pltpu.roll(th, HEAD - 16, 1), pltpu.roll(th, 16, 1))
        outs.append(th * cos + swapped * sin_signed)
    return outs


def _even_in_kernel(*refs, use_rope):
    if use_rope:
        (x_ref, xp_ref, xn_ref, mod_ref, g_ref, w_ref, cw_ref, cos_ref, sin_ref,
         q_ref, k_ref, v_ref, cb_ref, zs_ref) = refs
    else:
        (x_ref, xp_ref, xn_ref, mod_ref, g_ref, w_ref, cw_ref,
         q_ref, k_ref, v_ref, cb_ref, zs_ref) = refs
    j = pl.program_id(1)
    nt = pl.num_programs(1)
    tm = x_ref.shape[0]
    g = g_ref[0:1, :]
    shift = mod_ref[0:1, :]
    scale = mod_ref[1:2, :]
    h = _modulate(x_ref[...], g, shift, scale).astype(BF16)

    def proj(hh, lo):
        return _dot(hh, w_ref[:, lo:lo + 512])

    q = proj(h, 0) * (HALF_HEAD ** -0.5 * LOG2E)
    k = proj(h, 512)
    v = proj(h, 1024)
    if use_rope:
        lane = lax.broadcasted_iota(jnp.int32, (1, HEAD), 1)
        first_half = (lane % 32) < 16
        cos = cos_ref[...]
        sin_signed = sin_ref[...]
        qs = _rope(q, cos, sin_signed, first_half)
        ks = _rope(k, cos, sin_signed, first_half)
    else:
        qs = [q[:, HEAD * hh:HEAD * (hh + 1)] for hh in range(N_HEADS)]
        ks = [k[:, HEAD * hh:HEAD * (hh + 1)] for hh in range(N_HEADS)]
    for hh in range(N_HEADS):
        vh = v[:, HEAD * hh:HEAD * (hh + 1)]
        q_ref[hh] = qs[hh].astype(q_ref.dtype)
        k_ref[hh] = ks[hh].astype(k_ref.dtype)
        v_ref[hh] = (vh.T if use_rope else vh).astype(v_ref.dtype)

    gate_b = proj(h, 1536)
    z = proj(h, 2048) * proj(h, 2560)
    xh = jnp.concatenate([xp_ref[...], xn_ref[...]], axis=0)
    hh_ = _modulate(xh, g, shift, scale).astype(BF16)
    zh = proj(hh_, 2048) * proj(hh_, 2560)
    zs_ref[0:HALO, :] = jnp.where(j > 0, zh[0:HALO], 0.0)
    zs_ref[HALO:HALO + tm, :] = z
    zs_ref[HALO + tm:, :] = jnp.where(j < nt - 1, zh[HALO:], 0.0)
    conv = (cw_ref[0:1, :] * zs_ref[HALO - 1:HALO - 1 + tm, :] + cw_ref[1:2, :] * z
            + cw_ref[2:3, :] * zs_ref[HALO + 1:HALO + 1 + tm, :])
    cb_ref[...] = (gate_b * conv).astype(BF16)


def _even_in_call(x, mod, g, w_in, conv_w, rope, tm, mod_base, mod_stride):
    bsz, n, d = x.shape
    use_rope = rope is not None
    x_spec, prev_spec, next_spec, mod_spec = _tile_specs(n, tm, mod_base, mod_stride)
    in_specs = [x_spec, prev_spec, next_spec, mod_spec,
                _resident(g.shape), _resident(w_in.shape), _resident(conv_w.shape)]
    args = [x, x, x, mod, g, w_in, conv_w]
    if use_rope:
        tab = pl.BlockSpec((tm, HEAD), lambda b, j: (j, 0))
        in_specs += [tab, tab]
        args += list(rope)
    head_spec = pl.BlockSpec((None, N_HEADS, tm, HEAD), lambda b, j: (b, 0, j, 0))
    head_shape = (bsz, N_HEADS, n, HEAD)
    if use_rope:
        v_spec = pl.BlockSpec((None, N_HEADS, HEAD, tm), lambda b, j: (b, 0, 0, j))
        k_sds = jax.ShapeDtypeStruct(head_shape, BF16)
        v_sds = jax.ShapeDtypeStruct((bsz, N_HEADS, HEAD, n), BF16)
    else:
        v_spec = head_spec
        k_sds = v_sds = jax.ShapeDtypeStruct(head_shape, F32)
    return pl.pallas_call(
        functools.partial(_even_in_kernel, use_rope=use_rope),
        out_shape=(jax.ShapeDtypeStruct(head_shape, BF16), k_sds, v_sds,
                   jax.ShapeDtypeStruct((bsz, n, CONV_W), BF16)),
        grid=(bsz, n // tm),
        in_specs=in_specs,
        out_specs=(head_spec, head_spec, v_spec,
                   pl.BlockSpec((None, tm, CONV_W), lambda b, j: (b, j, 0))),
        scratch_shapes=[pltpu.VMEM((tm + 2 * HALO, CONV_W), F32)],
        compiler_params=_params(2),
        name="even_in_rope" if use_rope else "even_in",
    )(*args)


def _diff_lambda(lam_ref, lam_init):
    lp = lam_ref[...]
    return (jnp.exp(jnp.sum(lp[0:1] * lp[1:2], axis=-1, keepdims=True))
            - jnp.exp(jnp.sum(lp[2:3] * lp[3:4], axis=-1, keepdims=True)) + lam_init)


def _stack_components(q):
    lane = lax.broadcasted_iota(jnp.int32, (1, HEAD), 1)
    zero = jnp.zeros_like(q)
    return jnp.concatenate([jnp.where(lane < HALF_HEAD, q, zero),
                            jnp.where(lane >= HALF_HEAD, q, zero)], axis=0)


def _softmax_pv(s, v_ext):
    e = jnp.exp2(s - jnp.max(s, axis=-1, keepdims=True)).astype(BF16)
    return _dot(e, v_ext)


def _normalise(ov, lam, sg, lam_init):
    t = ov.shape[0] // 2
    o = ov[:t, :HEAD] / ov[:t, HEAD:] - lam * (ov[t:, :HEAD] / ov[t:, HEAD:])
    return (_rms(o, sg) * (1.0 - lam_init)).astype(BF16)


def _chain_pipeline(n_groups, width, scores_fn, pv_fn, finish_fn, s_ref, ov_ref):
    assert n_groups % 2 == 0 and n_groups >= 2

    def scores(g, par):
        for u in range(width):
            s_ref[par * width + u] = scores_fn(g, u)

    def values(g, par):
        for u in range(width):
            ov_ref[par * width + u] = pv_fn(g, u, s_ref[par * width + u])

    def finish(g, par):
        for u in range(width):
            finish_fn(g, u, ov_ref[par * width + u])

    scores(0, 0)
    scores(1, 1)
    values(0, 0)

    def body(t, carry):
        g = 2 * t
        scores(g, 0)
        values(g - 1, 1)
        finish(g - 2, 0)
        scores(g + 1, 1)
        values(g, 0)
        finish(g - 1, 1)
        return carry

    lax.fori_loop(1, n_groups // 2, body, 0)
    values(n_groups - 1, 1)
    finish(n_groups - 2, 0)
    finish(n_groups - 1, 1)


def _attn_prompt_kernel(lam_ref, sg_ref, q_ref, k_ref, v_ref, o_ref, s_ref, ov_ref, *, lam_init):
    lam = _diff_lambda(lam_ref, lam_init)
    sg = sg_ref[...]
    n = k_ref.shape[2]
    ones = jnp.ones((n, MXU_N - HEAD), BF16)

    def scores_fn(b, hh):
        return lax.dot_general(_stack_components(q_ref[b, hh]), k_ref[b, hh].astype(BF16),
                               (((1,), (1,)), ((), ())), preferred_element_type=F32)

    def pv_fn(b, hh, s):
        return _softmax_pv(s, jnp.concatenate([v_ref[b, hh].astype(BF16), ones], axis=1))

    def finish_fn(b, hh, ov):
        o_ref[b, :, HEAD * hh:HEAD * (hh + 1)] = _normalise(ov, lam, sg, lam_init)

    _chain_pipeline(q_ref.shape[0], N_HEADS, scores_fn, pv_fn, finish_fn, s_ref, ov_ref)


def _attn_prompt_call(q, k, v, lam_params, subln_g, lam_init, nb):
    bsz, nh, n, hd = q.shape
    spec = pl.BlockSpec((nb, nh, n, hd), lambda b: (b, 0, 0, 0))
    return pl.pallas_call(
        functools.partial(_attn_prompt_kernel, lam_init=lam_init),
        out_shape=jax.ShapeDtypeStruct((bsz, n, nh * hd), BF16),
        grid=(bsz // nb,),
        in_specs=[_resident(lam_params.shape), _resident(subln_g.shape), spec, spec, spec],
        out_specs=pl.BlockSpec((nb, n, nh * hd), lambda b: (b, 0, 0)),
        scratch_shapes=[pltpu.VMEM((2 * nh, 2 * n, n), F32), pltpu.VMEM((2 * nh, 2 * n, MXU_N), F32)],
        compiler_params=_params(1),
        name="attn",
    )(lam_params, subln_g, q, k, v)


POST_BLOCK = 256
ONES_ROWS = 16
ATTN_WIDTH = 2


def _attn_cache_kernel(lam_ref, sg_ref, q_ref, k_ref, vt_ref, ck_ref, cv_ref, o_ref,
                       kbuf_ref, vtbuf_ref, s_ref, ov_ref, *, lam_init):
    past = ck_ref.shape[0]
    lk = kbuf_ref.shape[0]
    kbuf_ref[0:past, :] = ck_ref[...].astype(BF16)
    kbuf_ref[past:, :] = k_ref[...]
    vtbuf_ref[0:HEAD, 0:past] = cv_ref[...].T.astype(BF16)
    vtbuf_ref[0:HEAD, past:] = vt_ref[...]
    vtbuf_ref[HEAD:, :] = jnp.ones((ONES_ROWS, lk), BF16)
    lam = _diff_lambda(lam_ref, lam_init)
    sg = sg_ref[...] * (1.0 - lam_init)

    def rows(g, u):
        start = (g * ATTN_WIDTH + u) * ATTN_SUB
        return pl.ds(pl.multiple_of(start, ATTN_SUB), ATTN_SUB)

    def scores_fn(g, u):
        return lax.dot_general(kbuf_ref[...], _stack_components(q_ref[rows(g, u), :]),
                               (((1,), (1,)), ((), ())), preferred_element_type=F32)

    def pv_fn(g, u, s):
        e = jnp.exp2(s - jnp.max(s, axis=0, keepdims=True)).astype(BF16)
        return _dot(vtbuf_ref[...], e)

    def finish_fn(g, u, ov):
        o_t = (ov[0:HEAD, 0:ATTN_SUB] / ov[HEAD:HEAD + 1, 0:ATTN_SUB]
               - lam * (ov[0:HEAD, ATTN_SUB:] / ov[HEAD:HEAD + 1, ATTN_SUB:]))
        ms = jnp.mean(o_t * o_t, axis=0, keepdims=True)
        o_ref[rows(g, u), :] = ((o_t * lax.rsqrt(ms + EPS)).T * sg).astype(BF16)

    _chain_pipeline(q_ref.shape[0] // (ATTN_SUB * ATTN_WIDTH), ATTN_WIDTH,
                    scores_fn, pv_fn, finish_fn, s_ref, ov_ref)


def _attn_cache_call(q, k, vt, lam_params, subln_g, cache_k, cache_v, layer, lam_init):
    bsz, nh, n, hd = q.shape
    past = cache_k.shape[3]
    seq_spec = pl.BlockSpec((None, None, n, hd), lambda b, h: (b, h, 0, 0))
    vt_spec = pl.BlockSpec((None, None, hd, n), lambda b, h: (b, h, 0, 0))
    c_spec = pl.BlockSpec((None, None, None, past, hd), lambda b, h: (b, layer, h, 0, 0))
    return pl.pallas_call(
        functools.partial(_attn_cache_kernel, lam_init=lam_init),
        out_shape=jax.ShapeDtypeStruct((bsz, n, nh * hd), BF16),
        grid=(bsz, nh),
        in_specs=[_resident(lam_params.shape), _resident(subln_g.shape),
                  seq_spec, seq_spec, vt_spec, c_spec, c_spec],
        out_specs=pl.BlockSpec((None, n, hd), lambda b, h: (b, 0, h)),
        scratch_shapes=[pltpu.VMEM((past + n, hd), BF16), pltpu.VMEM((hd + ONES_ROWS, past + n), BF16),
                        pltpu.VMEM((2 * ATTN_WIDTH, past + n, 2 * ATTN_SUB), F32),
                        pltpu.VMEM((2 * ATTN_WIDTH, hd + ONES_ROWS, 2 * ATTN_SUB), F32)],
        compiler_params=_params(2),
        name="attn_cache",
    )(lam_params, subln_g, q, k, vt, cache_k, cache_v)


def _post_kernel(x_ref, a_ref, b_ref, mod_ref, g_ref, wo_ref, wg_ref, wu_ref, wd_ref, o_ref):
    half = a_ref.shape[1]
    nb = x_ref.shape[0] // POST_BLOCK
    blocks = [slice(i * POST_BLOCK, (i + 1) * POST_BLOCK) for i in range(nb)]

    def out_proj(rows):
        return _dot(a_ref[rows, :], wo_ref[0:half, :]) + _dot(b_ref[rows, :], wo_ref[half:, :])

    def norms(rows, y):
        x1 = x_ref[rows, :] + mod_ref[2:3, :] * _rms(y, g_ref[1:2, :])
        return x1, _modulate(x1, g_ref[2:3, :], mod_ref[3:4, :], mod_ref[4:5, :]).astype(BF16)

    def gate_up(h):
        return _dot(h, wg_ref[...]), _dot(h, wu_ref[...])

    def down(gu):
        return _dot((_silu(gu[0]) * gu[1]).astype(BF16), wd_ref[...])

    def finish(rows, x1, f):
        o_ref[rows, :] = x1 + mod_ref[5:6, :] * _rms(f, g_ref[3:4, :])

    y = {0: out_proj(blocks[0])}
    x1, gu = {}, {}
    for i in range(nb + 1):
        if i + 1 < nb:
            y[i + 1] = out_proj(blocks[i + 1])
        if i < nb:
            x1[i], h = norms(blocks[i], y.pop(i))
        if i >= 1:
            f = down(gu.pop(i - 1))
        if i < nb:
            gu[i] = gate_up(h)
        if i >= 1:
            finish(blocks[i - 1], x1.pop(i - 1), f)


def _post_call(x, a, b, mod, g, w_out, w_gate, w_up, w_down, layer, tm, mod_base, mod_stride):
    bsz, n, d = x.shape
    x_spec, _, _, mod_spec = _tile_specs(n, tm, mod_base, mod_stride)
    half_spec = pl.BlockSpec((None, tm, a.shape[2]), lambda b_, j: (b_, j, 0))

    def layer_resident(w):
        return pl.BlockSpec((None,) + w.shape[1:], lambda *_: (layer, 0, 0), pipeline_mode=pl.Buffered(1))

    return pl.pallas_call(
        _post_kernel,
        out_shape=jax.ShapeDtypeStruct(x.shape, F32),
        grid=(bsz, n // tm),
        in_specs=[x_spec, half_spec, half_spec, mod_spec, _resident(g.shape),
                  _resident(w_out.shape), layer_resident(w_gate), layer_resident(w_up),
                  layer_resident(w_down)],
        out_specs=x_spec,
        compiler_params=_params(2),
        name="post",
    )(x, a, b, mod, g, w_out, w_gate, w_up, w_down)


def _odd_in_kernel(x_ref, xp_ref, xn_ref, mod_ref, g_ref, w_ref, wp_ref, ps_ref, cs_ref,
                   pc_ref, xc_ref, xs_ref, us_ref, f2_ref, f4_ref, *, n_seq):
    j = pl.program_id(1)
    nt = pl.num_programs(1)
    tm = x_ref.shape[0]
    g = g_ref[0:1, :]
    shift = mod_ref[0:1, :]
    scale = mod_ref[1:2, :]
    h = _modulate(x_ref[...], g, shift, scale).astype(BF16)
    up = _dot(h, w_ref[:, 0:POOL_W])
    uf = _dot(h, w_ref[:, POOL_W:])
    xh = jnp.concatenate([xp_ref[...], xn_ref[...]], axis=0)
    hh = _modulate(xh, g, shift, scale).astype(BF16)
    uph = _dot(hh, w_ref[:, 0:POOL_W])
    rows = tm + 2 * HALO
    us_ref[0:HALO, :] = jnp.where(j > 0, uph[0:HALO], 0.0)
    us_ref[HALO:HALO + tm, :] = up
    us_ref[HALO + tm:rows, :] = jnp.where(j < nt - 1, uph[HALO:], 0.0)
    us_ref[rows:, :] = jnp.zeros((HALO, POOL_W), F32)
    f2_ref[0:rows, :] = us_ref[0:rows, GROUP:] + us_ref[1:rows + 1, GROUP:]
    f2_ref[rows:, :] = jnp.zeros((HALO, POOL_W - GROUP), F32)
    f4_ref[0:rows, :] = f2_ref[0:rows, GROUP:] + f2_ref[2:rows + 2, GROUP:]
    f4_ref[rows:, :] = jnp.zeros((HALO, POOL_W - 2 * GROUP), F32)
    f8 = f4_ref[0:rows, GROUP:] + f4_ref[4:rows + 4, GROUP:]
    sums = (us_ref[HALO - 1:HALO - 1 + tm, 0:GROUP] + up[:, 0:GROUP],
            f2_ref[HALO - 2:HALO - 2 + tm, 0:GROUP] + f2_ref[HALO:HALO + tm, 0:GROUP],
            f4_ref[HALO - 4:HALO - 4 + tm, 0:GROUP] + f4_ref[HALO:HALO + tm, 0:GROUP],
            f8[0:tm] + f8[HALO:HALO + tm])

    t = (j * tm + lax.broadcasted_iota(jnp.int32, (tm, 1), 0)).astype(F32)
    for gi, win in enumerate(POOL_WINDOWS):
        lanes = slice(GROUP * gi, GROUP * (gi + 1))
        cnt = jnp.minimum(t + float(win // 2), float(n_seq)) - jnp.maximum(t - float(win // 2), 0.0)
        diff = (sums[gi] / cnt - up[:, lanes]).astype(BF16)
        pc_ref[:, lanes] = (_dot(diff, wp_ref[gi]) * ps_ref[0:1, lanes]).astype(BF16)
        cs = _dot(uf[:, lanes].astype(BF16), cs_ref[...])
        xc_ref[:, lanes] = cs[:, 0:GROUP].astype(BF16)
        xs_ref[:, lanes] = cs[:, GROUP:].astype(BF16)


def _odd_in_call(x, mod, g, w_in, w_pool, pool_scale, cs_mat, tm, mod_base, mod_stride):
    bsz, n, d = x.shape
    x_spec, prev_spec, next_spec, mod_spec = _tile_specs(n, tm, mod_base, mod_stride)
    out_spec = pl.BlockSpec((None, tm, POOL_W), lambda b, j: (b, j, 0))
    out_sds = jax.ShapeDtypeStruct((bsz, n, POOL_W), BF16)
    return pl.pallas_call(
        functools.partial(_odd_in_kernel, n_seq=n),
        out_shape=(out_sds, out_sds, out_sds),
        grid=(bsz, n // tm),
        in_specs=[x_spec, prev_spec, next_spec, mod_spec, _resident(g.shape), _resident(w_in.shape),
                  _resident(w_pool.shape), _resident(pool_scale.shape), _resident(cs_mat.shape)],
        out_specs=(out_spec, out_spec, out_spec),
        scratch_shapes=[pltpu.VMEM((tm + 3 * HALO, POOL_W), F32),
                        pltpu.VMEM((tm + 3 * HALO, POOL_W - GROUP), F32),
                        pltpu.VMEM((tm + 3 * HALO, POOL_W - 2 * GROUP), F32)],
        compiler_params=_params(2),
        name="odd_in",
    )(x, x, x, mod, g, w_in, w_pool, pool_scale, cs_mat)


def _four_kernel(c_ref, s_ref, xc_ref, xs_ref, wf_ref, o_ref, *, scale):
    for b in range(xc_ref.shape[0]):
        y = _dot(c_ref[...], xc_ref[b]) - _dot(s_ref[...], xs_ref[b])
        four = (y * scale).astype(BF16)
        for gi in range(FOURIER_W // GROUP):
            lanes = slice(GROUP * gi, GROUP * (gi + 1))
            o_ref[b, :, lanes] = _dot(four[:, lanes], wf_ref[gi]).astype(BF16)


def _four_call(cn, sn, xc, xs, w_four, tm, nb):
    bsz, n, w = xc.shape
    mat_spec = pl.BlockSpec((tm, n), lambda b, j: (j, 0))
    seq_spec = pl.BlockSpec((nb, n, w), lambda b, j: (b, 0, 0))
    return pl.pallas_call(
        functools.partial(_four_kernel, scale=float(1.0 / math.sqrt(n * GROUP))),
        out_shape=jax.ShapeDtypeStruct((bsz, n, w), BF16),
        grid=(bsz // nb, n // tm),
        in_specs=[mat_spec, mat_spec, seq_spec, seq_spec, _resident(w_four.shape)],
        out_specs=pl.BlockSpec((nb, tm, w), lambda b, j: (b, j, 0)),
        compiler_params=_params(2),
        name="fourier",
    )(cn, sn, xc, xs, w_four)


def _rope_tables(n_tok):
    rows = n_tok // GRID_W
    row = np.repeat(np.arange(rows), GRID_W).astype(np.float64)
    col = np.tile(np.arange(GRID_W), rows).astype(np.float64)
    inv = ROPE_BASE ** (-np.arange(0, ROPE_AXIS, 2, dtype=np.float64) / ROPE_AXIS)
    ang_r = row[:, None] * inv[None, :]
    ang_c = col[:, None] * inv[None, :]
    ang = np.concatenate([ang_r, ang_r, ang_c, ang_c], axis=-1)
    cos = np.concatenate([np.cos(ang)] * 2, axis=-1)
    sin = np.concatenate([np.sin(ang)] * 2, axis=-1)
    first_half = (np.arange(HEAD) % 32) < 16
    sin_signed = np.where(first_half[None, :], -sin, sin)
    return jnp.asarray(cos, F32), jnp.asarray(sin_signed, F32)


def _dft_mats(n):
    idx = np.arange(n, dtype=np.int64)
    ang = 2.0 * np.pi * ((idx[:, None] * idx[None, :]) % n).astype(np.float64) / n
    return np.cos(ang), np.sin(ang)


def kernel(x_prompt, x_sample, cache_k, cache_v, c, c_ctx, w_mod, b_mod, norm_g,
           w_in_even, lam_params, subln_g, conv_w, w_out_even,
           w_in_odd, w_pool, pool_scale, w_fourier, w_out_odd,
           w_gate, w_up, w_down):
    depth = w_mod.shape[0]
    n_dec = x_sample.shape[0]
    n_p, n_s = x_prompt.shape[1], x_sample.shape[1]
    tm_p, tm_s = n_p, 512

    pad_rows = 16 - 1 - n_dec
    cc = jnp.concatenate([c_ctx[None, :], c, jnp.zeros((pad_rows, D_MODEL), F32)], axis=0)
    mod_all = _mod_call(cc, w_mod, b_mod)[:, :1 + n_dec].reshape(depth, 1 + n_dec, 6, D_MODEL)

    rope = _rope_tables(n_s)
    cc_g, sc_g = _dft_mats(GROUP)
    cs_mat = jnp.asarray(np.concatenate([cc_g, sc_g], axis=1), F32).astype(BF16)
    dft = {n: tuple(jnp.asarray(m, F32).astype(BF16) for m in _dft_mats(n)) for n in (n_p, n_s)}

    wg, wu, wd = w_gate.astype(BF16), w_up.astype(BF16), w_down.astype(BF16)
    xp, xs = x_prompt, x_sample
    new_k, new_v = [], []
    for l in range(depth):
        mod = mod_all[l]
        g = norm_g[l]
        i = l // 2
        streams = []
        if l % 2 == 0:
            lam_init = 0.8 - 0.6 * math.exp(-0.3 * l)
            w_in = w_in_even[i].astype(BF16)
            w_out = w_out_even[i].astype(BF16)
            sg = subln_g[i][None, :]
            qp, kp, vp, cbp = _even_in_call(xp, mod, g, w_in, conv_w[i], None, tm_p, 0, 0)
            ap = _attn_prompt_call(qp, kp, vp, lam_params[i], sg, lam_init, 8)
            new_k.append(kp)
            new_v.append(vp)
            qs, ks, vts, cbs = _even_in_call(xs, mod, g, w_in, conv_w[i], rope, tm_s, 1, 1)
            a_s = _attn_cache_call(qs, ks, vts, lam_params[i], sg, cache_k, cache_v, i, lam_init)
            streams = [(ap, cbp), (a_s, cbs)]
        else:
            w_in = w_in_odd[i].astype(BF16)
            w_out = w_out_odd[i].astype(BF16)
            wp = w_pool[i].astype(BF16)
            wf = w_fourier[i].astype(BF16)
            ps = pool_scale[i][None, :]
            pcp, xcp, xsp = _odd_in_call(xp, mod, g, w_in, wp, ps, cs_mat, tm_p, 0, 0)
            fcp = _four_call(*dft[n_p], xcp, xsp, wf, n_p, 8)
            pcs, xcs, xss = _odd_in_call(xs, mod, g, w_in, wp, ps, cs_mat, tm_s, 1, 1)
            fcs = _four_call(*dft[n_s], xcs, xss, wf, 512, 1)
            streams = [(pcp, fcp), (pcs, fcs)]
        xp = _post_call(*(t.reshape(1, -1, t.shape[-1]) for t in (xp,) + streams[0]),
                        mod, g, w_out, wg, wu, wd, l, 1024, 0, 0).reshape(x_prompt.shape)
        xs = _post_call(xs, streams[1][0], streams[1][1], mod, g, w_out, wg, wu, wd, l, 1024, 1, 1)
    def stack_layers(parts):
        if len(parts) == 1:
            return parts[0][:, None]
        return jnp.stack(parts, axis=1)

    return xp, xs, stack_layers(new_k), stack_layers(new_v)
```

```python
import functools
import math

import numpy as np
import jax
import jax.numpy as jnp
from jax import lax
from jax.experimental import pallas as pl
from jax.experimental.pallas import tpu as pltpu

F32 = jnp.float32
BF16 = jnp.bfloat16

D_MODEL = 1024
GRID_W = 64
N_HEADS = 4
HEAD = 128
HALF_HEAD = 64
ROPE_AXIS = 32
ROPE_BASE = 10000.0
ATTN_W = 512
CONV_W = 512
POOL_W = 512
FOURIER_W = 512
GROUP = 128
POOL_WINDOWS = (2, 4, 8, 16)
D_FF = 2816
EPS = 1e-6
LOG2E = math.log2(math.e)
HALO = 8
MXU_N = 256
ATTN_SUB = 128
VMEM_LIMIT = 56 * 1024 * 1024


def _params(n_axes):
    return pltpu.CompilerParams(dimension_semantics=("arbitrary",) * n_axes,
                                vmem_limit_bytes=VMEM_LIMIT)


def _resident(shape):
    return pl.BlockSpec(shape, lambda *_: (0,) * len(shape), pipeline_mode=pl.Buffered(1))


def _rms(x, g):
    ms = jnp.mean(x * x, axis=-1, keepdims=True)
    return x * lax.rsqrt(ms + EPS) * g


def _modulate(x, g, shift, scale):
    return _rms(x, g) * (1.0 + scale) + shift


def _dot(a, b):
    return jnp.dot(a, b, preferred_element_type=F32)


def _silu(x):
    return x / (1.0 + jnp.exp(-x))


def _mod_kernel(cc_ref, w_ref, b_ref, o_ref):
    s = _silu(cc_ref[...]).astype(BF16)
    o_ref[...] = _dot(s, w_ref[...].astype(BF16)) + b_ref[...]


def _mod_call(cc, w_mod, b_mod):
    depth, d, n6 = w_mod.shape
    rows = cc.shape[0]
    tn = 2048
    return pl.pallas_call(
        _mod_kernel,
        out_shape=jax.ShapeDtypeStruct((depth, rows, n6), F32),
        grid=(depth, n6 // tn),
        in_specs=[
            pl.BlockSpec((rows, d), lambda l, j: (0, 0)),
            pl.BlockSpec((None, d, tn), lambda l, j: (l, 0, j)),
            pl.BlockSpec((None, 1, tn), lambda l, j: (l, 0, j)),
        ],
        out_specs=pl.BlockSpec((None, rows, tn), lambda l, j: (l, 0, j)),
        compiler_params=_params(2),
        name="mod",
    )(cc, w_mod, b_mod.reshape(depth, 1, n6))


def _tile_specs(n, tm, mod_base, mod_stride):
    nb8 = n // HALO
    t8 = tm // HALO
    x_spec = pl.BlockSpec((None, tm, D_MODEL), lambda b, j: (b, j, 0))
    prev_spec = pl.BlockSpec((None, HALO, D_MODEL),
                             lambda b, j: (b, jnp.maximum(j * t8 - 1, 0), 0))
    next_spec = pl.BlockSpec((None, HALO, D_MODEL),
                             lambda b, j: (b, jnp.minimum((j + 1) * t8, nb8 - 1), 0))
    mod_spec = pl.BlockSpec((None, 6, D_MODEL),
                            lambda b, j: (mod_base + mod_stride * b, 0, 0))
    return x_spec, prev_spec, next_spec, mod_spec


def _rope(t, cos, sin_signed, first_half):
    outs = []
    for hh in range(N_HEADS):
        th = t[:, HEAD * hh:HEAD * (hh + 1)]
        swapped = jnp.where(first_half, pltpu.roll(th, HEAD - 16, 1), pltpu.roll(th, 16, 1))
        outs.append(th * cos + swapped * sin_signed)
    return outs


def _even_in_kernel(*refs, use_rope):
    if use_rope:
        (x_ref, xp_ref, xn_ref, mod_ref, g_ref, w_ref, cw_ref, cos_ref, sin_ref,
         q_ref, k_ref, v_ref, cb_ref, zs_ref) = refs
    else:
        (x_ref, xp_ref, xn_ref, mod_ref, g_ref, w_ref, cw_ref,
         q_ref, k_ref, v_ref, cb_ref, zs_ref) = refs
    j = pl.program_id(1)
    nt = pl.num_programs(1)
    tm = x_ref.shape[0]
    g = g_ref[0:1, :]
    shift = mod_ref[0:1, :]
    scale = mod_ref[1:2, :]
    h = _modulate(x_ref[...], g, shift, scale).astype(BF16)

    def proj(hh, lo):
        return _dot(hh, w_ref[:, lo:lo + 512])

    q = proj(h, 0) * (HALF_HEAD ** -0.5 * LOG2E)
    k = proj(h, 512)
    v = proj(h, 1024)
    if use_rope:
        lane = lax.broadcasted_iota(jnp.int32, (1, HEAD), 1)
        first_half = (lane % 32) < 16
        cos = cos_ref[...]
        sin_signed = sin_ref[...]
        qs = _rope(q, cos, sin_signed, first_half)
        ks = _rope(k, cos, sin_signed, first_half)
    else:
        qs = [q[:, HEAD * hh:HEAD * (hh + 1)] for hh in range(N_HEADS)]
        ks = [k[:, HEAD * hh:HEAD * (hh + 1)] for hh in range(N_HEADS)]
    for hh in range(N_HEADS):
        vh = v[:, HEAD * hh:HEAD * (hh + 1)]
        q_ref[hh] = qs[hh].astype(q_ref.dtype)
        k_ref[hh] = ks[hh].astype(k_ref.dtype)
        v_ref[hh] = (vh.T if use_rope else vh).astype(v_ref.dtype)

    gate_b = proj(h, 1536)
    z = proj(h, 2048) * proj(h, 2560)
    xh = jnp.concatenate([xp_ref[...], xn_ref[...]], axis=0)
    hh_ = _modulate(xh, g, shift, scale).astype(BF16)
    zh = proj(hh_, 2048) * proj(hh_, 2560)
    zs_ref[0:HALO, :] = jnp.where(j > 0, zh[0:HALO], 0.0)
    zs_ref[HALO:HALO + tm, :] = z
    zs_ref[HALO + tm:, :] = jnp.where(j < nt - 1, zh[HALO:], 0.0)
    conv = (cw_ref[0:1, :] * zs_ref[HALO - 1:HALO - 1 + tm, :] + cw_ref[1:2, :] * z
            + cw_ref[2:3, :] * zs_ref[HALO + 1:HALO + 1 + tm, :])
    cb_ref[...] = (gate_b * conv).astype(BF16)


def _even_in_call(x, mod, g, w_in, conv_w, rope, tm, mod_base, mod_stride):
    bsz, n, d = x.shape
    use_rope = rope is not None
    x_spec, prev_spec, next_spec, mod_spec = _tile_specs(n, tm, mod_base, mod_stride)
    in_specs = [x_spec, prev_spec, next_spec, mod_spec,
                _resident(g.shape), _resident(w_in.shape), _resident(conv_w.shape)]
    args = [x, x, x, mod, g, w_in, conv_w]
    if use_rope:
        tab = pl.BlockSpec((tm, HEAD), lambda b, j: (j, 0))
        in_specs += [tab, tab]
        args += list(rope)
    head_spec = pl.BlockSpec((None, N_HEADS, tm, HEAD), lambda b, j: (b, 0, j, 0))
    head_shape = (bsz, N_HEADS, n, HEAD)
    if use_rope:
        v_spec = pl.BlockSpec((None, N_HEADS, HEAD, tm), lambda b, j: (b, 0, 0, j))
        k_sds = jax.ShapeDtypeStruct(head_shape, BF16)
        v_sds = jax.ShapeDtypeStruct((bsz, N_HEADS, HEAD, n), BF16)
    else:
        v_spec = head_spec
        k_sds = v_sds = jax.ShapeDtypeStruct(head_shape, F32)
    return pl.pallas_call(
        functools.partial(_even_in_kernel, use_rope=use_rope),
        out_shape=(jax.ShapeDtypeStruct(head_shape, BF16), k_sds, v_sds,
                   jax.ShapeDtypeStruct((bsz, n, CONV_W), BF16)),
        grid=(bsz, n // tm),
        in_specs=in_specs,
        out_specs=(head_spec, head_spec, v_spec,
                   pl.BlockSpec((None, tm, CONV_W), lambda b, j: (b, j, 0))),
        scratch_shapes=[pltpu.VMEM((tm + 2 * HALO, CONV_W), F32)],
        compiler_params=_params(2),
        name="even_in_rope" if use_rope else "even_in",
    )(*args)


def _diff_lambda(lam_ref, lam_init):
    lp = lam_ref[...]
    return (jnp.exp(jnp.sum(lp[0:1] * lp[1:2], axis=-1, keepdims=True))
            - jnp.exp(jnp.sum(lp[2:3] * lp[3:4], axis=-1, keepdims=True)) + lam_init)


def _stack_components(q):
    lane = lax.broadcasted_iota(jnp.int32, (1, HEAD), 1)
    zero = jnp.zeros_like(q)
    return jnp.concatenate([jnp.where(lane < HALF_HEAD, q, zero),
                            jnp.where(lane >= HALF_HEAD, q, zero)], axis=0)


def _softmax_pv(s, v_ext):
    e = jnp.exp2(s - jnp.max(s, axis=-1, keepdims=True)).astype(BF16)
    return _dot(e, v_ext)


def _normalise(ov, lam, sg, lam_init):
    t = ov.shape[0] // 2
    o = ov[:t, :HEAD] / ov[:t, HEAD:] - lam * (ov[t:, :HEAD] / ov[t:, HEAD:])
    return (_rms(o, sg) * (1.0 - lam_init)).astype(BF16)


def _chain_pipeline(n_groups, width, scores_fn, pv_fn, finish_fn, s_ref, ov_ref):
    assert n_groups % 2 == 0 and n_groups >= 2

    def scores(g, par):
        for u in range(width):
            s_ref[par * width + u] = scores_fn(g, u, par * width + u)

    def values(g, par):
        for u in range(width):
            ov_ref[par * width + u] = pv_fn(g, u, par * width + u, s_ref[par * width + u])

    def finish(g, par):
        for u in range(width):
            finish_fn(g, u, ov_ref[par * width + u])

    scores(0, 0)
    scores(1, 1)
    values(0, 0)

    def body(t, carry):
        g = 2 * t
        scores(g, 0)
        values(g - 1, 1)
        finish(g - 2, 0)
        scores(g + 1, 1)
        values(g, 0)
        finish(g - 1, 1)
        return carry

    lax.fori_loop(1, n_groups // 2, body, 0)
    values(n_groups - 1, 1)
    finish(n_groups - 2, 0)
    finish(n_groups - 1, 1)


def _attn_prompt_kernel(lam_ref, sg_ref, q_ref, k_ref, v_ref, o_ref, s_ref, ov_ref, *, lam_init):
    lam = _diff_lambda(lam_ref, lam_init)
    sg = sg_ref[...]
    n = k_ref.shape[2]
    ones = jnp.ones((n, MXU_N - HEAD), BF16)

    def scores_fn(b, hh, slot):
        return lax.dot_general(_stack_components(q_ref[b, hh]), k_ref[b, hh].astype(BF16),
                               (((1,), (1,)), ((), ())), preferred_element_type=F32)

    def pv_fn(b, hh, slot, s):
        return _softmax_pv(s, jnp.concatenate([v_ref[b, hh].astype(BF16), ones], axis=1))

    def finish_fn(b, hh, ov):
        o_ref[b, :, HEAD * hh:HEAD * (hh + 1)] = _normalise(ov, lam, sg, lam_init)

    _chain_pipeline(q_ref.shape[0], N_HEADS, scores_fn, pv_fn, finish_fn, s_ref, ov_ref)


def _attn_prompt_call(q, k, v, lam_params, subln_g, lam_init, nb):
    bsz, nh, n, hd = q.shape
    spec = pl.BlockSpec((nb, nh, n, hd), lambda b: (b, 0, 0, 0))
    return pl.pallas_call(
        functools.partial(_attn_prompt_kernel, lam_init=lam_init),
        out_shape=jax.ShapeDtypeStruct((bsz, n, nh * hd), BF16),
        grid=(bsz // nb,),
        in_specs=[_resident(lam_params.shape), _resident(subln_g.shape), spec, spec, spec],
        out_specs=pl.BlockSpec((nb, n, nh * hd), lambda b: (b, 0, 0)),
        scratch_shapes=[pltpu.VMEM((2 * nh, 2 * n, n), F32), pltpu.VMEM((2 * nh, 2 * n, MXU_N), F32)],
        compiler_params=_params(1),
        name="attn",
    )(lam_params, subln_g, q, k, v)


POST_BLOCK = 256
ONES_ROWS = 16
ATTN_WIDTH = 2


def _attn_cache_kernel(lam_ref, sg_ref, q_ref, k_ref, vt_ref, ck_ref, cv_ref, o_ref,
                       kbuf_ref, vtbuf_ref, s_ref, ov_ref, m_ref, *, lam_init):
    past = ck_ref.shape[0]
    lk = kbuf_ref.shape[0]
    kbuf_ref[0:past, :] = ck_ref[...].astype(BF16)
    kbuf_ref[past:, :] = k_ref[...]
    vtbuf_ref[0:HEAD, 0:past] = cv_ref[...].T.astype(BF16)
    vtbuf_ref[0:HEAD, past:] = vt_ref[...]
    vtbuf_ref[HEAD:, :] = jnp.ones((ONES_ROWS, lk), BF16)
    lam = _diff_lambda(lam_ref, lam_init)
    sg = sg_ref[...] * (1.0 - lam_init)

    def rows(g, u):
        start = (g * ATTN_WIDTH + u) * ATTN_SUB
        return pl.ds(pl.multiple_of(start, ATTN_SUB), ATTN_SUB)

    def scores_fn(g, u, slot):
        s = lax.dot_general(kbuf_ref[...], _stack_components(q_ref[rows(g, u), :]),
                            (((1,), (1,)), ((), ())), preferred_element_type=F32)
        m_ref[slot] = jnp.max(s, axis=0, keepdims=True)
        return s

    def pv_fn(g, u, slot, s):
        e = jnp.exp2(s - m_ref[slot]).astype(BF16)
        return _dot(vtbuf_ref[...], e)

    def finish_fn(g, u, ov):
        o_t = (ov[0:HEAD, 0:ATTN_SUB] / ov[HEAD:HEAD + 1, 0:ATTN_SUB]
               - lam * (ov[0:HEAD, ATTN_SUB:] / ov[HEAD:HEAD + 1, ATTN_SUB:]))
        ms = jnp.mean(o_t * o_t, axis=0, keepdims=True)
        o_ref[rows(g, u), :] = ((o_t * lax.rsqrt(ms + EPS)).T * sg).astype(BF16)

    _chain_pipeline(q_ref.shape[0] // (ATTN_SUB * ATTN_WIDTH), ATTN_WIDTH,
                    scores_fn, pv_fn, finish_fn, s_ref, ov_ref)


def _attn_cache_call(q, k, vt, lam_params, subln_g, cache_k, cache_v, layer, lam_init):
    bsz, nh, n, hd = q.shape
    past = cache_k.shape[3]
    seq_spec = pl.BlockSpec((None, None, n, hd), lambda b, h: (b, h, 0, 0))
    vt_spec = pl.BlockSpec((None, None, hd, n), lambda b, h: (b, h, 0, 0))
    c_spec = pl.BlockSpec((None, None, None, past, hd), lambda b, h: (b, layer, h, 0, 0))
    return pl.pallas_call(
        functools.partial(_attn_cache_kernel, lam_init=lam_init),
        out_shape=jax.ShapeDtypeStruct((bsz, n, nh * hd), BF16),
        grid=(bsz, nh),
        in_specs=[_resident(lam_params.shape), _resident(subln_g.shape),
                  seq_spec, seq_spec, vt_spec, c_spec, c_spec],
        out_specs=pl.BlockSpec((None, n, hd), lambda b, h: (b, 0, h)),
        scratch_shapes=[pltpu.VMEM((past + n, hd), BF16), pltpu.VMEM((hd + ONES_ROWS, past + n), BF16),
                        pltpu.VMEM((2 * ATTN_WIDTH, past + n, 2 * ATTN_SUB), F32),
                        pltpu.VMEM((2 * ATTN_WIDTH, hd + ONES_ROWS, 2 * ATTN_SUB), F32),
                        pltpu.VMEM((2 * ATTN_WIDTH, 1, 2 * ATTN_SUB), F32)],
        compiler_params=_params(2),
        name="attn_cache",
    )(lam_params, subln_g, q, k, vt, cache_k, cache_v)


def _post_kernel(x_ref, a_ref, b_ref, mod_ref, g_ref, wo_ref, wg_ref, wu_ref, wd_ref, o_ref):
    half = a_ref.shape[1]
    nb = x_ref.shape[0] // POST_BLOCK
    blocks = [slice(i * POST_BLOCK, (i + 1) * POST_BLOCK) for i in range(nb)]

    def out_proj(rows):
        return _dot(a_ref[rows, :], wo_ref[0:half, :]) + _dot(b_ref[rows, :], wo_ref[half:, :])

    def norms(rows, y):
        x1 = x_ref[rows, :] + mod_ref[2:3, :] * _rms(y, g_ref[1:2, :])
        return x1, _modulate(x1, g_ref[2:3, :], mod_ref[3:4, :], mod_ref[4:5, :]).astype(BF16)

    def gate_up(h):
        return _dot(h, wg_ref[...]), _dot(h, wu_ref[...])

    def down(gu):
        return _dot((_silu(gu[0]) * gu[1]).astype(BF16), wd_ref[...])

    def finish(rows, x1, f):
        o_ref[rows, :] = x1 + mod_ref[5:6, :] * _rms(f, g_ref[3:4, :])

    y = {0: out_proj(blocks[0])}
    x1, gu = {}, {}
    for i in range(nb + 1):
        if i + 1 < nb:
            y[i + 1] = out_proj(blocks[i + 1])
        if i < nb:
            x1[i], h = norms(blocks[i], y.pop(i))
        if i >= 1:
            f = down(gu.pop(i - 1))
        if i < nb:
            gu[i] = gate_up(h)
        if i >= 1:
            finish(blocks[i - 1], x1.pop(i - 1), f)


def _post_call(x, a, b, mod, g, w_out, w_gate, w_up, w_down, layer, tm, mod_base, mod_stride):
    bsz, n, d = x.shape
    x_spec, _, _, mod_spec = _tile_specs(n, tm, mod_base, mod_stride)
    half_spec = pl.BlockSpec((None, tm, a.shape[2]), lambda b_, j: (b_, j, 0))

    def layer_resident(w):
        return pl.BlockSpec((None,) + w.shape[1:], lambda *_: (layer, 0, 0), pipeline_mode=pl.Buffered(1))

    return pl.pallas_call(
        _post_kernel,
        out_shape=jax.ShapeDtypeStruct(x.shape, F32),
        grid=(bsz, n // tm),
        in_specs=[x_spec, half_spec, half_spec, mod_spec, _resident(g.shape),
                  _resident(w_out.shape), layer_resident(w_gate), layer_resident(w_up),
                  layer_resident(w_down)],
        out_specs=x_spec,
        compiler_params=_params(2),
        name="post",
    )(x, a, b, mod, g, w_out, w_gate, w_up, w_down)


def _odd_in_kernel(x_ref, xp_ref, xn_ref, mod_ref, g_ref, w_ref, wp_ref, ps_ref, cs_ref,
                   pc_ref, xc_ref, xs_ref, us_ref, f2_ref, f4_ref, *, n_seq):
    j = pl.program_id(1)
    nt = pl.num_programs(1)
    tm = x_ref.shape[0]
    g = g_ref[0:1, :]
    shift = mod_ref[0:1, :]
    scale = mod_ref[1:2, :]
    h = _modulate(x_ref[...], g, shift, scale).astype(BF16)
    up = _dot(h, w_ref[:, 0:POOL_W])
    uf = _dot(h, w_ref[:, POOL_W:])
    xh = jnp.concatenate([xp_ref[...], xn_ref[...]], axis=0)
    hh = _modulate(xh, g, shift, scale).astype(BF16)
    uph = _dot(hh, w_ref[:, 0:POOL_W])
    rows = tm + 2 * HALO
    us_ref[0:HALO, :] = jnp.where(j > 0, uph[0:HALO], 0.0)
    us_ref[HALO:HALO + tm, :] = up
    us_ref[HALO + tm:rows, :] = jnp.where(j < nt - 1, uph[HALO:], 0.0)
    us_ref[rows:, :] = jnp.zeros((HALO, POOL_W), F32)
    f2_ref[0:rows, :] = us_ref[0:rows, GROUP:] + us_ref[1:rows + 1, GROUP:]
    f2_ref[rows:, :] = jnp.zeros((HALO, POOL_W - GROUP), F32)
    f4_ref[0:rows, :] = f2_ref[0:rows, GROUP:] + f2_ref[2:rows + 2, GROUP:]
    f4_ref[rows:, :] = jnp.zeros((HALO, POOL_W - 2 * GROUP), F32)
    f8 = f4_ref[0:rows, GROUP:] + f4_ref[4:rows + 4, GROUP:]
    sums = (us_ref[HALO - 1:HALO - 1 + tm, 0:GROUP] + up[:, 0:GROUP],
            f2_ref[HALO - 2:HALO - 2 + tm, 0:GROUP] + f2_ref[HALO:HALO + tm, 0:GROUP],
            f4_ref[HALO - 4:HALO - 4 + tm, 0:GROUP] + f4_ref[HALO:HALO + tm, 0:GROUP],
            f8[0:tm] + f8[HALO:HALO + tm])

    t = (j * tm + lax.broadcasted_iota(jnp.int32, (tm, 1), 0)).astype(F32)
    for gi, win in enumerate(POOL_WINDOWS):
        lanes = slice(GROUP * gi, GROUP * (gi + 1))
        cnt = jnp.minimum(t + float(win // 2), float(n_seq)) - jnp.maximum(t - float(win // 2), 0.0)
        diff = (sums[gi] / cnt - up[:, lanes]).astype(BF16)
        pc_ref[:, lanes] = (_dot(diff, wp_ref[gi]) * ps_ref[0:1, lanes]).astype(BF16)
        cs = _dot(uf[:, lanes].astype(BF16), cs_ref[...])
        xc_ref[:, lanes] = cs[:, 0:GROUP].astype(BF16)
        xs_ref[:, lanes] = cs[:, GROUP:].astype(BF16)


def _odd_in_call(x, mod, g, w_in, w_pool, pool_scale, cs_mat, tm, mod_base, mod_stride):
    bsz, n, d = x.shape
    x_spec, prev_spec, next_spec, mod_spec = _tile_specs(n, tm, mod_base, mod_stride)
    out_spec = pl.BlockSpec((None, tm, POOL_W), lambda b, j: (b, j, 0))
    out_sds = jax.ShapeDtypeStruct((bsz, n, POOL_W), BF16)
    return pl.pallas_call(
        functools.partial(_odd_in_kernel, n_seq=n),
        out_shape=(out_sds, out_sds, out_sds),
        grid=(bsz, n // tm),
        in_specs=[x_spec, prev_spec, next_spec, mod_spec, _resident(g.shape), _resident(w_in.shape),
                  _resident(w_pool.shape), _resident(pool_scale.shape), _resident(cs_mat.shape)],
        out_specs=(out_spec, out_spec, out_spec),
        scratch_shapes=[pltpu.VMEM((tm + 3 * HALO, POOL_W), F32),
                        pltpu.VMEM((tm + 3 * HALO, POOL_W - GROUP), F32),
                        pltpu.VMEM((tm + 3 * HALO, POOL_W - 2 * GROUP), F32)],
        compiler_params=_params(2),
        name="odd_in",
    )(x, x, x, mod, g, w_in, w_pool, pool_scale, cs_mat)


def _four_kernel(c_ref, s_ref, xc_ref, xs_ref, wf_ref, o_ref, *, scale):
    for b in range(xc_ref.shape[0]):
        y = _dot(c_ref[...], xc_ref[b]) - _dot(s_ref[...], xs_ref[b])
        four = (y * scale).astype(BF16)
        for gi in range(FOURIER_W // GROUP):
            lanes = slice(GROUP * gi, GROUP * (gi + 1))
            o_ref[b, :, lanes] = _dot(four[:, lanes], wf_ref[gi]).astype(BF16)


def _four_call(cn, sn, xc, xs, w_four, tm, nb):
    bsz, n, w = xc.shape
    mat_spec = pl.BlockSpec((tm, n), lambda b, j: (j, 0))
    seq_spec = pl.BlockSpec((nb, n, w), lambda b, j: (b, 0, 0))
    return pl.pallas_call(
        functools.partial(_four_kernel, scale=float(1.0 / math.sqrt(n * GROUP))),
        out_shape=jax.ShapeDtypeStruct((bsz, n, w), BF16),
        grid=(bsz // nb, n // tm),
        in_specs=[mat_spec, mat_spec, seq_spec, seq_spec, _resident(w_four.shape)],
        out_specs=pl.BlockSpec((nb, tm, w), lambda b, j: (b, j, 0)),
        compiler_params=_params(2),
        name="fourier",
    )(cn, sn, xc, xs, w_four)


FLIP_BLOCK = 256


def _four_sym_kernel(c_ref, s_ref, pm_ref, xc_ref, xs_ref, wf_ref, o_ref, *, scale):
    half = o_ref.shape[0] // 2
    p = _dot(c_ref[...], xc_ref[...])
    q = _dot(s_ref[...], xs_ref[...])

    def project(rows, four):
        for gi in range(FOURIER_W // GROUP):
            lanes = slice(GROUP * gi, GROUP * (gi + 1))
            o_ref[rows, lanes] = _dot(four[:, lanes], wf_ref[gi]).astype(BF16)

    project(slice(0, half), ((p[0:half] - q[0:half]) * scale).astype(BF16))
    mirrored = ((p + q) * scale).astype(BF16)
    for b in range(half // FLIP_BLOCK):
        lo = half - FLIP_BLOCK * (b + 1)
        window = mirrored[lo:lo + FLIP_BLOCK + HALO, :]
        flipped = _dot(pm_ref[...], window).astype(BF16)
        project(slice(half + FLIP_BLOCK * b, half + FLIP_BLOCK * (b + 1)), flipped)


def _four_sym_call(c_half, s_half, perm, xc, xs, w_four):
    bsz, n, w = xc.shape
    seq_spec = pl.BlockSpec((None, n, w), lambda b: (b, 0, 0))
    return pl.pallas_call(
        functools.partial(_four_sym_kernel, scale=float(1.0 / math.sqrt(n * GROUP))),
        out_shape=jax.ShapeDtypeStruct((bsz, n, w), BF16),
        grid=(bsz,),
        in_specs=[_resident(c_half.shape), _resident(s_half.shape), _resident(perm.shape),
                  seq_spec, seq_spec, _resident(w_four.shape)],
        out_specs=seq_spec,
        compiler_params=_params(1),
        name="fourier_sym",
    )(c_half, s_half, perm, xc, xs, w_four)


def _flip_perm():
    pm = np.zeros((FLIP_BLOCK, FLIP_BLOCK + HALO), np.float32)
    pm[np.arange(FLIP_BLOCK), FLIP_BLOCK - np.arange(FLIP_BLOCK)] = 1.0
    return pm


def _rope_tables(n_tok):
    rows = n_tok // GRID_W
    row = np.repeat(np.arange(rows), GRID_W).astype(np.float64)
    col = np.tile(np.arange(GRID_W), rows).astype(np.float64)
    inv = ROPE_BASE ** (-np.arange(0, ROPE_AXIS, 2, dtype=np.float64) / ROPE_AXIS)
    ang_r = row[:, None] * inv[None, :]
    ang_c = col[:, None] * inv[None, :]
    ang = np.concatenate([ang_r, ang_r, ang_c, ang_c], axis=-1)
    cos = np.concatenate([np.cos(ang)] * 2, axis=-1)
    sin = np.concatenate([np.sin(ang)] * 2, axis=-1)
    first_half = (np.arange(HEAD) % 32) < 16
    sin_signed = np.where(first_half[None, :], -sin, sin)
    return jnp.asarray(cos, F32), jnp.asarray(sin_signed, F32)


def _dft_mats(n):
    idx = np.arange(n, dtype=np.int64)
    ang = 2.0 * np.pi * ((idx[:, None] * idx[None, :]) % n).astype(np.float64) / n
    return np.cos(ang), np.sin(ang)


def kernel(x_prompt, x_sample, cache_k, cache_v, c, c_ctx, w_mod, b_mod, norm_g,
           w_in_even, lam_params, subln_g, conv_w, w_out_even,
           w_in_odd, w_pool, pool_scale, w_fourier, w_out_odd,
           w_gate, w_up, w_down):
    depth = w_mod.shape[0]
    n_dec = x_sample.shape[0]
    n_p, n_s = x_prompt.shape[1], x_sample.shape[1]
    tm_p, tm_s = n_p, 512

    pad_rows = 16 - 1 - n_dec
    cc = jnp.concatenate([c_ctx[None, :], c, jnp.zeros((pad_rows, D_MODEL), F32)], axis=0)
    mod_all = _mod_call(cc, w_mod, b_mod)[:, :1 + n_dec].reshape(depth, 1 + n_dec, 6, D_MODEL)

    rope = _rope_tables(n_s)
    cc_g, sc_g = _dft_mats(GROUP)
    cs_mat = jnp.asarray(np.concatenate([cc_g, sc_g], axis=1), F32).astype(BF16)
    dft_p = tuple(jnp.asarray(m, F32).astype(BF16) for m in _dft_mats(n_p))
    dft_s = tuple(jnp.asarray(m[:n_s // 2 + HALO], F32).astype(BF16) for m in _dft_mats(n_s))
    flip = jnp.asarray(_flip_perm(), F32).astype(BF16)

    wg, wu, wd = w_gate.astype(BF16), w_up.astype(BF16), w_down.astype(BF16)
    xp, xs = x_prompt, x_sample
    new_k, new_v = [], []
    for l in range(depth):
        mod = mod_all[l]
        g = norm_g[l]
        i = l // 2
        streams = []
        if l % 2 == 0:
            lam_init = 0.8 - 0.6 * math.exp(-0.3 * l)
            w_in = w_in_even[i].astype(BF16)
            w_out = w_out_even[i].astype(BF16)
            sg = subln_g[i][None, :]
            qp, kp, vp, cbp = _even_in_call(xp, mod, g, w_in, conv_w[i], None, tm_p, 0, 0)
            ap = _attn_prompt_call(qp, kp, vp, lam_params[i], sg, lam_init, 8)
            new_k.append(kp)
            new_v.append(vp)
            qs, ks, vts, cbs = _even_in_call(xs, mod, g, w_in, conv_w[i], rope, tm_s, 1, 1)
            a_s = _attn_cache_call(qs, ks, vts, lam_params[i], sg, cache_k, cache_v, i, lam_init)
            streams = [(ap, cbp), (a_s, cbs)]
        else:
            w_in = w_in_odd[i].astype(BF16)
            w_out = w_out_odd[i].astype(BF16)
            wp = w_pool[i].astype(BF16)
            wf = w_fourier[i].astype(BF16)
            ps = pool_scale[i][None, :]
            pcp, xcp, xsp = _odd_in_call(xp, mod, g, w_in, wp, ps, cs_mat, tm_p, 0, 0)
            fcp = _four_call(*dft_p, xcp, xsp, wf, n_p, 8)
            pcs, xcs, xss = _odd_in_call(xs, mod, g, w_in, wp, ps, cs_mat, tm_s, 1, 1)
            fcs = _four_sym_call(*dft_s, flip, xcs, xss, wf)
            streams = [(pcp, fcp), (pcs, fcs)]
        xp = _post_call(*(t.reshape(1, -1, t.shape[-1]) for t in (xp,) + streams[0]),
                        mod, g, w_out, wg, wu, wd, l, 512, 0, 0).reshape(x_prompt.shape)
        xs = _post_call(xs, streams[1][0], streams[1][1], mod, g, w_out, wg, wu, wd, l, 512, 1, 1)
    def stack_layers(parts):
        if len(parts) == 1:
            return parts[0][:, None]
        return jnp.stack(parts, axis=1)

    return xp, xs, stack_layers(new_k), stack_layers(new_v)
```

```python
import functools
import math

import numpy as np
import jax
import jax.numpy as jnp
from jax import lax
from jax.experimental import pallas as pl
from jax.experimental.pallas import tpu as pltpu

F32 = jnp.float32
BF16 = jnp.bfloat16

D_MODEL = 1024
GRID_W = 64
N_HEADS = 4
HEAD = 128
HALF_HEAD = 64
ROPE_AXIS = 32
ROPE_BASE = 10000.0
ATTN_W = 512
CONV_W = 512
POOL_W = 512
FOURIER_W = 512
GROUP = 128
POOL_WINDOWS = (2, 4, 8, 16)
D_FF = 2816
EPS = 1e-6
LOG2E = math.log2(math.e)
HALO = 8
MXU_N = 256
ATTN_SUB = 128
EVEN_BLOCK = 256
VMEM_LIMIT = 56 * 1024 * 1024


def _params(n_axes):
    return pltpu.CompilerParams(dimension_semantics=("arbitrary",) * n_axes,
                                vmem_limit_bytes=VMEM_LIMIT)


def _resident(shape):
    return pl.BlockSpec(shape, lambda *_: (0,) * len(shape), pipeline_mode=pl.Buffered(1))


def _rms(x, g):
    ms = jnp.mean(x * x, axis=-1, keepdims=True)
    return x * lax.rsqrt(ms + EPS) * g


def _modulate(x, g, shift, scale):
    return _rms(x, g) * (1.0 + scale) + shift


def _dot(a, b):
    return jnp.dot(a, b, preferred_element_type=F32)


def _silu(x):
    return x / (1.0 + jnp.exp(-x))


def _mod_kernel(cc_ref, w_ref, b_ref, o_ref):
    s = _silu(cc_ref[...]).astype(BF16)
    o_ref[...] = _dot(s, w_ref[...].astype(BF16)) + b_ref[...]


def _mod_call(cc, w_mod, b_mod):
    depth, d, n6 = w_mod.shape
    rows = cc.shape[0]
    tn = 2048
    return pl.pallas_call(
        _mod_kernel,
        out_shape=jax.ShapeDtypeStruct((depth, rows, n6), F32),
        grid=(depth, n6 // tn),
        in_specs=[
            pl.BlockSpec((rows, d), lambda l, j: (0, 0)),
            pl.BlockSpec((None, d, tn), lambda l, j: (l, 0, j)),
            pl.BlockSpec((None, 1, tn), lambda l, j: (l, 0, j)),
        ],
        out_specs=pl.BlockSpec((None, rows, tn), lambda l, j: (l, 0, j)),
        compiler_params=_params(2),
        name="mod",
    )(cc, w_mod, b_mod.reshape(depth, 1, n6))


def _tile_specs(n, tm, mod_base, mod_stride):
    nb8 = n // HALO
    t8 = tm // HALO
    x_spec = pl.BlockSpec((None, tm, D_MODEL), lambda b, j: (b, j, 0))
    prev_spec = pl.BlockSpec((None, HALO, D_MODEL),
                             lambda b, j: (b, jnp.maximum(j * t8 - 1, 0), 0))
    next_spec = pl.BlockSpec((None, HALO, D_MODEL),
                             lambda b, j: (b, jnp.minimum((j + 1) * t8, nb8 - 1), 0))
    mod_spec = pl.BlockSpec((None, 6, D_MODEL),
                            lambda b, j: (mod_base + mod_stride * b, 0, 0))
    return x_spec, prev_spec, next_spec, mod_spec


def _rope(t, cos, sin_signed, first_half):
    outs = []
    for hh in range(N_HEADS):
        th = t[:, HEAD * hh:HEAD * (hh + 1)]
        swapped = jnp.where(first_half, pltpu.roll(th, HEAD - 16, 1), pltpu.roll(th, 16, 1))
        outs.append(th * cos + swapped * sin_signed)
    return outs


def _even_in_kernel(*refs, use_rope):
    if use_rope:
        (x_ref, xp_ref, xn_ref, mod_ref, g_ref, w_ref, cw_ref, cos_ref, sin_ref,
         q_ref, k_ref, v_ref, cb_ref, zs_ref) = refs
    else:
        (x_ref, xp_ref, xn_ref, mod_ref, g_ref, w_ref, cw_ref,
         q_ref, k_ref, v_ref, cb_ref, zs_ref) = refs
    j = pl.program_id(1)
    nt = pl.num_programs(1)
    tm = x_ref.shape[0]
    g = g_ref[0:1, :]
    shift = mod_ref[0:1, :]
    scale = mod_ref[1:2, :]
    nblk = tm // EVEN_BLOCK
    blocks = [slice(i * EVEN_BLOCK, (i + 1) * EVEN_BLOCK) for i in range(nblk)]
    if use_rope:
        lane = lax.broadcasted_iota(jnp.int32, (1, HEAD), 1)
        first_half = (lane % 32) < 16

    def proj(hh, lo):
        return _dot(hh, w_ref[:, lo:lo + 512])

    def modulated(rows):
        return _modulate(x_ref[rows, :], g, shift, scale).astype(BF16)

    def conv_inputs(i, h):
        if i == 0:
            xh = jnp.concatenate([xp_ref[...], xn_ref[...]], axis=0)
            hz = jnp.concatenate([h, _modulate(xh, g, shift, scale).astype(BF16)], axis=0)
        else:
            hz = h
        z = proj(hz, 2048) * proj(hz, 2560)
        if i == 0:
            zh = z[EVEN_BLOCK:]
            z = z[0:EVEN_BLOCK]
            zs_ref[0:HALO, :] = jnp.where(j > 0, zh[0:HALO], 0.0)
            zs_ref[HALO + tm:, :] = jnp.where(j < nt - 1, zh[HALO:], 0.0)
        zs_ref[HALO + blocks[i].start:HALO + blocks[i].stop, :] = z
        return z, proj(h, 1536)

    def conv_out(i, z, gate_b):
        lo, hi = blocks[i].start, blocks[i].stop
        conv = (cw_ref[0:1, :] * zs_ref[HALO - 1 + lo:HALO - 1 + hi, :] + cw_ref[1:2, :] * z
                + cw_ref[2:3, :] * zs_ref[HALO + 1 + lo:HALO + 1 + hi, :])
        cb_ref[blocks[i], :] = (gate_b * conv).astype(BF16)

    def qkv(i, h):
        rows = blocks[i]
        v = proj(h, 1024)
        for hh in range(N_HEADS):
            vh = v[:, HEAD * hh:HEAD * (hh + 1)]
            if use_rope:
                v_ref[hh, :, rows] = vh.T.astype(v_ref.dtype)
            else:
                v_ref[hh, rows, :] = vh.astype(v_ref.dtype)
        for ref, t in ((k_ref, proj(h, 512)), (q_ref, proj(h, 0) * (HALF_HEAD ** -0.5 * LOG2E))):
            if use_rope:
                heads = _rope(t, cos_ref[rows, :], sin_ref[rows, :], first_half)
            else:
                heads = [t[:, HEAD * hh:HEAD * (hh + 1)] for hh in range(N_HEADS)]
            for hh in range(N_HEADS):
                ref[hh, rows, :] = heads[hh].astype(ref.dtype)

    h = {0: modulated(blocks[0])}
    zg = {0: conv_inputs(0, h[0])}
    for i in range(nblk):
        if i + 1 < nblk:
            h[i + 1] = modulated(blocks[i + 1])
        else:
            conv_out(i, *zg.pop(i))
        qkv(i, h.pop(i))
        if i + 1 < nblk:
            zg[i + 1] = conv_inputs(i + 1, h[i + 1])
            conv_out(i, *zg.pop(i))


def _even_in_call(x, mod, g, w_in, conv_w, rope, tm, mod_base, mod_stride):
    bsz, n, d = x.shape
    use_rope = rope is not None
    x_spec, prev_spec, next_spec, mod_spec = _tile_specs(n, tm, mod_base, mod_stride)
    in_specs = [x_spec, prev_spec, next_spec, mod_spec,
                _resident(g.shape), _resident(w_in.shape), _resident(conv_w.shape)]
    args = [x, x, x, mod, g, w_in, conv_w]
    if use_rope:
        tab = pl.BlockSpec((tm, HEAD), lambda b, j: (j, 0))
        in_specs += [tab, tab]
        args += list(rope)
    head_spec = pl.BlockSpec((None, N_HEADS, tm, HEAD), lambda b, j: (b, 0, j, 0))
    head_shape = (bsz, N_HEADS, n, HEAD)
    if use_rope:
        v_spec = pl.BlockSpec((None, N_HEADS, HEAD, tm), lambda b, j: (b, 0, 0, j))
        k_sds = jax.ShapeDtypeStruct(head_shape, BF16)
        v_sds = jax.ShapeDtypeStruct((bsz, N_HEADS, HEAD, n), BF16)
    else:
        v_spec = head_spec
        k_sds = v_sds = jax.ShapeDtypeStruct(head_shape, F32)
    return pl.pallas_call(
        functools.partial(_even_in_kernel, use_rope=use_rope),
        out_shape=(jax.ShapeDtypeStruct(head_shape, BF16), k_sds, v_sds,
                   jax.ShapeDtypeStruct((bsz, n, CONV_W), BF16)),
        grid=(bsz, n // tm),
        in_specs=in_specs,
        out_specs=(head_spec, head_spec, v_spec,
                   pl.BlockSpec((None, tm, CONV_W), lambda b, j: (b, j, 0))),
        scratch_shapes=[pltpu.VMEM((tm + 2 * HALO, CONV_W), F32)],
        compiler_params=_params(2),
        name="even_in_rope" if use_rope else "even_in",
    )(*args)


def _diff_lambda(lam_ref, lam_init):
    lp = lam_ref[...]
    return (jnp.exp(jnp.sum(lp[0:1] * lp[1:2], axis=-1, keepdims=True))
            - jnp.exp(jnp.sum(lp[2:3] * lp[3:4], axis=-1, keepdims=True)) + lam_init)


def _stack_components(q):
    lane = lax.broadcasted_iota(jnp.int32, (1, HEAD), 1)
    zero = jnp.zeros_like(q)
    return jnp.concatenate([jnp.where(lane < HALF_HEAD, q, zero),
                            jnp.where(lane >= HALF_HEAD, q, zero)], axis=0)


def _softmax_pv(s, v_ext):
    e = jnp.exp2(s - jnp.max(s, axis=-1, keepdims=True)).astype(BF16)
    return _dot(e, v_ext)


def _normalise(ov, lam, sg, lam_init):
    t = ov.shape[0] // 2
    o = ov[:t, :HEAD] / ov[:t, HEAD:] - lam * (ov[t:, :HEAD] / ov[t:, HEAD:])
    return (_rms(o, sg) * (1.0 - lam_init)).astype(BF16)


def _chain_pipeline(n_groups, step_fn, finish_fn):
    assert n_groups % 2 == 0 and n_groups >= 2
    step_fn(0, 0, None, None)
    step_fn(1, 1, 0, 0)

    def body(t, carry):
        g = 2 * t
        step_fn(g, 0, g - 1, 1)
        finish_fn(g - 2, 0)
        step_fn(g + 1, 1, g, 0)
        finish_fn(g - 1, 1)
        return carry

    lax.fori_loop(1, n_groups // 2, body, 0)
    step_fn(None, None, n_groups - 1, 1)
    finish_fn(n_groups - 2, 0)
    finish_fn(n_groups - 1, 1)


def _attn_prompt_kernel(lam_ref, sg_ref, q_ref, k_ref, v_ref, o_ref, s_ref, ov_ref, *, lam_init):
    lam = _diff_lambda(lam_ref, lam_init)
    sg = sg_ref[...]
    n = k_ref.shape[2]
    ones = jnp.ones((n, MXU_N - HEAD), BF16)

    def step_fn(bs, ps, bv, pv):
        for hh in range(N_HEADS):
            if bs is not None:
                s_ref[ps * N_HEADS + hh] = lax.dot_general(
                    _stack_components(q_ref[bs, hh]), k_ref[bs, hh].astype(BF16),
                    (((1,), (1,)), ((), ())), preferred_element_type=F32)
        for hh in range(N_HEADS):
            if bv is not None:
                v_ext = jnp.concatenate([v_ref[bv, hh].astype(BF16), ones], axis=1)
                ov_ref[pv * N_HEADS + hh] = _softmax_pv(s_ref[pv * N_HEADS + hh], v_ext)

    def finish_fn(b, par):
        for hh in range(N_HEADS):
            o_ref[b, :, HEAD * hh:HEAD * (hh + 1)] = _normalise(ov_ref[par * N_HEADS + hh], lam, sg, lam_init)

    _chain_pipeline(q_ref.shape[0], step_fn, finish_fn)


def _attn_prompt_call(q, k, v, lam_params, subln_g, lam_init, nb):
    bsz, nh, n, hd = q.shape
    spec = pl.BlockSpec((nb, nh, n, hd), lambda b: (b, 0, 0, 0))
    return pl.pallas_call(
        functools.partial(_attn_prompt_kernel, lam_init=lam_init),
        out_shape=jax.ShapeDtypeStruct((bsz, n, nh * hd), BF16),
        grid=(bsz // nb,),
        in_specs=[_resident(lam_params.shape), _resident(subln_g.shape), spec, spec, spec],
        out_specs=pl.BlockSpec((nb, n, nh * hd), lambda b: (b, 0, 0)),
        scratch_shapes=[pltpu.VMEM((2 * nh, 2 * n, n), F32), pltpu.VMEM((2 * nh, 2 * n, MXU_N), F32)],
        compiler_params=_params(1),
        name="attn",
    )(lam_params, subln_g, q, k, v)


POST_BLOCK = 256
ONES_ROWS = 16
ATTN_WIDTH = 2
KEY_CHUNK = 256


def _attn_cache_kernel(lam_ref, sg_ref, q_ref, k_ref, vt_ref, ck_ref, cv_ref, o_ref,
                       kbuf_ref, vtbuf_ref, s_ref, ov_ref, m_ref, *, lam_init):
    past = ck_ref.shape[0]
    lk = kbuf_ref.shape[0]
    kbuf_ref[0:past, :] = ck_ref[...].astype(BF16)
    kbuf_ref[past:, :] = k_ref[...]
    vtbuf_ref[0:HEAD, 0:past] = cv_ref[...].T.astype(BF16)
    vtbuf_ref[0:HEAD, past:] = vt_ref[...]
    vtbuf_ref[HEAD:, :] = jnp.ones((ONES_ROWS, lk), BF16)
    lam = _diff_lambda(lam_ref, lam_init)
    sg = sg_ref[...] * (1.0 - lam_init)

    def rows(g, u):
        start = (g * ATTN_WIDTH + u) * ATTN_SUB
        return pl.ds(pl.multiple_of(start, ATTN_SUB), ATTN_SUB)

    def step_fn(gs, ps, gv, pv):
        chains = range(ATTN_WIDTH)
        if gs is not None:
            qq = [_stack_components(q_ref[rows(gs, u), :]) for u in chains]
            col_max = [None] * ATTN_WIDTH
        if gv is not None:
            m_prev = [m_ref[pv * ATTN_WIDTH + u] for u in chains]
            acc = [None] * ATTN_WIDTH
        for c0 in range(0, lk, KEY_CHUNK):
            keys = slice(c0, c0 + KEY_CHUNK)
            for u in chains:
                if gs is not None:
                    s = lax.dot_general(kbuf_ref[keys, :], qq[u], (((1,), (1,)), ((), ())),
                                        preferred_element_type=F32)
                    s_ref[ps * ATTN_WIDTH + u, keys, :] = s
                    cm = jnp.max(s, axis=0, keepdims=True)
                    col_max[u] = cm if col_max[u] is None else jnp.maximum(col_max[u], cm)
            for u in chains:
                if gv is not None:
                    e = jnp.exp2(s_ref[pv * ATTN_WIDTH + u, keys, :] - m_prev[u]).astype(BF16)
                    part = _dot(vtbuf_ref[:, keys], e)
                    acc[u] = part if acc[u] is None else acc[u] + part
        for u in chains:
            if gs is not None:
                m_ref[ps * ATTN_WIDTH + u] = col_max[u]
            if gv is not None:
                ov_ref[pv * ATTN_WIDTH + u] = acc[u]

    def finish_fn(g, par):
        for u in range(ATTN_WIDTH):
            ov = ov_ref[par * ATTN_WIDTH + u]
            o_t = (ov[0:HEAD, 0:ATTN_SUB] / ov[HEAD:HEAD + 1, 0:ATTN_SUB]
                   - lam * (ov[0:HEAD, ATTN_SUB:] / ov[HEAD:HEAD + 1, ATTN_SUB:]))
            ms = jnp.mean(o_t * o_t, axis=0, keepdims=True)
            o_ref[rows(g, u), :] = ((o_t * lax.rsqrt(ms + EPS)).T * sg).astype(BF16)

    _chain_pipeline(q_ref.shape[0] // (ATTN_SUB * ATTN_WIDTH), step_fn, finish_fn)


def _attn_cache_call(q, k, vt, lam_params, subln_g, cache_k, cache_v, layer, lam_init):
    bsz, nh, n, hd = q.shape
    past = cache_k.shape[3]
    seq_spec = pl.BlockSpec((None, None, n, hd), lambda b, h: (b, h, 0, 0))
    vt_spec = pl.BlockSpec((None, None, hd, n), lambda b, h: (b, h, 0, 0))
    c_spec = pl.BlockSpec((None, None, None, past, hd), lambda b, h: (b, layer, h, 0, 0))
    return pl.pallas_call(
        functools.partial(_attn_cache_kernel, lam_init=lam_init),
        out_shape=jax.ShapeDtypeStruct((bsz, n, nh * hd), BF16),
        grid=(bsz, nh),
        in_specs=[_resident(lam_params.shape), _resident(subln_g.shape),
                  seq_spec, seq_spec, vt_spec, c_spec, c_spec],
        out_specs=pl.BlockSpec((None, n, hd), lambda b, h: (b, 0, h)),
        scratch_shapes=[pltpu.VMEM((past + n, hd), BF16), pltpu.VMEM((hd + ONES_ROWS, past + n), BF16),
                        pltpu.VMEM((2 * ATTN_WIDTH, past + n, 2 * ATTN_SUB), F32),
                        pltpu.VMEM((2 * ATTN_WIDTH, hd + ONES_ROWS, 2 * ATTN_SUB), F32),
                        pltpu.VMEM((2 * ATTN_WIDTH, 1, 2 * ATTN_SUB), F32)],
        compiler_params=_params(2),
        name="attn_cache",
    )(lam_params, subln_g, q, k, vt, cache_k, cache_v)


def _post_kernel(x_ref, a_ref, b_ref, mod_ref, g_ref, wo_ref, wg_ref, wu_ref, wd_ref, o_ref):
    half = a_ref.shape[1]
    nb = x_ref.shape[0] // POST_BLOCK
    blocks = [slice(i * POST_BLOCK, (i + 1) * POST_BLOCK) for i in range(nb)]

    def out_proj(rows):
        return _dot(a_ref[rows, :], wo_ref[0:half, :]) + _dot(b_ref[rows, :], wo_ref[half:, :])

    def norms(rows, y):
        x1 = x_ref[rows, :] + mod_ref[2:3, :] * _rms(y, g_ref[1:2, :])
        return x1, _modulate(x1, g_ref[2:3, :], mod_ref[3:4, :], mod_ref[4:5, :]).astype(BF16)

    def gate_up(h):
        return _dot(h, wg_ref[...]), _dot(h, wu_ref[...])

    def down(gu):
        return _dot((_silu(gu[0]) * gu[1]).astype(BF16), wd_ref[...])

    def finish(rows, x1, f):
        o_ref[rows, :] = x1 + mod_ref[5:6, :] * _rms(f, g_ref[3:4, :])

    y = {0: out_proj(blocks[0])}
    x1, gu = {}, {}
    for i in range(nb + 1):
        if i + 1 < nb:
            y[i + 1] = out_proj(blocks[i + 1])
        if i < nb:
            x1[i], h = norms(blocks[i], y.pop(i))
        if i >= 1:
            f = down(gu.pop(i - 1))
        if i < nb:
            gu[i] = gate_up(h)
        if i >= 1:
            finish(blocks[i - 1], x1.pop(i - 1), f)


def _post_call(x, a, b, mod, g, w_out, w_gate, w_up, w_down, layer, tm, mod_base, mod_stride):
    bsz, n, d = x.shape
    x_spec, _, _, mod_spec = _tile_specs(n, tm, mod_base, mod_stride)
    half_spec = pl.BlockSpec((None, tm, a.shape[2]), lambda b_, j: (b_, j, 0))

    def layer_resident(w):
        return pl.BlockSpec((None,) + w.shape[1:], lambda *_: (layer, 0, 0), pipeline_mode=pl.Buffered(1))

    return pl.pallas_call(
        _post_kernel,
        out_shape=jax.ShapeDtypeStruct(x.shape, F32),
        grid=(bsz, n // tm),
        in_specs=[x_spec, half_spec, half_spec, mod_spec, _resident(g.shape),
                  _resident(w_out.shape), layer_resident(w_gate), layer_resident(w_up),
                  layer_resident(w_down)],
        out_specs=x_spec,
        compiler_params=_params(2),
        name="post",
    )(x, a, b, mod, g, w_out, w_gate, w_up, w_down)


def _odd_in_kernel(x_ref, xp_ref, xn_ref, mod_ref, g_ref, w_ref, wp_ref, ps_ref, cs_ref,
                   pc_ref, xc_ref, xs_ref, us_ref, f2_ref, f4_ref, *, n_seq):
    j = pl.program_id(1)
    nt = pl.num_programs(1)
    tm = x_ref.shape[0]
    g = g_ref[0:1, :]
    shift = mod_ref[0:1, :]
    scale = mod_ref[1:2, :]
    xh = jnp.concatenate([x_ref[...], xp_ref[...], xn_ref[...]], axis=0)
    hz = _modulate(xh, g, shift, scale).astype(BF16)
    h = hz[0:tm]
    upz = _dot(hz, w_ref[:, 0:POOL_W])
    up, uph = upz[0:tm], upz[tm:]
    uf = _dot(h, w_ref[:, POOL_W:]).astype(BF16)
    for gi in range(FOURIER_W // GROUP):
        lanes = slice(GROUP * gi, GROUP * (gi + 1))
        cs = _dot(uf[:, lanes], cs_ref[...])
        xc_ref[:, lanes] = cs[:, 0:GROUP].astype(BF16)
        xs_ref[:, lanes] = cs[:, GROUP:].astype(BF16)
    rows = tm + 2 * HALO
    us_ref[0:HALO, :] = jnp.where(j > 0, uph[0:HALO], 0.0)
    us_ref[HALO:HALO + tm, :] = up
    us_ref[HALO + tm:rows, :] = jnp.where(j < nt - 1, uph[HALO:], 0.0)
    us_ref[rows:, :] = jnp.zeros((HALO, POOL_W), F32)
    f2_ref[0:rows, :] = us_ref[0:rows, GROUP:] + us_ref[1:rows + 1, GROUP:]
    f2_ref[rows:, :] = jnp.zeros((HALO, POOL_W - GROUP), F32)
    f4_ref[0:rows, :] = f2_ref[0:rows, GROUP:] + f2_ref[2:rows + 2, GROUP:]
    f4_ref[rows:, :] = jnp.zeros((HALO, POOL_W - 2 * GROUP), F32)
    f8 = f4_ref[0:rows, GROUP:] + f4_ref[4:rows + 4, GROUP:]
    sums = (us_ref[HALO - 1:HALO - 1 + tm, 0:GROUP] + up[:, 0:GROUP],
            f2_ref[HALO - 2:HALO - 2 + tm, 0:GROUP] + f2_ref[HALO:HALO + tm, 0:GROUP],
            f4_ref[HALO - 4:HALO - 4 + tm, 0:GROUP] + f4_ref[HALO:HALO + tm, 0:GROUP],
            f8[0:tm] + f8[HALO:HALO + tm])

    t = (j * tm + lax.broadcasted_iota(jnp.int32, (tm, 1), 0)).astype(F32)
    for gi, win in enumerate(POOL_WINDOWS):
        lanes = slice(GROUP * gi, GROUP * (gi + 1))
        cnt = jnp.minimum(t + float(win // 2), float(n_seq)) - jnp.maximum(t - float(win // 2), 0.0)
        diff = (sums[gi] / cnt - up[:, lanes]).astype(BF16)
        pc_ref[:, lanes] = (_dot(diff, wp_ref[gi]) * ps_ref[0:1, lanes]).astype(BF16)


def _odd_in_call(x, mod, g, w_in, w_pool, pool_scale, cs_mat, tm, mod_base, mod_stride):
    bsz, n, d = x.shape
    x_spec, prev_spec, next_spec, mod_spec = _tile_specs(n, tm, mod_base, mod_stride)
    out_spec = pl.BlockSpec((None, tm, POOL_W), lambda b, j: (b, j, 0))
    out_sds = jax.ShapeDtypeStruct((bsz, n, POOL_W), BF16)
    return pl.pallas_call(
        functools.partial(_odd_in_kernel, n_seq=n),
        out_shape=(out_sds, out_sds, out_sds),
        grid=(bsz, n // tm),
        in_specs=[x_spec, prev_spec, next_spec, mod_spec, _resident(g.shape), _resident(w_in.shape),
                  _resident(w_pool.shape), _resident(pool_scale.shape), _resident(cs_mat.shape)],
        out_specs=(out_spec, out_spec, out_spec),
        scratch_shapes=[pltpu.VMEM((tm + 3 * HALO, POOL_W), F32),
                        pltpu.VMEM((tm + 3 * HALO, POOL_W - GROUP), F32),
                        pltpu.VMEM((tm + 3 * HALO, POOL_W - 2 * GROUP), F32)],
        compiler_params=_params(2),
        name="odd_in",
    )(x, x, x, mod, g, w_in, w_pool, pool_scale, cs_mat)


def _four_kernel(c_ref, s_ref, xc_ref, xs_ref, wf_ref, o_ref, *, scale):
    for b in range(xc_ref.shape[0]):
        y = _dot(c_ref[...], xc_ref[b]) - _dot(s_ref[...], xs_ref[b])
        four = (y * scale).astype(BF16)
        for gi in range(FOURIER_W // GROUP):
            lanes = slice(GROUP * gi, GROUP * (gi + 1))
            o_ref[b, :, lanes] = _dot(four[:, lanes], wf_ref[gi]).astype(BF16)


def _four_call(cn, sn, xc, xs, w_four, tm, nb):
    bsz, n, w = xc.shape
    mat_spec = pl.BlockSpec((tm, n), lambda b, j: (j, 0))
    seq_spec = pl.BlockSpec((nb, n, w), lambda b, j: (b, 0, 0))
    return pl.pallas_call(
        functools.partial(_four_kernel, scale=float(1.0 / math.sqrt(n * GROUP))),
        out_shape=jax.ShapeDtypeStruct((bsz, n, w), BF16),
        grid=(bsz // nb, n // tm),
        in_specs=[mat_spec, mat_spec, seq_spec, seq_spec, _resident(w_four.shape)],
        out_specs=pl.BlockSpec((nb, tm, w), lambda b, j: (b, j, 0)),
        compiler_params=_params(2),
        name="fourier",
    )(cn, sn, xc, xs, w_four)


FLIP_BLOCK = 256


def _four_sym_kernel(c_ref, s_ref, pm_ref, xc_ref, xs_ref, wf_ref, o_ref, *, scale):
    half = o_ref.shape[0] // 2
    p = _dot(c_ref[...], xc_ref[...])
    q = _dot(s_ref[...], xs_ref[...])

    def project(rows, four):
        for gi in range(FOURIER_W // GROUP):
            lanes = slice(GROUP * gi, GROUP * (gi + 1))
            o_ref[rows, lanes] = _dot(four[:, lanes], wf_ref[gi]).astype(BF16)

    project(slice(0, half), ((p[0:half] - q[0:half]) * scale).astype(BF16))
    mirrored = ((p + q) * scale).astype(BF16)
    for b in range(half // FLIP_BLOCK):
        lo = half - FLIP_BLOCK * (b + 1)
        window = mirrored[lo:lo + FLIP_BLOCK + HALO, :]
        flipped = _dot(pm_ref[...], window).astype(BF16)
        project(slice(half + FLIP_BLOCK * b, half + FLIP_BLOCK * (b + 1)), flipped)


def _four_sym_call(c_half, s_half, perm, xc, xs, w_four):
    bsz, n, w = xc.shape
    seq_spec = pl.BlockSpec((None, n, w), lambda b: (b, 0, 0))
    return pl.pallas_call(
        functools.partial(_four_sym_kernel, scale=float(1.0 / math.sqrt(n * GROUP))),
        out_shape=jax.ShapeDtypeStruct((bsz, n, w), BF16),
        grid=(bsz,),
        in_specs=[_resident(c_half.shape), _resident(s_half.shape), _resident(perm.shape),
                  seq_spec, seq_spec, _resident(w_four.shape)],
        out_specs=seq_spec,
        compiler_params=_params(1),
        name="fourier_sym",
    )(c_half, s_half, perm, xc, xs, w_four)


def _flip_perm():
    pm = np.zeros((FLIP_BLOCK, FLIP_BLOCK + HALO), np.float32)
    pm[np.arange(FLIP_BLOCK), FLIP_BLOCK - np.arange(FLIP_BLOCK)] = 1.0
    return pm


def _rope_tables(n_tok):
    rows = n_tok // GRID_W
    row = np.repeat(np.arange(rows), GRID_W).astype(np.float64)
    col = np.tile(np.arange(GRID_W), rows).astype(np.float64)
    inv = ROPE_BASE ** (-np.arange(0, ROPE_AXIS, 2, dtype=np.float64) / ROPE_AXIS)
    ang_r = row[:, None] * inv[None, :]
    ang_c = col[:, None] * inv[None, :]
    ang = np.concatenate([ang_r, ang_r, ang_c, ang_c], axis=-1)
    cos = np.concatenate([np.cos(ang)] * 2, axis=-1)
    sin = np.concatenate([np.sin(ang)] * 2, axis=-1)
    first_half = (np.arange(HEAD) % 32) < 16
    sin_signed = np.where(first_half[None, :], -sin, sin)
    return jnp.asarray(cos, F32), jnp.asarray(sin_signed, F32)


def _dft_mats(n):
    idx = np.arange(n, dtype=np.int64)
    ang = 2.0 * np.pi * ((idx[:, None] * idx[None, :]) % n).astype(np.float64) / n
    return np.cos(ang), np.sin(ang)


def kernel(x_prompt, x_sample, cache_k, cache_v, c, c_ctx, w_mod, b_mod, norm_g,
           w_in_even, lam_params, subln_g, conv_w, w_out_even,
           w_in_odd, w_pool, pool_scale, w_fourier, w_out_odd,
           w_gate, w_up, w_down):
    depth = w_mod.shape[0]
    n_dec = x_sample.shape[0]
    n_p, n_s = x_prompt.shape[1], x_sample.shape[1]
    tm_p, tm_s = n_p, 512

    pad_rows = 16 - 1 - n_dec
    cc = jnp.concatenate([c_ctx[None, :], c, jnp.zeros((pad_rows, D_MODEL), F32)], axis=0)
    mod_all = _mod_call(cc, w_mod, b_mod)[:, :1 + n_dec].reshape(depth, 1 + n_dec, 6, D_MODEL)

    rope = _rope_tables(n_s)
    cc_g, sc_g = _dft_mats(GROUP)
    cs_mat = jnp.asarray(np.concatenate([cc_g, sc_g], axis=1), F32).astype(BF16)
    dft_p = tuple(jnp.asarray(m, F32).astype(BF16) for m in _dft_mats(n_p))
    dft_s = tuple(jnp.asarray(m[:n_s // 2 + HALO], F32).astype(BF16) for m in _dft_mats(n_s))
    flip = jnp.asarray(_flip_perm(), F32).astype(BF16)

    wg, wu, wd = w_gate.astype(BF16), w_up.astype(BF16), w_down.astype(BF16)
    xp, xs = x_prompt, x_sample
    new_k, new_v = [], []
    for l in range(depth):
        mod = mod_all[l]
        g = norm_g[l]
        i = l // 2
        streams = []
        if l % 2 == 0:
            lam_init = 0.8 - 0.6 * math.exp(-0.3 * l)
            w_in = w_in_even[i].astype(BF16)
            w_out = w_out_even[i].astype(BF16)
            sg = subln_g[i][None, :]
            qp, kp, vp, cbp = _even_in_call(xp, mod, g, w_in, conv_w[i], None, tm_p, 0, 0)
            ap = _attn_prompt_call(qp, kp, vp, lam_params[i], sg, lam_init, 8)
            new_k.append(kp)
            new_v.append(vp)
            qs, ks, vts, cbs = _even_in_call(xs, mod, g, w_in, conv_w[i], rope, tm_s, 1, 1)
            a_s = _attn_cache_call(qs, ks, vts, lam_params[i], sg, cache_k, cache_v, i, lam_init)
            streams = [(ap, cbp), (a_s, cbs)]
        else:
            w_in = w_in_odd[i].astype(BF16)
            w_out = w_out_odd[i].astype(BF16)
            wp = w_pool[i].astype(BF16)
            wf = w_fourier[i].astype(BF16)
            ps = pool_scale[i][None, :]
            pcp, xcp, xsp = _odd_in_call(xp, mod, g, w_in, wp, ps, cs_mat, tm_p, 0, 0)
            fcp = _four_call(*dft_p, xcp, xsp, wf, n_p, 8)
            pcs, xcs, xss = _odd_in_call(xs, mod, g, w_in, wp, ps, cs_mat, tm_s, 1, 1)
            fcs = _four_sym_call(*dft_s, flip, xcs, xss, wf)
            streams = [(pcp, fcp), (pcs, fcs)]
        xp = _post_call(*(t.reshape(1, -1, t.shape[-1]) for t in (xp,) + streams[0]),
                        mod, g, w_out, wg, wu, wd, l, 512, 0, 0).reshape(x_prompt.shape)
        xs = _post_call(xs, streams[1][0], streams[1][1], mod, g, w_out, wg, wu, wd, l, 512, 1, 1)
    def stack_layers(parts):
        if len(parts) == 1:
            return parts[0][:, None]
        return jnp.stack(parts, axis=1)

    return xp, xs, stack_layers(new_k), stack_layers(new_v)
```

```python
import functools
import math

import numpy as np
import jax
import jax.numpy as jnp
from jax import lax
from jax.experimental import pallas as pl
from jax.experimental.pallas import tpu as pltpu

F32 = jnp.float32
BF16 = jnp.bfloat16

D_MODEL = 1024
GRID_W = 64
N_HEADS = 4
HEAD = 128
HALF_HEAD = 64
ROPE_AXIS = 32
ROPE_BASE = 10000.0
ATTN_W = 512
CONV_W = 512
POOL_W = 512
FOURIER_W = 512
GROUP = 128
POOL_WINDOWS = (2, 4, 8, 16)
D_FF = 2816
EPS = 1e-6
LOG2E = math.log2(math.e)
HALO = 8
MXU_N = 256
ATTN_SUB = 128
EVEN_BLOCK = 256
VMEM_LIMIT = 56 * 1024 * 1024


def _params(n_axes):
    return pltpu.CompilerParams(dimension_semantics=("arbitrary",) * n_axes,
                                vmem_limit_bytes=VMEM_LIMIT)


def _resident(shape):
    return pl.BlockSpec(shape, lambda *_: (0,) * len(shape), pipeline_mode=pl.Buffered(1))


def _rms(x, g):
    ms = jnp.mean(x * x, axis=-1, keepdims=True)
    return x * lax.rsqrt(ms + EPS) * g


def _modulate(x, g, shift, scale):
    return _rms(x, g) * (1.0 + scale) + shift


def _dot(a, b):
    return jnp.dot(a, b, preferred_element_type=F32)


def _silu(x):
    return x / (1.0 + jnp.exp(-x))


def _mod_kernel(cc_ref, w_ref, b_ref, o_ref):
    s = _silu(cc_ref[...]).astype(BF16)
    o_ref[...] = _dot(s, w_ref[...].astype(BF16)) + b_ref[...]


def _mod_call(cc, w_mod, b_mod):
    depth, d, n6 = w_mod.shape
    rows = cc.shape[0]
    tn = 2048
    return pl.pallas_call(
        _mod_kernel,
        out_shape=jax.ShapeDtypeStruct((depth, rows, n6), F32),
        grid=(depth, n6 // tn),
        in_specs=[
            pl.BlockSpec((rows, d), lambda l, j: (0, 0)),
            pl.BlockSpec((None, d, tn), lambda l, j: (l, 0, j)),
            pl.BlockSpec((None, 1, tn), lambda l, j: (l, 0, j)),
        ],
        out_specs=pl.BlockSpec((None, rows, tn), lambda l, j: (l, 0, j)),
        compiler_params=_params(2),
        name="mod",
    )(cc, w_mod, b_mod.reshape(depth, 1, n6))


def _tile_specs(n, tm, mod_base, mod_stride):
    nb8 = n // HALO
    t8 = tm // HALO
    x_spec = pl.BlockSpec((None, tm, D_MODEL), lambda b, j: (b, j, 0))
    prev_spec = pl.BlockSpec((None, HALO, D_MODEL),
                             lambda b, j: (b, jnp.maximum(j * t8 - 1, 0), 0))
    next_spec = pl.BlockSpec((None, HALO, D_MODEL),
                             lambda b, j: (b, jnp.minimum((j + 1) * t8, nb8 - 1), 0))
    mod_spec = pl.BlockSpec((None, 6, D_MODEL),
                            lambda b, j: (mod_base + mod_stride * b, 0, 0))
    return x_spec, prev_spec, next_spec, mod_spec


def _rope(t, cos, sin_signed, first_half):
    outs = []
    for hh in range(N_HEADS):
        th = t[:, HEAD * hh:HEAD * (hh + 1)]
        swapped = jnp.where(first_half, pltpu.roll(th, HEAD - 16, 1), pltpu.roll(th, 16, 1))
        outs.append(th * cos + swapped * sin_signed)
    return outs


def _even_in_kernel(*refs, use_rope):
    if use_rope:
        (x_ref, xp_ref, xn_ref, mod_ref, g_ref, w_ref, cw_ref, cos_ref, sin_ref,
         q_ref, k_ref, v_ref, cb_ref, zs_ref) = refs
    else:
        (x_ref, xp_ref, xn_ref, mod_ref, g_ref, w_ref, cw_ref,
         q_ref, k_ref, v_ref, cb_ref, zs_ref) = refs
    j = pl.program_id(1)
    nt = pl.num_programs(1)
    tm = x_ref.shape[0]
    g = g_ref[0:1, :]
    shift = mod_ref[0:1, :]
    scale = mod_ref[1:2, :]
    nblk = tm // EVEN_BLOCK
    blocks = [slice(i * EVEN_BLOCK, (i + 1) * EVEN_BLOCK) for i in range(nblk)]
    if use_rope:
        lane = lax.broadcasted_iota(jnp.int32, (1, HEAD), 1)
        first_half = (lane % 32) < 16

    def proj(hh, lo):
        return _dot(hh, w_ref[:, lo:lo + 512])

    def modulated(rows):
        return _modulate(x_ref[rows, :], g, shift, scale).astype(BF16)

    def conv_inputs(i, h):
        if i == 0:
            xh = jnp.concatenate([xp_ref[...], xn_ref[...]], axis=0)
            hz = jnp.concatenate([h, _modulate(xh, g, shift, scale).astype(BF16)], axis=0)
        else:
            hz = h
        z = proj(hz, 2048) * proj(hz, 2560)
        if i == 0:
            zh = z[EVEN_BLOCK:]
            z = z[0:EVEN_BLOCK]
            zs_ref[0:HALO, :] = jnp.where(j > 0, zh[0:HALO], 0.0)
            zs_ref[HALO + tm:, :] = jnp.where(j < nt - 1, zh[HALO:], 0.0)
        zs_ref[HALO + blocks[i].start:HALO + blocks[i].stop, :] = z
        return z, proj(h, 1536)

    def conv_out(i, z, gate_b):
        lo, hi = blocks[i].start, blocks[i].stop
        conv = (cw_ref[0:1, :] * zs_ref[HALO - 1 + lo:HALO - 1 + hi, :] + cw_ref[1:2, :] * z
                + cw_ref[2:3, :] * zs_ref[HALO + 1 + lo:HALO + 1 + hi, :])
        cb_ref[blocks[i], :] = (gate_b * conv).astype(BF16)

    def qkv(i, h):
        rows = blocks[i]
        v = proj(h, 1024)
        for hh in range(N_HEADS):
            vh = v[:, HEAD * hh:HEAD * (hh + 1)]
            if use_rope:
                v_ref[hh, :, rows] = vh.T.astype(v_ref.dtype)
            else:
                v_ref[hh, rows, :] = vh.astype(v_ref.dtype)
        for ref, t in ((k_ref, proj(h, 512)), (q_ref, proj(h, 0) * (HALF_HEAD ** -0.5 * LOG2E))):
            if use_rope:
                heads = _rope(t, cos_ref[rows, :], sin_ref[rows, :], first_half)
            else:
                heads = [t[:, HEAD * hh:HEAD * (hh + 1)] for hh in range(N_HEADS)]
            for hh in range(N_HEADS):
                ref[hh, rows, :] = heads[hh].astype(ref.dtype)

    h = {0: modulated(blocks[0])}
    zg = {0: conv_inputs(0, h[0])}
    for i in range(nblk):
        if i + 1 < nblk:
            h[i + 1] = modulated(blocks[i + 1])
        else:
            conv_out(i, *zg.pop(i))
        qkv(i, h.pop(i))
        if i + 1 < nblk:
            zg[i + 1] = conv_inputs(i + 1, h[i + 1])
            conv_out(i, *zg.pop(i))


def _even_in_call(x, mod, g, w_in, conv_w, rope, tm, mod_base, mod_stride):
    bsz, n, d = x.shape
    use_rope = rope is not None
    x_spec, prev_spec, next_spec, mod_spec = _tile_specs(n, tm, mod_base, mod_stride)
    in_specs = [x_spec, prev_spec, next_spec, mod_spec,
                _resident(g.shape), _resident(w_in.shape), _resident(conv_w.shape)]
    args = [x, x, x, mod, g, w_in, conv_w]
    if use_rope:
        tab = pl.BlockSpec((tm, HEAD), lambda b, j: (j, 0))
        in_specs += [tab, tab]
        args += list(rope)
    head_spec = pl.BlockSpec((None, N_HEADS, tm, HEAD), lambda b, j: (b, 0, j, 0))
    head_shape = (bsz, N_HEADS, n, HEAD)
    if use_rope:
        v_spec = pl.BlockSpec((None, N_HEADS, HEAD, tm), lambda b, j: (b, 0, 0, j))
        k_sds = jax.ShapeDtypeStruct(head_shape, BF16)
        v_sds = jax.ShapeDtypeStruct((bsz, N_HEADS, HEAD, n), BF16)
    else:
        v_spec = head_spec
        k_sds = v_sds = jax.ShapeDtypeStruct(head_shape, F32)
    return pl.pallas_call(
        functools.partial(_even_in_kernel, use_rope=use_rope),
        out_shape=(jax.ShapeDtypeStruct(head_shape, BF16), k_sds, v_sds,
                   jax.ShapeDtypeStruct((bsz, n, CONV_W), BF16)),
        grid=(bsz, n // tm),
        in_specs=in_specs,
        out_specs=(head_spec, head_spec, v_spec,
                   pl.BlockSpec((None, tm, CONV_W), lambda b, j: (b, j, 0))),
        scratch_shapes=[pltpu.VMEM((tm + 2 * HALO, CONV_W), F32)],
        compiler_params=_params(2),
        name="even_in_rope" if use_rope else "even_in",
    )(*args)


def _diff_lambda(lam_ref, lam_init):
    lp = lam_ref[...]
    return (jnp.exp(jnp.sum(lp[0:1] * lp[1:2], axis=-1, keepdims=True))
            - jnp.exp(jnp.sum(lp[2:3] * lp[3:4], axis=-1, keepdims=True)) + lam_init)


def _stack_components(q):
    lane = lax.broadcasted_iota(jnp.int32, (1, HEAD), 1)
    zero = jnp.zeros_like(q)
    return jnp.concatenate([jnp.where(lane < HALF_HEAD, q, zero),
                            jnp.where(lane >= HALF_HEAD, q, zero)], axis=0)


def _softmax_pv(s, v_ext):
    e = jnp.exp2(s - jnp.max(s, axis=-1, keepdims=True)).astype(BF16)
    return _dot(e, v_ext)


def _normalise(ov, lam, sg, lam_init):
    t = ov.shape[0] // 2
    o = ov[:t, :HEAD] / ov[:t, HEAD:] - lam * (ov[t:, :HEAD] / ov[t:, HEAD:])
    return (_rms(o, sg) * (1.0 - lam_init)).astype(BF16)


def _chain_pipeline(n_groups, step_fn, finish_fn):
    assert n_groups % 2 == 0 and n_groups >= 2
    step_fn(0, 0, None, None)
    step_fn(1, 1, 0, 0)

    def body(t, carry):
        g = 2 * t
        step_fn(g, 0, g - 1, 1)
        finish_fn(g - 2, 0)
        step_fn(g + 1, 1, g, 0)
        finish_fn(g - 1, 1)
        return carry

    lax.fori_loop(1, n_groups // 2, body, 0)
    step_fn(None, None, n_groups - 1, 1)
    finish_fn(n_groups - 2, 0)
    finish_fn(n_groups - 1, 1)


def _attn_prompt_kernel(lam_ref, sg_ref, q_ref, k_ref, v_ref, o_ref, s_ref, ov_ref, *, lam_init):
    lam = _diff_lambda(lam_ref, lam_init)
    sg = sg_ref[...]
    n = k_ref.shape[2]
    ones = jnp.ones((n, MXU_N - HEAD), BF16)

    def step_fn(bs, ps, bv, pv):
        for hh in range(N_HEADS):
            if bs is not None:
                s_ref[ps * N_HEADS + hh] = lax.dot_general(
                    _stack_components(q_ref[bs, hh]), k_ref[bs, hh].astype(BF16),
                    (((1,), (1,)), ((), ())), preferred_element_type=F32)
        for hh in range(N_HEADS):
            if bv is not None:
                v_ext = jnp.concatenate([v_ref[bv, hh].astype(BF16), ones], axis=1)
                ov_ref[pv * N_HEADS + hh] = _softmax_pv(s_ref[pv * N_HEADS + hh], v_ext)

    def finish_fn(b, par):
        for hh in range(N_HEADS):
            o_ref[b, :, HEAD * hh:HEAD * (hh + 1)] = _normalise(ov_ref[par * N_HEADS + hh], lam, sg, lam_init)

    _chain_pipeline(q_ref.shape[0], step_fn, finish_fn)


def _attn_prompt_call(q, k, v, lam_params, subln_g, lam_init, nb):
    bsz, nh, n, hd = q.shape
    spec = pl.BlockSpec((nb, nh, n, hd), lambda b: (b, 0, 0, 0))
    return pl.pallas_call(
        functools.partial(_attn_prompt_kernel, lam_init=lam_init),
        out_shape=jax.ShapeDtypeStruct((bsz, n, nh * hd), BF16),
        grid=(bsz // nb,),
        in_specs=[_resident(lam_params.shape), _resident(subln_g.shape), spec, spec, spec],
        out_specs=pl.BlockSpec((nb, n, nh * hd), lambda b: (b, 0, 0)),
        scratch_shapes=[pltpu.VMEM((2 * nh, 2 * n, n), F32), pltpu.VMEM((2 * nh, 2 * n, MXU_N), F32)],
        compiler_params=_params(1),
        name="attn",
    )(lam_params, subln_g, q, k, v)


POST_BLOCK = 256
ONES_ROWS = 16
ATTN_WIDTH = 2
KEY_CHUNK = 256


def _attn_cache_kernel(lam_ref, sg_ref, q_ref, k_ref, vt_ref, ck_ref, cv_ref, o_ref,
                       kbuf_ref, vtbuf_ref, s_ref, ov_ref, m_ref, *, lam_init):
    past = ck_ref.shape[0]
    lk = kbuf_ref.shape[0]
    kbuf_ref[0:past, :] = ck_ref[...].astype(BF16)
    kbuf_ref[past:, :] = k_ref[...]
    vtbuf_ref[0:HEAD, 0:past] = cv_ref[...].T.astype(BF16)
    vtbuf_ref[0:HEAD, past:] = vt_ref[...]
    vtbuf_ref[HEAD:, :] = jnp.ones((ONES_ROWS, lk), BF16)
    lam = _diff_lambda(lam_ref, lam_init)
    sg = sg_ref[...] * (1.0 - lam_init)

    def rows(g, u):
        start = (g * ATTN_WIDTH + u) * ATTN_SUB
        return pl.ds(pl.multiple_of(start, ATTN_SUB), ATTN_SUB)

    def step_fn(gs, ps, gv, pv):
        chains = range(ATTN_WIDTH)
        if gs is not None:
            qq = [_stack_components(q_ref[rows(gs, u), :]) for u in chains]
            col_max = [None] * ATTN_WIDTH
        if gv is not None:
            m_prev = [m_ref[pv * ATTN_WIDTH + u] for u in chains]
            acc = [None] * ATTN_WIDTH
        for c0 in range(0, lk, KEY_CHUNK):
            keys = slice(c0, c0 + KEY_CHUNK)
            for u in chains:
                if gs is not None:
                    s = lax.dot_general(kbuf_ref[keys, :], qq[u], (((1,), (1,)), ((), ())),
                                        preferred_element_type=F32)
                    s_ref[ps * ATTN_WIDTH + u, keys, :] = s
                    cm = jnp.max(s, axis=0, keepdims=True)
                    col_max[u] = cm if col_max[u] is None else jnp.maximum(col_max[u], cm)
            for u in chains:
                if gv is not None:
                    e = jnp.exp2(s_ref[pv * ATTN_WIDTH + u, keys, :] - m_prev[u]).astype(BF16)
                    part = _dot(vtbuf_ref[:, keys], e)
                    acc[u] = part if acc[u] is None else acc[u] + part
        for u in chains:
            if gs is not None:
                m_ref[ps * ATTN_WIDTH + u] = col_max[u]
            if gv is not None:
                ov_ref[pv * ATTN_WIDTH + u] = acc[u]

    def finish_fn(g, par):
        for u in range(ATTN_WIDTH):
            ov = ov_ref[par * ATTN_WIDTH + u]
            o_t = (ov[0:HEAD, 0:ATTN_SUB] / ov[HEAD:HEAD + 1, 0:ATTN_SUB]
                   - lam * (ov[0:HEAD, ATTN_SUB:] / ov[HEAD:HEAD + 1, ATTN_SUB:]))
            ms = jnp.mean(o_t * o_t, axis=0, keepdims=True)
            o_ref[rows(g, u), :] = ((o_t * lax.rsqrt(ms + EPS)).T * sg).astype(BF16)

    _chain_pipeline(q_ref.shape[0] // (ATTN_SUB * ATTN_WIDTH), step_fn, finish_fn)


def _attn_cache_call(q, k, vt, lam_params, subln_g, cache_k, cache_v, layer, lam_init):
    bsz, nh, n, hd = q.shape
    past = cache_k.shape[3]
    seq_spec = pl.BlockSpec((None, None, n, hd), lambda b, h: (b, h, 0, 0))
    vt_spec = pl.BlockSpec((None, None, hd, n), lambda b, h: (b, h, 0, 0))
    c_spec = pl.BlockSpec((None, None, None, past, hd), lambda b, h: (b, layer, h, 0, 0))
    return pl.pallas_call(
        functools.partial(_attn_cache_kernel, lam_init=lam_init),
        out_shape=jax.ShapeDtypeStruct((bsz, n, nh * hd), BF16),
        grid=(bsz, nh),
        in_specs=[_resident(lam_params.shape), _resident(subln_g.shape),
                  seq_spec, seq_spec, vt_spec, c_spec, c_spec],
        out_specs=pl.BlockSpec((None, n, hd), lambda b, h: (b, 0, h)),
        scratch_shapes=[pltpu.VMEM((past + n, hd), BF16), pltpu.VMEM((hd + ONES_ROWS, past + n), BF16),
                        pltpu.VMEM((2 * ATTN_WIDTH, past + n, 2 * ATTN_SUB), F32),
                        pltpu.VMEM((2 * ATTN_WIDTH, hd + ONES_ROWS, 2 * ATTN_SUB), F32),
                        pltpu.VMEM((2 * ATTN_WIDTH, 1, 2 * ATTN_SUB), F32)],
        compiler_params=_params(2),
        name="attn_cache",
    )(lam_params, subln_g, q, k, vt, cache_k, cache_v)


def _post_kernel(x_ref, a_ref, b_ref, mod_ref, g_ref, wo_ref, wg_ref, wu_ref, wd_ref, o_ref):
    half = a_ref.shape[1]
    nb = x_ref.shape[0] // POST_BLOCK
    blocks = [slice(i * POST_BLOCK, (i + 1) * POST_BLOCK) for i in range(nb)]

    def out_proj(rows):
        return _dot(a_ref[rows, :], wo_ref[0:half, :]) + _dot(b_ref[rows, :], wo_ref[half:, :])

    def norms(rows, y):
        x1 = x_ref[rows, :] + mod_ref[2:3, :] * _rms(y, g_ref[1:2, :])
        return x1, _modulate(x1, g_ref[2:3, :], mod_ref[3:4, :], mod_ref[4:5, :]).astype(BF16)

    def gate_up(h):
        return _dot(h, wg_ref[...]), _dot(h, wu_ref[...])

    def down(gu):
        return _dot((_silu(gu[0]) * gu[1]).astype(BF16), wd_ref[...])

    def finish(rows, x1, f):
        o_ref[rows, :] = x1 + mod_ref[5:6, :] * _rms(f, g_ref[3:4, :])

    y = {0: out_proj(blocks[0])}
    x1, gu = {}, {}
    for i in range(nb + 1):
        if i + 1 < nb:
            y[i + 1] = out_proj(blocks[i + 1])
        if i < nb:
            x1[i], h = norms(blocks[i], y.pop(i))
        if i >= 1:
            f = down(gu.pop(i - 1))
        if i < nb:
            gu[i] = gate_up(h)
        if i >= 1:
            finish(blocks[i - 1], x1.pop(i - 1), f)


def _post_call(x, a, b, mod, g, w_out, w_gate, w_up, w_down, layer, tm, mod_base, mod_stride):
    bsz, n, d = x.shape
    x_spec, _, _, mod_spec = _tile_specs(n, tm, mod_base, mod_stride)
    half_spec = pl.BlockSpec((None, tm, a.shape[2]), lambda b_, j: (b_, j, 0))

    def layer_resident(w):
        return pl.BlockSpec((None,) + w.shape[1:], lambda *_: (layer, 0, 0), pipeline_mode=pl.Buffered(1))

    return pl.pallas_call(
        _post_kernel,
        out_shape=jax.ShapeDtypeStruct(x.shape, F32),
        grid=(bsz, n // tm),
        in_specs=[x_spec, half_spec, half_spec, mod_spec, _resident(g.shape),
                  _resident(w_out.shape), layer_resident(w_gate), layer_resident(w_up),
                  layer_resident(w_down)],
        out_specs=x_spec,
        compiler_params=_params(2),
        name="post",
    )(x, a, b, mod, g, w_out, w_gate, w_up, w_down)


def _odd_in_kernel(x_ref, xp_ref, xn_ref, mod_ref, g_ref, w_ref, wp_ref, ps_ref, cs_ref,
                   pc_ref, xc_ref, xs_ref, us_ref, f2_ref, f4_ref, *, n_seq):
    j = pl.program_id(1)
    nt = pl.num_programs(1)
    tm = x_ref.shape[0]
    g = g_ref[0:1, :]
    shift = mod_ref[0:1, :]
    scale = mod_ref[1:2, :]
    xh = jnp.concatenate([x_ref[...], xp_ref[...], xn_ref[...]], axis=0)
    hz = _modulate(xh, g, shift, scale).astype(BF16)
    h = hz[0:tm]
    upz = _dot(hz, w_ref[:, 0:POOL_W])
    up, uph = upz[0:tm], upz[tm:]
    uf = _dot(h, w_ref[:, POOL_W:]).astype(BF16)
    for gi in range(FOURIER_W // GROUP):
        lanes = slice(GROUP * gi, GROUP * (gi + 1))
        cs = _dot(uf[:, lanes], cs_ref[...])
        xc_ref[:, lanes] = cs[:, 0:GROUP].astype(BF16)
        xs_ref[:, lanes] = cs[:, GROUP:].astype(BF16)
    rows = tm + 2 * HALO
    us_ref[0:HALO, :] = jnp.where(j > 0, uph[0:HALO], 0.0)
    us_ref[HALO:HALO + tm, :] = up
    us_ref[HALO + tm:rows, :] = jnp.where(j < nt - 1, uph[HALO:], 0.0)
    us_ref[rows:, :] = jnp.zeros((HALO, POOL_W), F32)
    f2_ref[0:rows, :] = us_ref[0:rows, GROUP:] + us_ref[1:rows + 1, GROUP:]
    f2_ref[rows:, :] = jnp.zeros((HALO, POOL_W - GROUP), F32)
    f4_ref[0:rows, :] = f2_ref[0:rows, GROUP:] + f2_ref[2:rows + 2, GROUP:]
    f4_ref[rows:, :] = jnp.zeros((HALO, POOL_W - 2 * GROUP), F32)
    f8 = f4_ref[0:rows, GROUP:] + f4_ref[4:rows + 4, GROUP:]
    sums = (us_ref[HALO - 1:HALO - 1 + tm, 0:GROUP] + up[:, 0:GROUP],
            f2_ref[HALO - 2:HALO - 2 + tm, 0:GROUP] + f2_ref[HALO:HALO + tm, 0:GROUP],
            f4_ref[HALO - 4:HALO - 4 + tm, 0:GROUP] + f4_ref[HALO:HALO + tm, 0:GROUP],
            f8[0:tm] + f8[HALO:HALO + tm])

    t = (j * tm + lax.broadcasted_iota(jnp.int32, (tm, 1), 0)).astype(F32)
    for gi, win in enumerate(POOL_WINDOWS):
        lanes = slice(GROUP * gi, GROUP * (gi + 1))
        cnt = jnp.minimum(t + float(win // 2), float(n_seq)) - jnp.maximum(t - float(win // 2), 0.0)
        diff = (sums[gi] / cnt - up[:, lanes]).astype(BF16)
        pc_ref[:, lanes] = (_dot(diff, wp_ref[gi]) * ps_ref[0:1, lanes]).astype(BF16)


def _odd_in_call(x, mod, g, w_in, w_pool, pool_scale, cs_mat, tm, mod_base, mod_stride):
    bsz, n, d = x.shape
    x_spec, prev_spec, next_spec, mod_spec = _tile_specs(n, tm, mod_base, mod_stride)
    out_spec = pl.BlockSpec((None, tm, POOL_W), lambda b, j: (b, j, 0))
    out_sds = jax.ShapeDtypeStruct((bsz, n, POOL_W), BF16)
    return pl.pallas_call(
        functools.partial(_odd_in_kernel, n_seq=n),
        out_shape=(out_sds, out_sds, out_sds),
        grid=(bsz, n // tm),
        in_specs=[x_spec, prev_spec, next_spec, mod_spec, _resident(g.shape), _resident(w_in.shape),
                  _resident(w_pool.shape), _resident(pool_scale.shape), _resident(cs_mat.shape)],
        out_specs=(out_spec, out_spec, out_spec),
        scratch_shapes=[pltpu.VMEM((tm + 3 * HALO, POOL_W), F32),
                        pltpu.VMEM((tm + 3 * HALO, POOL_W - GROUP), F32),
                        pltpu.VMEM((tm + 3 * HALO, POOL_W - 2 * GROUP), F32)],
        compiler_params=_params(2),
        name="odd_in",
    )(x, x, x, mod, g, w_in, w_pool, pool_scale, cs_mat)


def _four_kernel(c_ref, s_ref, xc_ref, xs_ref, wf_ref, o_ref, *, scale):
    for b in range(xc_ref.shape[0]):
        y = _dot(c_ref[...], xc_ref[b]) - _dot(s_ref[...], xs_ref[b])
        four = (y * scale).astype(BF16)
        for gi in range(FOURIER_W // GROUP):
            lanes = slice(GROUP * gi, GROUP * (gi + 1))
            o_ref[b, :, lanes] = _dot(four[:, lanes], wf_ref[gi]).astype(BF16)


def _four_call(cn, sn, xc, xs, w_four, tm, nb):
    bsz, n, w = xc.shape
    mat_spec = pl.BlockSpec((tm, n), lambda b, j: (j, 0))
    seq_spec = pl.BlockSpec((nb, n, w), lambda b, j: (b, 0, 0))
    return pl.pallas_call(
        functools.partial(_four_kernel, scale=float(1.0 / math.sqrt(n * GROUP))),
        out_shape=jax.ShapeDtypeStruct((bsz, n, w), BF16),
        grid=(bsz // nb, n // tm),
        in_specs=[mat_spec, mat_spec, seq_spec, seq_spec, _resident(w_four.shape)],
        out_specs=pl.BlockSpec((nb, tm, w), lambda b, j: (b, j, 0)),
        compiler_params=_params(2),
        name="fourier",
    )(cn, sn, xc, xs, w_four)


FLIP_BLOCK = 256


def _four_sym_kernel(c_ref, s_ref, pm_ref, xc_ref, xs_ref, wf_ref, o_ref, *, scale):
    half = o_ref.shape[0] // 2
    p = _dot(c_ref[...], xc_ref[...])
    q = _dot(s_ref[...], xs_ref[...])

    def project(rows, four):
        for gi in range(FOURIER_W // GROUP):
            lanes = slice(GROUP * gi, GROUP * (gi + 1))
            o_ref[rows, lanes] = _dot(four[:, lanes], wf_ref[gi]).astype(BF16)

    project(slice(0, half), ((p[0:half] - q[0:half]) * scale).astype(BF16))
    mirrored = ((p + q) * scale).astype(BF16)
    for b in range(half // FLIP_BLOCK):
        lo = half - FLIP_BLOCK * (b + 1)
        window = mirrored[lo:lo + FLIP_BLOCK + HALO, :]
        flipped = _dot(pm_ref[...], window).astype(BF16)
        project(slice(half + FLIP_BLOCK * b, half + FLIP_BLOCK * (b + 1)), flipped)


def _four_sym_call(c_half, s_half, perm, xc, xs, w_four):
    bsz, n, w = xc.shape
    seq_spec = pl.BlockSpec((None, n, w), lambda b: (b, 0, 0))
    return pl.pallas_call(
        functools.partial(_four_sym_kernel, scale=float(1.0 / math.sqrt(n * GROUP))),
        out_shape=jax.ShapeDtypeStruct((bsz, n, w), BF16),
        grid=(bsz,),
        in_specs=[_resident(c_half.shape), _resident(s_half.shape), _resident(perm.shape),
                  seq_spec, seq_spec, _resident(w_four.shape)],
        out_specs=seq_spec,
        compiler_params=_params(1),
        name="fourier_sym",
    )(c_half, s_half, perm, xc, xs, w_four)


def _flip_perm():
    pm = np.zeros((FLIP_BLOCK, FLIP_BLOCK + HALO), np.float32)
    pm[np.arange(FLIP_BLOCK), FLIP_BLOCK - np.arange(FLIP_BLOCK)] = 1.0
    return pm


def _rope_tables(n_tok):
    rows = n_tok // GRID_W
    row = np.repeat(np.arange(rows), GRID_W).astype(np.float64)
    col = np.tile(np.arange(GRID_W), rows).astype(np.float64)
    inv = ROPE_BASE ** (-np.arange(0, ROPE_AXIS, 2, dtype=np.float64) / ROPE_AXIS)
    ang_r = row[:, None] * inv[None, :]
    ang_c = col[:, None] * inv[None, :]
    ang = np.concatenate([ang_r, ang_r, ang_c, ang_c], axis=-1)
    cos = np.concatenate([np.cos(ang)] * 2, axis=-1)
    sin = np.concatenate([np.sin(ang)] * 2, axis=-1)
    first_half = (np.arange(HEAD) % 32) < 16
    sin_signed = np.where(first_half[None, :], -sin, sin)
    return jnp.asarray(cos, F32), jnp.asarray(sin_signed, F32)


def _dft_mats(n):
    idx = np.arange(n, dtype=np.int64)
    ang = 2.0 * np.pi * ((idx[:, None] * idx[None, :]) % n).astype(np.float64) / n
    return np.cos(ang), np.sin(ang)


def kernel(x_prompt, x_sample, cache_k, cache_v, c, c_ctx, w_mod, b_mod, norm_g,
           w_in_even, lam_params, subln_g, conv_w, w_out_even,
           w_in_odd, w_pool, pool_scale, w_fourier, w_out_odd,
           w_gate, w_up, w_down):
    depth = w_mod.shape[0]
    n_dec = x_sample.shape[0]
    n_p, n_s = x_prompt.shape[1], x_sample.shape[1]
    tm_p, tm_s = n_p, 512

    pad_rows = 16 - 1 - n_dec
    cc = jnp.concatenate([c_ctx[None, :], c, jnp.zeros((pad_rows, D_MODEL), F32)], axis=0)
    mod_all = _mod_call(cc, w_mod, b_mod)[:, :1 + n_dec].reshape(depth, 1 + n_dec, 6, D_MODEL)

    rope = _rope_tables(n_s)
    cc_g, sc_g = _dft_mats(GROUP)
    cs_mat = jnp.asarray(np.concatenate([cc_g, sc_g], axis=1), F32).astype(BF16)
    dft_p = tuple(jnp.asarray(m, F32).astype(BF16) for m in _dft_mats(n_p))
    dft_s = tuple(jnp.asarray(m[:n_s // 2 + HALO], F32).astype(BF16) for m in _dft_mats(n_s))
    flip = jnp.asarray(_flip_perm(), F32).astype(BF16)

    wg, wu, wd = w_gate.astype(BF16), w_up.astype(BF16), w_down.astype(BF16)
    xp, xs = x_prompt, x_sample
    new_k, new_v = [], []
    for l in range(depth):
        mod = mod_all[l]
        g = norm_g[l]
        i = l // 2
        streams = []
        if l % 2 == 0:
            lam_init = 0.8 - 0.6 * math.exp(-0.3 * l)
            w_in = w_in_even[i].astype(BF16)
            w_out = w_out_even[i].astype(BF16)
            sg = subln_g[i][None, :]
            qp, kp, vp, cbp = _even_in_call(xp, mod, g, w_in, conv_w[i], None, tm_p, 0, 0)
            ap = _attn_prompt_call(qp, kp, vp, lam_params[i], sg, lam_init, 8)
            new_k.append(kp)
            new_v.append(vp)
            qs, ks, vts, cbs = _even_in_call(xs, mod, g, w_in, conv_w[i], rope, 1024, 1, 1)
            a_s = _attn_cache_call(qs, ks, vts, lam_params[i], sg, cache_k, cache_v, i, lam_init)
            streams = [(ap, cbp), (a_s, cbs)]
        else:
            w_in = w_in_odd[i].astype(BF16)
            w_out = w_out_odd[i].astype(BF16)
            wp = w_pool[i].astype(BF16)
            wf = w_fourier[i].astype(BF16)
            ps = pool_scale[i][None, :]
            pcp, xcp, xsp = _odd_in_call(xp, mod, g, w_in, wp, ps, cs_mat, tm_p, 0, 0)
            fcp = _four_call(*dft_p, xcp, xsp, wf, n_p, 8)
            pcs, xcs, xss = _odd_in_call(xs, mod, g, w_in, wp, ps, cs_mat, tm_s, 1, 1)
            fcs = _four_sym_call(*dft_s, flip, xcs, xss, wf)
            streams = [(pcp, fcp), (pcs, fcs)]
        xp = _post_call(*(t.reshape(1, -1, t.shape[-1]) for t in (xp,) + streams[0]),
                        mod, g, w_out, wg, wu, wd, l, 512, 0, 0).reshape(x_prompt.shape)
        xs = _post_call(xs, streams[1][0], streams[1][1], mod, g, w_out, wg, wu, wd, l, 512, 1, 1)
    def stack_layers(parts):
        if len(parts) == 1:
            return parts[0][:, None]
        return jnp.stack(parts, axis=1)

    return xp, xs, stack_layers(new_k), stack_layers(new_v)
```

```python
import functools
import math

import numpy as np
import jax
import jax.numpy as jnp
from jax import lax
from jax.experimental import pallas as pl
from jax.experimental.pallas import tpu as pltpu

F32 = jnp.float32
BF16 = jnp.bfloat16

D_MODEL = 1024
GRID_W = 64
N_HEADS = 4
HEAD = 128
HALF_HEAD = 64
ROPE_AXIS = 32
ROPE_BASE = 10000.0
ATTN_W = 512
CONV_W = 512
POOL_W = 512
FOURIER_W = 512
GROUP = 128
POOL_WINDOWS = (2, 4, 8, 16)
D_FF = 2816
EPS = 1e-6
LOG2E = math.log2(math.e)
HALO = 8
MXU_N = 256
ATTN_SUB = 128
EVEN_BLOCK = 256
VMEM_LIMIT = 56 * 1024 * 1024


def _params(n_axes):
    return pltpu.CompilerParams(dimension_semantics=("arbitrary",) * n_axes,
                                vmem_limit_bytes=VMEM_LIMIT)


def _resident(shape):
    return pl.BlockSpec(shape, lambda *_: (0,) * len(shape), pipeline_mode=pl.Buffered(1))


def _rms(x, g):
    ms = jnp.mean(x * x, axis=-1, keepdims=True)
    return x * lax.rsqrt(ms + EPS) * g


def _modulate(x, g, shift, scale):
    return _rms(x, g) * (1.0 + scale) + shift


def _dot(a, b):
    return jnp.dot(a, b, preferred_element_type=F32)


def _silu(x):
    return x / (1.0 + jnp.exp(-x))


def _mod_kernel(cc_ref, w_ref, b_ref, o_ref):
    s = _silu(cc_ref[...]).astype(BF16)
    o_ref[...] = _dot(s, w_ref[...].astype(BF16)) + b_ref[...]


def _mod_call(cc, w_mod, b_mod):
    depth, d, n6 = w_mod.shape
    rows = cc.shape[0]
    tn = 2048
    return pl.pallas_call(
        _mod_kernel,
        out_shape=jax.ShapeDtypeStruct((depth, rows, n6), F32),
        grid=(depth, n6 // tn),
        in_specs=[
            pl.BlockSpec((rows, d), lambda l, j: (0, 0)),
            pl.BlockSpec((None, d, tn), lambda l, j: (l, 0, j)),
            pl.BlockSpec((None, 1, tn), lambda l, j: (l, 0, j)),
        ],
        out_specs=pl.BlockSpec((None, rows, tn), lambda l, j: (l, 0, j)),
        compiler_params=_params(2),
        name="mod",
    )(cc, w_mod, b_mod.reshape(depth, 1, n6))


def _tile_specs(n, tm, mod_base, mod_stride):
    nb8 = n // HALO
    t8 = tm // HALO
    x_spec = pl.BlockSpec((None, tm, D_MODEL), lambda b, j: (b, j, 0))
    prev_spec = pl.BlockSpec((None, HALO, D_MODEL),
                             lambda b, j: (b, jnp.maximum(j * t8 - 1, 0), 0))
    next_spec = pl.BlockSpec((None, HALO, D_MODEL),
                             lambda b, j: (b, jnp.minimum((j + 1) * t8, nb8 - 1), 0))
    mod_spec = pl.BlockSpec((None, 6, D_MODEL),
                            lambda b, j: (mod_base + mod_stride * b, 0, 0))
    return x_spec, prev_spec, next_spec, mod_spec


def _rope(t, cos, sin_signed, first_half):
    outs = []
    for hh in range(N_HEADS):
        th = t[:, HEAD * hh:HEAD * (hh + 1)]
        swapped = jnp.where(first_half, pltpu.roll(th, HEAD - 16, 1), pltpu.roll(th, 16, 1))
        outs.append(th * cos + swapped * sin_signed)
    return outs


def _even_in_kernel(*refs, use_rope):
    if use_rope:
        (x_ref, xp_ref, xn_ref, mod_ref, g_ref, w_ref, cw_ref, cos_ref, sin_ref,
         q_ref, k_ref, v_ref, cb_ref, zs_ref) = refs
        j = pl.program_id(1)
        nt = pl.num_programs(1)
        blk = EVEN_BLOCK
        nblk = x_ref.shape[0] // blk
        lane = lax.broadcasted_iota(jnp.int32, (1, HEAD), 1)
        first_half = (lane % 32) < 16
    else:
        x_ref, mod_ref, g_ref, w_ref, cw_ref, q_ref, k_ref, v_ref, cb_ref, zs_ref = refs
        nblk, blk = x_ref.shape[0], x_ref.shape[1]
    g = g_ref[0:1, :]
    shift = mod_ref[0:1, :]
    scale = mod_ref[1:2, :]
    stride = blk if use_rope else blk + HALO

    def z0(i):
        return HALO + i * stride

    def rows(i):
        return slice(i * blk, (i + 1) * blk)

    def proj(hh, lo):
        return _dot(hh, w_ref[:, lo:lo + 512])

    def modulated(i):
        xi = x_ref[rows(i), :] if use_rope else x_ref[i]
        return _modulate(xi, g, shift, scale).astype(BF16)

    def conv_inputs(i, h):
        outer = use_rope and i == 0
        if outer:
            xh = jnp.concatenate([xp_ref[...], xn_ref[...]], axis=0)
            hz = jnp.concatenate([h, _modulate(xh, g, shift, scale).astype(BF16)], axis=0)
        else:
            hz = h
        z = proj(hz, 2048) * proj(hz, 2560)
        if outer:
            zh = z[blk:]
            z = z[0:blk]
            zs_ref[0:HALO, :] = jnp.where(j > 0, zh[0:HALO], 0.0)
            zs_ref[z0(nblk):, :] = jnp.where(j < nt - 1, zh[HALO:], 0.0)
        zs_ref[z0(i):z0(i) + blk, :] = z
        return z, proj(h, 1536)

    def conv_out(i, z, gate_b):
        conv = (cw_ref[0:1, :] * zs_ref[z0(i) - 1:z0(i) - 1 + blk, :] + cw_ref[1:2, :] * z
                + cw_ref[2:3, :] * zs_ref[z0(i) + 1:z0(i) + 1 + blk, :])
        out = (gate_b * conv).astype(BF16)
        if use_rope:
            cb_ref[rows(i), :] = out
        else:
            cb_ref[i] = out

    def qkv(i, h):
        v = proj(h, 1024)
        for hh in range(N_HEADS):
            vh = v[:, HEAD * hh:HEAD * (hh + 1)]
            if use_rope:
                v_ref[hh, :, rows(i)] = vh.T.astype(v_ref.dtype)
            else:
                v_ref[i, hh] = vh.astype(v_ref.dtype)
        for ref, t in ((k_ref, proj(h, 512)), (q_ref, proj(h, 0) * (HALF_HEAD ** -0.5 * LOG2E))):
            if use_rope:
                heads = _rope(t, cos_ref[rows(i), :], sin_ref[rows(i), :], first_half)
            else:
                heads = [t[:, HEAD * hh:HEAD * (hh + 1)] for hh in range(N_HEADS)]
            for hh in range(N_HEADS):
                if use_rope:
                    ref[hh, rows(i), :] = heads[hh].astype(ref.dtype)
                else:
                    ref[i, hh] = heads[hh].astype(ref.dtype)

    if not use_rope:
        for i in range(nblk + 1):
            zs_ref[i * stride:i * stride + HALO, :] = jnp.zeros((HALO, CONV_W), F32)

    h = {0: modulated(0)}
    zg = {0: conv_inputs(0, h[0])}
    for i in range(nblk):
        if i + 1 < nblk:
            h[i + 1] = modulated(i + 1)
        else:
            conv_out(i, *zg.pop(i))
        qkv(i, h.pop(i))
        if i + 1 < nblk:
            zg[i + 1] = conv_inputs(i + 1, h[i + 1])
            conv_out(i, *zg.pop(i))


def _even_in_call(x, mod, g, w_in, conv_w, rope, tm, mod_base, mod_stride):
    bsz, n, d = x.shape
    use_rope = rope is not None
    head_shape = (bsz, N_HEADS, n, HEAD)
    if use_rope:
        x_spec, prev_spec, next_spec, mod_spec = _tile_specs(n, tm, mod_base, mod_stride)
        tab = pl.BlockSpec((tm, HEAD), lambda b, j: (j, 0))
        in_specs = [x_spec, prev_spec, next_spec, mod_spec, _resident(g.shape), _resident(w_in.shape),
                    _resident(conv_w.shape), tab, tab]
        args = [x, x, x, mod, g, w_in, conv_w, *rope]
        head_spec = pl.BlockSpec((None, N_HEADS, tm, HEAD), lambda b, j: (b, 0, j, 0))
        v_spec = pl.BlockSpec((None, N_HEADS, HEAD, tm), lambda b, j: (b, 0, 0, j))
        cb_spec = pl.BlockSpec((None, tm, CONV_W), lambda b, j: (b, j, 0))
        k_sds = jax.ShapeDtypeStruct(head_shape, BF16)
        v_sds = jax.ShapeDtypeStruct((bsz, N_HEADS, HEAD, n), BF16)
        grid = (bsz, n // tm)
        zs_rows = tm + 2 * HALO
    else:
        ns = tm // n
        in_specs = [pl.BlockSpec((ns, n, d), lambda b: (b, 0, 0)),
                    pl.BlockSpec((None, 6, d), lambda b: (mod_base, 0, 0)),
                    _resident(g.shape), _resident(w_in.shape), _resident(conv_w.shape)]
        args = [x, mod, g, w_in, conv_w]
        head_spec = v_spec = pl.BlockSpec((ns, N_HEADS, n, HEAD), lambda b: (b, 0, 0, 0))
        cb_spec = pl.BlockSpec((ns, n, CONV_W), lambda b: (b, 0, 0))
        k_sds = v_sds = jax.ShapeDtypeStruct(head_shape, F32)
        grid = (bsz // ns,)
        zs_rows = ns * (n + HALO) + HALO
    return pl.pallas_call(
        functools.partial(_even_in_kernel, use_rope=use_rope),
        out_shape=(jax.ShapeDtypeStruct(head_shape, BF16), k_sds, v_sds,
                   jax.ShapeDtypeStruct((bsz, n, CONV_W), BF16)),
        grid=grid,
        in_specs=in_specs,
        out_specs=(head_spec, head_spec, v_spec, cb_spec),
        scratch_shapes=[pltpu.VMEM((zs_rows, CONV_W), F32)],
        compiler_params=_params(len(grid)),
        name="even_in_rope" if use_rope else "even_in",
    )(*args)


def _diff_lambda(lam_ref, lam_init):
    lp = lam_ref[...]
    return (jnp.exp(jnp.sum(lp[0:1] * lp[1:2], axis=-1, keepdims=True))
            - jnp.exp(jnp.sum(lp[2:3] * lp[3:4], axis=-1, keepdims=True)) + lam_init)


def _stack_components(q):
    lane = lax.broadcasted_iota(jnp.int32, (1, HEAD), 1)
    zero = jnp.zeros_like(q)
    return jnp.concatenate([jnp.where(lane < HALF_HEAD, q, zero),
                            jnp.where(lane >= HALF_HEAD, q, zero)], axis=0)


def _softmax_pv(s, v_ext):
    e = jnp.exp2(s - jnp.max(s, axis=-1, keepdims=True)).astype(BF16)
    return _dot(e, v_ext)


def _normalise(ov, lam, sg, lam_init):
    t = ov.shape[0] // 2
    o = ov[:t, :HEAD] / ov[:t, HEAD:] - lam * (ov[t:, :HEAD] / ov[t:, HEAD:])
    return (_rms(o, sg) * (1.0 - lam_init)).astype(BF16)


def _chain_pipeline(n_groups, step_fn, finish_fn):
    assert n_groups % 2 == 0 and n_groups >= 2
    step_fn(0, 0, None, None)
    step_fn(1, 1, 0, 0)

    def body(t, carry):
        g = 2 * t
        step_fn(g, 0, g - 1, 1)
        finish_fn(g - 2, 0)
        step_fn(g + 1, 1, g, 0)
        finish_fn(g - 1, 1)
        return carry

    lax.fori_loop(1, n_groups // 2, body, 0)
    step_fn(None, None, n_groups - 1, 1)
    finish_fn(n_groups - 2, 0)
    finish_fn(n_groups - 1, 1)


def _attn_prompt_kernel(lam_ref, sg_ref, q_ref, k_ref, v_ref, o_ref, s_ref, ov_ref, *, lam_init):
    lam = _diff_lambda(lam_ref, lam_init)
    sg = sg_ref[...]
    n = k_ref.shape[2]
    ones = jnp.ones((n, MXU_N - HEAD), BF16)

    def step_fn(bs, ps, bv, pv):
        for hh in range(N_HEADS):
            if bs is not None:
                s_ref[ps * N_HEADS + hh] = lax.dot_general(
                    _stack_components(q_ref[bs, hh]), k_ref[bs, hh].astype(BF16),
                    (((1,), (1,)), ((), ())), preferred_element_type=F32)
        for hh in range(N_HEADS):
            if bv is not None:
                v_ext = jnp.concatenate([v_ref[bv, hh].astype(BF16), ones], axis=1)
                ov_ref[pv * N_HEADS + hh] = _softmax_pv(s_ref[pv * N_HEADS + hh], v_ext)

    def finish_fn(b, par):
        for hh in range(N_HEADS):
            o_ref[b, :, HEAD * hh:HEAD * (hh + 1)] = _normalise(ov_ref[par * N_HEADS + hh], lam, sg, lam_init)

    _chain_pipeline(q_ref.shape[0], step_fn, finish_fn)


def _attn_prompt_call(q, k, v, lam_params, subln_g, lam_init, nb):
    bsz, nh, n, hd = q.shape
    spec = pl.BlockSpec((nb, nh, n, hd), lambda b: (b, 0, 0, 0))
    return pl.pallas_call(
        functools.partial(_attn_prompt_kernel, lam_init=lam_init),
        out_shape=jax.ShapeDtypeStruct((bsz, n, nh * hd), BF16),
        grid=(bsz // nb,),
        in_specs=[_resident(lam_params.shape), _resident(subln_g.shape), spec, spec, spec],
        out_specs=pl.BlockSpec((nb, n, nh * hd), lambda b: (b, 0, 0)),
        scratch_shapes=[pltpu.VMEM((2 * nh, 2 * n, n), F32), pltpu.VMEM((2 * nh, 2 * n, MXU_N), F32)],
        compiler_params=_params(1),
        name="attn",
    )(lam_params, subln_g, q, k, v)


POST_BLOCK = 256
ONES_ROWS = 16
ATTN_WIDTH = 2
KEY_CHUNK = 256


def _attn_cache_kernel(lam_ref, sg_ref, q_ref, k_ref, vt_ref, ck_ref, cv_ref, o_ref,
                       kbuf_ref, vtbuf_ref, s_ref, ov_ref, m_ref, *, lam_init):
    past = ck_ref.shape[0]
    lk = kbuf_ref.shape[0]
    kbuf_ref[0:past, :] = ck_ref[...].astype(BF16)
    kbuf_ref[past:, :] = k_ref[...]
    vtbuf_ref[0:HEAD, 0:past] = cv_ref[...].T.astype(BF16)
    vtbuf_ref[0:HEAD, past:] = vt_ref[...]
    vtbuf_ref[HEAD:, :] = jnp.ones((ONES_ROWS, lk), BF16)
    lam = _diff_lambda(lam_ref, lam_init)
    sg = sg_ref[...] * (1.0 - lam_init)

    def rows(g, u):
        start = (g * ATTN_WIDTH + u) * ATTN_SUB
        return pl.ds(pl.multiple_of(start, ATTN_SUB), ATTN_SUB)

    def step_fn(gs, ps, gv, pv):
        chains = range(ATTN_WIDTH)
        if gs is not None:
            qq = [_stack_components(q_ref[rows(gs, u), :]) for u in chains]
            col_max = [None] * ATTN_WIDTH
        if gv is not None:
            m_prev = [m_ref[pv * ATTN_WIDTH + u] for u in chains]
            acc = [None] * ATTN_WIDTH
        for c0 in range(0, lk, KEY_CHUNK):
            keys = slice(c0, c0 + KEY_CHUNK)
            for u in chains:
                if gs is not None:
                    s = lax.dot_general(kbuf_ref[keys, :], qq[u], (((1,), (1,)), ((), ())),
                                        preferred_element_type=F32)
                    s_ref[ps * ATTN_WIDTH + u, keys, :] = s
                    cm = jnp.max(s, axis=0, keepdims=True)
                    col_max[u] = cm if col_max[u] is None else jnp.maximum(col_max[u], cm)
            for u in chains:
                if gv is not None:
                    e = jnp.exp2(s_ref[pv * ATTN_WIDTH + u, keys, :] - m_prev[u]).astype(BF16)
                    part = _dot(vtbuf_ref[:, keys], e)
                    acc[u] = part if acc[u] is None else acc[u] + part
        for u in chains:
            if gs is not None:
                m_ref[ps * ATTN_WIDTH + u] = col_max[u]
            if gv is not None:
                ov_ref[pv * ATTN_WIDTH + u] = acc[u]

    def finish_fn(g, par):
        for u in range(ATTN_WIDTH):
            ov = ov_ref[par * ATTN_WIDTH + u]
            o_t = (ov[0:HEAD, 0:ATTN_SUB] / ov[HEAD:HEAD + 1, 0:ATTN_SUB]
                   - lam * (ov[0:HEAD, ATTN_SUB:] / ov[HEAD:HEAD + 1, ATTN_SUB:]))
            ms = jnp.mean(o_t * o_t, axis=0, keepdims=True)
            o_ref[rows(g, u), :] = ((o_t * lax.rsqrt(ms + EPS)).T * sg).astype(BF16)

    _chain_pipeline(q_ref.shape[0] // (ATTN_SUB * ATTN_WIDTH), step_fn, finish_fn)


def _attn_cache_call(q, k, vt, lam_params, subln_g, cache_k, cache_v, layer, lam_init):
    bsz, nh, n, hd = q.shape
    past = cache_k.shape[3]
    seq_spec = pl.BlockSpec((None, None, n, hd), lambda b, h: (b, h, 0, 0))
    vt_spec = pl.BlockSpec((None, None, hd, n), lambda b, h: (b, h, 0, 0))
    c_spec = pl.BlockSpec((None, None, None, past, hd), lambda b, h: (b, layer, h, 0, 0))
    return pl.pallas_call(
        functools.partial(_attn_cache_kernel, lam_init=lam_init),
        out_shape=jax.ShapeDtypeStruct((bsz, n, nh * hd), BF16),
        grid=(bsz, nh),
        in_specs=[_resident(lam_params.shape), _resident(subln_g.shape),
                  seq_spec, seq_spec, vt_spec, c_spec, c_spec],
        out_specs=pl.BlockSpec((None, n, hd), lambda b, h: (b, 0, h)),
        scratch_shapes=[pltpu.VMEM((past + n, hd), BF16), pltpu.VMEM((hd + ONES_ROWS, past + n), BF16),
                        pltpu.VMEM((2 * ATTN_WIDTH, past + n, 2 * ATTN_SUB), F32),
                        pltpu.VMEM((2 * ATTN_WIDTH, hd + ONES_ROWS, 2 * ATTN_SUB), F32),
                        pltpu.VMEM((2 * ATTN_WIDTH, 1, 2 * ATTN_SUB), F32)],
        compiler_params=_params(2),
        name="attn_cache",
    )(lam_params, subln_g, q, k, vt, cache_k, cache_v)


def _post_kernel(x_ref, a_ref, b_ref, mod_ref, g_ref, wo_ref, wg_ref, wu_ref, wd_ref, o_ref):
    half = a_ref.shape[1]
    nb = x_ref.shape[0] // POST_BLOCK
    blocks = [slice(i * POST_BLOCK, (i + 1) * POST_BLOCK) for i in range(nb)]

    def out_proj(rows):
        return _dot(a_ref[rows, :], wo_ref[0:half, :]) + _dot(b_ref[rows, :], wo_ref[half:, :])

    def norms(rows, y):
        x1 = x_ref[rows, :] + mod_ref[2:3, :] * _rms(y, g_ref[1:2, :])
        return x1, _modulate(x1, g_ref[2:3, :], mod_ref[3:4, :], mod_ref[4:5, :]).astype(BF16)

    def gate_up(h):
        return _dot(h, wg_ref[...]), _dot(h, wu_ref[...])

    def down(gu):
        return _dot((_silu(gu[0]) * gu[1]).astype(BF16), wd_ref[...])

    def finish(rows, x1, f):
        o_ref[rows, :] = x1 + mod_ref[5:6, :] * _rms(f, g_ref[3:4, :])

    y = {0: out_proj(blocks[0])}
    x1, gu = {}, {}
    for i in range(nb + 1):
        if i + 1 < nb:
            y[i + 1] = out_proj(blocks[i + 1])
        if i < nb:
            x1[i], h = norms(blocks[i], y.pop(i))
        if i >= 1:
            f = down(gu.pop(i - 1))
        if i < nb:
            gu[i] = gate_up(h)
        if i >= 1:
            finish(blocks[i - 1], x1.pop(i - 1), f)


def _post_call(x, a, b, mod, g, w_out, w_gate, w_up, w_down, layer, tm, mod_base, mod_stride):
    bsz, n, d = x.shape
    x_spec, _, _, mod_spec = _tile_specs(n, tm, mod_base, mod_stride)
    half_spec = pl.BlockSpec((None, tm, a.shape[2]), lambda b_, j: (b_, j, 0))

    def layer_resident(w):
        return pl.BlockSpec((None,) + w.shape[1:], lambda *_: (layer, 0, 0), pipeline_mode=pl.Buffered(1))

    return pl.pallas_call(
        _post_kernel,
        out_shape=jax.ShapeDtypeStruct(x.shape, F32),
        grid=(bsz, n // tm),
        in_specs=[x_spec, half_spec, half_spec, mod_spec, _resident(g.shape),
                  _resident(w_out.shape), layer_resident(w_gate), layer_resident(w_up),
                  layer_resident(w_down)],
        out_specs=x_spec,
        compiler_params=_params(2),
        name="post",
    )(x, a, b, mod, g, w_out, w_gate, w_up, w_down)


def _odd_in_kernel(*refs, n_seq, whole_seqs):
    if whole_seqs:
        (x_ref, mod_ref, g_ref, w_ref, wp_ref, ps_ref, cs_ref,
         pc_ref, xc_ref, xs_ref, us_ref, f2_ref, f4_ref) = refs
        nblk, blk = x_ref.shape[0], x_ref.shape[1]
    else:
        (x_ref, xp_ref, xn_ref, mod_ref, g_ref, w_ref, wp_ref, ps_ref, cs_ref,
         pc_ref, xc_ref, xs_ref, us_ref, f2_ref, f4_ref) = refs
        j = pl.program_id(1)
        nt = pl.num_programs(1)
        nblk, blk = 1, x_ref.shape[0]
    tm = nblk * blk
    g = g_ref[0:1, :]
    shift = mod_ref[0:1, :]
    scale = mod_ref[1:2, :]
    if whole_seqs:
        h = _modulate(x_ref[...].reshape(tm, x_ref.shape[2]), g, shift, scale).astype(BF16)
        up = _dot(h, w_ref[:, 0:POOL_W])
    else:
        xh = jnp.concatenate([x_ref[...], xp_ref[...], xn_ref[...]], axis=0)
        hz = _modulate(xh, g, shift, scale).astype(BF16)
        h = hz[0:tm]
        upz = _dot(hz, w_ref[:, 0:POOL_W])
        up, uph = upz[0:tm], upz[tm:]
    uf = _dot(h, w_ref[:, POOL_W:]).astype(BF16)

    def store(ref, lanes, val):
        if whole_seqs:
            for i in range(nblk):
                ref[i, :, lanes] = val[i * blk:(i + 1) * blk]
        else:
            ref[:, lanes] = val

    for gi in range(FOURIER_W // GROUP):
        lanes = slice(GROUP * gi, GROUP * (gi + 1))
        cs = _dot(uf[:, lanes], cs_ref[...])
        store(xc_ref, lanes, cs[:, 0:GROUP].astype(BF16))
        store(xs_ref, lanes, cs[:, GROUP:].astype(BF16))

    stride = blk + HALO

    def u0(i):
        return HALO + i * stride

    rows = nblk * stride + HALO
    zeros = jnp.zeros((HALO, POOL_W), F32)
    for i in range(nblk):
        us_ref[u0(i):u0(i) + blk, :] = up[i * blk:(i + 1) * blk]
        if whole_seqs:
            us_ref[u0(i) - HALO:u0(i), :] = zeros
    if whole_seqs:
        us_ref[rows - HALO:rows, :] = zeros
    else:
        us_ref[0:HALO, :] = jnp.where(j > 0, uph[0:HALO], 0.0)
        us_ref[rows - HALO:rows, :] = jnp.where(j < nt - 1, uph[HALO:], 0.0)
    us_ref[rows:, :] = zeros
    f2_ref[0:rows, :] = us_ref[0:rows, GROUP:] + us_ref[1:rows + 1, GROUP:]
    f2_ref[rows:, :] = jnp.zeros((HALO, POOL_W - GROUP), F32)
    f4_ref[0:rows, :] = f2_ref[0:rows, GROUP:] + f2_ref[2:rows + 2, GROUP:]
    f4_ref[rows:, :] = jnp.zeros((HALO, POOL_W - 2 * GROUP), F32)
    f8 = f4_ref[0:rows, GROUP:] + f4_ref[4:rows + 4, GROUP:]

    def centred(i):
        a = u0(i)
        return (us_ref[a - 1:a - 1 + blk, 0:GROUP] + us_ref[a:a + blk, 0:GROUP],
                f2_ref[a - 2:a - 2 + blk, 0:GROUP] + f2_ref[a:a + blk, 0:GROUP],
                f4_ref[a - 4:a - 4 + blk, 0:GROUP] + f4_ref[a:a + blk, 0:GROUP],
                f8[a - HALO:a - HALO + blk] + f8[a:a + blk])

    sums = [centred(i) for i in range(nblk)]
    pos = lax.broadcasted_iota(jnp.int32, (blk, 1), 0)
    t = (pos if whole_seqs else j * blk + pos).astype(F32)
    for gi, win in enumerate(POOL_WINDOWS):
        lanes = slice(GROUP * gi, GROUP * (gi + 1))
        cnt = jnp.minimum(t + float(win // 2), float(n_seq)) - jnp.maximum(t - float(win // 2), 0.0)
        pooled = jnp.concatenate([s[gi] / cnt for s in sums], axis=0)
        diff = (pooled - up[:, lanes]).astype(BF16)
        store(pc_ref, lanes, (_dot(diff, wp_ref[gi]) * ps_ref[0:1, lanes]).astype(BF16))


def _odd_in_call(x, mod, g, w_in, w_pool, pool_scale, cs_mat, tm, mod_base, mod_stride):
    bsz, n, d = x.shape
    whole_seqs = tm >= n
    weights = [_resident(g.shape), _resident(w_in.shape), _resident(w_pool.shape),
               _resident(pool_scale.shape), _resident(cs_mat.shape)]
    if whole_seqs:
        ns = tm // n
        in_specs = [pl.BlockSpec((ns, n, d), lambda b: (b, 0, 0)),
                    pl.BlockSpec((None, 6, d), lambda b: (mod_base, 0, 0))] + weights
        args = [x, mod, g, w_in, w_pool, pool_scale, cs_mat]
        out_spec = pl.BlockSpec((ns, n, POOL_W), lambda b: (b, 0, 0))
        grid = (bsz // ns,)
        scratch_rows = ns * (n + HALO) + 2 * HALO
    else:
        x_spec, prev_spec, next_spec, mod_spec = _tile_specs(n, tm, mod_base, mod_stride)
        in_specs = [x_spec, prev_spec, next_spec, mod_spec] + weights
        args = [x, x, x, mod, g, w_in, w_pool, pool_scale, cs_mat]
        out_spec = pl.BlockSpec((None, tm, POOL_W), lambda b, j: (b, j, 0))
        grid = (bsz, n // tm)
        scratch_rows = tm + 3 * HALO
    out_sds = jax.ShapeDtypeStruct((bsz, n, POOL_W), BF16)
    return pl.pallas_call(
        functools.partial(_odd_in_kernel, n_seq=n, whole_seqs=whole_seqs),
        out_shape=(out_sds, out_sds, out_sds),
        grid=grid,
        in_specs=in_specs,
        out_specs=(out_spec, out_spec, out_spec),
        scratch_shapes=[pltpu.VMEM((scratch_rows, POOL_W), F32),
                        pltpu.VMEM((scratch_rows, POOL_W - GROUP), F32),
                        pltpu.VMEM((scratch_rows, POOL_W - 2 * GROUP), F32)],
        compiler_params=_params(len(grid)),
        name="odd_in",
    )(*args)


def _four_kernel(c_ref, s_ref, xc_ref, xs_ref, wf_ref, o_ref, *, scale):
    for b in range(xc_ref.shape[0]):
        y = _dot(c_ref[...], xc_ref[b]) - _dot(s_ref[...], xs_ref[b])
        four = (y * scale).astype(BF16)
        for gi in range(FOURIER_W // GROUP):
            lanes = slice(GROUP * gi, GROUP * (gi + 1))
            o_ref[b, :, lanes] = _dot(four[:, lanes], wf_ref[gi]).astype(BF16)


def _four_call(cn, sn, xc, xs, w_four, tm, nb):
    bsz, n, w = xc.shape
    mat_spec = pl.BlockSpec((tm, n), lambda b, j: (j, 0))
    seq_spec = pl.BlockSpec((nb, n, w), lambda b, j: (b, 0, 0))
    return pl.pallas_call(
        functools.partial(_four_kernel, scale=float(1.0 / math.sqrt(n * GROUP))),
        out_shape=jax.ShapeDtypeStruct((bsz, n, w), BF16),
        grid=(bsz // nb, n // tm),
        in_specs=[mat_spec, mat_spec, seq_spec, seq_spec, _resident(w_four.shape)],
        out_specs=pl.BlockSpec((nb, tm, w), lambda b, j: (b, j, 0)),
        compiler_params=_params(2),
        name="fourier",
    )(cn, sn, xc, xs, w_four)


FLIP_BLOCK = 256


def _four_sym_kernel(c_ref, s_ref, pm_ref, xc_ref, xs_ref, wf_ref, o_ref, *, scale):
    half = o_ref.shape[0] // 2
    p = _dot(c_ref[...], xc_ref[...])
    q = _dot(s_ref[...], xs_ref[...])

    def project(rows, four):
        for gi in range(FOURIER_W // GROUP):
            lanes = slice(GROUP * gi, GROUP * (gi + 1))
            o_ref[rows, lanes] = _dot(four[:, lanes], wf_ref[gi]).astype(BF16)

    project(slice(0, half), ((p[0:half] - q[0:half]) * scale).astype(BF16))
    mirrored = ((p + q) * scale).astype(BF16)
    for b in range(half // FLIP_BLOCK):
        lo = half - FLIP_BLOCK * (b + 1)
        window = mirrored[lo:lo + FLIP_BLOCK + HALO, :]
        flipped = _dot(pm_ref[...], window).astype(BF16)
        project(slice(half + FLIP_BLOCK * b, half + FLIP_BLOCK * (b + 1)), flipped)


def _four_sym_call(c_half, s_half, perm, xc, xs, w_four):
    bsz, n, w = xc.shape
    seq_spec = pl.BlockSpec((None, n, w), lambda b: (b, 0, 0))
    return pl.pallas_call(
        functools.partial(_four_sym_kernel, scale=float(1.0 / math.sqrt(n * GROUP))),
        out_shape=jax.ShapeDtypeStruct((bsz, n, w), BF16),
        grid=(bsz,),
        in_specs=[_resident(c_half.shape), _resident(s_half.shape), _resident(perm.shape),
                  seq_spec, seq_spec, _resident(w_four.shape)],
        out_specs=seq_spec,
        compiler_params=_params(1),
        name="fourier_sym",
    )(c_half, s_half, perm, xc, xs, w_four)


def _flip_perm():
    pm = np.zeros((FLIP_BLOCK, FLIP_BLOCK + HALO), np.float32)
    pm[np.arange(FLIP_BLOCK), FLIP_BLOCK - np.arange(FLIP_BLOCK)] = 1.0
    return pm


def _rope_tables(n_tok):
    rows = n_tok // GRID_W
    row = np.repeat(np.arange(rows), GRID_W).astype(np.float64)
    col = np.tile(np.arange(GRID_W), rows).astype(np.float64)
    inv = ROPE_BASE ** (-np.arange(0, ROPE_AXIS, 2, dtype=np.float64) / ROPE_AXIS)
    ang_r = row[:, None] * inv[None, :]
    ang_c = col[:, None] * inv[None, :]
    ang = np.concatenate([ang_r, ang_r, ang_c, ang_c], axis=-1)
    cos = np.concatenate([np.cos(ang)] * 2, axis=-1)
    sin = np.concatenate([np.sin(ang)] * 2, axis=-1)
    first_half = (np.arange(HEAD) % 32) < 16
    sin_signed = np.where(first_half[None, :], -sin, sin)
    return jnp.asarray(cos, F32), jnp.asarray(sin_signed, F32)


def _dft_mats(n):
    idx = np.arange(n, dtype=np.int64)
    ang = 2.0 * np.pi * ((idx[:, None] * idx[None, :]) % n).astype(np.float64) / n
    return np.cos(ang), np.sin(ang)


def kernel(x_prompt, x_sample, cache_k, cache_v, c, c_ctx, w_mod, b_mod, norm_g,
           w_in_even, lam_params, subln_g, conv_w, w_out_even,
           w_in_odd, w_pool, pool_scale, w_fourier, w_out_odd,
           w_gate, w_up, w_down):
    depth = w_mod.shape[0]
    n_dec = x_sample.shape[0]
    n_p, n_s = x_prompt.shape[1], x_sample.shape[1]
    tm_s = 512

    pad_rows = 16 - 1 - n_dec
    cc = jnp.concatenate([c_ctx[None, :], c, jnp.zeros((pad_rows, D_MODEL), F32)], axis=0)
    mod_all = _mod_call(cc, w_mod, b_mod)[:, :1 + n_dec].reshape(depth, 1 + n_dec, 6, D_MODEL)

    rope = _rope_tables(n_s)
    cc_g, sc_g = _dft_mats(GROUP)
    cs_mat = jnp.asarray(np.concatenate([cc_g, sc_g], axis=1), F32).astype(BF16)
    dft_p = tuple(jnp.asarray(m, F32).astype(BF16) for m in _dft_mats(n_p))
    dft_s = tuple(jnp.asarray(m[:n_s // 2 + HALO], F32).astype(BF16) for m in _dft_mats(n_s))
    flip = jnp.asarray(_flip_perm(), F32).astype(BF16)

    wg, wu, wd = w_gate.astype(BF16), w_up.astype(BF16), w_down.astype(BF16)
    xp, xs = x_prompt, x_sample
    new_k, new_v = [], []
    for l in range(depth):
        mod = mod_all[l]
        g = norm_g[l]
        i = l // 2
        streams = []
        if l % 2 == 0:
            lam_init = 0.8 - 0.6 * math.exp(-0.3 * l)
            w_in = w_in_even[i].astype(BF16)
            w_out = w_out_even[i].astype(BF16)
            sg = subln_g[i][None, :]
            qp, kp, vp, cbp = _even_in_call(xp, mod, g, w_in, conv_w[i], None, 4 * n_p, 0, 0)
            ap = _attn_prompt_call(qp, kp, vp, lam_params[i], sg, lam_init, 8)
            new_k.append(kp)
            new_v.append(vp)
            qs, ks, vts, cbs = _even_in_call(xs, mod, g, w_in, conv_w[i], rope, 1024, 1, 1)
            a_s = _attn_cache_call(qs, ks, vts, lam_params[i], sg, cache_k, cache_v, i, lam_init)
            streams = [(ap, cbp), (a_s, cbs)]
        else:
            w_in = w_in_odd[i].astype(BF16)
            w_out = w_out_odd[i].astype(BF16)
            wp = w_pool[i].astype(BF16)
            wf = w_fourier[i].astype(BF16)
            ps = pool_scale[i][None, :]
            pcp, xcp, xsp = _odd_in_call(xp, mod, g, w_in, wp, ps, cs_mat, 4 * n_p, 0, 0)
            fcp = _four_call(*dft_p, xcp, xsp, wf, n_p, 8)
            pcs, xcs, xss = _odd_in_call(xs, mod, g, w_in, wp, ps, cs_mat, tm_s, 1, 1)
            fcs = _four_sym_call(*dft_s, flip, xcs, xss, wf)
            streams = [(pcp, fcp), (pcs, fcs)]
        xp = _post_call(*(t.reshape(1, -1, t.shape[-1]) for t in (xp,) + streams[0]),
                        mod, g, w_out, wg, wu, wd, l, 512, 0, 0).reshape(x_prompt.shape)
        xs = _post_call(xs, streams[1][0], streams[1][1], mod, g, w_out, wg, wu, wd, l, 512, 1, 1)
    def stack_layers(parts):
        if len(parts) == 1:
            return parts[0][:, None]
        return jnp.stack(parts, axis=1)

    return xp, xs, stack_layers(new_k), stack_layers(new_v)
```

```python
import functools
import math

import numpy as np
import jax
import jax.numpy as jnp
from jax import lax
from jax.experimental import pallas as pl
from jax.experimental.pallas import tpu as pltpu

F32 = jnp.float32
BF16 = jnp.bfloat16

D_MODEL = 1024
GRID_W = 64
N_HEADS = 4
HEAD = 128
HALF_HEAD = 64
ROPE_AXIS = 32
ROPE_BASE = 10000.0
ATTN_W = 512
CONV_W = 512
POOL_W = 512
FOURIER_W = 512
GROUP = 128
POOL_WINDOWS = (2, 4, 8, 16)
D_FF = 2816
EPS = 1e-6
LOG2E = math.log2(math.e)
HALO = 8
MXU_N = 256
ATTN_SUB = 128
EVEN_BLOCK = 256
VMEM_LIMIT = 56 * 1024 * 1024


def _params(n_axes):
    return pltpu.CompilerParams(dimension_semantics=("arbitrary",) * n_axes,
                                vmem_limit_bytes=VMEM_LIMIT)


def _resident(shape):
    return pl.BlockSpec(shape, lambda *_: (0,) * len(shape), pipeline_mode=pl.Buffered(1))


def _rms(x, g):
    ms = jnp.mean(x * x, axis=-1, keepdims=True)
    return x * lax.rsqrt(ms + EPS) * g


def _modulate(x, g, shift, scale):
    return _rms(x, g) * (1.0 + scale) + shift


def _dot(a, b):
    return jnp.dot(a, b, preferred_element_type=F32)


def _silu(x):
    return x / (1.0 + jnp.exp(-x))


def _mod_kernel(cc_ref, w_ref, b_ref, o_ref):
    s = _silu(cc_ref[...]).astype(BF16)
    o_ref[...] = _dot(s, w_ref[...].astype(BF16)) + b_ref[...]


def _mod_call(cc, w_mod, b_mod):
    depth, d, n6 = w_mod.shape
    rows = cc.shape[0]
    tn = 2048
    return pl.pallas_call(
        _mod_kernel,
        out_shape=jax.ShapeDtypeStruct((depth, rows, n6), F32),
        grid=(depth, n6 // tn),
        in_specs=[
            pl.BlockSpec((rows, d), lambda l, j: (0, 0)),
            pl.BlockSpec((None, d, tn), lambda l, j: (l, 0, j)),
            pl.BlockSpec((None, 1, tn), lambda l, j: (l, 0, j)),
        ],
        out_specs=pl.BlockSpec((None, rows, tn), lambda l, j: (l, 0, j)),
        compiler_params=_params(2),
        name="mod",
    )(cc, w_mod, b_mod.reshape(depth, 1, n6))


def _tile_specs(n, tm, mod_base, mod_stride):
    nb8 = n // HALO
    t8 = tm // HALO
    x_spec = pl.BlockSpec((None, tm, D_MODEL), lambda b, j: (b, j, 0))
    prev_spec = pl.BlockSpec((None, HALO, D_MODEL),
                             lambda b, j: (b, jnp.maximum(j * t8 - 1, 0), 0))
    next_spec = pl.BlockSpec((None, HALO, D_MODEL),
                             lambda b, j: (b, jnp.minimum((j + 1) * t8, nb8 - 1), 0))
    mod_spec = pl.BlockSpec((None, 6, D_MODEL),
                            lambda b, j: (mod_base + mod_stride * b, 0, 0))
    return x_spec, prev_spec, next_spec, mod_spec


def _rope(t, cos, sin_signed, first_half):
    outs = []
    for hh in range(N_HEADS):
        th = t[:, HEAD * hh:HEAD * (hh + 1)]
        swapped = jnp.where(first_half, pltpu.roll(th, HEAD - 16, 1), pltpu.roll(th, 16, 1))
        outs.append(th * cos + swapped * sin_signed)
    return outs


def _even_in_kernel(*refs, use_rope):
    if use_rope:
        (x_ref, xp_ref, xn_ref, mod_ref, g_ref, w_ref, cw_ref, cos_ref, sin_ref,
         q_ref, k_ref, v_ref, cb_ref, zs_ref) = refs
        j = pl.program_id(1)
        nt = pl.num_programs(1)
        blk = EVEN_BLOCK
        nblk = x_ref.shape[0] // blk
        lane = lax.broadcasted_iota(jnp.int32, (1, HEAD), 1)
        first_half = (lane % 32) < 16
    else:
        x_ref, mod_ref, g_ref, w_ref, cw_ref, q_ref, k_ref, v_ref, cb_ref, zs_ref = refs
        nblk, blk = x_ref.shape[0], x_ref.shape[1]
    g = g_ref[0:1, :]
    shift = mod_ref[0:1, :]
    scale = mod_ref[1:2, :]
    stride = blk if use_rope else blk + HALO

    def z0(i):
        return HALO + i * stride

    def rows(i):
        return slice(i * blk, (i + 1) * blk)

    def proj(hh, lo):
        return _dot(hh, w_ref[:, lo:lo + 512])

    def modulated(i):
        xi = x_ref[rows(i), :] if use_rope else x_ref[i]
        return _modulate(xi, g, shift, scale).astype(BF16)

    def conv_inputs(i, h):
        outer = use_rope and i == 0
        if outer:
            xh = jnp.concatenate([xp_ref[...], xn_ref[...]], axis=0)
            hz = jnp.concatenate([h, _modulate(xh, g, shift, scale).astype(BF16)], axis=0)
        else:
            hz = h
        z = proj(hz, 2048) * proj(hz, 2560)
        if outer:
            zh = z[blk:]
            z = z[0:blk]
            zs_ref[0:HALO, :] = jnp.where(j > 0, zh[0:HALO], 0.0)
            zs_ref[z0(nblk):, :] = jnp.where(j < nt - 1, zh[HALO:], 0.0)
        zs_ref[z0(i):z0(i) + blk, :] = z
        return z, proj(h, 1536)

    def conv_out(i, z, gate_b):
        conv = (cw_ref[0:1, :] * zs_ref[z0(i) - 1:z0(i) - 1 + blk, :] + cw_ref[1:2, :] * z
                + cw_ref[2:3, :] * zs_ref[z0(i) + 1:z0(i) + 1 + blk, :])
        out = (gate_b * conv).astype(BF16)
        if use_rope:
            cb_ref[rows(i), :] = out
        else:
            cb_ref[i] = out

    def qkv(i, h):
        v = proj(h, 1024)
        for hh in range(N_HEADS):
            vh = v[:, HEAD * hh:HEAD * (hh + 1)]
            if use_rope:
                v_ref[hh, :, rows(i)] = vh.T.astype(v_ref.dtype)
            else:
                v_ref[i, hh] = vh.astype(v_ref.dtype)
        for ref, t in ((k_ref, proj(h, 512)), (q_ref, proj(h, 0) * (HALF_HEAD ** -0.5 * LOG2E))):
            if use_rope:
                heads = _rope(t, cos_ref[rows(i), :], sin_ref[rows(i), :], first_half)
            else:
                heads = [t[:, HEAD * hh:HEAD * (hh + 1)] for hh in range(N_HEADS)]
            for hh in range(N_HEADS):
                if use_rope:
                    ref[hh, rows(i), :] = heads[hh].astype(ref.dtype)
                else:
                    ref[i, hh] = heads[hh].astype(ref.dtype)

    if not use_rope:
        for i in range(nblk + 1):
            zs_ref[i * stride:i * stride + HALO, :] = jnp.zeros((HALO, CONV_W), F32)

    h = {0: modulated(0)}
    zg = {0: conv_inputs(0, h[0])}
    for i in range(nblk):
        if i + 1 < nblk:
            h[i + 1] = modulated(i + 1)
        else:
            conv_out(i, *zg.pop(i))
        qkv(i, h.pop(i))
        if i + 1 < nblk:
            zg[i + 1] = conv_inputs(i + 1, h[i + 1])
            conv_out(i, *zg.pop(i))


def _even_in_call(x, mod, g, w_in, conv_w, rope, tm, mod_base, mod_stride):
    bsz, n, d = x.shape
    use_rope = rope is not None
    head_shape = (bsz, N_HEADS, n, HEAD)
    if use_rope:
        x_spec, prev_spec, next_spec, mod_spec = _tile_specs(n, tm, mod_base, mod_stride)
        tab = pl.BlockSpec((tm, HEAD), lambda b, j: (j, 0))
        in_specs = [x_spec, prev_spec, next_spec, mod_spec, _resident(g.shape), _resident(w_in.shape),
                    _resident(conv_w.shape), tab, tab]
        args = [x, x, x, mod, g, w_in, conv_w, *rope]
        head_spec = pl.BlockSpec((None, N_HEADS, tm, HEAD), lambda b, j: (b, 0, j, 0))
        v_spec = pl.BlockSpec((None, N_HEADS, HEAD, tm), lambda b, j: (b, 0, 0, j))
        cb_spec = pl.BlockSpec((None, tm, CONV_W), lambda b, j: (b, j, 0))
        k_sds = jax.ShapeDtypeStruct(head_shape, BF16)
        v_sds = jax.ShapeDtypeStruct((bsz, N_HEADS, HEAD, n), BF16)
        grid = (bsz, n // tm)
        zs_rows = tm + 2 * HALO
    else:
        ns = tm // n
        in_specs = [pl.BlockSpec((ns, n, d), lambda b: (b, 0, 0)),
                    pl.BlockSpec((None, 6, d), lambda b: (mod_base, 0, 0)),
                    _resident(g.shape), _resident(w_in.shape), _resident(conv_w.shape)]
        args = [x, mod, g, w_in, conv_w]
        head_spec = v_spec = pl.BlockSpec((ns, N_HEADS, n, HEAD), lambda b: (b, 0, 0, 0))
        cb_spec = pl.BlockSpec((ns, n, CONV_W), lambda b: (b, 0, 0))
        k_sds = v_sds = jax.ShapeDtypeStruct(head_shape, F32)
        grid = (bsz // ns,)
        zs_rows = ns * (n + HALO) + HALO
    return pl.pallas_call(
        functools.partial(_even_in_kernel, use_rope=use_rope),
        out_shape=(jax.ShapeDtypeStruct(head_shape, BF16), k_sds, v_sds,
                   jax.ShapeDtypeStruct((bsz, n, CONV_W), BF16)),
        grid=grid,
        in_specs=in_specs,
        out_specs=(head_spec, head_spec, v_spec, cb_spec),
        scratch_shapes=[pltpu.VMEM((zs_rows, CONV_W), F32)],
        compiler_params=_params(len(grid)),
        name="even_in_rope" if use_rope else "even_in",
    )(*args)


def _diff_lambda(lam_ref, lam_init):
    lp = lam_ref[...]
    return (jnp.exp(jnp.sum(lp[0:1] * lp[1:2], axis=-1, keepdims=True))
            - jnp.exp(jnp.sum(lp[2:3] * lp[3:4], axis=-1, keepdims=True)) + lam_init)


def _stack_components(q):
    lane = lax.broadcasted_iota(jnp.int32, (1, HEAD), 1)
    zero = jnp.zeros_like(q)
    return jnp.concatenate([jnp.where(lane < HALF_HEAD, q, zero),
                            jnp.where(lane >= HALF_HEAD, q, zero)], axis=0)


def _softmax_pv(s, v_ext):
    e = jnp.exp2(s - jnp.max(s, axis=-1, keepdims=True)).astype(BF16)
    return _dot(e, v_ext)


def _normalise(ov, lam, sg, lam_init):
    t = ov.shape[0] // 2
    o = ov[:t, :HEAD] / ov[:t, HEAD:] - lam * (ov[t:, :HEAD] / ov[t:, HEAD:])
    return (_rms(o, sg) * (1.0 - lam_init)).astype(BF16)


def _chain_pipeline(n_groups, step_fn, finish_fn):
    assert n_groups % 2 == 0 and n_groups >= 2
    step_fn(0, 0, None, None)
    step_fn(1, 1, 0, 0)

    def body(t, carry):
        g = 2 * t
        step_fn(g, 0, g - 1, 1)
        finish_fn(g - 2, 0)
        step_fn(g + 1, 1, g, 0)
        finish_fn(g - 1, 1)
        return carry

    lax.fori_loop(1, n_groups // 2, body, 0)
    step_fn(None, None, n_groups - 1, 1)
    finish_fn(n_groups - 2, 0)
    finish_fn(n_groups - 1, 1)


def _attn_prompt_kernel(lam_ref, sg_ref, q_ref, k_ref, v_ref, o_ref, s_ref, ov_ref, *, lam_init):
    lam = _diff_lambda(lam_ref, lam_init)
    sg = sg_ref[...]
    n = k_ref.shape[2]
    ones = jnp.ones((n, MXU_N - HEAD), BF16)

    def step_fn(bs, ps, bv, pv):
        for hh in range(N_HEADS):
            if bs is not None:
                s_ref[ps * N_HEADS + hh] = lax.dot_general(
                    _stack_components(q_ref[bs, hh]), k_ref[bs, hh].astype(BF16),
                    (((1,), (1,)), ((), ())), preferred_element_type=F32)
        for hh in range(N_HEADS):
            if bv is not None:
                v_ext = jnp.concatenate([v_ref[bv, hh].astype(BF16), ones], axis=1)
                ov_ref[pv * N_HEADS + hh] = _softmax_pv(s_ref[pv * N_HEADS + hh], v_ext)

    def finish_fn(b, par):
        for hh in range(N_HEADS):
            o_ref[b, :, HEAD * hh:HEAD * (hh + 1)] = _normalise(ov_ref[par * N_HEADS + hh], lam, sg, lam_init)

    _chain_pipeline(q_ref.shape[0], step_fn, finish_fn)


def _attn_prompt_call(q, k, v, lam_params, subln_g, lam_init, nb):
    bsz, nh, n, hd = q.shape
    spec = pl.BlockSpec((nb, nh, n, hd), lambda b: (b, 0, 0, 0))
    return pl.pallas_call(
        functools.partial(_attn_prompt_kernel, lam_init=lam_init),
        out_shape=jax.ShapeDtypeStruct((bsz, n, nh * hd), BF16),
        grid=(bsz // nb,),
        in_specs=[_resident(lam_params.shape), _resident(subln_g.shape), spec, spec, spec],
        out_specs=pl.BlockSpec((nb, n, nh * hd), lambda b: (b, 0, 0)),
        scratch_shapes=[pltpu.VMEM((2 * nh, 2 * n, n), F32), pltpu.VMEM((2 * nh, 2 * n, MXU_N), F32)],
        compiler_params=_params(1),
        name="attn",
    )(lam_params, subln_g, q, k, v)


POST_BLOCK = 256
ONES_ROWS = 16
ATTN_WIDTH = 2
KEY_CHUNK = 256


def _attn_cache_kernel(*refs, lam_init, n_cast):
    lam_ref, sg_ref, q_ref, k_ref, vt_ref, ck_ref, cv_ref = refs[:7]
    cast_in = refs[7:7 + n_cast]
    o_ref = refs[7 + n_cast]
    cast_out = refs[8 + n_cast:8 + 2 * n_cast]
    kbuf_ref, vtbuf_ref, s_ref, ov_ref, m_ref = refs[8 + 2 * n_cast:]
    for src, dst in zip(cast_in, cast_out):
        dst[...] = src[...].astype(BF16)
    past = ck_ref.shape[0]
    lk = kbuf_ref.shape[0]
    kbuf_ref[0:past, :] = ck_ref[...].astype(BF16)
    kbuf_ref[past:, :] = k_ref[...]
    vtbuf_ref[0:HEAD, 0:past] = cv_ref[...].T.astype(BF16)
    vtbuf_ref[0:HEAD, past:] = vt_ref[...]
    vtbuf_ref[HEAD:, :] = jnp.ones((ONES_ROWS, lk), BF16)
    lam = _diff_lambda(lam_ref, lam_init)
    sg = sg_ref[...] * (1.0 - lam_init)

    def rows(g, u):
        start = (g * ATTN_WIDTH + u) * ATTN_SUB
        return pl.ds(pl.multiple_of(start, ATTN_SUB), ATTN_SUB)

    def step_fn(gs, ps, gv, pv):
        chains = range(ATTN_WIDTH)
        if gs is not None:
            qq = [_stack_components(q_ref[rows(gs, u), :]) for u in chains]
            col_max = [None] * ATTN_WIDTH
        if gv is not None:
            m_prev = [m_ref[pv * ATTN_WIDTH + u] for u in chains]
            acc = [None] * ATTN_WIDTH
        for c0 in range(0, lk, KEY_CHUNK):
            keys = slice(c0, c0 + KEY_CHUNK)
            for u in chains:
                if gs is not None:
                    s = lax.dot_general(kbuf_ref[keys, :], qq[u], (((1,), (1,)), ((), ())),
                                        preferred_element_type=F32)
                    s_ref[ps * ATTN_WIDTH + u, keys, :] = s
                    cm = jnp.max(s, axis=0, keepdims=True)
                    col_max[u] = cm if col_max[u] is None else jnp.maximum(col_max[u], cm)
            for u in chains:
                if gv is not None:
                    e = jnp.exp2(s_ref[pv * ATTN_WIDTH + u, keys, :] - m_prev[u]).astype(BF16)
                    part = _dot(vtbuf_ref[:, keys], e)
                    acc[u] = part if acc[u] is None else acc[u] + part
        for u in chains:
            if gs is not None:
                m_ref[ps * ATTN_WIDTH + u] = col_max[u]
            if gv is not None:
                ov_ref[pv * ATTN_WIDTH + u] = acc[u]

    def finish_fn(g, par):
        for u in range(ATTN_WIDTH):
            ov = ov_ref[par * ATTN_WIDTH + u]
            o_t = (ov[0:HEAD, 0:ATTN_SUB] / ov[HEAD:HEAD + 1, 0:ATTN_SUB]
                   - lam * (ov[0:HEAD, ATTN_SUB:] / ov[HEAD:HEAD + 1, ATTN_SUB:]))
            ms = jnp.mean(o_t * o_t, axis=0, keepdims=True)
            o_ref[rows(g, u), :] = ((o_t * lax.rsqrt(ms + EPS)).T * sg).astype(BF16)

    _chain_pipeline(q_ref.shape[0] // (ATTN_SUB * ATTN_WIDTH), step_fn, finish_fn)


def _attn_cache_call(q, k, vt, lam_params, subln_g, cache_k, cache_v, layer, lam_init, to_cast):
    bsz, nh, n, hd = q.shape
    past = cache_k.shape[3]
    seq_spec = pl.BlockSpec((None, None, n, hd), lambda b, h: (b, h, 0, 0))
    vt_spec = pl.BlockSpec((None, None, hd, n), lambda b, h: (b, h, 0, 0))
    c_spec = pl.BlockSpec((None, None, None, past, hd), lambda b, h: (b, layer, h, 0, 0))
    flat = [w.reshape(-1, w.shape[-1]) for w in to_cast]
    slab_specs = [pl.BlockSpec((w.shape[0] // (bsz * nh), w.shape[1]), lambda b, h: (b * nh + h, 0))
                  for w in flat]
    outs = pl.pallas_call(
        functools.partial(_attn_cache_kernel, lam_init=lam_init, n_cast=len(flat)),
        out_shape=[jax.ShapeDtypeStruct((bsz, n, nh * hd), BF16)]
        + [jax.ShapeDtypeStruct(w.shape, BF16) for w in flat],
        grid=(bsz, nh),
        in_specs=[_resident(lam_params.shape), _resident(subln_g.shape),
                  seq_spec, seq_spec, vt_spec, c_spec, c_spec] + slab_specs,
        out_specs=[pl.BlockSpec((None, n, hd), lambda b, h: (b, 0, h))] + slab_specs,
        scratch_shapes=[pltpu.VMEM((past + n, hd), BF16), pltpu.VMEM((hd + ONES_ROWS, past + n), BF16),
                        pltpu.VMEM((2 * ATTN_WIDTH, past + n, 2 * ATTN_SUB), F32),
                        pltpu.VMEM((2 * ATTN_WIDTH, hd + ONES_ROWS, 2 * ATTN_SUB), F32),
                        pltpu.VMEM((2 * ATTN_WIDTH, 1, 2 * ATTN_SUB), F32)],
        compiler_params=_params(2),
        name="attn_cache",
    )(lam_params, subln_g, q, k, vt, cache_k, cache_v, *flat)
    return outs[0], [o.reshape(w.shape) for o, w in zip(outs[1:], to_cast)]


def _post_kernel(x_ref, a_ref, b_ref, mod_ref, g_ref, wo_ref, wg_ref, wu_ref, wd_ref, o_ref):
    half = a_ref.shape[1]
    nb = x_ref.shape[0] // POST_BLOCK
    blocks = [slice(i * POST_BLOCK, (i + 1) * POST_BLOCK) for i in range(nb)]

    def out_proj(rows):
        return _dot(a_ref[rows, :], wo_ref[0:half, :]) + _dot(b_ref[rows, :], wo_ref[half:, :])

    def norms(rows, y):
        x1 = x_ref[rows, :] + mod_ref[2:3, :] * _rms(y, g_ref[1:2, :])
        return x1, _modulate(x1, g_ref[2:3, :], mod_ref[3:4, :], mod_ref[4:5, :]).astype(BF16)

    def gate_up(h):
        return _dot(h, wg_ref[...]), _dot(h, wu_ref[...])

    def down(gu):
        return _dot((_silu(gu[0]) * gu[1]).astype(BF16), wd_ref[...])

    def finish(rows, x1, f):
        o_ref[rows, :] = x1 + mod_ref[5:6, :] * _rms(f, g_ref[3:4, :])

    y = {0: out_proj(blocks[0])}
    x1, gu = {}, {}
    for i in range(nb + 1):
        if i + 1 < nb:
            y[i + 1] = out_proj(blocks[i + 1])
        if i < nb:
            x1[i], h = norms(blocks[i], y.pop(i))
        if i >= 1:
            f = down(gu.pop(i - 1))
        if i < nb:
            gu[i] = gate_up(h)
        if i >= 1:
            finish(blocks[i - 1], x1.pop(i - 1), f)


def _post_call(x, a, b, mod, g, w_out, w_gate, w_up, w_down, layer, tm, mod_base, mod_stride):
    bsz, n, d = x.shape
    x_spec, _, _, mod_spec = _tile_specs(n, tm, mod_base, mod_stride)
    half_spec = pl.BlockSpec((None, tm, a.shape[2]), lambda b_, j: (b_, j, 0))

    def layer_resident(w):
        return pl.BlockSpec((None,) + w.shape[1:], lambda *_: (layer, 0, 0), pipeline_mode=pl.Buffered(1))

    return pl.pallas_call(
        _post_kernel,
        out_shape=jax.ShapeDtypeStruct(x.shape, F32),
        grid=(bsz, n // tm),
        in_specs=[x_spec, half_spec, half_spec, mod_spec, _resident(g.shape),
                  _resident(w_out.shape), layer_resident(w_gate), layer_resident(w_up),
                  layer_resident(w_down)],
        out_specs=x_spec,
        compiler_params=_params(2),
        name="post",
    )(x, a, b, mod, g, w_out, w_gate, w_up, w_down)


def _odd_in_kernel(*refs, n_seq, whole_seqs):
    if whole_seqs:
        (x_ref, mod_ref, g_ref, w_ref, wp_ref, ps_ref, cs_ref,
         pc_ref, xc_ref, xs_ref, us_ref, f2_ref, f4_ref) = refs
        nblk, blk = x_ref.shape[0], x_ref.shape[1]
    else:
        (x_ref, xp_ref, xn_ref, mod_ref, g_ref, w_ref, wp_ref, ps_ref, cs_ref,
         pc_ref, xc_ref, xs_ref, us_ref, f2_ref, f4_ref) = refs
        j = pl.program_id(1)
        nt = pl.num_programs(1)
        nblk, blk = 1, x_ref.shape[0]
    tm = nblk * blk
    g = g_ref[0:1, :]
    shift = mod_ref[0:1, :]
    scale = mod_ref[1:2, :]
    if whole_seqs:
        h = _modulate(x_ref[...].reshape(tm, x_ref.shape[2]), g, shift, scale).astype(BF16)
        up = _dot(h, w_ref[:, 0:POOL_W])
    else:
        xh = jnp.concatenate([x_ref[...], xp_ref[...], xn_ref[...]], axis=0)
        hz = _modulate(xh, g, shift, scale).astype(BF16)
        h = hz[0:tm]
        upz = _dot(hz, w_ref[:, 0:POOL_W])
        up, uph = upz[0:tm], upz[tm:]
    uf = _dot(h, w_ref[:, POOL_W:]).astype(BF16)

    def store(ref, lanes, val):
        if whole_seqs:
            for i in range(nblk):
                ref[i, :, lanes] = val[i * blk:(i + 1) * blk]
        else:
            ref[:, lanes] = val

    for gi in range(FOURIER_W // GROUP):
        lanes = slice(GROUP * gi, GROUP * (gi + 1))
        cs = _dot(uf[:, lanes], cs_ref[...])
        store(xc_ref, lanes, cs[:, 0:GROUP].astype(BF16))
        store(xs_ref, lanes, cs[:, GROUP:].astype(BF16))

    stride = blk + HALO

    def u0(i):
        return HALO + i * stride

    rows = nblk * stride + HALO
    zeros = jnp.zeros((HALO, POOL_W), F32)
    for i in range(nblk):
        us_ref[u0(i):u0(i) + blk, :] = up[i * blk:(i + 1) * blk]
        if whole_seqs:
            us_ref[u0(i) - HALO:u0(i), :] = zeros
    if whole_seqs:
        us_ref[rows - HALO:rows, :] = zeros
    else:
        us_ref[0:HALO, :] = jnp.where(j > 0, uph[0:HALO], 0.0)
        us_ref[rows - HALO:rows, :] = jnp.where(j < nt - 1, uph[HALO:], 0.0)
    us_ref[rows:, :] = zeros
    f2_ref[0:rows, :] = us_ref[0:rows, GROUP:] + us_ref[1:rows + 1, GROUP:]
    f2_ref[rows:, :] = jnp.zeros((HALO, POOL_W - GROUP), F32)
    f4_ref[0:rows, :] = f2_ref[0:rows, GROUP:] + f2_ref[2:rows + 2, GROUP:]
    f4_ref[rows:, :] = jnp.zeros((HALO, POOL_W - 2 * GROUP), F32)
    f8 = f4_ref[0:rows, GROUP:] + f4_ref[4:rows + 4, GROUP:]

    def centred(i):
        a = u0(i)
        return (us_ref[a - 1:a - 1 + blk, 0:GROUP] + us_ref[a:a + blk, 0:GROUP],
                f2_ref[a - 2:a - 2 + blk, 0:GROUP] + f2_ref[a:a + blk, 0:GROUP],
                f4_ref[a - 4:a - 4 + blk, 0:GROUP] + f4_ref[a:a + blk, 0:GROUP],
                f8[a - HALO:a - HALO + blk] + f8[a:a + blk])

    sums = [centred(i) for i in range(nblk)]
    pos = lax.broadcasted_iota(jnp.int32, (blk, 1), 0)
    t = (pos if whole_seqs else j * blk + pos).astype(F32)
    for gi, win in enumerate(POOL_WINDOWS):
        lanes = slice(GROUP * gi, GROUP * (gi + 1))
        cnt = jnp.minimum(t + float(win // 2), float(n_seq)) - jnp.maximum(t - float(win // 2), 0.0)
        pooled = jnp.concatenate([s[gi] / cnt for s in sums], axis=0)
        diff = (pooled - up[:, lanes]).astype(BF16)
        store(pc_ref, lanes, (_dot(diff, wp_ref[gi]) * ps_ref[0:1, lanes]).astype(BF16))


def _odd_in_call(x, mod, g, w_in, w_pool, pool_scale, cs_mat, tm, mod_base, mod_stride):
    bsz, n, d = x.shape
    whole_seqs = tm >= n
    weights = [_resident(g.shape), _resident(w_in.shape), _resident(w_pool.shape),
               _resident(pool_scale.shape), _resident(cs_mat.shape)]
    if whole_seqs:
        ns = tm // n
        in_specs = [pl.BlockSpec((ns, n, d), lambda b: (b, 0, 0)),
                    pl.BlockSpec((None, 6, d), lambda b: (mod_base, 0, 0))] + weights
        args = [x, mod, g, w_in, w_pool, pool_scale, cs_mat]
        out_spec = pl.BlockSpec((ns, n, POOL_W), lambda b: (b, 0, 0))
        grid = (bsz // ns,)
        scratch_rows = ns * (n + HALO) + 2 * HALO
    else:
        x_spec, prev_spec, next_spec, mod_spec = _tile_specs(n, tm, mod_base, mod_stride)
        in_specs = [x_spec, prev_spec, next_spec, mod_spec] + weights
        args = [x, x, x, mod, g, w_in, w_pool, pool_scale, cs_mat]
        out_spec = pl.BlockSpec((None, tm, POOL_W), lambda b, j: (b, j, 0))
        grid = (bsz, n // tm)
        scratch_rows = tm + 3 * HALO
    out_sds = jax.ShapeDtypeStruct((bsz, n, POOL_W), BF16)
    return pl.pallas_call(
        functools.partial(_odd_in_kernel, n_seq=n, whole_seqs=whole_seqs),
        out_shape=(out_sds, out_sds, out_sds),
        grid=grid,
        in_specs=in_specs,
        out_specs=(out_spec, out_spec, out_spec),
        scratch_shapes=[pltpu.VMEM((scratch_rows, POOL_W), F32),
                        pltpu.VMEM((scratch_rows, POOL_W - GROUP), F32),
                        pltpu.VMEM((scratch_rows, POOL_W - 2 * GROUP), F32)],
        compiler_params=_params(len(grid)),
        name="odd_in",
    )(*args)


def _four_kernel(c_ref, s_ref, xc_ref, xs_ref, wf_ref, o_ref, *, scale):
    for b in range(xc_ref.shape[0]):
        y = _dot(c_ref[...], xc_ref[b]) - _dot(s_ref[...], xs_ref[b])
        four = (y * scale).astype(BF16)
        for gi in range(FOURIER_W // GROUP):
            lanes = slice(GROUP * gi, GROUP * (gi + 1))
            o_ref[b, :, lanes] = _dot(four[:, lanes], wf_ref[gi]).astype(BF16)


def _four_call(cn, sn, xc, xs, w_four, tm, nb):
    bsz, n, w = xc.shape
    mat_spec = pl.BlockSpec((tm, n), lambda b, j: (j, 0))
    seq_spec = pl.BlockSpec((nb, n, w), lambda b, j: (b, 0, 0))
    return pl.pallas_call(
        functools.partial(_four_kernel, scale=float(1.0 / math.sqrt(n * GROUP))),
        out_shape=jax.ShapeDtypeStruct((bsz, n, w), BF16),
        grid=(bsz // nb, n // tm),
        in_specs=[mat_spec, mat_spec, seq_spec, seq_spec, _resident(w_four.shape)],
        out_specs=pl.BlockSpec((nb, tm, w), lambda b, j: (b, j, 0)),
        compiler_params=_params(2),
        name="fourier",
    )(cn, sn, xc, xs, w_four)


FLIP_BLOCK = 256


def _four_sym_kernel(c_ref, s_ref, pm_ref, xc_ref, xs_ref, wf_ref, o_ref, *, scale):
    half = o_ref.shape[0] // 2
    p = _dot(c_ref[...], xc_ref[...])
    q = _dot(s_ref[...], xs_ref[...])

    def project(rows, four):
        for gi in range(FOURIER_W // GROUP):
            lanes = slice(GROUP * gi, GROUP * (gi + 1))
            o_ref[rows, lanes] = _dot(four[:, lanes], wf_ref[gi]).astype(BF16)

    project(slice(0, half), ((p[0:half] - q[0:half]) * scale).astype(BF16))
    mirrored = ((p + q) * scale).astype(BF16)
    for b in range(half // FLIP_BLOCK):
        lo = half - FLIP_BLOCK * (b + 1)
        window = mirrored[lo:lo + FLIP_BLOCK + HALO, :]
        flipped = _dot(pm_ref[...], window).astype(BF16)
        project(slice(half + FLIP_BLOCK * b, half + FLIP_BLOCK * (b + 1)), flipped)


def _four_sym_call(c_half, s_half, perm, xc, xs, w_four):
    bsz, n, w = xc.shape
    seq_spec = pl.BlockSpec((None, n, w), lambda b: (b, 0, 0))
    return pl.pallas_call(
        functools.partial(_four_sym_kernel, scale=float(1.0 / math.sqrt(n * GROUP))),
        out_shape=jax.ShapeDtypeStruct((bsz, n, w), BF16),
        grid=(bsz,),
        in_specs=[_resident(c_half.shape), _resident(s_half.shape), _resident(perm.shape),
                  seq_spec, seq_spec, _resident(w_four.shape)],
        out_specs=seq_spec,
        compiler_params=_params(1),
        name="fourier_sym",
    )(c_half, s_half, perm, xc, xs, w_four)


def _flip_perm():
    pm = np.zeros((FLIP_BLOCK, FLIP_BLOCK + HALO), np.float32)
    pm[np.arange(FLIP_BLOCK), FLIP_BLOCK - np.arange(FLIP_BLOCK)] = 1.0
    return pm


def _rope_tables(n_tok):
    rows = n_tok // GRID_W
    row = np.repeat(np.arange(rows), GRID_W).astype(np.float64)
    col = np.tile(np.arange(GRID_W), rows).astype(np.float64)
    inv = ROPE_BASE ** (-np.arange(0, ROPE_AXIS, 2, dtype=np.float64) / ROPE_AXIS)
    ang_r = row[:, None] * inv[None, :]
    ang_c = col[:, None] * inv[None, :]
    ang = np.concatenate([ang_r, ang_r, ang_c, ang_c], axis=-1)
    cos = np.concatenate([np.cos(ang)] * 2, axis=-1)
    sin = np.concatenate([np.sin(ang)] * 2, axis=-1)
    first_half = (np.arange(HEAD) % 32) < 16
    sin_signed = np.where(first_half[None, :], -sin, sin)
    return jnp.asarray(cos, F32), jnp.asarray(sin_signed, F32)


def _dft_mats(n):
    idx = np.arange(n, dtype=np.int64)
    ang = 2.0 * np.pi * ((idx[:, None] * idx[None, :]) % n).astype(np.float64) / n
    return np.cos(ang), np.sin(ang)


def kernel(x_prompt, x_sample, cache_k, cache_v, c, c_ctx, w_mod, b_mod, norm_g,
           w_in_even, lam_params, subln_g, conv_w, w_out_even,
           w_in_odd, w_pool, pool_scale, w_fourier, w_out_odd,
           w_gate, w_up, w_down):
    depth = w_mod.shape[0]
    n_dec = x_sample.shape[0]
    n_p, n_s = x_prompt.shape[1], x_sample.shape[1]

    pad_rows = 16 - 1 - n_dec
    cc = jnp.concatenate([c_ctx[None, :], c, jnp.zeros((pad_rows, D_MODEL), F32)], axis=0)
    mod_all = _mod_call(cc, w_mod, b_mod)[:, :1 + n_dec].reshape(depth, 1 + n_dec, 6, D_MODEL)

    rope = _rope_tables(n_s)
    cc_g, sc_g = _dft_mats(GROUP)
    cs_mat = jnp.asarray(np.concatenate([cc_g, sc_g], axis=1), F32).astype(BF16)
    dft_p = tuple(jnp.asarray(m, F32).astype(BF16) for m in _dft_mats(n_p))
    dft_s = tuple(jnp.asarray(m[:n_s // 2 + HALO], F32).astype(BF16) for m in _dft_mats(n_s))
    flip = jnp.asarray(_flip_perm(), F32).astype(BF16)

    late_weights = [w_gate, w_up, w_down, w_out_even, w_in_odd, w_out_odd]
    xp, xs = x_prompt, x_sample
    new_k, new_v = [], []
    for l in range(depth):
        mod = mod_all[l]
        g = norm_g[l]
        i = l // 2
        streams = []
        if l % 2 == 0:
            lam_init = 0.8 - 0.6 * math.exp(-0.3 * l)
            w_in = w_in_even[i].astype(BF16)
            sg = subln_g[i][None, :]
            qp, kp, vp, cbp = _even_in_call(xp, mod, g, w_in, conv_w[i], None, 4 * n_p, 0, 0)
            ap = _attn_prompt_call(qp, kp, vp, lam_params[i], sg, lam_init, 8)
            new_k.append(kp)
            new_v.append(vp)
            qs, ks, vts, cbs = _even_in_call(xs, mod, g, w_in, conv_w[i], rope, 1024, 1, 1)
            a_s, cast = _attn_cache_call(qs, ks, vts, lam_params[i], sg, cache_k, cache_v, i, lam_init,
                                         late_weights if l == 0 else [])
            if l == 0:
                wg, wu, wd, wo_even, wi_odd, wo_odd = cast
            w_out = wo_even[i]
            streams = [(ap, cbp), (a_s, cbs)]
        else:
            w_in = wi_odd[i]
            w_out = wo_odd[i]
            wp = w_pool[i].astype(BF16)
            wf = w_fourier[i].astype(BF16)
            ps = pool_scale[i][None, :]
            pcp, xcp, xsp = _odd_in_call(xp, mod, g, w_in, wp, ps, cs_mat, 4 * n_p, 0, 0)
            fcp = _four_call(*dft_p, xcp, xsp, wf, n_p, 8)
            pcs, xcs, xss = _odd_in_call(xs, mod, g, w_in, wp, ps, cs_mat, 1024, 1, 1)
            fcs = _four_sym_call(*dft_s, flip, xcs, xss, wf)
            streams = [(pcp, fcp), (pcs, fcs)]
        xp = _post_call(*(t.reshape(1, -1, t.shape[-1]) for t in (xp,) + streams[0]),
                        mod, g, w_out, wg, wu, wd, l, 512, 0, 0).reshape(x_prompt.shape)
        xs = _post_call(xs, streams[1][0], streams[1][1], mod, g, w_out, wg, wu, wd, l, 512, 1, 1)
    def stack_layers(parts):
        if len(parts) == 1:
            return parts[0][:, None]
        return jnp.stack(parts, axis=1)

    return xp, xs, stack_layers(new_k), stack_layers(new_v)
```

```python
import functools
import math

import numpy as np
import jax
import jax.numpy as jnp
from jax import lax
from jax.experimental import pallas as pl
from jax.experimental.pallas import tpu as pltpu

F32 = jnp.float32
BF16 = jnp.bfloat16

D_MODEL = 1024
GRID_W = 64
N_HEADS = 4
HEAD = 128
HALF_HEAD = 64
ROPE_AXIS = 32
ROPE_BASE = 10000.0
ATTN_W = 512
CONV_W = 512
POOL_W = 512
FOURIER_W = 512
GROUP = 128
POOL_WINDOWS = (2, 4, 8, 16)
D_FF = 2816
EPS = 1e-6
LOG2E = math.log2(math.e)
HALO = 8
MXU_N = 256
ATTN_SUB = 128
EVEN_BLOCK = 256
VMEM_LIMIT = 56 * 1024 * 1024


def _params(n_axes):
    return pltpu.CompilerParams(dimension_semantics=("arbitrary",) * n_axes,
                                vmem_limit_bytes=VMEM_LIMIT)


def _resident(shape):
    return pl.BlockSpec(shape, lambda *_: (0,) * len(shape), pipeline_mode=pl.Buffered(1))


def _rms(x, g):
    ms = jnp.mean(x * x, axis=-1, keepdims=True)
    return x * lax.rsqrt(ms + EPS) * g


def _modulate(x, g, shift, scale):
    return _rms(x, g) * (1.0 + scale) + shift


def _dot(a, b):
    return jnp.dot(a, b, preferred_element_type=F32)


def _silu(x):
    return x / (1.0 + jnp.exp(-x))


def _mod_kernel(cc_ref, w_ref, b_ref, o_ref):
    s = _silu(cc_ref[...]).astype(BF16)
    o_ref[...] = _dot(s, w_ref[...].astype(BF16)) + b_ref[...]


def _mod_call(cc, w_mod, b_mod):
    depth, d, n6 = w_mod.shape
    rows = cc.shape[0]
    tn = 2048
    return pl.pallas_call(
        _mod_kernel,
        out_shape=jax.ShapeDtypeStruct((depth, rows, n6), F32),
        grid=(depth, n6 // tn),
        in_specs=[
            pl.BlockSpec((rows, d), lambda l, j: (0, 0)),
            pl.BlockSpec((None, d, tn), lambda l, j: (l, 0, j)),
            pl.BlockSpec((None, 1, tn), lambda l, j: (l, 0, j)),
        ],
        out_specs=pl.BlockSpec((None, rows, tn), lambda l, j: (l, 0, j)),
        compiler_params=_params(2),
        name="mod",
    )(cc, w_mod, b_mod.reshape(depth, 1, n6))


def _tile_specs(n, tm, mod_base, mod_stride):
    nb8 = n // HALO
    t8 = tm // HALO
    x_spec = pl.BlockSpec((None, tm, D_MODEL), lambda b, j: (b, j, 0))
    prev_spec = pl.BlockSpec((None, HALO, D_MODEL),
                             lambda b, j: (b, jnp.maximum(j * t8 - 1, 0), 0))
    next_spec = pl.BlockSpec((None, HALO, D_MODEL),
                             lambda b, j: (b, jnp.minimum((j + 1) * t8, nb8 - 1), 0))
    mod_spec = pl.BlockSpec((None, 6, D_MODEL),
                            lambda b, j: (mod_base + mod_stride * b, 0, 0))
    return x_spec, prev_spec, next_spec, mod_spec


def _rope(t, cos, sin_signed, first_half):
    outs = []
    for hh in range(N_HEADS):
        th = t[:, HEAD * hh:HEAD * (hh + 1)]
        swapped = jnp.where(first_half, pltpu.roll(th, HEAD - 16, 1), pltpu.roll(th, 16, 1))
        outs.append(th * cos + swapped * sin_signed)
    return outs


def _even_in_kernel(*refs, use_rope):
    if use_rope:
        (x_ref, xp_ref, xn_ref, mod_ref, g_ref, w_ref, cw_ref, cos_ref, sin_ref,
         q_ref, k_ref, v_ref, cb_ref, zs_ref) = refs
        j = pl.program_id(1)
        nt = pl.num_programs(1)
        blk = EVEN_BLOCK
        nblk = x_ref.shape[0] // blk
        lane = lax.broadcasted_iota(jnp.int32, (1, HEAD), 1)
        first_half = (lane % 32) < 16
    else:
        x_ref, mod_ref, g_ref, w_ref, cw_ref, q_ref, k_ref, v_ref, cb_ref, zs_ref = refs
        nblk, blk = x_ref.shape[0], x_ref.shape[1]
    g = g_ref[0:1, :]
    shift = mod_ref[0:1, :]
    scale = mod_ref[1:2, :]
    stride = blk if use_rope else blk + HALO

    def z0(i):
        return HALO + i * stride

    def rows(i):
        return slice(i * blk, (i + 1) * blk)

    def proj(hh, lo):
        return _dot(hh, w_ref[:, lo:lo + 512])

    def modulated(i):
        xi = x_ref[rows(i), :] if use_rope else x_ref[i]
        return _modulate(xi, g, shift, scale).astype(BF16)

    def conv_inputs(i, h):
        outer = use_rope and i == 0
        if outer:
            xh = jnp.concatenate([xp_ref[...], xn_ref[...]], axis=0)
            hz = jnp.concatenate([h, _modulate(xh, g, shift, scale).astype(BF16)], axis=0)
        else:
            hz = h
        z = proj(hz, 2048) * proj(hz, 2560)
        if outer:
            zh = z[blk:]
            z = z[0:blk]
            zs_ref[0:HALO, :] = jnp.where(j > 0, zh[0:HALO], 0.0)
            zs_ref[z0(nblk):, :] = jnp.where(j < nt - 1, zh[HALO:], 0.0)
        zs_ref[z0(i):z0(i) + blk, :] = z
        return z, proj(h, 1536)

    def conv_out(i, z, gate_b):
        conv = (cw_ref[0:1, :] * zs_ref[z0(i) - 1:z0(i) - 1 + blk, :] + cw_ref[1:2, :] * z
                + cw_ref[2:3, :] * zs_ref[z0(i) + 1:z0(i) + 1 + blk, :])
        out = (gate_b * conv).astype(BF16)
        if use_rope:
            cb_ref[rows(i), :] = out
        else:
            cb_ref[i] = out

    def qkv(i, h):
        v = proj(h, 1024)
        for hh in range(N_HEADS):
            vh = v[:, HEAD * hh:HEAD * (hh + 1)]
            if use_rope:
                v_ref[hh, :, rows(i)] = vh.T.astype(v_ref.dtype)
            else:
                v_ref[i, hh] = vh.astype(v_ref.dtype)
        for ref, t in ((k_ref, proj(h, 512)), (q_ref, proj(h, 0) * (HALF_HEAD ** -0.5 * LOG2E))):
            if use_rope:
                heads = _rope(t, cos_ref[rows(i), :], sin_ref[rows(i), :], first_half)
            else:
                heads = [t[:, HEAD * hh:HEAD * (hh + 1)] for hh in range(N_HEADS)]
            for hh in range(N_HEADS):
                if use_rope:
                    ref[hh, rows(i), :] = heads[hh].astype(ref.dtype)
                else:
                    ref[i, hh] = heads[hh].astype(ref.dtype)

    if not use_rope:
        for i in range(nblk + 1):
            zs_ref[i * stride:i * stride + HALO, :] = jnp.zeros((HALO, CONV_W), F32)

    h = {0: modulated(0)}
    zg = {0: conv_inputs(0, h[0])}
    for i in range(nblk):
        if i + 1 < nblk:
            h[i + 1] = modulated(i + 1)
        else:
            conv_out(i, *zg.pop(i))
        qkv(i, h.pop(i))
        if i + 1 < nblk:
            zg[i + 1] = conv_inputs(i + 1, h[i + 1])
            conv_out(i, *zg.pop(i))


def _even_in_call(x, mod, g, w_in, conv_w, rope, tm, mod_base, mod_stride):
    bsz, n, d = x.shape
    use_rope = rope is not None
    head_shape = (bsz, N_HEADS, n, HEAD)
    if use_rope:
        x_spec, prev_spec, next_spec, mod_spec = _tile_specs(n, tm, mod_base, mod_stride)
        tab = pl.BlockSpec((tm, HEAD), lambda b, j: (j, 0))
        in_specs = [x_spec, prev_spec, next_spec, mod_spec, _resident(g.shape), _resident(w_in.shape),
                    _resident(conv_w.shape), tab, tab]
        args = [x, x, x, mod, g, w_in, conv_w, *rope]
        head_spec = pl.BlockSpec((None, N_HEADS, tm, HEAD), lambda b, j: (b, 0, j, 0))
        v_spec = pl.BlockSpec((None, N_HEADS, HEAD, tm), lambda b, j: (b, 0, 0, j))
        cb_spec = pl.BlockSpec((None, tm, CONV_W), lambda b, j: (b, j, 0))
        k_sds = jax.ShapeDtypeStruct(head_shape, BF16)
        v_sds = jax.ShapeDtypeStruct((bsz, N_HEADS, HEAD, n), BF16)
        grid = (bsz, n // tm)
        zs_rows = tm + 2 * HALO
    else:
        ns = tm // n
        in_specs = [pl.BlockSpec((ns, n, d), lambda b: (b, 0, 0)),
                    pl.BlockSpec((None, 6, d), lambda b: (mod_base, 0, 0)),
                    _resident(g.shape), _resident(w_in.shape), _resident(conv_w.shape)]
        args = [x, mod, g, w_in, conv_w]
        head_spec = v_spec = pl.BlockSpec((ns, N_HEADS, n, HEAD), lambda b: (b, 0, 0, 0))
        cb_spec = pl.BlockSpec((ns, n, CONV_W), lambda b: (b, 0, 0))
        k_sds = v_sds = jax.ShapeDtypeStruct(head_shape, F32)
        grid = (bsz // ns,)
        zs_rows = ns * (n + HALO) + HALO
    return pl.pallas_call(
        functools.partial(_even_in_kernel, use_rope=use_rope),
        out_shape=(jax.ShapeDtypeStruct(head_shape, BF16), k_sds, v_sds,
                   jax.ShapeDtypeStruct((bsz, n, CONV_W), BF16)),
        grid=grid,
        in_specs=in_specs,
        out_specs=(head_spec, head_spec, v_spec, cb_spec),
        scratch_shapes=[pltpu.VMEM((zs_rows, CONV_W), F32)],
        compiler_params=_params(len(grid)),
        name="even_in_rope" if use_rope else "even_in",
    )(*args)


def _diff_lambda(lam_ref, lam_init):
    lp = lam_ref[...]
    return (jnp.exp(jnp.sum(lp[0:1] * lp[1:2], axis=-1, keepdims=True))
            - jnp.exp(jnp.sum(lp[2:3] * lp[3:4], axis=-1, keepdims=True)) + lam_init)


def _stack_components(q):
    lane = lax.broadcasted_iota(jnp.int32, (1, HEAD), 1)
    zero = jnp.zeros_like(q)
    return jnp.concatenate([jnp.where(lane < HALF_HEAD, q, zero),
                            jnp.where(lane >= HALF_HEAD, q, zero)], axis=0)


def _softmax_pv(s, v_ext):
    e = jnp.exp2(s - jnp.max(s, axis=-1, keepdims=True)).astype(BF16)
    return _dot(e, v_ext)


def _normalise(ov, lam, sg, lam_init):
    t = ov.shape[0] // 2
    o = ov[:t, :HEAD] / ov[:t, HEAD:] - lam * (ov[t:, :HEAD] / ov[t:, HEAD:])
    return (_rms(o, sg) * (1.0 - lam_init)).astype(BF16)


def _chain_pipeline(n_groups, step_fn, finish_fn):
    assert n_groups % 2 == 0 and n_groups >= 2
    step_fn(0, 0, None, None)
    step_fn(1, 1, 0, 0)

    def body(t, carry):
        g = 2 * t
        step_fn(g, 0, g - 1, 1)
        finish_fn(g - 2, 0)
        step_fn(g + 1, 1, g, 0)
        finish_fn(g - 1, 1)
        return carry

    lax.fori_loop(1, n_groups // 2, body, 0)
    step_fn(None, None, n_groups - 1, 1)
    finish_fn(n_groups - 2, 0)
    finish_fn(n_groups - 1, 1)


def _attn_prompt_kernel(lam_ref, sg_ref, q_ref, k_ref, v_ref, o_ref, s_ref, ov_ref, *, lam_init):
    lam = _diff_lambda(lam_ref, lam_init)
    sg = sg_ref[...]
    n = k_ref.shape[2]
    ones = jnp.ones((n, MXU_N - HEAD), BF16)

    def step_fn(bs, ps, bv, pv):
        for hh in range(N_HEADS):
            if bs is not None:
                s_ref[ps * N_HEADS + hh] = lax.dot_general(
                    _stack_components(q_ref[bs, hh]), k_ref[bs, hh].astype(BF16),
                    (((1,), (1,)), ((), ())), preferred_element_type=F32)
        for hh in range(N_HEADS):
            if bv is not None:
                v_ext = jnp.concatenate([v_ref[bv, hh].astype(BF16), ones], axis=1)
                ov_ref[pv * N_HEADS + hh] = _softmax_pv(s_ref[pv * N_HEADS + hh], v_ext)

    def finish_fn(b, par):
        for hh in range(N_HEADS):
            o_ref[b, :, HEAD * hh:HEAD * (hh + 1)] = _normalise(ov_ref[par * N_HEADS + hh], lam, sg, lam_init)

    _chain_pipeline(q_ref.shape[0], step_fn, finish_fn)


def _attn_prompt_call(q, k, v, lam_params, subln_g, lam_init, nb):
    bsz, nh, n, hd = q.shape
    spec = pl.BlockSpec((nb, nh, n, hd), lambda b: (b, 0, 0, 0))
    return pl.pallas_call(
        functools.partial(_attn_prompt_kernel, lam_init=lam_init),
        out_shape=jax.ShapeDtypeStruct((bsz, n, nh * hd), BF16),
        grid=(bsz // nb,),
        in_specs=[_resident(lam_params.shape), _resident(subln_g.shape), spec, spec, spec],
        out_specs=pl.BlockSpec((nb, n, nh * hd), lambda b: (b, 0, 0)),
        scratch_shapes=[pltpu.VMEM((2 * nh, 2 * n, n), F32), pltpu.VMEM((2 * nh, 2 * n, MXU_N), F32)],
        compiler_params=_params(1),
        name="attn",
    )(lam_params, subln_g, q, k, v)


POST_BLOCK = 256
ONES_ROWS = 16
ATTN_WIDTH = 2
KEY_CHUNK = 256


def _attn_cache_kernel(*refs, lam_init, n_cast):
    lam_ref, sg_ref, q_ref, k_ref, vt_ref, ck_ref, cv_ref = refs[:7]
    cast_in = refs[7:7 + n_cast]
    o_ref = refs[7 + n_cast]
    cast_out = refs[8 + n_cast:8 + 2 * n_cast]
    kbuf_ref, vtbuf_ref, s_ref, ov_ref, m_ref, qt_ref = refs[8 + 2 * n_cast:]
    for src, dst in zip(cast_in, cast_out):
        dst[...] = src[...].astype(BF16)
    past = ck_ref.shape[0]
    lk = kbuf_ref.shape[0]
    kbuf_ref[0:past, :] = ck_ref[...].astype(BF16)
    kbuf_ref[past:, :] = k_ref[...]
    vtbuf_ref[0:HEAD, 0:past] = cv_ref[...].T.astype(BF16)
    vtbuf_ref[0:HEAD, past:] = vt_ref[...]
    vtbuf_ref[HEAD:, :] = jnp.ones((ONES_ROWS, lk), BF16)
    lam = _diff_lambda(lam_ref, lam_init)
    sg = sg_ref[...] * (1.0 - lam_init)

    def rows(g, u):
        start = (g * ATTN_WIDTH + u) * ATTN_SUB
        return pl.ds(pl.multiple_of(start, ATTN_SUB), ATTN_SUB)

    def step_fn(gs, ps, gv, pv):
        chains = range(ATTN_WIDTH)
        if gs is not None:
            for u in chains:
                qt_ref[u] = _stack_components(q_ref[rows(gs, u), :]).T
            qq = [qt_ref[u] for u in chains]
            col_max = [None] * ATTN_WIDTH
        if gv is not None:
            m_prev = [m_ref[pv * ATTN_WIDTH + u] for u in chains]
            acc = [None] * ATTN_WIDTH
        for c0 in range(0, lk, KEY_CHUNK):
            keys = slice(c0, c0 + KEY_CHUNK)
            for u in chains:
                if gs is not None:
                    s = _dot(kbuf_ref[keys, :], qq[u])
                    s_ref[ps * ATTN_WIDTH + u, keys, :] = s
                    cm = jnp.max(s, axis=0, keepdims=True)
                    col_max[u] = cm if col_max[u] is None else jnp.maximum(col_max[u], cm)
            for u in chains:
                if gv is not None:
                    e = jnp.exp2(s_ref[pv * ATTN_WIDTH + u, keys, :] - m_prev[u]).astype(BF16)
                    part = _dot(vtbuf_ref[:, keys], e)
                    acc[u] = part if acc[u] is None else acc[u] + part
        for u in chains:
            if gs is not None:
                m_ref[ps * ATTN_WIDTH + u] = col_max[u]
            if gv is not None:
                ov_ref[pv * ATTN_WIDTH + u] = acc[u]

    def finish_fn(g, par):
        for u in range(ATTN_WIDTH):
            ov = ov_ref[par * ATTN_WIDTH + u]
            o_t = (ov[0:HEAD, 0:ATTN_SUB] / ov[HEAD:HEAD + 1, 0:ATTN_SUB]
                   - lam * (ov[0:HEAD, ATTN_SUB:] / ov[HEAD:HEAD + 1, ATTN_SUB:]))
            ms = jnp.mean(o_t * o_t, axis=0, keepdims=True)
            o_ref[rows(g, u), :] = ((o_t * lax.rsqrt(ms + EPS)).T * sg).astype(BF16)

    _chain_pipeline(q_ref.shape[0] // (ATTN_SUB * ATTN_WIDTH), step_fn, finish_fn)


def _attn_cache_call(q, k, vt, lam_params, subln_g, cache_k, cache_v, layer, lam_init, to_cast):
    bsz, nh, n, hd = q.shape
    past = cache_k.shape[3]
    seq_spec = pl.BlockSpec((None, None, n, hd), lambda b, h: (b, h, 0, 0))
    vt_spec = pl.BlockSpec((None, None, hd, n), lambda b, h: (b, h, 0, 0))
    c_spec = pl.BlockSpec((None, None, None, past, hd), lambda b, h: (b, layer, h, 0, 0))
    flat = [w.reshape(-1, w.shape[-1]) for w in to_cast]
    slab_specs = [pl.BlockSpec((w.shape[0] // (bsz * nh), w.shape[1]), lambda b, h: (b * nh + h, 0))
                  for w in flat]
    outs = pl.pallas_call(
        functools.partial(_attn_cache_kernel, lam_init=lam_init, n_cast=len(flat)),
        out_shape=[jax.ShapeDtypeStruct((bsz, n, nh * hd), BF16)]
        + [jax.ShapeDtypeStruct(w.shape, BF16) for w in flat],
        grid=(bsz, nh),
        in_specs=[_resident(lam_params.shape), _resident(subln_g.shape),
                  seq_spec, seq_spec, vt_spec, c_spec, c_spec] + slab_specs,
        out_specs=[pl.BlockSpec((None, n, hd), lambda b, h: (b, 0, h))] + slab_specs,
        scratch_shapes=[pltpu.VMEM((past + n, hd), BF16), pltpu.VMEM((hd + ONES_ROWS, past + n), BF16),
                        pltpu.VMEM((2 * ATTN_WIDTH, past + n, 2 * ATTN_SUB), F32),
                        pltpu.VMEM((2 * ATTN_WIDTH, hd + ONES_ROWS, 2 * ATTN_SUB), F32),
                        pltpu.VMEM((2 * ATTN_WIDTH, 1, 2 * ATTN_SUB), F32),
                        pltpu.VMEM((ATTN_WIDTH, hd, 2 * ATTN_SUB), BF16)],
        compiler_params=_params(2),
        name="attn_cache",
    )(lam_params, subln_g, q, k, vt, cache_k, cache_v, *flat)
    return outs[0], [o.reshape(w.shape) for o, w in zip(outs[1:], to_cast)]


def _post_kernel(x_ref, a_ref, b_ref, mod_ref, g_ref, wo_ref, wg_ref, wu_ref, wd_ref, o_ref):
    nb = x_ref.shape[0] // POST_BLOCK
    blocks = [slice(i * POST_BLOCK, (i + 1) * POST_BLOCK) for i in range(nb)]

    def out_proj(rows):
        return _dot(jnp.concatenate([a_ref[rows, :], b_ref[rows, :]], axis=1), wo_ref[...])

    def norms(rows, y):
        x1 = x_ref[rows, :] + mod_ref[2:3, :] * _rms(y, g_ref[1:2, :])
        return x1, _modulate(x1, g_ref[2:3, :], mod_ref[3:4, :], mod_ref[4:5, :]).astype(BF16)

    def gate_up(h):
        return _dot(h, wg_ref[...]), _dot(h, wu_ref[...])

    def down(gu):
        return _dot((_silu(gu[0]) * gu[1]).astype(BF16), wd_ref[...])

    def finish(rows, x1, f):
        o_ref[rows, :] = x1 + mod_ref[5:6, :] * _rms(f, g_ref[3:4, :])

    y = {0: out_proj(blocks[0])}
    x1, gu = {}, {}
    for i in range(nb + 1):
        if i + 1 < nb:
            y[i + 1] = out_proj(blocks[i + 1])
        if i < nb:
            x1[i], h = norms(blocks[i], y.pop(i))
        if i >= 1:
            f = down(gu.pop(i - 1))
        if i < nb:
            gu[i] = gate_up(h)
        if i >= 1:
            finish(blocks[i - 1], x1.pop(i - 1), f)


def _post_call(x, a, b, mod, g, w_out, w_gate, w_up, w_down, layer, tm, mod_base, mod_stride):
    bsz, n, d = x.shape
    x_spec, _, _, mod_spec = _tile_specs(n, tm, mod_base, mod_stride)
    half_spec = pl.BlockSpec((None, tm, a.shape[2]), lambda b_, j: (b_, j, 0))

    def layer_resident(w):
        return pl.BlockSpec((None,) + w.shape[1:], lambda *_: (layer, 0, 0), pipeline_mode=pl.Buffered(1))

    return pl.pallas_call(
        _post_kernel,
        out_shape=jax.ShapeDtypeStruct(x.shape, F32),
        grid=(bsz, n // tm),
        in_specs=[x_spec, half_spec, half_spec, mod_spec, _resident(g.shape),
                  _resident(w_out.shape), layer_resident(w_gate), layer_resident(w_up),
                  layer_resident(w_down)],
        out_specs=x_spec,
        compiler_params=_params(2),
        name="post",
    )(x, a, b, mod, g, w_out, w_gate, w_up, w_down)


def _odd_in_kernel(*refs, n_seq, whole_seqs):
    if whole_seqs:
        (x_ref, mod_ref, g_ref, w_ref, wp_ref, ps_ref, cs_ref,
         pc_ref, xc_ref, xs_ref, us_ref, f2_ref, f4_ref) = refs
        nblk, blk = x_ref.shape[0], x_ref.shape[1]
    else:
        (x_ref, xp_ref, xn_ref, mod_ref, g_ref, w_ref, wp_ref, ps_ref, cs_ref,
         pc_ref, xc_ref, xs_ref, us_ref, f2_ref, f4_ref) = refs
        j = pl.program_id(1)
        nt = pl.num_programs(1)
        nblk, blk = 1, x_ref.shape[0]
    tm = nblk * blk
    g = g_ref[0:1, :]
    shift = mod_ref[0:1, :]
    scale = mod_ref[1:2, :]
    if whole_seqs:
        h = _modulate(x_ref[...].reshape(tm, x_ref.shape[2]), g, shift, scale).astype(BF16)
        up = _dot(h, w_ref[:, 0:POOL_W])
    else:
        xh = jnp.concatenate([x_ref[...], xp_ref[...], xn_ref[...]], axis=0)
        hz = _modulate(xh, g, shift, scale).astype(BF16)
        h = hz[0:tm]
        upz = _dot(hz, w_ref[:, 0:POOL_W])
        up, uph = upz[0:tm], upz[tm:]
    uf = _dot(h, w_ref[:, POOL_W:]).astype(BF16)

    def store(ref, lanes, val):
        if whole_seqs:
            for i in range(nblk):
                ref[i, :, lanes] = val[i * blk:(i + 1) * blk]
        else:
            ref[:, lanes] = val

    for gi in range(FOURIER_W // GROUP):
        lanes = slice(GROUP * gi, GROUP * (gi + 1))
        cs = _dot(uf[:, lanes], cs_ref[...])
        store(xc_ref, lanes, cs[:, 0:GROUP].astype(BF16))
        store(xs_ref, lanes, cs[:, GROUP:].astype(BF16))

    stride = blk + HALO

    def u0(i):
        return HALO + i * stride

    rows = nblk * stride + HALO
    zeros = jnp.zeros((HALO, POOL_W), F32)
    for i in range(nblk):
        us_ref[u0(i):u0(i) + blk, :] = up[i * blk:(i + 1) * blk]
        if whole_seqs:
            us_ref[u0(i) - HALO:u0(i), :] = zeros
    if whole_seqs:
        us_ref[rows - HALO:rows, :] = zeros
    else:
        us_ref[0:HALO, :] = jnp.where(j > 0, uph[0:HALO], 0.0)
        us_ref[rows - HALO:rows, :] = jnp.where(j < nt - 1, uph[HALO:], 0.0)
    us_ref[rows:, :] = zeros
    f2_ref[0:rows, :] = us_ref[0:rows, GROUP:] + us_ref[1:rows + 1, GROUP:]
    f2_ref[rows:, :] = jnp.zeros((HALO, POOL_W - GROUP), F32)
    f4_ref[0:rows, :] = f2_ref[0:rows, GROUP:] + f2_ref[2:rows + 2, GROUP:]
    f4_ref[rows:, :] = jnp.zeros((HALO, POOL_W - 2 * GROUP), F32)
    f8 = f4_ref[0:rows, GROUP:] + f4_ref[4:rows + 4, GROUP:]

    def centred(i):
        a = u0(i)
        return (us_ref[a - 1:a - 1 + blk, 0:GROUP] + us_ref[a:a + blk, 0:GROUP],
                f2_ref[a - 2:a - 2 + blk, 0:GROUP] + f2_ref[a:a + blk, 0:GROUP],
                f4_ref[a - 4:a - 4 + blk, 0:GROUP] + f4_ref[a:a + blk, 0:GROUP],
                f8[a - HALO:a - HALO + blk] + f8[a:a + blk])

    sums = [centred(i) for i in range(nblk)]
    pos = lax.broadcasted_iota(jnp.int32, (blk, 1), 0)
    t = (pos if whole_seqs else j * blk + pos).astype(F32)
    for gi, win in enumerate(POOL_WINDOWS):
        lanes = slice(GROUP * gi, GROUP * (gi + 1))
        cnt = jnp.minimum(t + float(win // 2), float(n_seq)) - jnp.maximum(t - float(win // 2), 0.0)
        pooled = jnp.concatenate([s[gi] / cnt for s in sums], axis=0)
        diff = (pooled - up[:, lanes]).astype(BF16)
        store(pc_ref, lanes, (_dot(diff, wp_ref[gi]) * ps_ref[0:1, lanes]).astype(BF16))


def _odd_in_call(x, mod, g, w_in, w_pool, pool_scale, cs_mat, tm, mod_base, mod_stride):
    bsz, n, d = x.shape
    whole_seqs = tm >= n
    weights = [_resident(g.shape), _resident(w_in.shape), _resident(w_pool.shape),
               _resident(pool_scale.shape), _resident(cs_mat.shape)]
    if whole_seqs:
        ns = tm // n
        in_specs = [pl.BlockSpec((ns, n, d), lambda b: (b, 0, 0)),
                    pl.BlockSpec((None, 6, d), lambda b: (mod_base, 0, 0))] + weights
        args = [x, mod, g, w_in, w_pool, pool_scale, cs_mat]
        out_spec = pl.BlockSpec((ns, n, POOL_W), lambda b: (b, 0, 0))
        grid = (bsz // ns,)
        scratch_rows = ns * (n + HALO) + 2 * HALO
    else:
        x_spec, prev_spec, next_spec, mod_spec = _tile_specs(n, tm, mod_base, mod_stride)
        in_specs = [x_spec, prev_spec, next_spec, mod_spec] + weights
        args = [x, x, x, mod, g, w_in, w_pool, pool_scale, cs_mat]
        out_spec = pl.BlockSpec((None, tm, POOL_W), lambda b, j: (b, j, 0))
        grid = (bsz, n // tm)
        scratch_rows = tm + 3 * HALO
    out_sds = jax.ShapeDtypeStruct((bsz, n, POOL_W), BF16)
    return pl.pallas_call(
        functools.partial(_odd_in_kernel, n_seq=n, whole_seqs=whole_seqs),
        out_shape=(out_sds, out_sds, out_sds),
        grid=grid,
        in_specs=in_specs,
        out_specs=(out_spec, out_spec, out_spec),
        scratch_shapes=[pltpu.VMEM((scratch_rows, POOL_W), F32),
                        pltpu.VMEM((scratch_rows, POOL_W - GROUP), F32),
                        pltpu.VMEM((scratch_rows, POOL_W - 2 * GROUP), F32)],
        compiler_params=_params(len(grid)),
        name="odd_in",
    )(*args)


def _four_kernel(c_ref, s_ref, xc_ref, xs_ref, wf_ref, o_ref, *, scale):
    for b in range(xc_ref.shape[0]):
        y = _dot(c_ref[...], xc_ref[b]) - _dot(s_ref[...], xs_ref[b])
        four = (y * scale).astype(BF16)
        for gi in range(FOURIER_W // GROUP):
            lanes = slice(GROUP * gi, GROUP * (gi + 1))
            o_ref[b, :, lanes] = _dot(four[:, lanes], wf_ref[gi]).astype(BF16)


def _four_call(cn, sn, xc, xs, w_four, tm, nb):
    bsz, n, w = xc.shape
    mat_spec = pl.BlockSpec((tm, n), lambda b, j: (j, 0))
    seq_spec = pl.BlockSpec((nb, n, w), lambda b, j: (b, 0, 0))
    return pl.pallas_call(
        functools.partial(_four_kernel, scale=float(1.0 / math.sqrt(n * GROUP))),
        out_shape=jax.ShapeDtypeStruct((bsz, n, w), BF16),
        grid=(bsz // nb, n // tm),
        in_specs=[mat_spec, mat_spec, seq_spec, seq_spec, _resident(w_four.shape)],
        out_specs=pl.BlockSpec((nb, tm, w), lambda b, j: (b, j, 0)),
        compiler_params=_params(2),
        name="fourier",
    )(cn, sn, xc, xs, w_four)


FLIP_BLOCK = 256


def _four_sym_kernel(c_ref, s_ref, pm_ref, xc_ref, xs_ref, wf_ref, o_ref, *, scale):
    half = o_ref.shape[0] // 2
    p = _dot(c_ref[...], xc_ref[...])
    q = _dot(s_ref[...], xs_ref[...])

    def project(rows, four):
        for gi in range(FOURIER_W // GROUP):
            lanes = slice(GROUP * gi, GROUP * (gi + 1))
            o_ref[rows, lanes] = _dot(four[:, lanes], wf_ref[gi]).astype(BF16)

    project(slice(0, half), ((p[0:half] - q[0:half]) * scale).astype(BF16))
    mirrored = ((p + q) * scale).astype(BF16)
    for b in range(half // FLIP_BLOCK):
        lo = half - FLIP_BLOCK * (b + 1)
        window = mirrored[lo:lo + FLIP_BLOCK + HALO, :]
        flipped = _dot(pm_ref[...], window).astype(BF16)
        project(slice(half + FLIP_BLOCK * b, half + FLIP_BLOCK * (b + 1)), flipped)


def _four_sym_call(c_half, s_half, perm, xc, xs, w_four):
    bsz, n, w = xc.shape
    seq_spec = pl.BlockSpec((None, n, w), lambda b: (b, 0, 0))
    return pl.pallas_call(
        functools.partial(_four_sym_kernel, scale=float(1.0 / math.sqrt(n * GROUP))),
        out_shape=jax.ShapeDtypeStruct((bsz, n, w), BF16),
        grid=(bsz,),
        in_specs=[_resident(c_half.shape), _resident(s_half.shape), _resident(perm.shape),
                  seq_spec, seq_spec, _resident(w_four.shape)],
        out_specs=seq_spec,
        compiler_params=_params(1),
        name="fourier_sym",
    )(c_half, s_half, perm, xc, xs, w_four)


def _flip_perm():
    pm = np.zeros((FLIP_BLOCK, FLIP_BLOCK + HALO), np.float32)
    pm[np.arange(FLIP_BLOCK), FLIP_BLOCK - np.arange(FLIP_BLOCK)] = 1.0
    return pm


def _rope_tables(n_tok):
    rows = n_tok // GRID_W
    row = np.repeat(np.arange(rows), GRID_W).astype(np.float64)
    col = np.tile(np.arange(GRID_W), rows).astype(np.float64)
    inv = ROPE_BASE ** (-np.arange(0, ROPE_AXIS, 2, dtype=np.float64) / ROPE_AXIS)
    ang_r = row[:, None] * inv[None, :]
    ang_c = col[:, None] * inv[None, :]
    ang = np.concatenate([ang_r, ang_r, ang_c, ang_c], axis=-1)
    cos = np.concatenate([np.cos(ang)] * 2, axis=-1)
    sin = np.concatenate([np.sin(ang)] * 2, axis=-1)
    first_half = (np.arange(HEAD) % 32) < 16
    sin_signed = np.where(first_half[None, :], -sin, sin)
    return jnp.asarray(cos, F32), jnp.asarray(sin_signed, F32)


def _dft_mats(n):
    idx = np.arange(n, dtype=np.int64)
    ang = 2.0 * np.pi * ((idx[:, None] * idx[None, :]) % n).astype(np.float64) / n
    return np.cos(ang), np.sin(ang)


def kernel(x_prompt, x_sample, cache_k, cache_v, c, c_ctx, w_mod, b_mod, norm_g,
           w_in_even, lam_params, subln_g, conv_w, w_out_even,
           w_in_odd, w_pool, pool_scale, w_fourier, w_out_odd,
           w_gate, w_up, w_down):
    depth = w_mod.shape[0]
    n_dec = x_sample.shape[0]
    n_p, n_s = x_prompt.shape[1], x_sample.shape[1]

    pad_rows = 16 - 1 - n_dec
    cc = jnp.concatenate([c_ctx[None, :], c, jnp.zeros((pad_rows, D_MODEL), F32)], axis=0)
    mod_all = _mod_call(cc, w_mod, b_mod)[:, :1 + n_dec].reshape(depth, 1 + n_dec, 6, D_MODEL)

    rope = _rope_tables(n_s)
    cc_g, sc_g = _dft_mats(GROUP)
    cs_mat = jnp.asarray(np.concatenate([cc_g, sc_g], axis=1), F32).astype(BF16)
    dft_p = tuple(jnp.asarray(m, F32).astype(BF16) for m in _dft_mats(n_p))
    dft_s = tuple(jnp.asarray(m[:n_s // 2 + HALO], F32).astype(BF16) for m in _dft_mats(n_s))
    flip = jnp.asarray(_flip_perm(), F32).astype(BF16)

    late_weights = [w_gate, w_up, w_down, w_out_even, w_in_odd, w_out_odd]
    xp, xs = x_prompt, x_sample
    new_k, new_v = [], []
    for l in range(depth):
        mod = mod_all[l]
        g = norm_g[l]
        i = l // 2
        streams = []
        if l % 2 == 0:
            lam_init = 0.8 - 0.6 * math.exp(-0.3 * l)
            w_in = w_in_even[i].astype(BF16)
            sg = subln_g[i][None, :]
            qp, kp, vp, cbp = _even_in_call(xp, mod, g, w_in, conv_w[i], None, 4 * n_p, 0, 0)
            ap = _attn_prompt_call(qp, kp, vp, lam_params[i], sg, lam_init, 8)
            new_k.append(kp)
            new_v.append(vp)
            qs, ks, vts, cbs = _even_in_call(xs, mod, g, w_in, conv_w[i], rope, 1024, 1, 1)
            a_s, cast = _attn_cache_call(qs, ks, vts, lam_params[i], sg, cache_k, cache_v, i, lam_init,
                                         late_weights if l == 0 else [])
            if l == 0:
                wg, wu, wd, wo_even, wi_odd, wo_odd = cast
            w_out = wo_even[i]
            streams = [(ap, cbp), (a_s, cbs)]
        else:
            w_in = wi_odd[i]
            w_out = wo_odd[i]
            wp = w_pool[i].astype(BF16)
            wf = w_fourier[i].astype(BF16)
            ps = pool_scale[i][None, :]
            pcp, xcp, xsp = _odd_in_call(xp, mod, g, w_in, wp, ps, cs_mat, 4 * n_p, 0, 0)
            fcp = _four_call(*dft_p, xcp, xsp, wf, n_p, 8)
            pcs, xcs, xss = _odd_in_call(xs, mod, g, w_in, wp, ps, cs_mat, 1024, 1, 1)
            fcs = _four_sym_call(*dft_s, flip, xcs, xss, wf)
            streams = [(pcp, fcp), (pcs, fcs)]
        xp = _post_call(*(t.reshape(1, -1, t.shape[-1]) for t in (xp,) + streams[0]),
                        mod, g, w_out, wg, wu, wd, l, 512, 0, 0).reshape(x_prompt.shape)
        xs = _post_call(xs, streams[1][0], streams[1][1], mod, g, w_out, wg, wu, wd, l, 512, 1, 1)
    def stack_layers(parts):
        if len(parts) == 1:
            return parts[0][:, None]
        return jnp.stack(parts, axis=1)

    return xp, xs, stack_layers(new_k), stack_layers(new_v)
```

```python
import functools
import math

import numpy as np
import jax
import jax.numpy as jnp
from jax import lax
from jax.experimental import pallas as pl
from jax.experimental.pallas import tpu as pltpu

F32 = jnp.float32
BF16 = jnp.bfloat16

D_MODEL = 1024
GRID_W = 64
N_HEADS = 4
HEAD = 128
HALF_HEAD = 64
ROPE_AXIS = 32
ROPE_BASE = 10000.0
ATTN_W = 512
CONV_W = 512
POOL_W = 512
FOURIER_W = 512
GROUP = 128
POOL_WINDOWS = (2, 4, 8, 16)
D_FF = 2816
EPS = 1e-6
LOG2E = math.log2(math.e)
HALO = 8
MXU_N = 256
ATTN_SUB = 128
EVEN_BLOCK = 256
VMEM_LIMIT = 56 * 1024 * 1024


def _params(n_axes):
    return pltpu.CompilerParams(dimension_semantics=("arbitrary",) * n_axes,
                                vmem_limit_bytes=VMEM_LIMIT)


def _resident(shape):
    return pl.BlockSpec(shape, lambda *_: (0,) * len(shape), pipeline_mode=pl.Buffered(1))


def _rms(x, g):
    ms = jnp.mean(x * x, axis=-1, keepdims=True)
    return x * lax.rsqrt(ms + EPS) * g


def _modulate(x, g, shift, scale):
    return _rms(x, g) * (1.0 + scale) + shift


def _dot(a, b):
    return jnp.dot(a, b, preferred_element_type=F32)


def _silu(x):
    return x / (1.0 + jnp.exp(-x))


def _mod_kernel(cc_ref, w_ref, b_ref, o_ref):
    s = _silu(cc_ref[...]).astype(BF16)
    o_ref[...] = _dot(s, w_ref[...].astype(BF16)) + b_ref[...]


def _mod_call(cc, w_mod, b_mod):
    depth, d, n6 = w_mod.shape
    rows = cc.shape[0]
    tn = 1024
    return pl.pallas_call(
        _mod_kernel,
        out_shape=jax.ShapeDtypeStruct((depth, rows, n6), F32),
        grid=(depth, n6 // tn),
        in_specs=[
            pl.BlockSpec((rows, d), lambda l, j: (0, 0)),
            pl.BlockSpec((None, d, tn), lambda l, j: (l, 0, j)),
            pl.BlockSpec((None, 1, tn), lambda l, j: (l, 0, j)),
        ],
        out_specs=pl.BlockSpec((None, rows, tn), lambda l, j: (l, 0, j)),
        compiler_params=_params(2),
        name="mod",
    )(cc, w_mod, b_mod.reshape(depth, 1, n6))


def _tile_specs(n, tm, mod_base, mod_stride):
    nb8 = n // HALO
    t8 = tm // HALO
    x_spec = pl.BlockSpec((None, tm, D_MODEL), lambda b, j: (b, j, 0))
    prev_spec = pl.BlockSpec((None, HALO, D_MODEL),
                             lambda b, j: (b, jnp.maximum(j * t8 - 1, 0), 0))
    next_spec = pl.BlockSpec((None, HALO, D_MODEL),
                             lambda b, j: (b, jnp.minimum((j + 1) * t8, nb8 - 1), 0))
    mod_spec = pl.BlockSpec((None, 6, D_MODEL),
                            lambda b, j: (mod_base + mod_stride * b, 0, 0))
    return x_spec, prev_spec, next_spec, mod_spec


def _rope(t, cos, sin_signed, first_half):
    outs = []
    for hh in range(N_HEADS):
        th = t[:, HEAD * hh:HEAD * (hh + 1)]
        swapped = jnp.where(first_half, pltpu.roll(th, HEAD - 16, 1), pltpu.roll(th, 16, 1))
        outs.append(th * cos + swapped * sin_signed)
    return outs


def _even_in_kernel(*refs, use_rope):
    if use_rope:
        (x_ref, xp_ref, xn_ref, mod_ref, g_ref, w_ref, cw_ref, cos_ref, sin_ref,
         q_ref, k_ref, v_ref, cb_ref, zs_ref) = refs
        j = pl.program_id(1)
        nt = pl.num_programs(1)
        blk = EVEN_BLOCK
        nblk = x_ref.shape[0] // blk
        lane = lax.broadcasted_iota(jnp.int32, (1, HEAD), 1)
        first_half = (lane % 32) < 16
    else:
        x_ref, mod_ref, g_ref, w_ref, cw_ref, q_ref, k_ref, v_ref, cb_ref, zs_ref = refs
        nblk, blk = x_ref.shape[0], x_ref.shape[1]
    g = g_ref[0:1, :]
    shift = mod_ref[0:1, :]
    scale = mod_ref[1:2, :]
    stride = blk if use_rope else blk + HALO

    def z0(i):
        return HALO + i * stride

    def rows(i):
        return slice(i * blk, (i + 1) * blk)

    def proj(hh, lo):
        return _dot(hh, w_ref[:, lo:lo + 512])

    def modulated(i):
        xi = x_ref[rows(i), :] if use_rope else x_ref[i]
        return _modulate(xi, g, shift, scale).astype(BF16)

    def conv_inputs(i, h):
        outer = use_rope and i == 0
        if outer:
            xh = jnp.concatenate([xp_ref[...], xn_ref[...]], axis=0)
            hz = jnp.concatenate([h, _modulate(xh, g, shift, scale).astype(BF16)], axis=0)
        else:
            hz = h
        z = proj(hz, 2048) * proj(hz, 2560)
        if outer:
            zh = z[blk:]
            z = z[0:blk]
            zs_ref[0:HALO, :] = jnp.where(j > 0, zh[0:HALO], 0.0)
            zs_ref[z0(nblk):, :] = jnp.where(j < nt - 1, zh[HALO:], 0.0)
        zs_ref[z0(i):z0(i) + blk, :] = z
        return z, proj(h, 1536)

    def conv_out(i, z, gate_b):
        conv = (cw_ref[0:1, :] * zs_ref[z0(i) - 1:z0(i) - 1 + blk, :] + cw_ref[1:2, :] * z
                + cw_ref[2:3, :] * zs_ref[z0(i) + 1:z0(i) + 1 + blk, :])
        out = (gate_b * conv).astype(BF16)
        if use_rope:
            cb_ref[rows(i), :] = out
        else:
            cb_ref[i] = out

    def qkv(i, h):
        v = proj(h, 1024)
        for hh in range(N_HEADS):
            vh = v[:, HEAD * hh:HEAD * (hh + 1)]
            if use_rope:
                v_ref[hh, 0:HEAD, rows(i)] = vh.T.astype(v_ref.dtype)
                v_ref[hh, HEAD:, rows(i)] = jnp.ones((ONES_ROWS, blk), v_ref.dtype)
            else:
                v_ref[i, hh] = vh.astype(v_ref.dtype)
        for ref, t in ((k_ref, proj(h, 512)), (q_ref, proj(h, 0) * (HALF_HEAD ** -0.5 * LOG2E))):
            if use_rope:
                heads = _rope(t, cos_ref[rows(i), :], sin_ref[rows(i), :], first_half)
            else:
                heads = [t[:, HEAD * hh:HEAD * (hh + 1)] for hh in range(N_HEADS)]
            for hh in range(N_HEADS):
                if use_rope:
                    ref[hh, rows(i), :] = heads[hh].astype(ref.dtype)
                else:
                    ref[i, hh] = heads[hh].astype(ref.dtype)

    if not use_rope:
        for i in range(nblk + 1):
            zs_ref[i * stride:i * stride + HALO, :] = jnp.zeros((HALO, CONV_W), F32)

    h = {0: modulated(0)}
    zg = {0: conv_inputs(0, h[0])}
    for i in range(nblk):
        if i + 1 < nblk:
            h[i + 1] = modulated(i + 1)
        else:
            conv_out(i, *zg.pop(i))
        qkv(i, h.pop(i))
        if i + 1 < nblk:
            zg[i + 1] = conv_inputs(i + 1, h[i + 1])
            conv_out(i, *zg.pop(i))


def _even_in_call(x, mod, g, w_in, conv_w, rope, tm, mod_base, mod_stride):
    bsz, n, d = x.shape
    use_rope = rope is not None
    head_shape = (bsz, N_HEADS, n, HEAD)
    if use_rope:
        x_spec, prev_spec, next_spec, mod_spec = _tile_specs(n, tm, mod_base, mod_stride)
        tab = pl.BlockSpec((tm, HEAD), lambda b, j: (j, 0))
        in_specs = [x_spec, prev_spec, next_spec, mod_spec, _resident(g.shape), _resident(w_in.shape),
                    _resident(conv_w.shape), tab, tab]
        args = [x, x, x, mod, g, w_in, conv_w, *rope]
        head_spec = pl.BlockSpec((None, N_HEADS, tm, HEAD), lambda b, j: (b, 0, j, 0))
        v_spec = pl.BlockSpec((None, N_HEADS, HEAD + ONES_ROWS, tm), lambda b, j: (b, 0, 0, j))
        cb_spec = pl.BlockSpec((None, tm, CONV_W), lambda b, j: (b, j, 0))
        k_sds = jax.ShapeDtypeStruct(head_shape, BF16)
        v_sds = jax.ShapeDtypeStruct((bsz, N_HEADS, HEAD + ONES_ROWS, n), BF16)
        grid = (bsz, n // tm)
        zs_rows = tm + 2 * HALO
    else:
        ns = tm // n
        in_specs = [pl.BlockSpec((ns, n, d), lambda b: (b, 0, 0)),
                    pl.BlockSpec((None, 6, d), lambda b: (mod_base, 0, 0)),
                    _resident(g.shape), _resident(w_in.shape), _resident(conv_w.shape)]
        args = [x, mod, g, w_in, conv_w]
        head_spec = v_spec = pl.BlockSpec((ns, N_HEADS, n, HEAD), lambda b: (b, 0, 0, 0))
        cb_spec = pl.BlockSpec((ns, n, CONV_W), lambda b: (b, 0, 0))
        k_sds = v_sds = jax.ShapeDtypeStruct(head_shape, F32)
        grid = (bsz // ns,)
        zs_rows = ns * (n + HALO) + HALO
    return pl.pallas_call(
        functools.partial(_even_in_kernel, use_rope=use_rope),
        out_shape=(jax.ShapeDtypeStruct(head_shape, BF16), k_sds, v_sds,
                   jax.ShapeDtypeStruct((bsz, n, CONV_W), BF16)),
        grid=grid,
        in_specs=in_specs,
        out_specs=(head_spec, head_spec, v_spec, cb_spec),
        scratch_shapes=[pltpu.VMEM((zs_rows, CONV_W), F32)],
        compiler_params=_params(len(grid)),
        name="even_in_rope" if use_rope else "even_in",
    )(*args)


def _diff_lambda(lam_ref, lam_init):
    lp = lam_ref[...]
    return (jnp.exp(jnp.sum(lp[0:1] * lp[1:2], axis=-1, keepdims=True))
            - jnp.exp(jnp.sum(lp[2:3] * lp[3:4], axis=-1, keepdims=True)) + lam_init)


def _stack_components(q):
    lane = lax.broadcasted_iota(jnp.int32, (1, HEAD), 1)
    zero = jnp.zeros_like(q)
    return jnp.concatenate([jnp.where(lane < HALF_HEAD, q, zero),
                            jnp.where(lane >= HALF_HEAD, q, zero)], axis=0)


def _softmax_pv(s, v_ext):
    e = jnp.exp2(s - jnp.max(s, axis=-1, keepdims=True)).astype(BF16)
    return _dot(e, v_ext)


def _normalise(ov, lam, sg, lam_init):
    t = ov.shape[0] // 2
    o = ov[:t, :HEAD] / ov[:t, HEAD:] - lam * (ov[t:, :HEAD] / ov[t:, HEAD:])
    return (_rms(o, sg) * (1.0 - lam_init)).astype(BF16)


def _chain_pipeline(n_groups, step_fn, finish_fn):
    assert n_groups % 2 == 0 and n_groups >= 2
    step_fn(0, 0, None, None)
    step_fn(1, 1, 0, 0)

    def body(t, carry):
        g = 2 * t
        step_fn(g, 0, g - 1, 1)
        finish_fn(g - 2, 0)
        step_fn(g + 1, 1, g, 0)
        finish_fn(g - 1, 1)
        return carry

    lax.fori_loop(1, n_groups // 2, body, 0)
    step_fn(None, None, n_groups - 1, 1)
    finish_fn(n_groups - 2, 0)
    finish_fn(n_groups - 1, 1)


def _attn_prompt_kernel(lam_ref, sg_ref, q_ref, k_ref, v_ref, o_ref, s_ref, ov_ref, *, lam_init):
    lam = _diff_lambda(lam_ref, lam_init)
    sg = sg_ref[...]
    n = k_ref.shape[2]
    ones = jnp.ones((n, MXU_N - HEAD), BF16)

    def step_fn(bs, ps, bv, pv):
        for hh in range(N_HEADS):
            if bs is not None:
                s_ref[ps * N_HEADS + hh] = lax.dot_general(
                    _stack_components(q_ref[bs, hh]), k_ref[bs, hh].astype(BF16),
                    (((1,), (1,)), ((), ())), preferred_element_type=F32)
        for hh in range(N_HEADS):
            if bv is not None:
                v_ext = jnp.concatenate([v_ref[bv, hh].astype(BF16), ones], axis=1)
                ov_ref[pv * N_HEADS + hh] = _softmax_pv(s_ref[pv * N_HEADS + hh], v_ext)

    def finish_fn(b, par):
        for hh in range(N_HEADS):
            o_ref[b, :, HEAD * hh:HEAD * (hh + 1)] = _normalise(ov_ref[par * N_HEADS + hh], lam, sg, lam_init)

    _chain_pipeline(q_ref.shape[0], step_fn, finish_fn)


def _attn_prompt_call(q, k, v, lam_params, subln_g, lam_init, nb):
    bsz, nh, n, hd = q.shape
    spec = pl.BlockSpec((nb, nh, n, hd), lambda b: (b, 0, 0, 0))
    return pl.pallas_call(
        functools.partial(_attn_prompt_kernel, lam_init=lam_init),
        out_shape=jax.ShapeDtypeStruct((bsz, n, nh * hd), BF16),
        grid=(bsz // nb,),
        in_specs=[_resident(lam_params.shape), _resident(subln_g.shape), spec, spec, spec],
        out_specs=pl.BlockSpec((nb, n, nh * hd), lambda b: (b, 0, 0)),
        scratch_shapes=[pltpu.VMEM((2 * nh, 2 * n, n), F32), pltpu.VMEM((2 * nh, 2 * n, MXU_N), F32)],
        compiler_params=_params(1),
        name="attn",
    )(lam_params, subln_g, q, k, v)


POST_BLOCK = 256
ONES_ROWS = 16
ATTN_WIDTH = 2
KEY_CHUNK = 256


def _attn_cache_kernel(*refs, lam_init, n_cast):
    lam_ref, sg_ref, q_ref, k_ref, vt_ref, ck_ref, cv_ref = refs[:7]
    cast_in = refs[7:7 + n_cast]
    o_ref = refs[7 + n_cast]
    cast_out = refs[8 + n_cast:8 + 2 * n_cast]
    kc_ref, vtc_ref, s_ref, ov_ref, m_ref, qt_ref = refs[8 + 2 * n_cast:]
    for src, dst in zip(cast_in, cast_out):
        dst[...] = src[...].astype(BF16)
    past = ck_ref.shape[0]
    assert past == KEY_CHUNK
    lk = past + k_ref.shape[0]
    kc_ref[...] = ck_ref[...].astype(BF16)
    vtc_ref[0:HEAD, :] = cv_ref[...].T.astype(BF16)
    vtc_ref[HEAD:, :] = jnp.ones((ONES_ROWS, past), BF16)

    def key_rows(c0):
        return kc_ref[...] if c0 == 0 else k_ref[c0 - past:c0 - past + KEY_CHUNK, :]

    def value_cols(c0):
        return vtc_ref[...] if c0 == 0 else vt_ref[:, c0 - past:c0 - past + KEY_CHUNK]

    lam = _diff_lambda(lam_ref, lam_init)
    sg = sg_ref[...] * (1.0 - lam_init)

    def rows(g, u):
        start = (g * ATTN_WIDTH + u) * ATTN_SUB
        return pl.ds(pl.multiple_of(start, ATTN_SUB), ATTN_SUB)

    def step_fn(gs, ps, gv, pv):
        chains = range(ATTN_WIDTH)
        if gs is not None:
            for u in chains:
                qt_ref[u] = _stack_components(q_ref[rows(gs, u), :]).T
            qq = [qt_ref[u] for u in chains]
            col_max = [None] * ATTN_WIDTH
        if gv is not None:
            m_prev = [m_ref[pv * ATTN_WIDTH + u] for u in chains]
            acc = [None] * ATTN_WIDTH
        for c0 in range(0, lk, KEY_CHUNK):
            keys = slice(c0, c0 + KEY_CHUNK)
            for u in chains:
                if gs is not None:
                    s = _dot(key_rows(c0), qq[u])
                    s_ref[ps * ATTN_WIDTH + u, keys, :] = s
                    cm = jnp.max(s, axis=0, keepdims=True)
                    col_max[u] = cm if col_max[u] is None else jnp.maximum(col_max[u], cm)
            for u in chains:
                if gv is not None:
                    e = jnp.exp2(s_ref[pv * ATTN_WIDTH + u, keys, :] - m_prev[u]).astype(BF16)
                    part = _dot(value_cols(c0), e)
                    acc[u] = part if acc[u] is None else acc[u] + part
        for u in chains:
            if gs is not None:
                m_ref[ps * ATTN_WIDTH + u] = col_max[u]
            if gv is not None:
                ov_ref[pv * ATTN_WIDTH + u] = acc[u]

    def finish_fn(g, par):
        for u in range(ATTN_WIDTH):
            ov = ov_ref[par * ATTN_WIDTH + u]
            o_t = (ov[0:HEAD, 0:ATTN_SUB] / ov[HEAD:HEAD + 1, 0:ATTN_SUB]
                   - lam * (ov[0:HEAD, ATTN_SUB:] / ov[HEAD:HEAD + 1, ATTN_SUB:]))
            ms = jnp.mean(o_t * o_t, axis=0, keepdims=True)
            o_ref[rows(g, u), :] = ((o_t * lax.rsqrt(ms + EPS)).T * sg).astype(BF16)

    _chain_pipeline(q_ref.shape[0] // (ATTN_SUB * ATTN_WIDTH), step_fn, finish_fn)


def _attn_cache_call(q, k, vt, lam_params, subln_g, cache_k, cache_v, layer, lam_init, to_cast):
    bsz, nh, n, hd = q.shape
    past = cache_k.shape[3]
    seq_spec = pl.BlockSpec((None, None, n, hd), lambda b, h: (b, h, 0, 0))
    vt_spec = pl.BlockSpec((None, None, hd + ONES_ROWS, n), lambda b, h: (b, h, 0, 0))
    c_spec = pl.BlockSpec((None, None, None, past, hd), lambda b, h: (b, layer, h, 0, 0))
    flat = [w.reshape(-1, w.shape[-1]) for w in to_cast]
    slab_specs = [pl.BlockSpec((w.shape[0] // (bsz * nh), w.shape[1]), lambda b, h: (b * nh + h, 0))
                  for w in flat]
    outs = pl.pallas_call(
        functools.partial(_attn_cache_kernel, lam_init=lam_init, n_cast=len(flat)),
        out_shape=[jax.ShapeDtypeStruct((bsz, n, nh * hd), BF16)]
        + [jax.ShapeDtypeStruct(w.shape, BF16) for w in flat],
        grid=(bsz, nh),
        in_specs=[_resident(lam_params.shape), _resident(subln_g.shape),
                  seq_spec, seq_spec, vt_spec, c_spec, c_spec] + slab_specs,
        out_specs=[pl.BlockSpec((None, n, hd), lambda b, h: (b, 0, h))] + slab_specs,
        scratch_shapes=[pltpu.VMEM((past, hd), BF16), pltpu.VMEM((hd + ONES_ROWS, past), BF16),
                        pltpu.VMEM((2 * ATTN_WIDTH, past + n, 2 * ATTN_SUB), F32),
                        pltpu.VMEM((2 * ATTN_WIDTH, hd + ONES_ROWS, 2 * ATTN_SUB), F32),
                        pltpu.VMEM((2 * ATTN_WIDTH, 1, 2 * ATTN_SUB), F32),
                        pltpu.VMEM((ATTN_WIDTH, hd, 2 * ATTN_SUB), BF16)],
        compiler_params=_params(2),
        name="attn_cache",
    )(lam_params, subln_g, q, k, vt, cache_k, cache_v, *flat)
    return outs[0], [o.reshape(w.shape) for o, w in zip(outs[1:], to_cast)]


def _post_kernel(x_ref, a_ref, b_ref, mod_ref, g_ref, wo_ref, wg_ref, wu_ref, wd_ref, o_ref):
    nb = x_ref.shape[0] // POST_BLOCK
    blocks = [slice(i * POST_BLOCK, (i + 1) * POST_BLOCK) for i in range(nb)]

    def out_proj(rows):
        return _dot(jnp.concatenate([a_ref[rows, :], b_ref[rows, :]], axis=1), wo_ref[...])

    def norms(rows, y):
        x1 = x_ref[rows, :] + mod_ref[2:3, :] * _rms(y, g_ref[1:2, :])
        return x1, _modulate(x1, g_ref[2:3, :], mod_ref[3:4, :], mod_ref[4:5, :]).astype(BF16)

    def gate_up(h):
        return _dot(h, wg_ref[...]), _dot(h, wu_ref[...])

    def down(gu):
        return _dot((_silu(gu[0]) * gu[1]).astype(BF16), wd_ref[...])

    def finish(rows, x1, f):
        o_ref[rows, :] = x1 + mod_ref[5:6, :] * _rms(f, g_ref[3:4, :])

    y = {0: out_proj(blocks[0])}
    x1, gu = {}, {}
    for i in range(nb + 1):
        if i + 1 < nb:
            y[i + 1] = out_proj(blocks[i + 1])
        if i < nb:
            x1[i], h = norms(blocks[i], y.pop(i))
        if i >= 1:
            f = down(gu.pop(i - 1))
        if i < nb:
            gu[i] = gate_up(h)
        if i >= 1:
            finish(blocks[i - 1], x1.pop(i - 1), f)


def _post_call(x, a, b, mod, g, w_out, w_gate, w_up, w_down, layer, tm, mod_base, mod_stride):
    bsz, n, d = x.shape
    x_spec, _, _, mod_spec = _tile_specs(n, tm, mod_base, mod_stride)
    half_spec = pl.BlockSpec((None, tm, a.shape[2]), lambda b_, j: (b_, j, 0))

    def layer_resident(w):
        return pl.BlockSpec((None,) + w.shape[1:], lambda *_: (layer, 0, 0), pipeline_mode=pl.Buffered(1))

    return pl.pallas_call(
        _post_kernel,
        out_shape=jax.ShapeDtypeStruct(x.shape, F32),
        grid=(bsz, n // tm),
        in_specs=[x_spec, half_spec, half_spec, mod_spec, _resident(g.shape),
                  _resident(w_out.shape), layer_resident(w_gate), layer_resident(w_up),
                  layer_resident(w_down)],
        out_specs=x_spec,
        compiler_params=_params(2),
        name="post",
    )(x, a, b, mod, g, w_out, w_gate, w_up, w_down)


def _odd_in_kernel(*refs, n_seq, whole_seqs):
    if whole_seqs:
        (x_ref, mod_ref, g_ref, w_ref, wp_ref, ps_ref, cs_ref,
         pc_ref, xc_ref, xs_ref, us_ref, f2_ref, f4_ref) = refs
        nblk, blk = x_ref.shape[0], x_ref.shape[1]
    else:
        (x_ref, xp_ref, xn_ref, mod_ref, g_ref, w_ref, wp_ref, ps_ref, cs_ref,
         pc_ref, xc_ref, xs_ref, us_ref, f2_ref, f4_ref) = refs
        j = pl.program_id(1)
        nt = pl.num_programs(1)
        nblk, blk = 1, x_ref.shape[0]
    tm = nblk * blk
    g = g_ref[0:1, :]
    shift = mod_ref[0:1, :]
    scale = mod_ref[1:2, :]
    if whole_seqs:
        h = _modulate(x_ref[...].reshape(tm, x_ref.shape[2]), g, shift, scale).astype(BF16)
        up = _dot(h, w_ref[:, 0:POOL_W])
    else:
        xh = jnp.concatenate([x_ref[...], xp_ref[...], xn_ref[...]], axis=0)
        hz = _modulate(xh, g, shift, scale).astype(BF16)
        h = hz[0:tm]
        upz = _dot(hz, w_ref[:, 0:POOL_W])
        up, uph = upz[0:tm], upz[tm:]
    uf = _dot(h, w_ref[:, POOL_W:]).astype(BF16)

    def store(ref, lanes, val):
        if whole_seqs:
            for i in range(nblk):
                ref[i, :, lanes] = val[i * blk:(i + 1) * blk]
        else:
            ref[:, lanes] = val

    for gi in range(FOURIER_W // GROUP):
        lanes = slice(GROUP * gi, GROUP * (gi + 1))
        cs = _dot(uf[:, lanes], cs_ref[...])
        store(xc_ref, lanes, cs[:, 0:GROUP].astype(BF16))
        store(xs_ref, lanes, cs[:, GROUP:].astype(BF16))

    stride = blk + HALO

    def u0(i):
        return HALO + i * stride

    rows = nblk * stride + HALO
    zeros = jnp.zeros((HALO, POOL_W), F32)
    for i in range(nblk):
        us_ref[u0(i):u0(i) + blk, :] = up[i * blk:(i + 1) * blk]
        if whole_seqs:
            us_ref[u0(i) - HALO:u0(i), :] = zeros
    if whole_seqs:
        us_ref[rows - HALO:rows, :] = zeros
    else:
        us_ref[0:HALO, :] = jnp.where(j > 0, uph[0:HALO], 0.0)
        us_ref[rows - HALO:rows, :] = jnp.where(j < nt - 1, uph[HALO:], 0.0)
    us_ref[rows:, :] = zeros
    f2_ref[0:rows, :] = us_ref[0:rows, GROUP:] + us_ref[1:rows + 1, GROUP:]
    f2_ref[rows:, :] = jnp.zeros((HALO, POOL_W - GROUP), F32)
    f4_ref[0:rows, :] = f2_ref[0:rows, GROUP:] + f2_ref[2:rows + 2, GROUP:]
    f4_ref[rows:, :] = jnp.zeros((HALO, POOL_W - 2 * GROUP), F32)
    f8 = f4_ref[0:rows, GROUP:] + f4_ref[4:rows + 4, GROUP:]

    def centred(i):
        a = u0(i)
        return (us_ref[a - 1:a - 1 + blk, 0:GROUP] + us_ref[a:a + blk, 0:GROUP],
                f2_ref[a - 2:a - 2 + blk, 0:GROUP] + f2_ref[a:a + blk, 0:GROUP],
                f4_ref[a - 4:a - 4 + blk, 0:GROUP] + f4_ref[a:a + blk, 0:GROUP],
                f8[a - HALO:a - HALO + blk] + f8[a:a + blk])

    sums = [centred(i) for i in range(nblk)]
    pos = lax.broadcasted_iota(jnp.int32, (blk, 1), 0)
    t = (pos if whole_seqs else j * blk + pos).astype(F32)
    for gi, win in enumerate(POOL_WINDOWS):
        lanes = slice(GROUP * gi, GROUP * (gi + 1))
        cnt = jnp.minimum(t + float(win // 2), float(n_seq)) - jnp.maximum(t - float(win // 2), 0.0)
        pooled = jnp.concatenate([s[gi] / cnt for s in sums], axis=0)
        diff = (pooled - up[:, lanes]).astype(BF16)
        store(pc_ref, lanes, (_dot(diff, wp_ref[gi]) * ps_ref[0:1, lanes]).astype(BF16))


def _odd_in_call(x, mod, g, w_in, w_pool, pool_scale, cs_mat, tm, mod_base, mod_stride):
    bsz, n, d = x.shape
    whole_seqs = tm >= n
    weights = [_resident(g.shape), _resident(w_in.shape), _resident(w_pool.shape),
               _resident(pool_scale.shape), _resident(cs_mat.shape)]
    if whole_seqs:
        ns = tm // n
        in_specs = [pl.BlockSpec((ns, n, d), lambda b: (b, 0, 0)),
                    pl.BlockSpec((None, 6, d), lambda b: (mod_base, 0, 0))] + weights
        args = [x, mod, g, w_in, w_pool, pool_scale, cs_mat]
        out_spec = pl.BlockSpec((ns, n, POOL_W), lambda b: (b, 0, 0))
        grid = (bsz // ns,)
        scratch_rows = ns * (n + HALO) + 2 * HALO
    else:
        x_spec, prev_spec, next_spec, mod_spec = _tile_specs(n, tm, mod_base, mod_stride)
        in_specs = [x_spec, prev_spec, next_spec, mod_spec] + weights
        args = [x, x, x, mod, g, w_in, w_pool, pool_scale, cs_mat]
        out_spec = pl.BlockSpec((None, tm, POOL_W), lambda b, j: (b, j, 0))
        grid = (bsz, n // tm)
        scratch_rows = tm + 3 * HALO
    out_sds = jax.ShapeDtypeStruct((bsz, n, POOL_W), BF16)
    return pl.pallas_call(
        functools.partial(_odd_in_kernel, n_seq=n, whole_seqs=whole_seqs),
        out_shape=(out_sds, out_sds, out_sds),
        grid=grid,
        in_specs=in_specs,
        out_specs=(out_spec, out_spec, out_spec),
        scratch_shapes=[pltpu.VMEM((scratch_rows, POOL_W), F32),
                        pltpu.VMEM((scratch_rows, POOL_W - GROUP), F32),
                        pltpu.VMEM((scratch_rows, POOL_W - 2 * GROUP), F32)],
        compiler_params=_params(len(grid)),
        name="odd_in",
    )(*args)


def _four_kernel(c_ref, s_ref, xc_ref, xs_ref, wf_ref, o_ref, *, scale):
    for b in range(xc_ref.shape[0]):
        y = _dot(c_ref[...], xc_ref[b]) - _dot(s_ref[...], xs_ref[b])
        four = (y * scale).astype(BF16)
        for gi in range(FOURIER_W // GROUP):
            lanes = slice(GROUP * gi, GROUP * (gi + 1))
            o_ref[b, :, lanes] = _dot(four[:, lanes], wf_ref[gi]).astype(BF16)


def _four_call(cn, sn, xc, xs, w_four, tm, nb):
    bsz, n, w = xc.shape
    mat_spec = pl.BlockSpec((tm, n), lambda b, j: (j, 0))
    seq_spec = pl.BlockSpec((nb, n, w), lambda b, j: (b, 0, 0))
    return pl.pallas_call(
        functools.partial(_four_kernel, scale=float(1.0 / math.sqrt(n * GROUP))),
        out_shape=jax.ShapeDtypeStruct((bsz, n, w), BF16),
        grid=(bsz // nb, n // tm),
        in_specs=[mat_spec, mat_spec, seq_spec, seq_spec, _resident(w_four.shape)],
        out_specs=pl.BlockSpec((nb, tm, w), lambda b, j: (b, j, 0)),
        compiler_params=_params(2),
        name="fourier",
    )(cn, sn, xc, xs, w_four)


FLIP_BLOCK = 256


def _four_sym_kernel(c_ref, s_ref, pm_ref, xc_ref, xs_ref, wf_ref, o_ref, *, scale):
    half = o_ref.shape[0] // 2
    p = _dot(c_ref[...], xc_ref[...])
    q = _dot(s_ref[...], xs_ref[...])

    def project(rows, four):
        for gi in range(FOURIER_W // GROUP):
            lanes = slice(GROUP * gi, GROUP * (gi + 1))
            o_ref[rows, lanes] = _dot(four[:, lanes], wf_ref[gi]).astype(BF16)

    project(slice(0, half), ((p[0:half] - q[0:half]) * scale).astype(BF16))
    mirrored = ((p + q) * scale).astype(BF16)
    for b in range(half // FLIP_BLOCK):
        lo = half - FLIP_BLOCK * (b + 1)
        window = mirrored[lo:lo + FLIP_BLOCK + HALO, :]
        flipped = _dot(pm_ref[...], window).astype(BF16)
        project(slice(half + FLIP_BLOCK * b, half + FLIP_BLOCK * (b + 1)), flipped)


def _four_sym_call(c_half, s_half, perm, xc, xs, w_four):
    bsz, n, w = xc.shape
    seq_spec = pl.BlockSpec((None, n, w), lambda b: (b, 0, 0))
    return pl.pallas_call(
        functools.partial(_four_sym_kernel, scale=float(1.0 / math.sqrt(n * GROUP))),
        out_shape=jax.ShapeDtypeStruct((bsz, n, w), BF16),
        grid=(bsz,),
        in_specs=[_resident(c_half.shape), _resident(s_half.shape), _resident(perm.shape),
                  seq_spec, seq_spec, _resident(w_four.shape)],
        out_specs=seq_spec,
        compiler_params=_params(1),
        name="fourier_sym",
    )(c_half, s_half, perm, xc, xs, w_four)


def _flip_perm():
    pm = np.zeros((FLIP_BLOCK, FLIP_BLOCK + HALO), np.float32)
    pm[np.arange(FLIP_BLOCK), FLIP_BLOCK - np.arange(FLIP_BLOCK)] = 1.0
    return pm


def _rope_tables(n_tok):
    rows = n_tok // GRID_W
    row = np.repeat(np.arange(rows), GRID_W).astype(np.float64)
    col = np.tile(np.arange(GRID_W), rows).astype(np.float64)
    inv = ROPE_BASE ** (-np.arange(0, ROPE_AXIS, 2, dtype=np.float64) / ROPE_AXIS)
    ang_r = row[:, None] * inv[None, :]
    ang_c = col[:, None] * inv[None, :]
    ang = np.concatenate([ang_r, ang_r, ang_c, ang_c], axis=-1)
    cos = np.concatenate([np.cos(ang)] * 2, axis=-1)
    sin = np.concatenate([np.sin(ang)] * 2, axis=-1)
    first_half = (np.arange(HEAD) % 32) < 16
    sin_signed = np.where(first_half[None, :], -sin, sin)
    return jnp.asarray(cos, F32), jnp.asarray(sin_signed, F32)


def _dft_mats(n):
    idx = np.arange(n, dtype=np.int64)
    ang = 2.0 * np.pi * ((idx[:, None] * idx[None, :]) % n).astype(np.float64) / n
    return np.cos(ang), np.sin(ang)


def kernel(x_prompt, x_sample, cache_k, cache_v, c, c_ctx, w_mod, b_mod, norm_g,
           w_in_even, lam_params, subln_g, conv_w, w_out_even,
           w_in_odd, w_pool, pool_scale, w_fourier, w_out_odd,
           w_gate, w_up, w_down):
    depth = w_mod.shape[0]
    n_dec = x_sample.shape[0]
    n_p, n_s = x_prompt.shape[1], x_sample.shape[1]

    pad_rows = 16 - 1 - n_dec
    cc = jnp.concatenate([c_ctx[None, :], c, jnp.zeros((pad_rows, D_MODEL), F32)], axis=0)
    mod_all = _mod_call(cc, w_mod, b_mod)[:, :1 + n_dec].reshape(depth, 1 + n_dec, 6, D_MODEL)

    rope = _rope_tables(n_s)
    cc_g, sc_g = _dft_mats(GROUP)
    cs_mat = jnp.asarray(np.concatenate([cc_g, sc_g], axis=1), F32).astype(BF16)
    dft_p = tuple(jnp.asarray(m, F32).astype(BF16) for m in _dft_mats(n_p))
    dft_s = tuple(jnp.asarray(m[:n_s // 2 + HALO], F32).astype(BF16) for m in _dft_mats(n_s))
    flip = jnp.asarray(_flip_perm(), F32).astype(BF16)

    late_weights = [w_gate, w_up, w_down, w_out_even, w_in_odd, w_out_odd]
    xp, xs = x_prompt, x_sample
    new_k, new_v = [], []
    for l in range(depth):
        mod = mod_all[l]
        g = norm_g[l]
        i = l // 2
        streams = []
        if l % 2 == 0:
            lam_init = 0.8 - 0.6 * math.exp(-0.3 * l)
            w_in = w_in_even[i].astype(BF16)
            sg = subln_g[i][None, :]
            qp, kp, vp, cbp = _even_in_call(xp, mod, g, w_in, conv_w[i], None, 4 * n_p, 0, 0)
            ap = _attn_prompt_call(qp, kp, vp, lam_params[i], sg, lam_init, 8)
            new_k.append(kp)
            new_v.append(vp)
            qs, ks, vts, cbs = _even_in_call(xs, mod, g, w_in, conv_w[i], rope, 1024, 1, 1)
            a_s, cast = _attn_cache_call(qs, ks, vts, lam_params[i], sg, cache_k, cache_v, i, lam_init,
                                         late_weights if l == 0 else [])
            if l == 0:
                wg, wu, wd, wo_even, wi_odd, wo_odd = cast
            w_out = wo_even[i]
            streams = [(ap, cbp), (a_s, cbs)]
        else:
            w_in = wi_odd[i]
            w_out = wo_odd[i]
            wp = w_pool[i].astype(BF16)
            wf = w_fourier[i].astype(BF16)
            ps = pool_scale[i][None, :]
            pcp, xcp, xsp = _odd_in_call(xp, mod, g, w_in, wp, ps, cs_mat, 4 * n_p, 0, 0)
            fcp = _four_call(*dft_p, xcp, xsp, wf, n_p, 8)
            pcs, xcs, xss = _odd_in_call(xs, mod, g, w_in, wp, ps, cs_mat, 1024, 1, 1)
            fcs = _four_sym_call(*dft_s, flip, xcs, xss, wf)
            streams = [(pcp, fcp), (pcs, fcs)]
        xp = _post_call(*(t.reshape(1, -1, t.shape[-1]) for t in (xp,) + streams[0]),
                        mod, g, w_out, wg, wu, wd, l, 512, 0, 0).reshape(x_prompt.shape)
        xs = _post_call(xs, streams[1][0], streams[1][1], mod, g, w_out, wg, wu, wd, l, 512, 1, 1)
    def stack_layers(parts):
        if len(parts) == 1:
            return parts[0][:, None]
        return jnp.stack(parts, axis=1)

    return xp, xs, stack_layers(new_k), stack_layers(new_v)
```

```python
import functools
import math

import numpy as np
import jax
import jax.numpy as jnp
from jax import lax
from jax.experimental import pallas as pl
from jax.experimental.pallas import tpu as pltpu

F32 = jnp.float32
BF16 = jnp.bfloat16

D_MODEL = 1024
GRID_W = 64
N_HEADS = 4
HEAD = 128
HALF_HEAD = 64
ROPE_AXIS = 32
ROPE_BASE = 10000.0
ATTN_W = 512
CONV_W = 512
POOL_W = 512
FOURIER_W = 512
GROUP = 128
POOL_WINDOWS = (2, 4, 8, 16)
D_FF = 2816
EPS = 1e-6
LOG2E = math.log2(math.e)
HALO = 8
MXU_N = 256
ATTN_SUB = 128
EVEN_BLOCK = 256
VMEM_LIMIT = 56 * 1024 * 1024


def _params(n_axes):
    return pltpu.CompilerParams(dimension_semantics=("arbitrary",) * n_axes,
                                vmem_limit_bytes=VMEM_LIMIT)


def _resident(shape):
    return pl.BlockSpec(shape, lambda *_: (0,) * len(shape), pipeline_mode=pl.Buffered(1))


def _rms(x, g):
    ms = jnp.mean(x * x, axis=-1, keepdims=True)
    return x * lax.rsqrt(ms + EPS) * g


def _modulate(x, g, shift, scale):
    return _rms(x, g) * (1.0 + scale) + shift


def _dot(a, b):
    return jnp.dot(a, b, preferred_element_type=F32)


def _silu(x):
    return x / (1.0 + jnp.exp(-x))


def _mod_kernel(cc_ref, w_ref, b_ref, o_ref):
    s = _silu(cc_ref[...]).astype(BF16)
    o_ref[...] = _dot(s, w_ref[...].astype(BF16)) + b_ref[...]


def _mod_call(cc, w_mod, b_mod):
    depth, d, n6 = w_mod.shape
    rows = cc.shape[0]
    tn = 2048
    return pl.pallas_call(
        _mod_kernel,
        out_shape=jax.ShapeDtypeStruct((depth, rows, n6), F32),
        grid=(depth, n6 // tn),
        in_specs=[
            pl.BlockSpec((rows, d), lambda l, j: (0, 0)),
            pl.BlockSpec((None, d, tn), lambda l, j: (l, 0, j)),
            pl.BlockSpec((None, 1, tn), lambda l, j: (l, 0, j)),
        ],
        out_specs=pl.BlockSpec((None, rows, tn), lambda l, j: (l, 0, j)),
        compiler_params=_params(2),
        name="mod",
    )(cc, w_mod, b_mod.reshape(depth, 1, n6))


def _tile_specs(n, tm, mod_base, mod_stride):
    nb8 = n // HALO
    t8 = tm // HALO
    x_spec = pl.BlockSpec((None, tm, D_MODEL), lambda b, j: (b, j, 0))
    prev_spec = pl.BlockSpec((None, HALO, D_MODEL),
                             lambda b, j: (b, jnp.maximum(j * t8 - 1, 0), 0))
    next_spec = pl.BlockSpec((None, HALO, D_MODEL),
                             lambda b, j: (b, jnp.minimum((j + 1) * t8, nb8 - 1), 0))
    mod_spec = pl.BlockSpec((None, 6, D_MODEL),
                            lambda b, j: (mod_base + mod_stride * b, 0, 0))
    return x_spec, prev_spec, next_spec, mod_spec


def _rope(t, cos, sin_signed, first_half):
    outs = []
    for hh in range(N_HEADS):
        th = t[:, HEAD * hh:HEAD * (hh + 1)]
        swapped = jnp.where(first_half, pltpu.roll(th, HEAD - 16, 1), pltpu.roll(th, 16, 1))
        outs.append(th * cos + swapped * sin_signed)
    return outs


def _even_in_kernel(*refs, use_rope):
    if use_rope:
        (x_ref, xp_ref, xn_ref, mod_ref, g_ref, w_ref, cw_ref, cos_ref, sin_ref,
         q_ref, k_ref, v_ref, cb_ref, zs_ref) = refs
        j = pl.program_id(1)
        nt = pl.num_programs(1)
        blk = EVEN_BLOCK
        nblk = x_ref.shape[0] // blk
        lane = lax.broadcasted_iota(jnp.int32, (1, HEAD), 1)
        first_half = (lane % 32) < 16
    else:
        x_ref, mod_ref, g_ref, w_ref, cw_ref, q_ref, k_ref, v_ref, cb_ref, zs_ref = refs
        nblk, blk = x_ref.shape[0], x_ref.shape[1]
    g = g_ref[0:1, :]
    shift = mod_ref[0:1, :]
    scale = mod_ref[1:2, :]
    stride = blk if use_rope else blk + HALO

    def z0(i):
        return HALO + i * stride

    def rows(i):
        return slice(i * blk, (i + 1) * blk)

    def proj(hh, lo):
        return _dot(hh, w_ref[:, lo:lo + 512])

    def modulated(i):
        xi = x_ref[rows(i), :] if use_rope else x_ref[i]
        return _modulate(xi, g, shift, scale).astype(BF16)

    def conv_inputs(i, h):
        outer = use_rope and i == 0
        if outer:
            xh = jnp.concatenate([xp_ref[...], xn_ref[...]], axis=0)
            hz = jnp.concatenate([h, _modulate(xh, g, shift, scale).astype(BF16)], axis=0)
        else:
            hz = h
        z = proj(hz, 2048) * proj(hz, 2560)
        if outer:
            zh = z[blk:]
            z = z[0:blk]
            zs_ref[0:HALO, :] = jnp.where(j > 0, zh[0:HALO], 0.0)
            zs_ref[z0(nblk):, :] = jnp.where(j < nt - 1, zh[HALO:], 0.0)
        zs_ref[z0(i):z0(i) + blk, :] = z
        return z, proj(h, 1536)

    def conv_out(i, z, gate_b):
        conv = (cw_ref[0:1, :] * zs_ref[z0(i) - 1:z0(i) - 1 + blk, :] + cw_ref[1:2, :] * z
                + cw_ref[2:3, :] * zs_ref[z0(i) + 1:z0(i) + 1 + blk, :])
        out = (gate_b * conv).astype(BF16)
        if use_rope:
            cb_ref[rows(i), :] = out
        else:
            cb_ref[i] = out

    def qkv(i, h):
        v = proj(h, 1024)
        for hh in range(N_HEADS):
            vh = v[:, HEAD * hh:HEAD * (hh + 1)]
            if use_rope:
                v_ref[hh, 0:HEAD, rows(i)] = vh.T.astype(v_ref.dtype)
                v_ref[hh, HEAD:, rows(i)] = jnp.ones((ONES_ROWS, blk), v_ref.dtype)
            else:
                v_ref[i, hh] = vh.astype(v_ref.dtype)
        for ref, t in ((k_ref, proj(h, 512)), (q_ref, proj(h, 0) * (HALF_HEAD ** -0.5 * LOG2E))):
            if use_rope:
                heads = _rope(t, cos_ref[rows(i), :], sin_ref[rows(i), :], first_half)
            else:
                heads = [t[:, HEAD * hh:HEAD * (hh + 1)] for hh in range(N_HEADS)]
            for hh in range(N_HEADS):
                if use_rope:
                    ref[hh, rows(i), :] = heads[hh].astype(ref.dtype)
                else:
                    ref[i, hh] = heads[hh].astype(ref.dtype)

    if not use_rope:
        for i in range(nblk + 1):
            zs_ref[i * stride:i * stride + HALO, :] = jnp.zeros((HALO, CONV_W), F32)

    h = {0: modulated(0)}
    zg = {0: conv_inputs(0, h[0])}
    for i in range(nblk):
        if i + 1 < nblk:
            h[i + 1] = modulated(i + 1)
        else:
            conv_out(i, *zg.pop(i))
        qkv(i, h.pop(i))
        if i + 1 < nblk:
            zg[i + 1] = conv_inputs(i + 1, h[i + 1])
            conv_out(i, *zg.pop(i))


def _even_in_call(x, mod, g, w_in, conv_w, rope, tm, mod_base, mod_stride):
    bsz, n, d = x.shape
    use_rope = rope is not None
    head_shape = (bsz, N_HEADS, n, HEAD)
    if use_rope:
        x_spec, prev_spec, next_spec, mod_spec = _tile_specs(n, tm, mod_base, mod_stride)
        tab = pl.BlockSpec((tm, HEAD), lambda b, j: (j, 0))
        in_specs = [x_spec, prev_spec, next_spec, mod_spec, _resident(g.shape), _resident(w_in.shape),
                    _resident(conv_w.shape), tab, tab]
        args = [x, x, x, mod, g, w_in, conv_w, *rope]
        head_spec = pl.BlockSpec((None, N_HEADS, tm, HEAD), lambda b, j: (b, 0, j, 0))
        v_spec = pl.BlockSpec((None, N_HEADS, HEAD + ONES_ROWS, tm), lambda b, j: (b, 0, 0, j))
        cb_spec = pl.BlockSpec((None, tm, CONV_W), lambda b, j: (b, j, 0))
        k_sds = jax.ShapeDtypeStruct(head_shape, BF16)
        v_sds = jax.ShapeDtypeStruct((bsz, N_HEADS, HEAD + ONES_ROWS, n), BF16)
        grid = (bsz, n // tm)
        zs_rows = tm + 2 * HALO
    else:
        ns = tm // n
        in_specs = [pl.BlockSpec((ns, n, d), lambda b: (b, 0, 0)),
                    pl.BlockSpec((None, 6, d), lambda b: (mod_base, 0, 0)),
                    _resident(g.shape), _resident(w_in.shape), _resident(conv_w.shape)]
        args = [x, mod, g, w_in, conv_w]
        head_spec = v_spec = pl.BlockSpec((ns, N_HEADS, n, HEAD), lambda b: (b, 0, 0, 0))
        cb_spec = pl.BlockSpec((ns, n, CONV_W), lambda b: (b, 0, 0))
        k_sds = v_sds = jax.ShapeDtypeStruct(head_shape, F32)
        grid = (bsz // ns,)
        zs_rows = ns * (n + HALO) + HALO
    return pl.pallas_call(
        functools.partial(_even_in_kernel, use_rope=use_rope),
        out_shape=(jax.ShapeDtypeStruct(head_shape, BF16), k_sds, v_sds,
                   jax.ShapeDtypeStruct((bsz, n, CONV_W), BF16)),
        grid=grid,
        in_specs=in_specs,
        out_specs=(head_spec, head_spec, v_spec, cb_spec),
        scratch_shapes=[pltpu.VMEM((zs_rows, CONV_W), F32)],
        compiler_params=_params(len(grid)),
        name="even_in_rope" if use_rope else "even_in",
    )(*args)


def _diff_lambda(lam_ref, lam_init):
    lp = lam_ref[...]
    return (jnp.exp(jnp.sum(lp[0:1] * lp[1:2], axis=-1, keepdims=True))
            - jnp.exp(jnp.sum(lp[2:3] * lp[3:4], axis=-1, keepdims=True)) + lam_init)


def _stack_components(q):
    lane = lax.broadcasted_iota(jnp.int32, (1, HEAD), 1)
    zero = jnp.zeros_like(q)
    return jnp.concatenate([jnp.where(lane < HALF_HEAD, q, zero),
                            jnp.where(lane >= HALF_HEAD, q, zero)], axis=0)


def _softmax_pv(s, v_ext):
    e = jnp.exp2(s - jnp.max(s, axis=-1, keepdims=True)).astype(BF16)
    return _dot(e, v_ext)


def _normalise(ov, lam, sg, lam_init):
    t = ov.shape[0] // 2
    o = ov[:t, :HEAD] / ov[:t, HEAD:] - lam * (ov[t:, :HEAD] / ov[t:, HEAD:])
    return (_rms(o, sg) * (1.0 - lam_init)).astype(BF16)


def _chain_pipeline(n_groups, step_fn, finish_fn):
    assert n_groups % 2 == 0 and n_groups >= 2
    step_fn(0, 0, None, None)
    step_fn(1, 1, 0, 0)

    def body(t, carry):
        g = 2 * t
        step_fn(g, 0, g - 1, 1)
        finish_fn(g - 2, 0)
        step_fn(g + 1, 1, g, 0)
        finish_fn(g - 1, 1)
        return carry

    lax.fori_loop(1, n_groups // 2, body, 0)
    step_fn(None, None, n_groups - 1, 1)
    finish_fn(n_groups - 2, 0)
    finish_fn(n_groups - 1, 1)


def _attn_prompt_kernel(lam_ref, sg_ref, q_ref, k_ref, v_ref, o_ref, s_ref, ov_ref, *, lam_init):
    lam = _diff_lambda(lam_ref, lam_init)
    sg = sg_ref[...]
    n = k_ref.shape[2]
    ones = jnp.ones((n, MXU_N - HEAD), BF16)

    def step_fn(bs, ps, bv, pv):
        for hh in range(N_HEADS):
            if bs is not None:
                s_ref[ps * N_HEADS + hh] = lax.dot_general(
                    _stack_components(q_ref[bs, hh]), k_ref[bs, hh].astype(BF16),
                    (((1,), (1,)), ((), ())), preferred_element_type=F32)
        for hh in range(N_HEADS):
            if bv is not None:
                v_ext = jnp.concatenate([v_ref[bv, hh].astype(BF16), ones], axis=1)
                ov_ref[pv * N_HEADS + hh] = _softmax_pv(s_ref[pv * N_HEADS + hh], v_ext)

    def finish_fn(b, par):
        for hh in range(N_HEADS):
            o_ref[b, :, HEAD * hh:HEAD * (hh + 1)] = _normalise(ov_ref[par * N_HEADS + hh], lam, sg, lam_init)

    _chain_pipeline(q_ref.shape[0], step_fn, finish_fn)


def _attn_prompt_call(q, k, v, lam_params, subln_g, lam_init, nb):
    bsz, nh, n, hd = q.shape
    spec = pl.BlockSpec((nb, nh, n, hd), lambda b: (b, 0, 0, 0))
    return pl.pallas_call(
        functools.partial(_attn_prompt_kernel, lam_init=lam_init),
        out_shape=jax.ShapeDtypeStruct((bsz, n, nh * hd), BF16),
        grid=(bsz // nb,),
        in_specs=[_resident(lam_params.shape), _resident(subln_g.shape), spec, spec, spec],
        out_specs=pl.BlockSpec((nb, n, nh * hd), lambda b: (b, 0, 0)),
        scratch_shapes=[pltpu.VMEM((2 * nh, 2 * n, n), F32), pltpu.VMEM((2 * nh, 2 * n, MXU_N), F32)],
        compiler_params=_params(1),
        name="attn",
    )(lam_params, subln_g, q, k, v)


POST_BLOCK = 256
ONES_ROWS = 16
ATTN_WIDTH = 2
KEY_CHUNK = 256


def _attn_cache_kernel(*refs, lam_init, n_cast):
    lam_ref, sg_ref, q_ref, k_ref, vt_ref, ck_ref, cv_ref = refs[:7]
    cast_in = refs[7:7 + n_cast]
    o_ref = refs[7 + n_cast]
    cast_out = refs[8 + n_cast:8 + 2 * n_cast]
    kc_ref, vtc_ref, s_ref, ov_ref, m_ref, qt_ref = refs[8 + 2 * n_cast:]
    for src, dst in zip(cast_in, cast_out):
        dst[...] = src[...].astype(BF16)
    past = ck_ref.shape[0]
    assert past == KEY_CHUNK
    lk = past + k_ref.shape[0]
    kc_ref[...] = ck_ref[...].astype(BF16)
    vtc_ref[0:HEAD, :] = cv_ref[...].T.astype(BF16)
    vtc_ref[HEAD:, :] = jnp.ones((ONES_ROWS, past), BF16)

    def key_rows(c0):
        return kc_ref[...] if c0 == 0 else k_ref[c0 - past:c0 - past + KEY_CHUNK, :]

    def value_cols(c0):
        return vtc_ref[...] if c0 == 0 else vt_ref[:, c0 - past:c0 - past + KEY_CHUNK]

    lam = _diff_lambda(lam_ref, lam_init)
    sg = sg_ref[...] * (1.0 - lam_init)

    def rows(g, u):
        start = (g * ATTN_WIDTH + u) * ATTN_SUB
        return pl.ds(pl.multiple_of(start, ATTN_SUB), ATTN_SUB)

    def step_fn(gs, ps, gv, pv):
        chains = range(ATTN_WIDTH)
        if gs is not None:
            for u in chains:
                qt_ref[u] = _stack_components(q_ref[rows(gs, u), :]).T
            qq = [qt_ref[u] for u in chains]
            col_max = [None] * ATTN_WIDTH
        if gv is not None:
            m_prev = [m_ref[pv * ATTN_WIDTH + u] for u in chains]
            acc = [None] * ATTN_WIDTH
        for c0 in range(0, lk, KEY_CHUNK):
            keys = slice(c0, c0 + KEY_CHUNK)
            for u in chains:
                if gs is not None:
                    s = _dot(key_rows(c0), qq[u])
                    s_ref[ps * ATTN_WIDTH + u, keys, :] = s
                    cm = jnp.max(s, axis=0, keepdims=True)
                    col_max[u] = cm if col_max[u] is None else jnp.maximum(col_max[u], cm)
            for u in chains:
                if gv is not None:
                    e = jnp.exp2(s_ref[pv * ATTN_WIDTH + u, keys, :] - m_prev[u]).astype(BF16)
                    part = _dot(value_cols(c0), e)
                    acc[u] = part if acc[u] is None else acc[u] + part
        for u in chains:
            if gs is not None:
                m_ref[ps * ATTN_WIDTH + u] = col_max[u]
            if gv is not None:
                ov_ref[pv * ATTN_WIDTH + u] = acc[u]

    def finish_fn(g, par):
        for u in range(ATTN_WIDTH):
            ov = ov_ref[par * ATTN_WIDTH + u]
            o_t = (ov[0:HEAD, 0:ATTN_SUB] / ov[HEAD:HEAD + 1, 0:ATTN_SUB]
                   - lam * (ov[0:HEAD, ATTN_SUB:] / ov[HEAD:HEAD + 1, ATTN_SUB:]))
            ms = jnp.mean(o_t * o_t, axis=0, keepdims=True)
            o_ref[rows(g, u), :] = ((o_t * lax.rsqrt(ms + EPS)).T * sg).astype(BF16)

    _chain_pipeline(q_ref.shape[0] // (ATTN_SUB * ATTN_WIDTH), step_fn, finish_fn)


def _attn_cache_call(q, k, vt, lam_params, subln_g, cache_k, cache_v, layer, lam_init, to_cast):
    bsz, nh, n, hd = q.shape
    past = cache_k.shape[3]
    seq_spec = pl.BlockSpec((None, None, n, hd), lambda b, h: (b, h, 0, 0))
    vt_spec = pl.BlockSpec((None, None, hd + ONES_ROWS, n), lambda b, h: (b, h, 0, 0))
    c_spec = pl.BlockSpec((None, None, None, past, hd), lambda b, h: (b, layer, h, 0, 0))
    flat = [w.reshape(-1, w.shape[-1]) for w in to_cast]
    slab_specs = [pl.BlockSpec((w.shape[0] // (bsz * nh), w.shape[1]), lambda b, h: (b * nh + h, 0))
                  for w in flat]
    outs = pl.pallas_call(
        functools.partial(_attn_cache_kernel, lam_init=lam_init, n_cast=len(flat)),
        out_shape=[jax.ShapeDtypeStruct((bsz, n, nh * hd), BF16)]
        + [jax.ShapeDtypeStruct(w.shape, BF16) for w in flat],
        grid=(bsz, nh),
        in_specs=[_resident(lam_params.shape), _resident(subln_g.shape),
                  seq_spec, seq_spec, vt_spec, c_spec, c_spec] + slab_specs,
        out_specs=[pl.BlockSpec((None, n, hd), lambda b, h: (b, 0, h))] + slab_specs,
        scratch_shapes=[pltpu.VMEM((past, hd), BF16), pltpu.VMEM((hd + ONES_ROWS, past), BF16),
                        pltpu.VMEM((2 * ATTN_WIDTH, past + n, 2 * ATTN_SUB), F32),
                        pltpu.VMEM((2 * ATTN_WIDTH, hd + ONES_ROWS, 2 * ATTN_SUB), F32),
                        pltpu.VMEM((2 * ATTN_WIDTH, 1, 2 * ATTN_SUB), F32),
                        pltpu.VMEM((ATTN_WIDTH, hd, 2 * ATTN_SUB), BF16)],
        compiler_params=_params(2),
        name="attn_cache",
    )(lam_params, subln_g, q, k, vt, cache_k, cache_v, *flat)
    return outs[0], [o.reshape(w.shape) for o, w in zip(outs[1:], to_cast)]


def _post_kernel(x_ref, a_ref, b_ref, mod_ref, g_ref, wo_ref, wg_ref, wu_ref, wd_ref, o_ref):
    nb = x_ref.shape[0] // POST_BLOCK
    blocks = [slice(i * POST_BLOCK, (i + 1) * POST_BLOCK) for i in range(nb)]

    def out_proj(rows):
        return _dot(jnp.concatenate([a_ref[rows, :], b_ref[rows, :]], axis=1), wo_ref[...])

    def norms(rows, y):
        x1 = x_ref[rows, :] + mod_ref[2:3, :] * _rms(y, g_ref[1:2, :])
        return x1, _modulate(x1, g_ref[2:3, :], mod_ref[3:4, :], mod_ref[4:5, :]).astype(BF16)

    def gate_up(h):
        return _dot(h, wg_ref[...]), _dot(h, wu_ref[...])

    def down(gu):
        return _dot((_silu(gu[0]) * gu[1]).astype(BF16), wd_ref[...])

    def finish(rows, x1, f):
        o_ref[rows, :] = x1 + mod_ref[5:6, :] * _rms(f, g_ref[3:4, :])

    y = {0: out_proj(blocks[0])}
    x1, gu = {}, {}
    for i in range(nb + 1):
        if i + 1 < nb:
            y[i + 1] = out_proj(blocks[i + 1])
        if i < nb:
            x1[i], h = norms(blocks[i], y.pop(i))
        if i >= 1:
            f = down(gu.pop(i - 1))
        if i < nb:
            gu[i] = gate_up(h)
        if i >= 1:
            finish(blocks[i - 1], x1.pop(i - 1), f)


def _post_call(x, a, b, mod, g, w_out, w_gate, w_up, w_down, layer, tm, mod_base, mod_stride):
    bsz, n, d = x.shape
    x_spec, _, _, mod_spec = _tile_specs(n, tm, mod_base, mod_stride)
    half_spec = pl.BlockSpec((None, tm, a.shape[2]), lambda b_, j: (b_, j, 0))

    def layer_resident(w):
        return pl.BlockSpec((None,) + w.shape[1:], lambda *_: (layer, 0, 0), pipeline_mode=pl.Buffered(1))

    return pl.pallas_call(
        _post_kernel,
        out_shape=jax.ShapeDtypeStruct(x.shape, F32),
        grid=(bsz, n // tm),
        in_specs=[x_spec, half_spec, half_spec, mod_spec, _resident(g.shape),
                  _resident(w_out.shape), layer_resident(w_gate), layer_resident(w_up),
                  layer_resident(w_down)],
        out_specs=x_spec,
        compiler_params=_params(2),
        name="post",
    )(x, a, b, mod, g, w_out, w_gate, w_up, w_down)


def _odd_in_kernel(*refs, n_seq, whole_seqs):
    if whole_seqs:
        (x_ref, mod_ref, g_ref, w_ref, wp_ref, ps_ref, cs_ref, dc_ref, ds_ref, wf_ref,
         pc_ref, fc_ref, us_ref, f2_ref, f4_ref) = refs
        nblk, blk = x_ref.shape[0], x_ref.shape[1]
    else:
        (x_ref, xp_ref, xn_ref, mod_ref, g_ref, w_ref, wp_ref, ps_ref, cs_ref,
         pc_ref, xc_ref, xs_ref, us_ref, f2_ref, f4_ref) = refs
        j = pl.program_id(1)
        nt = pl.num_programs(1)
        nblk, blk = 1, x_ref.shape[0]
    tm = nblk * blk
    g = g_ref[0:1, :]
    shift = mod_ref[0:1, :]
    scale = mod_ref[1:2, :]
    if whole_seqs:
        h = _modulate(x_ref[...].reshape(tm, x_ref.shape[2]), g, shift, scale).astype(BF16)
        up = _dot(h, w_ref[:, 0:POOL_W])
    else:
        xh = jnp.concatenate([x_ref[...], xp_ref[...], xn_ref[...]], axis=0)
        hz = _modulate(xh, g, shift, scale).astype(BF16)
        h = hz[0:tm]
        upz = _dot(hz, w_ref[:, 0:POOL_W])
        up, uph = upz[0:tm], upz[tm:]
    uf = _dot(h, w_ref[:, POOL_W:]).astype(BF16)

    def store(ref, lanes, val):
        if whole_seqs:
            for i in range(nblk):
                ref[i, :, lanes] = val[i * blk:(i + 1) * blk]
        else:
            ref[:, lanes] = val

    xc_groups, xs_groups = [], []
    for gi in range(FOURIER_W // GROUP):
        lanes = slice(GROUP * gi, GROUP * (gi + 1))
        cs = _dot(uf[:, lanes], cs_ref[...])
        xc_groups.append(cs[:, 0:GROUP].astype(BF16))
        xs_groups.append(cs[:, GROUP:].astype(BF16))
        if not whole_seqs:
            xc_ref[:, lanes] = xc_groups[gi]
            xs_ref[:, lanes] = xs_groups[gi]
    if whole_seqs:
        xc = jnp.concatenate(xc_groups, axis=1)
        xs = jnp.concatenate(xs_groups, axis=1)
        ortho = float(1.0 / math.sqrt(blk * GROUP))
        for i in range(nblk):
            seq = slice(i * blk, (i + 1) * blk)
            y = _dot(dc_ref[...], xc[seq]) - _dot(ds_ref[...], xs[seq])
            four = (y * ortho).astype(BF16)
            for gi in range(FOURIER_W // GROUP):
                lanes = slice(GROUP * gi, GROUP * (gi + 1))
                fc_ref[i, :, lanes] = _dot(four[:, lanes], wf_ref[gi]).astype(BF16)

    stride = blk + HALO

    def u0(i):
        return HALO + i * stride

    rows = nblk * stride + HALO
    zeros = jnp.zeros((HALO, POOL_W), F32)
    for i in range(nblk):
        us_ref[u0(i):u0(i) + blk, :] = up[i * blk:(i + 1) * blk]
        if whole_seqs:
            us_ref[u0(i) - HALO:u0(i), :] = zeros
    if whole_seqs:
        us_ref[rows - HALO:rows, :] = zeros
    else:
        us_ref[0:HALO, :] = jnp.where(j > 0, uph[0:HALO], 0.0)
        us_ref[rows - HALO:rows, :] = jnp.where(j < nt - 1, uph[HALO:], 0.0)
    us_ref[rows:, :] = zeros
    f2_ref[0:rows, :] = us_ref[0:rows, GROUP:] + us_ref[1:rows + 1, GROUP:]
    f2_ref[rows:, :] = jnp.zeros((HALO, POOL_W - GROUP), F32)
    f4_ref[0:rows, :] = f2_ref[0:rows, GROUP:] + f2_ref[2:rows + 2, GROUP:]
    f4_ref[rows:, :] = jnp.zeros((HALO, POOL_W - 2 * GROUP), F32)
    f8 = f4_ref[0:rows, GROUP:] + f4_ref[4:rows + 4, GROUP:]

    def centred(i):
        a = u0(i)
        return (us_ref[a - 1:a - 1 + blk, 0:GROUP] + us_ref[a:a + blk, 0:GROUP],
                f2_ref[a - 2:a - 2 + blk, 0:GROUP] + f2_ref[a:a + blk, 0:GROUP],
                f4_ref[a - 4:a - 4 + blk, 0:GROUP] + f4_ref[a:a + blk, 0:GROUP],
                f8[a - HALO:a - HALO + blk] + f8[a:a + blk])

    sums = [centred(i) for i in range(nblk)]
    pos = lax.broadcasted_iota(jnp.int32, (blk, 1), 0)
    t = (pos if whole_seqs else j * blk + pos).astype(F32)
    for gi, win in enumerate(POOL_WINDOWS):
        lanes = slice(GROUP * gi, GROUP * (gi + 1))
        cnt = jnp.minimum(t + float(win // 2), float(n_seq)) - jnp.maximum(t - float(win // 2), 0.0)
        pooled = jnp.concatenate([s[gi] / cnt for s in sums], axis=0)
        diff = (pooled - up[:, lanes]).astype(BF16)
        store(pc_ref, lanes, (_dot(diff, wp_ref[gi]) * ps_ref[0:1, lanes]).astype(BF16))


def _odd_in_call(x, mod, g, w_in, w_pool, pool_scale, cs_mat, tm, mod_base, mod_stride, seq_dft=None):
    bsz, n, d = x.shape
    whole_seqs = tm >= n
    weights = [_resident(g.shape), _resident(w_in.shape), _resident(w_pool.shape),
               _resident(pool_scale.shape), _resident(cs_mat.shape)]
    if whole_seqs:
        ns = tm // n
        in_specs = ([pl.BlockSpec((ns, n, d), lambda b: (b, 0, 0)),
                     pl.BlockSpec((None, 6, d), lambda b: (mod_base, 0, 0))] + weights
                    + [_resident(t.shape) for t in seq_dft])
        args = [x, mod, g, w_in, w_pool, pool_scale, cs_mat, *seq_dft]
        out_spec = pl.BlockSpec((ns, n, POOL_W), lambda b: (b, 0, 0))
        n_out = 2
        grid = (bsz // ns,)
        scratch_rows = ns * (n + HALO) + 2 * HALO
    else:
        x_spec, prev_spec, next_spec, mod_spec = _tile_specs(n, tm, mod_base, mod_stride)
        in_specs = [x_spec, prev_spec, next_spec, mod_spec] + weights
        args = [x, x, x, mod, g, w_in, w_pool, pool_scale, cs_mat]
        out_spec = pl.BlockSpec((None, tm, POOL_W), lambda b, j: (b, j, 0))
        n_out = 3
        grid = (bsz, n // tm)
        scratch_rows = tm + 3 * HALO
    out_sds = jax.ShapeDtypeStruct((bsz, n, POOL_W), BF16)
    return pl.pallas_call(
        functools.partial(_odd_in_kernel, n_seq=n, whole_seqs=whole_seqs),
        out_shape=(out_sds,) * n_out,
        grid=grid,
        in_specs=in_specs,
        out_specs=(out_spec,) * n_out,
        scratch_shapes=[pltpu.VMEM((scratch_rows, POOL_W), F32),
                        pltpu.VMEM((scratch_rows, POOL_W - GROUP), F32),
                        pltpu.VMEM((scratch_rows, POOL_W - 2 * GROUP), F32)],
        compiler_params=_params(len(grid)),
        name="odd_in",
    )(*args)


FLIP_BLOCK = 256


def _four_sym_kernel(c_ref, s_ref, pm_ref, xc_ref, xs_ref, wf_ref, o_ref, *, scale):
    half = o_ref.shape[0] // 2
    p = _dot(c_ref[...], xc_ref[...])
    q = _dot(s_ref[...], xs_ref[...])

    def project(rows, four):
        for gi in range(FOURIER_W // GROUP):
            lanes = slice(GROUP * gi, GROUP * (gi + 1))
            o_ref[rows, lanes] = _dot(four[:, lanes], wf_ref[gi]).astype(BF16)

    project(slice(0, half), ((p[0:half] - q[0:half]) * scale).astype(BF16))
    mirrored = ((p + q) * scale).astype(BF16)
    for b in range(half // FLIP_BLOCK):
        lo = half - FLIP_BLOCK * (b + 1)
        window = mirrored[lo:lo + FLIP_BLOCK + HALO, :]
        flipped = _dot(pm_ref[...], window).astype(BF16)
        project(slice(half + FLIP_BLOCK * b, half + FLIP_BLOCK * (b + 1)), flipped)


def _four_sym_call(c_half, s_half, perm, xc, xs, w_four):
    bsz, n, w = xc.shape
    seq_spec = pl.BlockSpec((None, n, w), lambda b: (b, 0, 0))
    return pl.pallas_call(
        functools.partial(_four_sym_kernel, scale=float(1.0 / math.sqrt(n * GROUP))),
        out_shape=jax.ShapeDtypeStruct((bsz, n, w), BF16),
        grid=(bsz,),
        in_specs=[_resident(c_half.shape), _resident(s_half.shape), _resident(perm.shape),
                  seq_spec, seq_spec, _resident(w_four.shape)],
        out_specs=seq_spec,
        compiler_params=_params(1),
        name="fourier_sym",
    )(c_half, s_half, perm, xc, xs, w_four)


def _flip_perm():
    pm = np.zeros((FLIP_BLOCK, FLIP_BLOCK + HALO), np.float32)
    pm[np.arange(FLIP_BLOCK), FLIP_BLOCK - np.arange(FLIP_BLOCK)] = 1.0
    return pm


def _rope_tables(n_tok):
    rows = n_tok // GRID_W
    row = np.repeat(np.arange(rows), GRID_W).astype(np.float64)
    col = np.tile(np.arange(GRID_W), rows).astype(np.float64)
    inv = ROPE_BASE ** (-np.arange(0, ROPE_AXIS, 2, dtype=np.float64) / ROPE_AXIS)
    ang_r = row[:, None] * inv[None, :]
    ang_c = col[:, None] * inv[None, :]
    ang = np.concatenate([ang_r, ang_r, ang_c, ang_c], axis=-1)
    cos = np.concatenate([np.cos(ang)] * 2, axis=-1)
    sin = np.concatenate([np.sin(ang)] * 2, axis=-1)
    first_half = (np.arange(HEAD) % 32) < 16
    sin_signed = np.where(first_half[None, :], -sin, sin)
    return jnp.asarray(cos, F32), jnp.asarray(sin_signed, F32)


def _dft_mats(n):
    idx = np.arange(n, dtype=np.int64)
    ang = 2.0 * np.pi * ((idx[:, None] * idx[None, :]) % n).astype(np.float64) / n
    return np.cos(ang), np.sin(ang)


def kernel(x_prompt, x_sample, cache_k, cache_v, c, c_ctx, w_mod, b_mod, norm_g,
           w_in_even, lam_params, subln_g, conv_w, w_out_even,
           w_in_odd, w_pool, pool_scale, w_fourier, w_out_odd,
           w_gate, w_up, w_down):
    depth = w_mod.shape[0]
    n_dec = x_sample.shape[0]
    n_p, n_s = x_prompt.shape[1], x_sample.shape[1]

    pad_rows = 16 - 1 - n_dec
    cc = jnp.concatenate([c_ctx[None, :], c, jnp.zeros((pad_rows, D_MODEL), F32)], axis=0)
    mod_all = _mod_call(cc, w_mod, b_mod)[:, :1 + n_dec].reshape(depth, 1 + n_dec, 6, D_MODEL)

    rope = _rope_tables(n_s)
    cc_g, sc_g = _dft_mats(GROUP)
    cs_mat = jnp.asarray(np.concatenate([cc_g, sc_g], axis=1), F32).astype(BF16)
    dft_p = tuple(jnp.asarray(m, F32).astype(BF16) for m in _dft_mats(n_p))
    dft_s = tuple(jnp.asarray(m[:n_s // 2 + HALO], F32).astype(BF16) for m in _dft_mats(n_s))
    flip = jnp.asarray(_flip_perm(), F32).astype(BF16)

    late_weights = [w_gate, w_up, w_down, w_out_even, w_in_odd, w_out_odd]
    xp, xs = x_prompt, x_sample
    new_k, new_v = [], []
    for l in range(depth):
        mod = mod_all[l]
        g = norm_g[l]
        i = l // 2
        streams = []
        if l % 2 == 0:
            lam_init = 0.8 - 0.6 * math.exp(-0.3 * l)
            w_in = w_in_even[i].astype(BF16)
            sg = subln_g[i][None, :]
            qp, kp, vp, cbp = _even_in_call(xp, mod, g, w_in, conv_w[i], None, 4 * n_p, 0, 0)
            ap = _attn_prompt_call(qp, kp, vp, lam_params[i], sg, lam_init, 8)
            new_k.append(kp)
            new_v.append(vp)
            qs, ks, vts, cbs = _even_in_call(xs, mod, g, w_in, conv_w[i], rope, 1024, 1, 1)
            a_s, cast = _attn_cache_call(qs, ks, vts, lam_params[i], sg, cache_k, cache_v, i, lam_init,
                                         late_weights if l == 0 else [])
            if l == 0:
                wg, wu, wd, wo_even, wi_odd, wo_odd = cast
            w_out = wo_even[i]
            streams = [(ap, cbp), (a_s, cbs)]
        else:
            w_in = wi_odd[i]
            w_out = wo_odd[i]
            wp = w_pool[i].astype(BF16)
            wf = w_fourier[i].astype(BF16)
            ps = pool_scale[i][None, :]
            pcp, fcp = _odd_in_call(xp, mod, g, w_in, wp, ps, cs_mat, 4 * n_p, 0, 0, (*dft_p, wf))
            pcs, xcs, xss = _odd_in_call(xs, mod, g, w_in, wp, ps, cs_mat, 1024, 1, 1)
            fcs = _four_sym_call(*dft_s, flip, xcs, xss, wf)
            streams = [(pcp, fcp), (pcs, fcs)]
        xp = _post_call(*(t.reshape(1, -1, t.shape[-1]) for t in (xp,) + streams[0]),
                        mod, g, w_out, wg, wu, wd, l, 512, 0, 0).reshape(x_prompt.shape)
        xs = _post_call(xs, streams[1][0], streams[1][1], mod, g, w_out, wg, wu, wd, l, 512, 1, 1)
    def stack_layers(parts):
        if len(parts) == 1:
            return parts[0][:, None]
        return jnp.stack(parts, axis=1)

    return xp, xs, stack_layers(new_k), stack_layers(new_v)
```

```python
import functools
import math

import numpy as np
import jax
import jax.numpy as jnp
from jax import lax
from jax.experimental import pallas as pl
from jax.experimental.pallas import tpu as pltpu

F32 = jnp.float32
BF16 = jnp.bfloat16

D_MODEL = 1024
GRID_W = 64
N_HEADS = 4
HEAD = 128
HALF_HEAD = 64
ROPE_AXIS = 32
ROPE_BASE = 10000.0
ATTN_W = 512
CONV_W = 512
POOL_W = 512
FOURIER_W = 512
GROUP = 128
POOL_WINDOWS = (2, 4, 8, 16)
D_FF = 2816
EPS = 1e-6
LOG2E = math.log2(math.e)
HALO = 8
MXU_N = 256
ATTN_SUB = 128
EVEN_BLOCK = 256
VMEM_LIMIT = 56 * 1024 * 1024


def _params(n_axes):
    return pltpu.CompilerParams(dimension_semantics=("arbitrary",) * n_axes,
                                vmem_limit_bytes=VMEM_LIMIT)


def _resident(shape):
    return pl.BlockSpec(shape, lambda *_: (0,) * len(shape), pipeline_mode=pl.Buffered(1))


def _rms(x, g):
    ms = jnp.mean(x * x, axis=-1, keepdims=True)
    return x * lax.rsqrt(ms + EPS) * g


def _modulate(x, g, shift, scale):
    return _rms(x, g) * (1.0 + scale) + shift


def _dot(a, b):
    return jnp.dot(a, b, preferred_element_type=F32)


def _silu(x):
    return x / (1.0 + jnp.exp(-x))


def _mod_kernel(cc_ref, w_ref, b_ref, o_ref):
    s = _silu(cc_ref[...]).astype(BF16)
    o_ref[...] = _dot(s, w_ref[...].astype(BF16)) + b_ref[...]


def _mod_call(cc, w_mod, b_mod):
    depth, d, n6 = w_mod.shape
    rows = cc.shape[0]
    tn = 2048
    return pl.pallas_call(
        _mod_kernel,
        out_shape=jax.ShapeDtypeStruct((depth, rows, n6), F32),
        grid=(depth, n6 // tn),
        in_specs=[
            pl.BlockSpec((rows, d), lambda l, j: (0, 0)),
            pl.BlockSpec((None, d, tn), lambda l, j: (l, 0, j)),
            pl.BlockSpec((None, 1, tn), lambda l, j: (l, 0, j)),
        ],
        out_specs=pl.BlockSpec((None, rows, tn), lambda l, j: (l, 0, j)),
        compiler_params=_params(2),
        name="mod",
    )(cc, w_mod, b_mod.reshape(depth, 1, n6))


def _tile_specs(n, tm, mod_base, mod_stride):
    nb8 = n // HALO
    t8 = tm // HALO
    x_spec = pl.BlockSpec((None, tm, D_MODEL), lambda b, j: (b, j, 0))
    prev_spec = pl.BlockSpec((None, HALO, D_MODEL),
                             lambda b, j: (b, jnp.maximum(j * t8 - 1, 0), 0))
    next_spec = pl.BlockSpec((None, HALO, D_MODEL),
                             lambda b, j: (b, jnp.minimum((j + 1) * t8, nb8 - 1), 0))
    mod_spec = pl.BlockSpec((None, 6, D_MODEL),
                            lambda b, j: (mod_base + mod_stride * b, 0, 0))
    return x_spec, prev_spec, next_spec, mod_spec


def _rope(t, cos, sin_signed, first_half):
    outs = []
    for hh in range(N_HEADS):
        th = t[:, HEAD * hh:HEAD * (hh + 1)]
        swapped = jnp.where(first_half, pltpu.roll(th, HEAD - 16, 1), pltpu.roll(th, 16, 1))
        outs.append(th * cos + swapped * sin_signed)
    return outs


def _even_in_kernel(*refs, use_rope):
    if use_rope:
        (x_ref, xp_ref, xn_ref, mod_ref, g_ref, w_ref, cw_ref, cos_ref, sin_ref,
         q_ref, k_ref, v_ref, cb_ref, zs_ref) = refs
        j = pl.program_id(1)
        nt = pl.num_programs(1)
        blk = EVEN_BLOCK
        nblk = x_ref.shape[0] // blk
        lane = lax.broadcasted_iota(jnp.int32, (1, HEAD), 1)
        first_half = (lane % 32) < 16
    else:
        x_ref, mod_ref, g_ref, w_ref, cw_ref, q_ref, k_ref, v_ref, cb_ref, zs_ref = refs
        nblk, blk = x_ref.shape[0], x_ref.shape[1]
    g = g_ref[0:1, :]
    shift = mod_ref[0:1, :]
    scale = mod_ref[1:2, :]
    stride = blk if use_rope else blk + HALO

    def z0(i):
        return HALO + i * stride

    def rows(i):
        return slice(i * blk, (i + 1) * blk)

    def proj(hh, lo):
        return _dot(hh, w_ref[:, lo:lo + 512])

    def modulated(i):
        xi = x_ref[rows(i), :] if use_rope else x_ref[i]
        return _modulate(xi, g, shift, scale).astype(BF16)

    def conv_inputs(i, h):
        outer = use_rope and i == 0
        if outer:
            xh = jnp.concatenate([xp_ref[...], xn_ref[...]], axis=0)
            hz = jnp.concatenate([h, _modulate(xh, g, shift, scale).astype(BF16)], axis=0)
        else:
            hz = h
        z = proj(hz, 2048) * proj(hz, 2560)
        if outer:
            zh = z[blk:]
            z = z[0:blk]
            zs_ref[0:HALO, :] = jnp.where(j > 0, zh[0:HALO], 0.0)
            zs_ref[z0(nblk):, :] = jnp.where(j < nt - 1, zh[HALO:], 0.0)
        zs_ref[z0(i):z0(i) + blk, :] = z
        return z, proj(h, 1536)

    def conv_out(i, z, gate_b):
        conv = (cw_ref[0:1, :] * zs_ref[z0(i) - 1:z0(i) - 1 + blk, :] + cw_ref[1:2, :] * z
                + cw_ref[2:3, :] * zs_ref[z0(i) + 1:z0(i) + 1 + blk, :])
        out = (gate_b * conv).astype(BF16)
        if use_rope:
            cb_ref[rows(i), :] = out
        else:
            cb_ref[i] = out

    def qkv(i, h):
        v = proj(h, 1024)
        for hh in range(N_HEADS):
            vh = v[:, HEAD * hh:HEAD * (hh + 1)]
            if use_rope:
                v_ref[hh, 0:HEAD, rows(i)] = vh.T.astype(v_ref.dtype)
                v_ref[hh, HEAD:, rows(i)] = jnp.ones((ONES_ROWS, blk), v_ref.dtype)
            else:
                v_ref[i, hh] = vh.astype(v_ref.dtype)
        for ref, t in ((k_ref, proj(h, 512)), (q_ref, proj(h, 0) * (HALF_HEAD ** -0.5 * LOG2E))):
            if use_rope:
                heads = _rope(t, cos_ref[rows(i), :], sin_ref[rows(i), :], first_half)
            else:
                heads = [t[:, HEAD * hh:HEAD * (hh + 1)] for hh in range(N_HEADS)]
            for hh in range(N_HEADS):
                if use_rope:
                    ref[hh, rows(i), :] = heads[hh].astype(ref.dtype)
                else:
                    ref[i, hh] = heads[hh].astype(ref.dtype)

    if not use_rope:
        for i in range(nblk + 1):
            zs_ref[i * stride:i * stride + HALO, :] = jnp.zeros((HALO, CONV_W), F32)

    h = {0: modulated(0)}
    zg = {0: conv_inputs(0, h[0])}
    for i in range(nblk):
        if i + 1 < nblk:
            h[i + 1] = modulated(i + 1)
        else:
            conv_out(i, *zg.pop(i))
        qkv(i, h.pop(i))
        if i + 1 < nblk:
            zg[i + 1] = conv_inputs(i + 1, h[i + 1])
            conv_out(i, *zg.pop(i))


def _even_in_call(x, mod, g, w_in, conv_w, rope, tm, mod_base, mod_stride):
    bsz, n, d = x.shape
    use_rope = rope is not None
    head_shape = (bsz, N_HEADS, n, HEAD)
    if use_rope:
        x_spec, prev_spec, next_spec, mod_spec = _tile_specs(n, tm, mod_base, mod_stride)
        tab = pl.BlockSpec((tm, HEAD), lambda b, j: (j, 0))
        in_specs = [x_spec, prev_spec, next_spec, mod_spec, _resident(g.shape), _resident(w_in.shape),
                    _resident(conv_w.shape), tab, tab]
        args = [x, x, x, mod, g, w_in, conv_w, *rope]
        head_spec = pl.BlockSpec((None, N_HEADS, tm, HEAD), lambda b, j: (b, 0, j, 0))
        v_spec = pl.BlockSpec((None, N_HEADS, HEAD + ONES_ROWS, tm), lambda b, j: (b, 0, 0, j))
        cb_spec = pl.BlockSpec((None, tm, CONV_W), lambda b, j: (b, j, 0))
        k_sds = jax.ShapeDtypeStruct(head_shape, BF16)
        v_sds = jax.ShapeDtypeStruct((bsz, N_HEADS, HEAD + ONES_ROWS, n), BF16)
        grid = (bsz, n // tm)
        zs_rows = tm + 2 * HALO
    else:
        ns = tm // n
        in_specs = [pl.BlockSpec((ns, n, d), lambda b: (b, 0, 0)),
                    pl.BlockSpec((None, 6, d), lambda b: (mod_base, 0, 0)),
                    _resident(g.shape), _resident(w_in.shape), _resident(conv_w.shape)]
        args = [x, mod, g, w_in, conv_w]
        head_spec = v_spec = pl.BlockSpec((ns, N_HEADS, n, HEAD), lambda b: (b, 0, 0, 0))
        cb_spec = pl.BlockSpec((ns, n, CONV_W), lambda b: (b, 0, 0))
        k_sds = v_sds = jax.ShapeDtypeStruct(head_shape, F32)
        grid = (bsz // ns,)
        zs_rows = ns * (n + HALO) + HALO
    return pl.pallas_call(
        functools.partial(_even_in_kernel, use_rope=use_rope),
        out_shape=(jax.ShapeDtypeStruct(head_shape, BF16), k_sds, v_sds,
                   jax.ShapeDtypeStruct((bsz, n, CONV_W), BF16)),
        grid=grid,
        in_specs=in_specs,
        out_specs=(head_spec, head_spec, v_spec, cb_spec),
        scratch_shapes=[pltpu.VMEM((zs_rows, CONV_W), F32)],
        compiler_params=_params(len(grid)),
        name="even_in_rope" if use_rope else "even_in",
    )(*args)


def _diff_lambda(lam_ref, lam_init):
    lp = lam_ref[...]
    return (jnp.exp(jnp.sum(lp[0:1] * lp[1:2], axis=-1, keepdims=True))
            - jnp.exp(jnp.sum(lp[2:3] * lp[3:4], axis=-1, keepdims=True)) + lam_init)


def _stack_components(q):
    lane = lax.broadcasted_iota(jnp.int32, (1, HEAD), 1)
    zero = jnp.zeros_like(q)
    return jnp.concatenate([jnp.where(lane < HALF_HEAD, q, zero),
                            jnp.where(lane >= HALF_HEAD, q, zero)], axis=0)


def _softmax_pv(s, v_ext):
    e = jnp.exp2(s - jnp.max(s, axis=-1, keepdims=True)).astype(BF16)
    return _dot(e, v_ext)


def _normalise(ov, lam, sg, lam_init):
    t = ov.shape[0] // 2
    o = ov[:t, :HEAD] / ov[:t, HEAD:] - lam * (ov[t:, :HEAD] / ov[t:, HEAD:])
    return (_rms(o, sg) * (1.0 - lam_init)).astype(BF16)


def _chain_pipeline(n_groups, step_fn, finish_fn):
    assert n_groups % 2 == 0 and n_groups >= 2
    step_fn(0, 0, None, None)
    step_fn(1, 1, 0, 0)

    def body(t, carry):
        g = 2 * t
        step_fn(g, 0, g - 1, 1)
        finish_fn(g - 2, 0)
        step_fn(g + 1, 1, g, 0)
        finish_fn(g - 1, 1)
        return carry

    lax.fori_loop(1, n_groups // 2, body, 0)
    step_fn(None, None, n_groups - 1, 1)
    finish_fn(n_groups - 2, 0)
    finish_fn(n_groups - 1, 1)


def _attn_prompt_kernel(lam_ref, sg_ref, q_ref, k_ref, v_ref, o_ref, s_ref, ov_ref, *, lam_init):
    lam = _diff_lambda(lam_ref, lam_init)
    sg = sg_ref[...]
    n = k_ref.shape[2]
    ones = jnp.ones((n, MXU_N - HEAD), BF16)

    def step_fn(bs, ps, bv, pv):
        for hh in range(N_HEADS):
            if bs is not None:
                s_ref[ps * N_HEADS + hh] = lax.dot_general(
                    _stack_components(q_ref[bs, hh]), k_ref[bs, hh].astype(BF16),
                    (((1,), (1,)), ((), ())), preferred_element_type=F32)
        for hh in range(N_HEADS):
            if bv is not None:
                v_ext = jnp.concatenate([v_ref[bv, hh].astype(BF16), ones], axis=1)
                ov_ref[pv * N_HEADS + hh] = _softmax_pv(s_ref[pv * N_HEADS + hh], v_ext)

    def finish_fn(b, par):
        for hh in range(N_HEADS):
            o_ref[b, :, HEAD * hh:HEAD * (hh + 1)] = _normalise(ov_ref[par * N_HEADS + hh], lam, sg, lam_init)

    _chain_pipeline(q_ref.shape[0], step_fn, finish_fn)


def _attn_prompt_call(q, k, v, lam_params, subln_g, lam_init, nb):
    bsz, nh, n, hd = q.shape
    spec = pl.BlockSpec((nb, nh, n, hd), lambda b: (b, 0, 0, 0))
    return pl.pallas_call(
        functools.partial(_attn_prompt_kernel, lam_init=lam_init),
        out_shape=jax.ShapeDtypeStruct((bsz, n, nh * hd), BF16),
        grid=(bsz // nb,),
        in_specs=[_resident(lam_params.shape), _resident(subln_g.shape), spec, spec, spec],
        out_specs=pl.BlockSpec((nb, n, nh * hd), lambda b: (b, 0, 0)),
        scratch_shapes=[pltpu.VMEM((2 * nh, 2 * n, n), F32), pltpu.VMEM((2 * nh, 2 * n, MXU_N), F32)],
        compiler_params=_params(1),
        name="attn",
    )(lam_params, subln_g, q, k, v)


POST_BLOCK = 256
ONES_ROWS = 16
ATTN_WIDTH = 2
KEY_CHUNK = 256


def _attn_cache_kernel(*refs, lam_init, n_cast):
    lam_ref, sg_ref, q_ref, k_ref, vt_ref, ck_ref, cv_ref = refs[:7]
    cast_in = refs[7:7 + n_cast]
    o_ref = refs[7 + n_cast]
    cast_out = refs[8 + n_cast:8 + 2 * n_cast]
    kc_ref, vtc_ref, s_ref, ov_ref, m_ref, qt_ref = refs[8 + 2 * n_cast:]
    for src, dst in zip(cast_in, cast_out):
        dst[...] = src[...].astype(BF16)
    past = ck_ref.shape[0]
    assert past == KEY_CHUNK
    lk = past + k_ref.shape[0]
    kc_ref[...] = ck_ref[...].astype(BF16)
    vtc_ref[0:HEAD, :] = cv_ref[...].T.astype(BF16)
    vtc_ref[HEAD:, :] = jnp.ones((ONES_ROWS, past), BF16)

    def key_rows(c0):
        return kc_ref[...] if c0 == 0 else k_ref[c0 - past:c0 - past + KEY_CHUNK, :]

    def value_cols(c0):
        return vtc_ref[...] if c0 == 0 else vt_ref[:, c0 - past:c0 - past + KEY_CHUNK]

    lam = _diff_lambda(lam_ref, lam_init)
    sg = sg_ref[...] * (1.0 - lam_init)

    def rows(g, u):
        start = (g * ATTN_WIDTH + u) * ATTN_SUB
        return pl.ds(pl.multiple_of(start, ATTN_SUB), ATTN_SUB)

    def step_fn(gs, ps, gv, pv):
        chains = range(ATTN_WIDTH)
        if gs is not None:
            for u in chains:
                qt_ref[u] = _stack_components(q_ref[rows(gs, u), :]).T
            qq = [qt_ref[u] for u in chains]
            col_max = [None] * ATTN_WIDTH
        if gv is not None:
            m_prev = [m_ref[pv * ATTN_WIDTH + u] for u in chains]
            acc = [None] * ATTN_WIDTH
        for c0 in range(0, lk, KEY_CHUNK):
            keys = slice(c0, c0 + KEY_CHUNK)
            for u in chains:
                if gs is not None:
                    s = _dot(key_rows(c0), qq[u])
                    s_ref[ps * ATTN_WIDTH + u, keys, :] = s
                    cm = jnp.max(s, axis=0, keepdims=True)
                    col_max[u] = cm if col_max[u] is None else jnp.maximum(col_max[u], cm)
            for u in chains:
                if gv is not None:
                    e = jnp.exp2(s_ref[pv * ATTN_WIDTH + u, keys, :] - m_prev[u]).astype(BF16)
                    part = _dot(value_cols(c0), e)
                    acc[u] = part if acc[u] is None else acc[u] + part
        for u in chains:
            if gs is not None:
                m_ref[ps * ATTN_WIDTH + u] = col_max[u]
            if gv is not None:
                ov_ref[pv * ATTN_WIDTH + u] = acc[u]

    def finish_fn(g, par):
        for u in range(ATTN_WIDTH):
            ov = ov_ref[par * ATTN_WIDTH + u]
            o_t = (ov[0:HEAD, 0:ATTN_SUB] / ov[HEAD:HEAD + 1, 0:ATTN_SUB]
                   - lam * (ov[0:HEAD, ATTN_SUB:] / ov[HEAD:HEAD + 1, ATTN_SUB:]))
            ms = jnp.mean(o_t * o_t, axis=0, keepdims=True)
            o_ref[rows(g, u), :] = ((o_t * lax.rsqrt(ms + EPS)).T * sg).astype(BF16)

    _chain_pipeline(q_ref.shape[0] // (ATTN_SUB * ATTN_WIDTH), step_fn, finish_fn)


def _attn_cache_call(q, k, vt, lam_params, subln_g, cache_k, cache_v, layer, lam_init, to_cast):
    bsz, nh, n, hd = q.shape
    past = cache_k.shape[3]
    seq_spec = pl.BlockSpec((None, None, n, hd), lambda b, h: (b, h, 0, 0))
    vt_spec = pl.BlockSpec((None, None, hd + ONES_ROWS, n), lambda b, h: (b, h, 0, 0))
    c_spec = pl.BlockSpec((None, None, None, past, hd), lambda b, h: (b, layer, h, 0, 0))
    flat = [w.reshape(-1, w.shape[-1]) for w in to_cast]
    slab_specs = [pl.BlockSpec((w.shape[0] // (bsz * nh), w.shape[1]), lambda b, h: (b * nh + h, 0))
                  for w in flat]
    outs = pl.pallas_call(
        functools.partial(_attn_cache_kernel, lam_init=lam_init, n_cast=len(flat)),
        out_shape=[jax.ShapeDtypeStruct((bsz, n, nh * hd), BF16)]
        + [jax.ShapeDtypeStruct(w.shape, BF16) for w in flat],
        grid=(bsz, nh),
        in_specs=[_resident(lam_params.shape), _resident(subln_g.shape),
                  seq_spec, seq_spec, vt_spec, c_spec, c_spec] + slab_specs,
        out_specs=[pl.BlockSpec((None, n, hd), lambda b, h: (b, 0, h))] + slab_specs,
        scratch_shapes=[pltpu.VMEM((past, hd), BF16), pltpu.VMEM((hd + ONES_ROWS, past), BF16),
                        pltpu.VMEM((2 * ATTN_WIDTH, past + n, 2 * ATTN_SUB), F32),
                        pltpu.VMEM((2 * ATTN_WIDTH, hd + ONES_ROWS, 2 * ATTN_SUB), F32),
                        pltpu.VMEM((2 * ATTN_WIDTH, 1, 2 * ATTN_SUB), F32),
                        pltpu.VMEM((ATTN_WIDTH, hd, 2 * ATTN_SUB), BF16)],
        compiler_params=_params(2),
        name="attn_cache",
    )(lam_params, subln_g, q, k, vt, cache_k, cache_v, *flat)
    return outs[0], [o.reshape(w.shape) for o, w in zip(outs[1:], to_cast)]


def _post_kernel(x_ref, a_ref, b_ref, mod_ref, g_ref, wo_ref, wg_ref, wu_ref, wd_ref, o_ref):
    nb = x_ref.shape[0] // POST_BLOCK
    blocks = [slice(i * POST_BLOCK, (i + 1) * POST_BLOCK) for i in range(nb)]

    def out_proj(rows):
        return _dot(jnp.concatenate([a_ref[rows, :], b_ref[rows, :]], axis=1), wo_ref[...])

    def norms(rows, y):
        x1 = x_ref[rows, :] + mod_ref[2:3, :] * _rms(y, g_ref[1:2, :])
        return x1, _modulate(x1, g_ref[2:3, :], mod_ref[3:4, :], mod_ref[4:5, :]).astype(BF16)

    def gate_up(h):
        return _dot(h, wg_ref[...]), _dot(h, wu_ref[...])

    def down(gu):
        return _dot((_silu(gu[0]) * gu[1]).astype(BF16), wd_ref[...])

    def finish(rows, x1, f):
        o_ref[rows, :] = x1 + mod_ref[5:6, :] * _rms(f, g_ref[3:4, :])

    y = {0: out_proj(blocks[0])}
    x1, gu = {}, {}
    for i in range(nb + 1):
        if i + 1 < nb:
            y[i + 1] = out_proj(blocks[i + 1])
        if i < nb:
            x1[i], h = norms(blocks[i], y.pop(i))
        if i >= 1:
            f = down(gu.pop(i - 1))
        if i < nb:
            gu[i] = gate_up(h)
        if i >= 1:
            finish(blocks[i - 1], x1.pop(i - 1), f)


def _post_call(x, a, b, mod, g, w_out, w_gate, w_up, w_down, layer, tm, mod_base, mod_stride):
    bsz, n, d = x.shape
    x_spec, _, _, mod_spec = _tile_specs(n, tm, mod_base, mod_stride)
    half_spec = pl.BlockSpec((None, tm, a.shape[2]), lambda b_, j: (b_, j, 0))

    def layer_resident(w):
        return pl.BlockSpec((None,) + w.shape[1:], lambda *_: (layer, 0, 0), pipeline_mode=pl.Buffered(1))

    return pl.pallas_call(
        _post_kernel,
        out_shape=jax.ShapeDtypeStruct(x.shape, F32),
        grid=(bsz, n // tm),
        in_specs=[x_spec, half_spec, half_spec, mod_spec, _resident(g.shape),
                  _resident(w_out.shape), layer_resident(w_gate), layer_resident(w_up),
                  layer_resident(w_down)],
        out_specs=x_spec,
        compiler_params=_params(2),
        name="post",
    )(x, a, b, mod, g, w_out, w_gate, w_up, w_down)


def _odd_in_kernel(*refs, n_seq, whole_seqs):
    if whole_seqs:
        (x_ref, mod_ref, g_ref, w_ref, wp_ref, ps_ref, cs_ref, dc_ref, ds_ref, wf_ref,
         pc_ref, fc_ref, us_ref, f2_ref, f4_ref) = refs
        nblk, blk = x_ref.shape[0], x_ref.shape[1]
    else:
        (x_ref, xp_ref, xn_ref, mod_ref, g_ref, w_ref, wp_ref, ps_ref, cs_ref,
         pc_ref, xc_ref, xs_ref, us_ref, f2_ref, f4_ref) = refs
        j = pl.program_id(1)
        nt = pl.num_programs(1)
        nblk, blk = 1, x_ref.shape[0]
    tm = nblk * blk
    g = g_ref[0:1, :]
    shift = mod_ref[0:1, :]
    scale = mod_ref[1:2, :]
    if whole_seqs:
        h = _modulate(x_ref[...].reshape(tm, x_ref.shape[2]), g, shift, scale).astype(BF16)
        up = _dot(h, w_ref[:, 0:POOL_W])
    else:
        xh = jnp.concatenate([x_ref[...], xp_ref[...], xn_ref[...]], axis=0)
        hz = _modulate(xh, g, shift, scale).astype(BF16)
        h = hz[0:tm]
        upz = _dot(hz, w_ref[:, 0:POOL_W])
        up, uph = upz[0:tm], upz[tm:]
    uf = _dot(h, w_ref[:, POOL_W:]).astype(BF16)

    def store(ref, lanes, val):
        if whole_seqs:
            for i in range(nblk):
                ref[i, :, lanes] = val[i * blk:(i + 1) * blk]
        else:
            ref[:, lanes] = val

    xc_groups, xs_groups = [], []
    for gi in range(FOURIER_W // GROUP):
        lanes = slice(GROUP * gi, GROUP * (gi + 1))
        cs = _dot(uf[:, lanes], cs_ref[...])
        xc_groups.append(cs[:, 0:GROUP].astype(BF16))
        xs_groups.append(cs[:, GROUP:].astype(BF16))
        if not whole_seqs:
            xc_ref[:, lanes] = xc_groups[gi]
            xs_ref[:, lanes] = xs_groups[gi]
    if whole_seqs:
        xc = jnp.concatenate(xc_groups, axis=1)
        xs = jnp.concatenate(xs_groups, axis=1)
        ortho = float(1.0 / math.sqrt(blk * GROUP))
        for i in range(nblk):
            seq = slice(i * blk, (i + 1) * blk)
            y = _dot(dc_ref[...], xc[seq]) - _dot(ds_ref[...], xs[seq])
            four = (y * ortho).astype(BF16)
            for gi in range(FOURIER_W // GROUP):
                lanes = slice(GROUP * gi, GROUP * (gi + 1))
                fc_ref[i, :, lanes] = _dot(four[:, lanes], wf_ref[gi]).astype(BF16)

    stride = blk + HALO

    def u0(i):
        return HALO + i * stride

    rows = nblk * stride + HALO
    zeros = jnp.zeros((HALO, POOL_W), F32)
    for i in range(nblk):
        us_ref[u0(i):u0(i) + blk, :] = up[i * blk:(i + 1) * blk]
        if whole_seqs:
            us_ref[u0(i) - HALO:u0(i), :] = zeros
    if whole_seqs:
        us_ref[rows - HALO:rows, :] = zeros
    else:
        us_ref[0:HALO, :] = jnp.where(j > 0, uph[0:HALO], 0.0)
        us_ref[rows - HALO:rows, :] = jnp.where(j < nt - 1, uph[HALO:], 0.0)
    us_ref[rows:, :] = zeros
    f2_ref[0:rows, :] = us_ref[0:rows, GROUP:] + us_ref[1:rows + 1, GROUP:]
    f2_ref[rows:, :] = jnp.zeros((HALO, POOL_W - GROUP), F32)
    f4_ref[0:rows, :] = f2_ref[0:rows, GROUP:] + f2_ref[2:rows + 2, GROUP:]
    f4_ref[rows:, :] = jnp.zeros((HALO, POOL_W - 2 * GROUP), F32)
    f8 = f4_ref[0:rows, GROUP:] + f4_ref[4:rows + 4, GROUP:]

    def centred(i):
        a = u0(i)
        return (us_ref[a - 1:a - 1 + blk, 0:GROUP] + us_ref[a:a + blk, 0:GROUP],
                f2_ref[a - 2:a - 2 + blk, 0:GROUP] + f2_ref[a:a + blk, 0:GROUP],
                f4_ref[a - 4:a - 4 + blk, 0:GROUP] + f4_ref[a:a + blk, 0:GROUP],
                f8[a - HALO:a - HALO + blk] + f8[a:a + blk])

    sums = [centred(i) for i in range(nblk)]
    pos = lax.broadcasted_iota(jnp.int32, (blk, 1), 0)
    t = (pos if whole_seqs else j * blk + pos).astype(F32)
    for gi, win in enumerate(POOL_WINDOWS):
        lanes = slice(GROUP * gi, GROUP * (gi + 1))
        cnt = jnp.minimum(t + float(win // 2), float(n_seq)) - jnp.maximum(t - float(win // 2), 0.0)
        pooled = jnp.concatenate([s[gi] / cnt for s in sums], axis=0)
        diff = (pooled - up[:, lanes]).astype(BF16)
        store(pc_ref, lanes, (_dot(diff, wp_ref[gi]) * ps_ref[0:1, lanes]).astype(BF16))


def _odd_in_call(x, mod, g, w_in, w_pool, pool_scale, cs_mat, tm, mod_base, mod_stride, seq_dft=None):
    bsz, n, d = x.shape
    whole_seqs = tm >= n
    weights = [_resident(g.shape), _resident(w_in.shape), _resident(w_pool.shape),
               _resident(pool_scale.shape), _resident(cs_mat.shape)]
    if whole_seqs:
        ns = tm // n
        in_specs = ([pl.BlockSpec((ns, n, d), lambda b: (b, 0, 0)),
                     pl.BlockSpec((None, 6, d), lambda b: (mod_base, 0, 0))] + weights
                    + [_resident(t.shape) for t in seq_dft])
        args = [x, mod, g, w_in, w_pool, pool_scale, cs_mat, *seq_dft]
        out_spec = pl.BlockSpec((ns, n, POOL_W), lambda b: (b, 0, 0))
        n_out = 2
        grid = (bsz // ns,)
        scratch_rows = ns * (n + HALO) + 2 * HALO
    else:
        x_spec, prev_spec, next_spec, mod_spec = _tile_specs(n, tm, mod_base, mod_stride)
        in_specs = [x_spec, prev_spec, next_spec, mod_spec] + weights
        args = [x, x, x, mod, g, w_in, w_pool, pool_scale, cs_mat]
        out_spec = pl.BlockSpec((None, tm, POOL_W), lambda b, j: (b, j, 0))
        n_out = 3
        grid = (bsz, n // tm)
        scratch_rows = tm + 3 * HALO
    out_sds = jax.ShapeDtypeStruct((bsz, n, POOL_W), BF16)
    return pl.pallas_call(
        functools.partial(_odd_in_kernel, n_seq=n, whole_seqs=whole_seqs),
        out_shape=(out_sds,) * n_out,
        grid=grid,
        in_specs=in_specs,
        out_specs=(out_spec,) * n_out,
        scratch_shapes=[pltpu.VMEM((scratch_rows, POOL_W), F32),
                        pltpu.VMEM((scratch_rows, POOL_W - GROUP), F32),
                        pltpu.VMEM((scratch_rows, POOL_W - 2 * GROUP), F32)],
        compiler_params=_params(len(grid)),
        name="odd_in",
    )(*args)


FLIP_BLOCK = 256


def _four_sym_kernel(c_ref, s_ref, pm_ref, pm0_ref, xc_ref, xs_ref, wf_ref, o_ref, *, scale):
    n = o_ref.shape[0]
    half = n // 2

    def folded(x_ref, sign):
        parts = []
        for b in range(half // FLIP_BLOCK):
            lo = n - FLIP_BLOCK * (b + 1)
            if b == 0:
                partner = _dot(pm0_ref[...], x_ref[lo:lo + FLIP_BLOCK, :])
            else:
                partner = _dot(pm_ref[...], x_ref[lo:lo + FLIP_BLOCK + HALO, :])
            direct = x_ref[FLIP_BLOCK * b:FLIP_BLOCK * (b + 1), :].astype(F32)
            parts.append((direct + sign * partner).astype(BF16))
        return jnp.concatenate(parts, axis=0)

    j = lax.broadcasted_iota(jnp.int32, (c_ref.shape[0], 1), 0)
    nyquist = (1 - 2 * (j & 1)).astype(F32) * xc_ref[half:half + 1, :].astype(F32)
    p = _dot(c_ref[...], folded(xc_ref, 1.0)) + nyquist
    q = _dot(s_ref[...], folded(xs_ref, -1.0))

    def project(rows, four):
        for gi in range(FOURIER_W // GROUP):
            lanes = slice(GROUP * gi, GROUP * (gi + 1))
            o_ref[rows, lanes] = _dot(four[:, lanes], wf_ref[gi]).astype(BF16)

    project(slice(0, half), ((p[0:half] - q[0:half]) * scale).astype(BF16))
    mirrored = ((p + q) * scale).astype(BF16)
    for b in range(half // FLIP_BLOCK):
        lo = half - FLIP_BLOCK * (b + 1)
        window = mirrored[lo:lo + FLIP_BLOCK + HALO, :]
        flipped = _dot(pm_ref[...], window).astype(BF16)
        project(slice(half + FLIP_BLOCK * b, half + FLIP_BLOCK * (b + 1)), flipped)


def _four_sym_call(c_quarter, s_quarter, perm, perm0, xc, xs, w_four):
    bsz, n, w = xc.shape
    seq_spec = pl.BlockSpec((None, n, w), lambda b: (b, 0, 0))
    return pl.pallas_call(
        functools.partial(_four_sym_kernel, scale=float(1.0 / math.sqrt(n * GROUP))),
        out_shape=jax.ShapeDtypeStruct((bsz, n, w), BF16),
        grid=(bsz,),
        in_specs=[_resident(c_quarter.shape), _resident(s_quarter.shape), _resident(perm.shape),
                  _resident(perm0.shape), seq_spec, seq_spec, _resident(w_four.shape)],
        out_specs=seq_spec,
        compiler_params=_params(1),
        name="fourier_sym",
    )(c_quarter, s_quarter, perm, perm0, xc, xs, w_four)


def _flip_perms():
    i = np.arange(FLIP_BLOCK)
    pm = np.zeros((FLIP_BLOCK, FLIP_BLOCK + HALO), np.float32)
    pm[i, FLIP_BLOCK - i] = 1.0
    pm0 = np.zeros((FLIP_BLOCK, FLIP_BLOCK), np.float32)
    pm0[i[1:], FLIP_BLOCK - i[1:]] = 1.0
    return pm, pm0


def _rope_tables(n_tok):
    rows = n_tok // GRID_W
    row = np.repeat(np.arange(rows), GRID_W).astype(np.float64)
    col = np.tile(np.arange(GRID_W), rows).astype(np.float64)
    inv = ROPE_BASE ** (-np.arange(0, ROPE_AXIS, 2, dtype=np.float64) / ROPE_AXIS)
    ang_r = row[:, None] * inv[None, :]
    ang_c = col[:, None] * inv[None, :]
    ang = np.concatenate([ang_r, ang_r, ang_c, ang_c], axis=-1)
    cos = np.concatenate([np.cos(ang)] * 2, axis=-1)
    sin = np.concatenate([np.sin(ang)] * 2, axis=-1)
    first_half = (np.arange(HEAD) % 32) < 16
    sin_signed = np.where(first_half[None, :], -sin, sin)
    return jnp.asarray(cos, F32), jnp.asarray(sin_signed, F32)


def _dft_mats(n):
    idx = np.arange(n, dtype=np.int64)
    ang = 2.0 * np.pi * ((idx[:, None] * idx[None, :]) % n).astype(np.float64) / n
    return np.cos(ang), np.sin(ang)


def kernel(x_prompt, x_sample, cache_k, cache_v, c, c_ctx, w_mod, b_mod, norm_g,
           w_in_even, lam_params, subln_g, conv_w, w_out_even,
           w_in_odd, w_pool, pool_scale, w_fourier, w_out_odd,
           w_gate, w_up, w_down):
    depth = w_mod.shape[0]
    n_dec = x_sample.shape[0]
    n_p, n_s = x_prompt.shape[1], x_sample.shape[1]

    pad_rows = 16 - 1 - n_dec
    cc = jnp.concatenate([c_ctx[None, :], c, jnp.zeros((pad_rows, D_MODEL), F32)], axis=0)
    mod_all = _mod_call(cc, w_mod, b_mod)[:, :1 + n_dec].reshape(depth, 1 + n_dec, 6, D_MODEL)

    rope = _rope_tables(n_s)
    cc_g, sc_g = _dft_mats(GROUP)
    cs_mat = jnp.asarray(np.concatenate([cc_g, sc_g], axis=1), F32).astype(BF16)
    dft_p = tuple(jnp.asarray(m, F32).astype(BF16) for m in _dft_mats(n_p))
    dft_s = tuple(jnp.asarray(m[:n_s // 2 + HALO, :n_s // 2], F32).astype(BF16) for m in _dft_mats(n_s))
    flips = tuple(jnp.asarray(m, F32).astype(BF16) for m in _flip_perms())

    late_weights = [w_gate, w_up, w_down, w_out_even, w_in_odd, w_out_odd]
    xp, xs = x_prompt, x_sample
    new_k, new_v = [], []
    for l in range(depth):
        mod = mod_all[l]
        g = norm_g[l]
        i = l // 2
        streams = []
        if l % 2 == 0:
            lam_init = 0.8 - 0.6 * math.exp(-0.3 * l)
            w_in = w_in_even[i].astype(BF16)
            sg = subln_g[i][None, :]
            qp, kp, vp, cbp = _even_in_call(xp, mod, g, w_in, conv_w[i], None, 4 * n_p, 0, 0)
            ap = _attn_prompt_call(qp, kp, vp, lam_params[i], sg, lam_init, 8)
            new_k.append(kp)
            new_v.append(vp)
            qs, ks, vts, cbs = _even_in_call(xs, mod, g, w_in, conv_w[i], rope, 1024, 1, 1)
            a_s, cast = _attn_cache_call(qs, ks, vts, lam_params[i], sg, cache_k, cache_v, i, lam_init,
                                         late_weights if l == 0 else [])
            if l == 0:
                wg, wu, wd, wo_even, wi_odd, wo_odd = cast
            w_out = wo_even[i]
            streams = [(ap, cbp), (a_s, cbs)]
        else:
            w_in = wi_odd[i]
            w_out = wo_odd[i]
            wp = w_pool[i].astype(BF16)
            wf = w_fourier[i].astype(BF16)
            ps = pool_scale[i][None, :]
            pcp, fcp = _odd_in_call(xp, mod, g, w_in, wp, ps, cs_mat, 4 * n_p, 0, 0, (*dft_p, wf))
            pcs, xcs, xss = _odd_in_call(xs, mod, g, w_in, wp, ps, cs_mat, 1024, 1, 1)
            fcs = _four_sym_call(*dft_s, *flips, xcs, xss, wf)
            streams = [(pcp, fcp), (pcs, fcs)]
        xp = _post_call(*(t.reshape(1, -1, t.shape[-1]) for t in (xp,) + streams[0]),
                        mod, g, w_out, wg, wu, wd, l, 512, 0, 0).reshape(x_prompt.shape)
        xs = _post_call(xs, streams[1][0], streams[1][1], mod, g, w_out, wg, wu, wd, l, 512, 1, 1)
    def stack_layers(parts):
        if len(parts) == 1:
            return parts[0][:, None]
        return jnp.stack(parts, axis=1)

    return xp, xs, stack_layers(new_k), stack_layers(new_v)
```

```python
import functools
import math

import numpy as np
import jax
import jax.numpy as jnp
from jax import lax
from jax.experimental import pallas as pl
from jax.experimental.pallas import tpu as pltpu

F32 = jnp.float32
BF16 = jnp.bfloat16

D_MODEL = 1024
GRID_W = 64
N_HEADS = 4
HEAD = 128
HALF_HEAD = 64
ROPE_AXIS = 32
ROPE_BASE = 10000.0
ATTN_W = 512
CONV_W = 512
POOL_W = 512
FOURIER_W = 512
GROUP = 128
POOL_WINDOWS = (2, 4, 8, 16)
D_FF = 2816
EPS = 1e-6
LOG2E = math.log2(math.e)
HALO = 8
MXU_N = 256
ATTN_SUB = 128
EVEN_BLOCK = 256
VMEM_LIMIT = 56 * 1024 * 1024


def _params(n_axes):
    return pltpu.CompilerParams(dimension_semantics=("arbitrary",) * n_axes,
                                vmem_limit_bytes=VMEM_LIMIT)


def _resident(shape):
    return pl.BlockSpec(shape, lambda *_: (0,) * len(shape), pipeline_mode=pl.Buffered(1))


def _rms(x, g):
    ms = jnp.mean(x * x, axis=-1, keepdims=True)
    return x * lax.rsqrt(ms + EPS) * g


def _modulate(x, g, shift, scale):
    return _rms(x, g) * (1.0 + scale) + shift


def _dot(a, b):
    return jnp.dot(a, b, preferred_element_type=F32)


def _silu(x):
    return x / (1.0 + jnp.exp(-x))


def _mod_kernel(cc_ref, w_ref, b_ref, o_ref):
    s = _silu(cc_ref[...]).astype(BF16)
    o_ref[...] = _dot(s, w_ref[...].astype(BF16)) + b_ref[...]


def _mod_call(cc, w_mod, b_mod):
    depth, d, n6 = w_mod.shape
    rows = cc.shape[0]
    tn = 2048
    return pl.pallas_call(
        _mod_kernel,
        out_shape=jax.ShapeDtypeStruct((depth, rows, n6), F32),
        grid=(depth, n6 // tn),
        in_specs=[
            pl.BlockSpec((rows, d), lambda l, j: (0, 0)),
            pl.BlockSpec((None, d, tn), lambda l, j: (l, 0, j)),
            pl.BlockSpec((None, 1, tn), lambda l, j: (l, 0, j)),
        ],
        out_specs=pl.BlockSpec((None, rows, tn), lambda l, j: (l, 0, j)),
        compiler_params=_params(2),
        name="mod",
    )(cc, w_mod, b_mod.reshape(depth, 1, n6))


def _tile_specs(n, tm, mod_base, mod_stride):
    nb8 = n // HALO
    t8 = tm // HALO
    x_spec = pl.BlockSpec((None, tm, D_MODEL), lambda b, j: (b, j, 0))
    prev_spec = pl.BlockSpec((None, HALO, D_MODEL),
                             lambda b, j: (b, jnp.maximum(j * t8 - 1, 0), 0))
    next_spec = pl.BlockSpec((None, HALO, D_MODEL),
                             lambda b, j: (b, jnp.minimum((j + 1) * t8, nb8 - 1), 0))
    mod_spec = pl.BlockSpec((None, 6, D_MODEL),
                            lambda b, j: (mod_base + mod_stride * b, 0, 0))
    return x_spec, prev_spec, next_spec, mod_spec


def _rope(t, cos, sin_signed, first_half):
    outs = []
    for hh in range(N_HEADS):
        th = t[:, HEAD * hh:HEAD * (hh + 1)]
        swapped = jnp.where(first_half, pltpu.roll(th, HEAD - 16, 1), pltpu.roll(th, 16, 1))
        outs.append(th * cos + swapped * sin_signed)
    return outs


def _even_in_kernel(*refs, use_rope):
    if use_rope:
        (x_ref, xp_ref, xn_ref, mod_ref, g_ref, w_ref, cw_ref, cos_ref, sin_ref,
         q_ref, k_ref, v_ref, cb_ref, zs_ref) = refs
        j = pl.program_id(1)
        nt = pl.num_programs(1)
        blk = EVEN_BLOCK
        nblk = x_ref.shape[0] // blk
        lane = lax.broadcasted_iota(jnp.int32, (1, HEAD), 1)
        first_half = (lane % 32) < 16
    else:
        x_ref, mod_ref, g_ref, w_ref, cw_ref, q_ref, k_ref, v_ref, cb_ref, zs_ref = refs
        nblk, blk = x_ref.shape[0], x_ref.shape[1]
    g = g_ref[0:1, :]
    shift = mod_ref[0:1, :]
    scale = mod_ref[1:2, :]
    stride = blk if use_rope else blk + HALO

    def z0(i):
        return HALO + i * stride

    def rows(i):
        return slice(i * blk, (i + 1) * blk)

    def proj(hh, lo):
        return _dot(hh, w_ref[:, lo:lo + 512])

    def modulated(i):
        xi = x_ref[rows(i), :] if use_rope else x_ref[i]
        return _modulate(xi, g, shift, scale).astype(BF16)

    def conv_inputs(i, h):
        outer = use_rope and i == 0
        if outer:
            xh = jnp.concatenate([xp_ref[...], xn_ref[...]], axis=0)
            hz = jnp.concatenate([h, _modulate(xh, g, shift, scale).astype(BF16)], axis=0)
        else:
            hz = h
        z = proj(hz, 2048) * proj(hz, 2560)
        if outer:
            zh = z[blk:]
            z = z[0:blk]
            zs_ref[0:HALO, :] = jnp.where(j > 0, zh[0:HALO], 0.0)
            zs_ref[z0(nblk):, :] = jnp.where(j < nt - 1, zh[HALO:], 0.0)
        zs_ref[z0(i):z0(i) + blk, :] = z
        return z, proj(h, 1536)

    def conv_out(i, z, gate_b):
        conv = (cw_ref[0:1, :] * zs_ref[z0(i) - 1:z0(i) - 1 + blk, :] + cw_ref[1:2, :] * z
                + cw_ref[2:3, :] * zs_ref[z0(i) + 1:z0(i) + 1 + blk, :])
        out = (gate_b * conv).astype(BF16)
        if use_rope:
            cb_ref[rows(i), :] = out
        else:
            cb_ref[i] = out

    def qkv(i, h):
        v = proj(h, 1024)
        for hh in range(N_HEADS):
            vh = v[:, HEAD * hh:HEAD * (hh + 1)]
            if use_rope:
                v_ref[hh, :, rows(i)] = vh.T.astype(v_ref.dtype)
            else:
                v_ref[i, hh] = vh.astype(v_ref.dtype)
        for ref, t in ((k_ref, proj(h, 512)), (q_ref, proj(h, 0) * (HALF_HEAD ** -0.5 * LOG2E))):
            if use_rope:
                heads = _rope(t, cos_ref[rows(i), :], sin_ref[rows(i), :], first_half)
            else:
                heads = [t[:, HEAD * hh:HEAD * (hh + 1)] for hh in range(N_HEADS)]
            for hh in range(N_HEADS):
                if use_rope:
                    ref[hh, rows(i), :] = heads[hh].astype(ref.dtype)
                else:
                    ref[i, hh] = heads[hh].astype(ref.dtype)

    if not use_rope:
        for i in range(nblk + 1):
            zs_ref[i * stride:i * stride + HALO, :] = jnp.zeros((HALO, CONV_W), F32)

    h = {0: modulated(0)}
    zg = {0: conv_inputs(0, h[0])}
    for i in range(nblk):
        if i + 1 < nblk:
            h[i + 1] = modulated(i + 1)
        else:
            conv_out(i, *zg.pop(i))
        qkv(i, h.pop(i))
        if i + 1 < nblk:
            zg[i + 1] = conv_inputs(i + 1, h[i + 1])
            conv_out(i, *zg.pop(i))


def _even_in_call(x, mod, g, w_in, conv_w, rope, tm, mod_base, mod_stride):
    bsz, n, d = x.shape
    use_rope = rope is not None
    head_shape = (bsz, N_HEADS, n, HEAD)
    if use_rope:
        x_spec, prev_spec, next_spec, mod_spec = _tile_specs(n, tm, mod_base, mod_stride)
        tab = pl.BlockSpec((tm, HEAD), lambda b, j: (j, 0))
        in_specs = [x_spec, prev_spec, next_spec, mod_spec, _resident(g.shape), _resident(w_in.shape),
                    _resident(conv_w.shape), tab, tab]
        args = [x, x, x, mod, g, w_in, conv_w, *rope]
        head_spec = pl.BlockSpec((None, N_HEADS, tm, HEAD), lambda b, j: (b, 0, j, 0))
        v_spec = pl.BlockSpec((None, N_HEADS, HEAD, tm), lambda b, j: (b, 0, 0, j))
        cb_spec = pl.BlockSpec((None, tm, CONV_W), lambda b, j: (b, j, 0))
        k_sds = jax.ShapeDtypeStruct(head_shape, BF16)
        v_sds = jax.ShapeDtypeStruct((bsz, N_HEADS, HEAD, n), BF16)
        grid = (bsz, n // tm)
        zs_rows = tm + 2 * HALO
    else:
        ns = tm // n
        in_specs = [pl.BlockSpec((ns, n, d), lambda b: (b, 0, 0)),
                    pl.BlockSpec((None, 6, d), lambda b: (mod_base, 0, 0)),
                    _resident(g.shape), _resident(w_in.shape), _resident(conv_w.shape)]
        args = [x, mod, g, w_in, conv_w]
        head_spec = v_spec = pl.BlockSpec((ns, N_HEADS, n, HEAD), lambda b: (b, 0, 0, 0))
        cb_spec = pl.BlockSpec((ns, n, CONV_W), lambda b: (b, 0, 0))
        k_sds = v_sds = jax.ShapeDtypeStruct(head_shape, F32)
        grid = (bsz // ns,)
        zs_rows = ns * (n + HALO) + HALO
    return pl.pallas_call(
        functools.partial(_even_in_kernel, use_rope=use_rope),
        out_shape=(jax.ShapeDtypeStruct(head_shape, BF16), k_sds, v_sds,
                   jax.ShapeDtypeStruct((bsz, n, CONV_W), BF16)),
        grid=grid,
        in_specs=in_specs,
        out_specs=(head_spec, head_spec, v_spec, cb_spec),
        scratch_shapes=[pltpu.VMEM((zs_rows, CONV_W), F32)],
        compiler_params=_params(len(grid)),
        name="even_in_rope" if use_rope else "even_in",
    )(*args)


def _diff_lambda(lam_ref, lam_init):
    lp = lam_ref[...]
    return (jnp.exp(jnp.sum(lp[0:1] * lp[1:2], axis=-1, keepdims=True))
            - jnp.exp(jnp.sum(lp[2:3] * lp[3:4], axis=-1, keepdims=True)) + lam_init)


def _stack_components(q):
    lane = lax.broadcasted_iota(jnp.int32, (1, HEAD), 1)
    zero = jnp.zeros_like(q)
    return jnp.concatenate([jnp.where(lane < HALF_HEAD, q, zero),
                            jnp.where(lane >= HALF_HEAD, q, zero)], axis=0)


def _softmax_pv(s, v_ext):
    e = jnp.exp2(s - jnp.max(s, axis=-1, keepdims=True)).astype(BF16)
    return _dot(e, v_ext)


def _normalise(ov, lam, sg, lam_init):
    t = ov.shape[0] // 2
    o = ov[:t, :HEAD] / ov[:t, HEAD:] - lam * (ov[t:, :HEAD] / ov[t:, HEAD:])
    return (_rms(o, sg) * (1.0 - lam_init)).astype(BF16)


def _chain_pipeline(n_groups, step_fn, finish_fn):
    assert n_groups % 2 == 0 and n_groups >= 2
    step_fn(0, 0, None, None)
    step_fn(1, 1, 0, 0)

    def body(t, carry):
        g = 2 * t
        step_fn(g, 0, g - 1, 1)
        finish_fn(g - 2, 0)
        step_fn(g + 1, 1, g, 0)
        finish_fn(g - 1, 1)
        return carry

    lax.fori_loop(1, n_groups // 2, body, 0)
    step_fn(None, None, n_groups - 1, 1)
    finish_fn(n_groups - 2, 0)
    finish_fn(n_groups - 1, 1)


def _attn_prompt_kernel(lam_ref, sg_ref, q_ref, k_ref, v_ref, o_ref, s_ref, ov_ref, *, lam_init):
    lam = _diff_lambda(lam_ref, lam_init)
    sg = sg_ref[...]
    n = k_ref.shape[2]
    ones = jnp.ones((n, MXU_N - HEAD), BF16)

    def step_fn(bs, ps, bv, pv):
        for hh in range(N_HEADS):
            if bs is not None:
                s_ref[ps * N_HEADS + hh] = lax.dot_general(
                    _stack_components(q_ref[bs, hh]), k_ref[bs, hh].astype(BF16),
                    (((1,), (1,)), ((), ())), preferred_element_type=F32)
        for hh in range(N_HEADS):
            if bv is not None:
                v_ext = jnp.concatenate([v_ref[bv, hh].astype(BF16), ones], axis=1)
                ov_ref[pv * N_HEADS + hh] = _softmax_pv(s_ref[pv * N_HEADS + hh], v_ext)

    def finish_fn(b, par):
        for hh in range(N_HEADS):
            o_ref[b, :, HEAD * hh:HEAD * (hh + 1)] = _normalise(ov_ref[par * N_HEADS + hh], lam, sg, lam_init)

    _chain_pipeline(q_ref.shape[0], step_fn, finish_fn)


def _attn_prompt_call(q, k, v, lam_params, subln_g, lam_init, nb):
    bsz, nh, n, hd = q.shape
    spec = pl.BlockSpec((nb, nh, n, hd), lambda b: (b, 0, 0, 0))
    return pl.pallas_call(
        functools.partial(_attn_prompt_kernel, lam_init=lam_init),
        out_shape=jax.ShapeDtypeStruct((bsz, n, nh * hd), BF16),
        grid=(bsz // nb,),
        in_specs=[_resident(lam_params.shape), _resident(subln_g.shape), spec, spec, spec],
        out_specs=pl.BlockSpec((nb, n, nh * hd), lambda b: (b, 0, 0)),
        scratch_shapes=[pltpu.VMEM((2 * nh, 2 * n, n), F32), pltpu.VMEM((2 * nh, 2 * n, MXU_N), F32)],
        compiler_params=_params(1),
        name="attn",
    )(lam_params, subln_g, q, k, v)


POST_BLOCK = 256
ONES_ROWS = 16
ATTN_WIDTH = 2
KEY_CHUNK = 256


def _attn_cache_kernel(*refs, lam_init, n_cast):
    lam_ref, sg_ref, q_ref, k_ref, vt_ref, ck_ref, cv_ref = refs[:7]
    cast_in = refs[7:7 + n_cast]
    o_ref = refs[7 + n_cast]
    cast_out = refs[8 + n_cast:8 + 2 * n_cast]
    kbuf_ref, vtbuf_ref, s_ref, ov_ref, m_ref, qt_ref = refs[8 + 2 * n_cast:]
    for src, dst in zip(cast_in, cast_out):
        dst[...] = src[...].astype(BF16)
    past = ck_ref.shape[0]
    lk = kbuf_ref.shape[0]
    kbuf_ref[0:past, :] = ck_ref[...].astype(BF16)
    kbuf_ref[past:, :] = k_ref[...]
    vtbuf_ref[0:HEAD, 0:past] = cv_ref[...].T.astype(BF16)
    vtbuf_ref[0:HEAD, past:] = vt_ref[...]
    vtbuf_ref[HEAD:, :] = jnp.ones((ONES_ROWS, lk), BF16)

    lam = _diff_lambda(lam_ref, lam_init)
    sg = sg_ref[...] * (1.0 - lam_init)

    def rows(g, u):
        start = (g * ATTN_WIDTH + u) * ATTN_SUB
        return pl.ds(pl.multiple_of(start, ATTN_SUB), ATTN_SUB)

    def step_fn(gs, ps, gv, pv):
        chains = range(ATTN_WIDTH)
        if gs is not None:
            for u in chains:
                qt_ref[u] = _stack_components(q_ref[rows(gs, u), :]).T
            qq = [qt_ref[u] for u in chains]
            col_max = [None] * ATTN_WIDTH
        if gv is not None:
            m_prev = [m_ref[pv * ATTN_WIDTH + u] for u in chains]
            acc = [None] * ATTN_WIDTH
        for c0 in range(0, lk, KEY_CHUNK):
            keys = slice(c0, c0 + KEY_CHUNK)
            for u in chains:
                if gs is not None:
                    s = _dot(kbuf_ref[keys, :], qq[u])
                    s_ref[ps * ATTN_WIDTH + u, keys, :] = s
                    cm = jnp.max(s, axis=0, keepdims=True)
                    col_max[u] = cm if col_max[u] is None else jnp.maximum(col_max[u], cm)
            for u in chains:
                if gv is not None:
                    e = jnp.exp2(s_ref[pv * ATTN_WIDTH + u, keys, :] - m_prev[u]).astype(BF16)
                    part = _dot(vtbuf_ref[:, keys], e)
                    acc[u] = part if acc[u] is None else acc[u] + part
        for u in chains:
            if gs is not None:
                m_ref[ps * ATTN_WIDTH + u] = col_max[u]
            if gv is not None:
                ov_ref[pv * ATTN_WIDTH + u] = acc[u]

    def finish_fn(g, par):
        for u in range(ATTN_WIDTH):
            ov = ov_ref[par * ATTN_WIDTH + u]
            o_t = (ov[0:HEAD, 0:ATTN_SUB] / ov[HEAD:HEAD + 1, 0:ATTN_SUB]
                   - lam * (ov[0:HEAD, ATTN_SUB:] / ov[HEAD:HEAD + 1, ATTN_SUB:]))
            ms = jnp.mean(o_t * o_t, axis=0, keepdims=True)
            o_ref[rows(g, u), :] = ((o_t * lax.rsqrt(ms + EPS)).T * sg).astype(BF16)

    _chain_pipeline(q_ref.shape[0] // (ATTN_SUB * ATTN_WIDTH), step_fn, finish_fn)


def _attn_cache_call(q, k, vt, lam_params, subln_g, cache_k, cache_v, layer, lam_init, to_cast):
    bsz, nh, n, hd = q.shape
    past = cache_k.shape[3]
    seq_spec = pl.BlockSpec((None, None, n, hd), lambda b, h: (b, h, 0, 0))
    vt_spec = pl.BlockSpec((None, None, hd, n), lambda b, h: (b, h, 0, 0))
    c_spec = pl.BlockSpec((None, None, None, past, hd), lambda b, h: (b, layer, h, 0, 0))
    flat = [w.reshape(-1, w.shape[-1]) for w in to_cast]
    slab_specs = [pl.BlockSpec((w.shape[0] // (bsz * nh), w.shape[1]), lambda b, h: (b * nh + h, 0))
                  for w in flat]
    outs = pl.pallas_call(
        functools.partial(_attn_cache_kernel, lam_init=lam_init, n_cast=len(flat)),
        out_shape=[jax.ShapeDtypeStruct((bsz, n, nh * hd), BF16)]
        + [jax.ShapeDtypeStruct(w.shape, BF16) for w in flat],
        grid=(bsz, nh),
        in_specs=[_resident(lam_params.shape), _resident(subln_g.shape),
                  seq_spec, seq_spec, vt_spec, c_spec, c_spec] + slab_specs,
        out_specs=[pl.BlockSpec((None, n, hd), lambda b, h: (b, 0, h))] + slab_specs,
        scratch_shapes=[pltpu.VMEM((past + n, hd), BF16), pltpu.VMEM((hd + ONES_ROWS, past + n), BF16),
                        pltpu.VMEM((2 * ATTN_WIDTH, past + n, 2 * ATTN_SUB), F32),
                        pltpu.VMEM((2 * ATTN_WIDTH, hd + ONES_ROWS, 2 * ATTN_SUB), F32),
                        pltpu.VMEM((2 * ATTN_WIDTH, 1, 2 * ATTN_SUB), F32),
                        pltpu.VMEM((ATTN_WIDTH, hd, 2 * ATTN_SUB), BF16)],
        compiler_params=_params(2),
        name="attn_cache",
    )(lam_params, subln_g, q, k, vt, cache_k, cache_v, *flat)
    return outs[0], [o.reshape(w.shape) for o, w in zip(outs[1:], to_cast)]


def _post_kernel(x_ref, a_ref, b_ref, mod_ref, g_ref, wo_ref, wg_ref, wu_ref, wd_ref, o_ref):
    nb = x_ref.shape[0] // POST_BLOCK
    blocks = [slice(i * POST_BLOCK, (i + 1) * POST_BLOCK) for i in range(nb)]

    def out_proj(rows):
        return _dot(jnp.concatenate([a_ref[rows, :], b_ref[rows, :]], axis=1), wo_ref[...])

    def norms(rows, y):
        x1 = x_ref[rows, :] + mod_ref[2:3, :] * _rms(y, g_ref[1:2, :])
        return x1, _modulate(x1, g_ref[2:3, :], mod_ref[3:4, :], mod_ref[4:5, :]).astype(BF16)

    def gate_up(h):
        return _dot(h, wg_ref[...]), _dot(h, wu_ref[...])

    def down(gu):
        return _dot((_silu(gu[0]) * gu[1]).astype(BF16), wd_ref[...])

    def finish(rows, x1, f):
        o_ref[rows, :] = x1 + mod_ref[5:6, :] * _rms(f, g_ref[3:4, :])

    y = {0: out_proj(blocks[0])}
    x1, gu = {}, {}
    for i in range(nb + 1):
        if i + 1 < nb:
            y[i + 1] = out_proj(blocks[i + 1])
        if i < nb:
            x1[i], h = norms(blocks[i], y.pop(i))
        if i >= 1:
            f = down(gu.pop(i - 1))
        if i < nb:
            gu[i] = gate_up(h)
        if i >= 1:
            finish(blocks[i - 1], x1.pop(i - 1), f)


def _post_call(x, a, b, mod, g, w_out, w_gate, w_up, w_down, layer, tm, mod_base, mod_stride):
    bsz, n, d = x.shape
    x_spec, _, _, mod_spec = _tile_specs(n, tm, mod_base, mod_stride)
    half_spec = pl.BlockSpec((None, tm, a.shape[2]), lambda b_, j: (b_, j, 0))

    def layer_resident(w):
        return pl.BlockSpec((None,) + w.shape[1:], lambda *_: (layer, 0, 0), pipeline_mode=pl.Buffered(1))

    return pl.pallas_call(
        _post_kernel,
        out_shape=jax.ShapeDtypeStruct(x.shape, F32),
        grid=(bsz, n // tm),
        in_specs=[x_spec, half_spec, half_spec, mod_spec, _resident(g.shape),
                  _resident(w_out.shape), layer_resident(w_gate), layer_resident(w_up),
                  layer_resident(w_down)],
        out_specs=x_spec,
        compiler_params=_params(2),
        name="post",
    )(x, a, b, mod, g, w_out, w_gate, w_up, w_down)


def _odd_in_kernel(*refs, n_seq, whole_seqs):
    if whole_seqs:
        (x_ref, mod_ref, g_ref, w_ref, wp_ref, ps_ref, cs_ref, dc_ref, ds_ref, wf_ref,
         pc_ref, fc_ref, us_ref, f2_ref, f4_ref) = refs
        nblk, blk = x_ref.shape[0], x_ref.shape[1]
    else:
        (x_ref, xp_ref, xn_ref, mod_ref, g_ref, w_ref, wp_ref, ps_ref, cs_ref,
         pc_ref, xc_ref, xs_ref, us_ref, f2_ref, f4_ref) = refs
        j = pl.program_id(1)
        nt = pl.num_programs(1)
        nblk, blk = 1, x_ref.shape[0]
    tm = nblk * blk
    g = g_ref[0:1, :]
    shift = mod_ref[0:1, :]
    scale = mod_ref[1:2, :]
    if whole_seqs:
        h = _modulate(x_ref[...].reshape(tm, x_ref.shape[2]), g, shift, scale).astype(BF16)
        up = _dot(h, w_ref[:, 0:POOL_W])
    else:
        xh = jnp.concatenate([x_ref[...], xp_ref[...], xn_ref[...]], axis=0)
        hz = _modulate(xh, g, shift, scale).astype(BF16)
        h = hz[0:tm]
        upz = _dot(hz, w_ref[:, 0:POOL_W])
        up, uph = upz[0:tm], upz[tm:]
    uf = _dot(h, w_ref[:, POOL_W:]).astype(BF16)

    def store(ref, lanes, val):
        if whole_seqs:
            for i in range(nblk):
                ref[i, :, lanes] = val[i * blk:(i + 1) * blk]
        else:
            ref[:, lanes] = val

    xc_groups, xs_groups = [], []
    for gi in range(FOURIER_W // GROUP):
        lanes = slice(GROUP * gi, GROUP * (gi + 1))
        cs = _dot(uf[:, lanes], cs_ref[...])
        xc_groups.append(cs[:, 0:GROUP].astype(BF16))
        xs_groups.append(cs[:, GROUP:].astype(BF16))
        if not whole_seqs:
            xc_ref[:, lanes] = xc_groups[gi]
            xs_ref[:, lanes] = xs_groups[gi]
    if whole_seqs:
        xc = jnp.concatenate(xc_groups, axis=1)
        xs = jnp.concatenate(xs_groups, axis=1)
        ortho = float(1.0 / math.sqrt(blk * GROUP))
        for i in range(nblk):
            seq = slice(i * blk, (i + 1) * blk)
            y = _dot(dc_ref[...], xc[seq]) - _dot(ds_ref[...], xs[seq])
            four = (y * ortho).astype(BF16)
            for gi in range(FOURIER_W // GROUP):
                lanes = slice(GROUP * gi, GROUP * (gi + 1))
                fc_ref[i, :, lanes] = _dot(four[:, lanes], wf_ref[gi]).astype(BF16)

    stride = blk + HALO

    def u0(i):
        return HALO + i * stride

    rows = nblk * stride + HALO
    zeros = jnp.zeros((HALO, POOL_W), F32)
    for i in range(nblk):
        us_ref[u0(i):u0(i) + blk, :] = up[i * blk:(i + 1) * blk]
        if whole_seqs:
            us_ref[u0(i) - HALO:u0(i), :] = zeros
    if whole_seqs:
        us_ref[rows - HALO:rows, :] = zeros
    else:
        us_ref[0:HALO, :] = jnp.where(j > 0, uph[0:HALO], 0.0)
        us_ref[rows - HALO:rows, :] = jnp.where(j < nt - 1, uph[HALO:], 0.0)
    us_ref[rows:, :] = zeros
    f2_ref[0:rows, :] = us_ref[0:rows, GROUP:] + us_ref[1:rows + 1, GROUP:]
    f2_ref[rows:, :] = jnp.zeros((HALO, POOL_W - GROUP), F32)
    f4_ref[0:rows, :] = f2_ref[0:rows, GROUP:] + f2_ref[2:rows + 2, GROUP:]
    f4_ref[rows:, :] = jnp.zeros((HALO, POOL_W - 2 * GROUP), F32)
    f8 = f4_ref[0:rows, GROUP:] + f4_ref[4:rows + 4, GROUP:]

    def centred(i):
        a = u0(i)
        return (us_ref[a - 1:a - 1 + blk, 0:GROUP] + us_ref[a:a + blk, 0:GROUP],
                f2_ref[a - 2:a - 2 + blk, 0:GROUP] + f2_ref[a:a + blk, 0:GROUP],
                f4_ref[a - 4:a - 4 + blk, 0:GROUP] + f4_ref[a:a + blk, 0:GROUP],
                f8[a - HALO:a - HALO + blk] + f8[a:a + blk])

    sums = [centred(i) for i in range(nblk)]
    pos = lax.broadcasted_iota(jnp.int32, (blk, 1), 0)
    t = (pos if whole_seqs else j * blk + pos).astype(F32)
    for gi, win in enumerate(POOL_WINDOWS):
        lanes = slice(GROUP * gi, GROUP * (gi + 1))
        cnt = jnp.minimum(t + float(win // 2), float(n_seq)) - jnp.maximum(t - float(win // 2), 0.0)
        pooled = jnp.concatenate([s[gi] / cnt for s in sums], axis=0)
        diff = (pooled - up[:, lanes]).astype(BF16)
        store(pc_ref, lanes, (_dot(diff, wp_ref[gi]) * ps_ref[0:1, lanes]).astype(BF16))


def _odd_in_call(x, mod, g, w_in, w_pool, pool_scale, cs_mat, tm, mod_base, mod_stride, seq_dft=None):
    bsz, n, d = x.shape
    whole_seqs = tm >= n
    weights = [_resident(g.shape), _resident(w_in.shape), _resident(w_pool.shape),
               _resident(pool_scale.shape), _resident(cs_mat.shape)]
    if whole_seqs:
        ns = tm // n
        in_specs = ([pl.BlockSpec((ns, n, d), lambda b: (b, 0, 0)),
                     pl.BlockSpec((None, 6, d), lambda b: (mod_base, 0, 0))] + weights
                    + [_resident(t.shape) for t in seq_dft])
        args = [x, mod, g, w_in, w_pool, pool_scale, cs_mat, *seq_dft]
        out_spec = pl.BlockSpec((ns, n, POOL_W), lambda b: (b, 0, 0))
        n_out = 2
        grid = (bsz // ns,)
        scratch_rows = ns * (n + HALO) + 2 * HALO
    else:
        x_spec, prev_spec, next_spec, mod_spec = _tile_specs(n, tm, mod_base, mod_stride)
        in_specs = [x_spec, prev_spec, next_spec, mod_spec] + weights
        args = [x, x, x, mod, g, w_in, w_pool, pool_scale, cs_mat]
        out_spec = pl.BlockSpec((None, tm, POOL_W), lambda b, j: (b, j, 0))
        n_out = 3
        grid = (bsz, n // tm)
        scratch_rows = tm + 3 * HALO
    out_sds = jax.ShapeDtypeStruct((bsz, n, POOL_W), BF16)
    return pl.pallas_call(
        functools.partial(_odd_in_kernel, n_seq=n, whole_seqs=whole_seqs),
        out_shape=(out_sds,) * n_out,
        grid=grid,
        in_specs=in_specs,
        out_specs=(out_spec,) * n_out,
        scratch_shapes=[pltpu.VMEM((scratch_rows, POOL_W), F32),
                        pltpu.VMEM((scratch_rows, POOL_W - GROUP), F32),
                        pltpu.VMEM((scratch_rows, POOL_W - 2 * GROUP), F32)],
        compiler_params=_params(len(grid)),
        name="odd_in",
    )(*args)


FLIP_BLOCK = 256


def _four_sym_kernel(c_ref, s_ref, pm_ref, pm0_ref, xc_ref, xs_ref, wf_ref, o_ref, *, scale):
    n = o_ref.shape[0]
    half = n // 2

    def folded(x_ref, sign):
        parts = []
        for b in range(half // FLIP_BLOCK):
            lo = n - FLIP_BLOCK * (b + 1)
            if b == 0:
                partner = _dot(pm0_ref[...], x_ref[lo:lo + FLIP_BLOCK, :])
            else:
                partner = _dot(pm_ref[...], x_ref[lo:lo + FLIP_BLOCK + HALO, :])
            direct = x_ref[FLIP_BLOCK * b:FLIP_BLOCK * (b + 1), :].astype(F32)
            parts.append((direct + sign * partner).astype(BF16))
        return jnp.concatenate(parts, axis=0)

    j = lax.broadcasted_iota(jnp.int32, (c_ref.shape[0], 1), 0)
    nyquist = (1 - 2 * (j & 1)).astype(F32) * xc_ref[half:half + 1, :].astype(F32)
    p = _dot(c_ref[...], folded(xc_ref, 1.0)) + nyquist
    q = _dot(s_ref[...], folded(xs_ref, -1.0))

    def project(rows, four):
        for gi in range(FOURIER_W // GROUP):
            lanes = slice(GROUP * gi, GROUP * (gi + 1))
            o_ref[rows, lanes] = _dot(four[:, lanes], wf_ref[gi]).astype(BF16)

    project(slice(0, half), ((p[0:half] - q[0:half]) * scale).astype(BF16))
    mirrored = ((p + q) * scale).astype(BF16)
    for b in range(half // FLIP_BLOCK):
        lo = half - FLIP_BLOCK * (b + 1)
        window = mirrored[lo:lo + FLIP_BLOCK + HALO, :]
        flipped = _dot(pm_ref[...], window).astype(BF16)
        project(slice(half + FLIP_BLOCK * b, half + FLIP_BLOCK * (b + 1)), flipped)


def _four_sym_call(c_quarter, s_quarter, perm, perm0, xc, xs, w_four):
    bsz, n, w = xc.shape
    seq_spec = pl.BlockSpec((None, n, w), lambda b: (b, 0, 0))
    return pl.pallas_call(
        functools.partial(_four_sym_kernel, scale=float(1.0 / math.sqrt(n * GROUP))),
        out_shape=jax.ShapeDtypeStruct((bsz, n, w), BF16),
        grid=(bsz,),
        in_specs=[_resident(c_quarter.shape), _resident(s_quarter.shape), _resident(perm.shape),
                  _resident(perm0.shape), seq_spec, seq_spec, _resident(w_four.shape)],
        out_specs=seq_spec,
        compiler_params=_params(1),
        name="fourier_sym",
    )(c_quarter, s_quarter, perm, perm0, xc, xs, w_four)


def _flip_perms():
    i = np.arange(FLIP_BLOCK)
    pm = np.zeros((FLIP_BLOCK, FLIP_BLOCK + HALO), np.float32)
    pm[i, FLIP_BLOCK - i] = 1.0
    pm0 = np.zeros((FLIP_BLOCK, FLIP_BLOCK), np.float32)
    pm0[i[1:], FLIP_BLOCK - i[1:]] = 1.0
    return pm, pm0


def _rope_tables(n_tok):
    rows = n_tok // GRID_W
    row = np.repeat(np.arange(rows), GRID_W).astype(np.float64)
    col = np.tile(np.arange(GRID_W), rows).astype(np.float64)
    inv = ROPE_BASE ** (-np.arange(0, ROPE_AXIS, 2, dtype=np.float64) / ROPE_AXIS)
    ang_r = row[:, None] * inv[None, :]
    ang_c = col[:, None] * inv[None, :]
    ang = np.concatenate([ang_r, ang_r, ang_c, ang_c], axis=-1)
    cos = np.concatenate([np.cos(ang)] * 2, axis=-1)
    sin = np.concatenate([np.sin(ang)] * 2, axis=-1)
    first_half = (np.arange(HEAD) % 32) < 16
    sin_signed = np.where(first_half[None, :], -sin, sin)
    return jnp.asarray(cos, F32), jnp.asarray(sin_signed, F32)


def _dft_mats(n):
    idx = np.arange(n, dtype=np.int64)
    ang = 2.0 * np.pi * ((idx[:, None] * idx[None, :]) % n).astype(np.float64) / n
    return np.cos(ang), np.sin(ang)


def kernel(x_prompt, x_sample, cache_k, cache_v, c, c_ctx, w_mod, b_mod, norm_g,
           w_in_even, lam_params, subln_g, conv_w, w_out_even,
           w_in_odd, w_pool, pool_scale, w_fourier, w_out_odd,
           w_gate, w_up, w_down):
    depth = w_mod.shape[0]
    n_dec = x_sample.shape[0]
    n_p, n_s = x_prompt.shape[1], x_sample.shape[1]

    pad_rows = 16 - 1 - n_dec
    cc = jnp.concatenate([c_ctx[None, :], c, jnp.zeros((pad_rows, D_MODEL), F32)], axis=0)
    mod_all = _mod_call(cc, w_mod, b_mod)[:, :1 + n_dec].reshape(depth, 1 + n_dec, 6, D_MODEL)

    rope = _rope_tables(n_s)
    cc_g, sc_g = _dft_mats(GROUP)
    cs_mat = jnp.asarray(np.concatenate([cc_g, sc_g], axis=1), F32).astype(BF16)
    dft_p = tuple(jnp.asarray(m, F32).astype(BF16) for m in _dft_mats(n_p))
    dft_s = tuple(jnp.asarray(m[:n_s // 2 + HALO, :n_s // 2], F32).astype(BF16) for m in _dft_mats(n_s))
    flips = tuple(jnp.asarray(m, F32).astype(BF16) for m in _flip_perms())

    late_weights = [w_gate, w_up, w_down, w_out_even, w_in_odd, w_out_odd]
    xp, xs = x_prompt, x_sample
    new_k, new_v = [], []
    for l in range(depth):
        mod = mod_all[l]
        g = norm_g[l]
        i = l // 2
        streams = []
        if l % 2 == 0:
            lam_init = 0.8 - 0.6 * math.exp(-0.3 * l)
            w_in = w_in_even[i].astype(BF16)
            sg = subln_g[i][None, :]
            qp, kp, vp, cbp = _even_in_call(xp, mod, g, w_in, conv_w[i], None, 4 * n_p, 0, 0)
            ap = _attn_prompt_call(qp, kp, vp, lam_params[i], sg, lam_init, 8)
            new_k.append(kp)
            new_v.append(vp)
            qs, ks, vts, cbs = _even_in_call(xs, mod, g, w_in, conv_w[i], rope, 1024, 1, 1)
            a_s, cast = _attn_cache_call(qs, ks, vts, lam_params[i], sg, cache_k, cache_v, i, lam_init,
                                         late_weights if l == 0 else [])
            if l == 0:
                wg, wu, wd, wo_even, wi_odd, wo_odd = cast
            w_out = wo_even[i]
            streams = [(ap, cbp), (a_s, cbs)]
        else:
            w_in = wi_odd[i]
            w_out = wo_odd[i]
            wp = w_pool[i].astype(BF16)
            wf = w_fourier[i].astype(BF16)
            ps = pool_scale[i][None, :]
            pcp, fcp = _odd_in_call(xp, mod, g, w_in, wp, ps, cs_mat, 4 * n_p, 0, 0, (*dft_p, wf))
            pcs, xcs, xss = _odd_in_call(xs, mod, g, w_in, wp, ps, cs_mat, 1024, 1, 1)
            fcs = _four_sym_call(*dft_s, *flips, xcs, xss, wf)
            streams = [(pcp, fcp), (pcs, fcs)]
        xp = _post_call(*(t.reshape(1, -1, t.shape[-1]) for t in (xp,) + streams[0]),
                        mod, g, w_out, wg, wu, wd, l, 512, 0, 0).reshape(x_prompt.shape)
        xs = _post_call(xs, streams[1][0], streams[1][1], mod, g, w_out, wg, wu, wd, l, 512, 1, 1)
    def stack_layers(parts):
        if len(parts) == 1:
            return parts[0][:, None]
        return jnp.stack(parts, axis=1)

    return xp, xs, stack_layers(new_k), stack_layers(new_v)
```

```python
import functools
import math

import numpy as np
import jax
import jax.numpy as jnp
from jax import lax
from jax.experimental import pallas as pl
from jax.experimental.pallas import tpu as pltpu

F32 = jnp.float32
BF16 = jnp.bfloat16

D_MODEL = 1024
GRID_W = 64
N_HEADS = 4
HEAD = 128
HALF_HEAD = 64
ROPE_AXIS = 32
ROPE_BASE = 10000.0
ATTN_W = 512
CONV_W = 512
POOL_W = 512
FOURIER_W = 512
GROUP = 128
POOL_WINDOWS = (2, 4, 8, 16)
D_FF = 2816
EPS = 1e-6
LOG2E = math.log2(math.e)
HALO = 8
MXU_N = 256
ATTN_SUB = 128
EVEN_BLOCK = 256
VMEM_LIMIT = 56 * 1024 * 1024


def _params(n_axes):
    return pltpu.CompilerParams(dimension_semantics=("arbitrary",) * n_axes,
                                vmem_limit_bytes=VMEM_LIMIT)


def _resident(shape):
    return pl.BlockSpec(shape, lambda *_: (0,) * len(shape), pipeline_mode=pl.Buffered(1))


def _rms(x, g):
    ms = jnp.mean(x * x, axis=-1, keepdims=True)
    return x * lax.rsqrt(ms + EPS) * g


def _modulate(x, g, shift, scale):
    return _rms(x, g) * (1.0 + scale) + shift


def _dot(a, b):
    return jnp.dot(a, b, preferred_element_type=F32)


def _silu(x):
    return x / (1.0 + jnp.exp(-x))


def _mod_kernel(cc_ref, w_ref, b_ref, o_ref):
    s = _silu(cc_ref[...]).astype(BF16)
    o_ref[...] = _dot(s, w_ref[...].astype(BF16)) + b_ref[...]


def _mod_call(cc, w_mod, b_mod):
    depth, d, n6 = w_mod.shape
    rows = cc.shape[0]
    tn = 2048
    return pl.pallas_call(
        _mod_kernel,
        out_shape=jax.ShapeDtypeStruct((depth, rows, n6), F32),
        grid=(depth, n6 // tn),
        in_specs=[
            pl.BlockSpec((rows, d), lambda l, j: (0, 0)),
            pl.BlockSpec((None, d, tn), lambda l, j: (l, 0, j)),
            pl.BlockSpec((None, 1, tn), lambda l, j: (l, 0, j)),
        ],
        out_specs=pl.BlockSpec((None, rows, tn), lambda l, j: (l, 0, j)),
        compiler_params=_params(2),
        name="mod",
    )(cc, w_mod, b_mod.reshape(depth, 1, n6))


def _tile_specs(n, tm, mod_base, mod_stride):
    nb8 = n // HALO
    t8 = tm // HALO
    x_spec = pl.BlockSpec((None, tm, D_MODEL), lambda b, j: (b, j, 0))
    prev_spec = pl.BlockSpec((None, HALO, D_MODEL),
                             lambda b, j: (b, jnp.maximum(j * t8 - 1, 0), 0))
    next_spec = pl.BlockSpec((None, HALO, D_MODEL),
                             lambda b, j: (b, jnp.minimum((j + 1) * t8, nb8 - 1), 0))
    mod_spec = pl.BlockSpec((None, 6, D_MODEL),
                            lambda b, j: (mod_base + mod_stride * b, 0, 0))
    return x_spec, prev_spec, next_spec, mod_spec


def _rope(t, cos, sin_signed, first_half):
    outs = []
    for hh in range(N_HEADS):
        th = t[:, HEAD * hh:HEAD * (hh + 1)]
        swapped = jnp.where(first_half, pltpu.roll(th, HEAD - 16, 1), pltpu.roll(th, 16, 1))
        outs.append(th * cos + swapped * sin_signed)
    return outs


def _even_in_kernel(*refs, use_rope):
    if use_rope:
        (x_ref, xp_ref, xn_ref, mod_ref, g_ref, w_ref, cw_ref, cos_ref, sin_ref,
         q_ref, k_ref, v_ref, cb_ref, zs_ref) = refs
        j = pl.program_id(1)
        nt = pl.num_programs(1)
        blk = EVEN_BLOCK
        nblk = x_ref.shape[0] // blk
        lane = lax.broadcasted_iota(jnp.int32, (1, HEAD), 1)
        first_half = (lane % 32) < 16
    else:
        x_ref, mod_ref, g_ref, w_ref, cw_ref, q_ref, k_ref, v_ref, cb_ref, zs_ref = refs
        nblk, blk = x_ref.shape[0], x_ref.shape[1]
    g = g_ref[0:1, :]
    shift = mod_ref[0:1, :]
    scale = mod_ref[1:2, :]
    stride = blk if use_rope else blk + HALO

    def z0(i):
        return HALO + i * stride

    def rows(i):
        return slice(i * blk, (i + 1) * blk)

    def proj(hh, lo):
        return _dot(hh, w_ref[:, lo:lo + 512])

    def modulated(i):
        xi = x_ref[rows(i), :] if use_rope else x_ref[i]
        return _modulate(xi, g, shift, scale).astype(BF16)

    def conv_inputs(i, h):
        outer = use_rope and i == 0
        if outer:
            xh = jnp.concatenate([xp_ref[...], xn_ref[...]], axis=0)
            hz = jnp.concatenate([h, _modulate(xh, g, shift, scale).astype(BF16)], axis=0)
        else:
            hz = h
        z = proj(hz, 2048) * proj(hz, 2560)
        if outer:
            zh = z[blk:]
            z = z[0:blk]
            zs_ref[0:HALO, :] = jnp.where(j > 0, zh[0:HALO], 0.0)
            zs_ref[z0(nblk):, :] = jnp.where(j < nt - 1, zh[HALO:], 0.0)
        zs_ref[z0(i):z0(i) + blk, :] = z
        return z, proj(h, 1536)

    def conv_out(i, z, gate_b):
        conv = (cw_ref[0:1, :] * zs_ref[z0(i) - 1:z0(i) - 1 + blk, :] + cw_ref[1:2, :] * z
                + cw_ref[2:3, :] * zs_ref[z0(i) + 1:z0(i) + 1 + blk, :])
        out = (gate_b * conv).astype(BF16)
        if use_rope:
            cb_ref[rows(i), :] = out
        else:
            cb_ref[i] = out

    def qkv(i, h):
        v = proj(h, 1024)
        for hh in range(N_HEADS):
            vh = v[:, HEAD * hh:HEAD * (hh + 1)]
            if use_rope:
                v_ref[hh, :, rows(i)] = vh.T.astype(v_ref.dtype)
            else:
                v_ref[i, hh] = vh.astype(v_ref.dtype)
        for ref, t in ((k_ref, proj(h, 512)), (q_ref, proj(h, 0) * (HALF_HEAD ** -0.5 * LOG2E))):
            if use_rope:
                heads = _rope(t, cos_ref[rows(i), :], sin_ref[rows(i), :], first_half)
            else:
                heads = [t[:, HEAD * hh:HEAD * (hh + 1)] for hh in range(N_HEADS)]
            for hh in range(N_HEADS):
                if use_rope:
                    ref[hh, rows(i), :] = heads[hh].astype(ref.dtype)
                else:
                    ref[i, hh] = heads[hh].astype(ref.dtype)

    if not use_rope:
        for i in range(nblk + 1):
            zs_ref[i * stride:i * stride + HALO, :] = jnp.zeros((HALO, CONV_W), F32)

    h = {0: modulated(0)}
    zg = {0: conv_inputs(0, h[0])}
    for i in range(nblk):
        if i + 1 < nblk:
            h[i + 1] = modulated(i + 1)
        else:
            conv_out(i, *zg.pop(i))
        qkv(i, h.pop(i))
        if i + 1 < nblk:
            zg[i + 1] = conv_inputs(i + 1, h[i + 1])
            conv_out(i, *zg.pop(i))


def _even_in_call(x, mod, g, w_in, conv_w, rope, tm, mod_base, mod_stride):
    bsz, n, d = x.shape
    use_rope = rope is not None
    head_shape = (bsz, N_HEADS, n, HEAD)
    if use_rope:
        x_spec, prev_spec, next_spec, mod_spec = _tile_specs(n, tm, mod_base, mod_stride)
        tab = pl.BlockSpec((tm, HEAD), lambda b, j: (j, 0))
        in_specs = [x_spec, prev_spec, next_spec, mod_spec, _resident(g.shape), _resident(w_in.shape),
                    _resident(conv_w.shape), tab, tab]
        args = [x, x, x, mod, g, w_in, conv_w, *rope]
        head_spec = pl.BlockSpec((None, N_HEADS, tm, HEAD), lambda b, j: (b, 0, j, 0))
        v_spec = pl.BlockSpec((None, N_HEADS, HEAD, tm), lambda b, j: (b, 0, 0, j))
        cb_spec = pl.BlockSpec((None, tm, CONV_W), lambda b, j: (b, j, 0))
        k_sds = jax.ShapeDtypeStruct(head_shape, BF16)
        v_sds = jax.ShapeDtypeStruct((bsz, N_HEADS, HEAD, n), BF16)
        grid = (bsz, n // tm)
        zs_rows = tm + 2 * HALO
    else:
        ns = tm // n
        in_specs = [pl.BlockSpec((ns, n, d), lambda b: (b, 0, 0)),
                    pl.BlockSpec((None, 6, d), lambda b: (mod_base, 0, 0)),
                    _resident(g.shape), _resident(w_in.shape), _resident(conv_w.shape)]
        args = [x, mod, g, w_in, conv_w]
        head_spec = v_spec = pl.BlockSpec((ns, N_HEADS, n, HEAD), lambda b: (b, 0, 0, 0))
        cb_spec = pl.BlockSpec((ns, n, CONV_W), lambda b: (b, 0, 0))
        k_sds = v_sds = jax.ShapeDtypeStruct(head_shape, F32)
        grid = (bsz // ns,)
        zs_rows = ns * (n + HALO) + HALO
    return pl.pallas_call(
        functools.partial(_even_in_kernel, use_rope=use_rope),
        out_shape=(jax.ShapeDtypeStruct(head_shape, BF16), k_sds, v_sds,
                   jax.ShapeDtypeStruct((bsz, n, CONV_W), BF16)),
        grid=grid,
        in_specs=in_specs,
        out_specs=(head_spec, head_spec, v_spec, cb_spec),
        scratch_shapes=[pltpu.VMEM((zs_rows, CONV_W), F32)],
        compiler_params=_params(len(grid)),
        name="even_in_rope" if use_rope else "even_in",
    )(*args)


def _diff_lambda(lam_ref, lam_init):
    lp = lam_ref[...]
    return (jnp.exp(jnp.sum(lp[0:1] * lp[1:2], axis=-1, keepdims=True))
            - jnp.exp(jnp.sum(lp[2:3] * lp[3:4], axis=-1, keepdims=True)) + lam_init)


def _stack_components(q):
    lane = lax.broadcasted_iota(jnp.int32, (1, HEAD), 1)
    zero = jnp.zeros_like(q)
    return jnp.concatenate([jnp.where(lane < HALF_HEAD, q, zero),
                            jnp.where(lane >= HALF_HEAD, q, zero)], axis=0)


def _softmax_pv(s, v_ext):
    e = jnp.exp2(s - jnp.max(s, axis=-1, keepdims=True)).astype(BF16)
    return _dot(e, v_ext)


def _normalise(ov, lam, sg, lam_init):
    t = ov.shape[0] // 2
    o = ov[:t, :HEAD] / ov[:t, HEAD:] - lam * (ov[t:, :HEAD] / ov[t:, HEAD:])
    return (_rms(o, sg) * (1.0 - lam_init)).astype(BF16)


def _chain_pipeline(n_groups, step_fn, finish_fn):
    assert n_groups % 2 == 0 and n_groups >= 2
    step_fn(0, 0, None, None)
    step_fn(1, 1, 0, 0)

    def body(t, carry):
        g = 2 * t
        step_fn(g, 0, g - 1, 1)
        finish_fn(g - 2, 0)
        step_fn(g + 1, 1, g, 0)
        finish_fn(g - 1, 1)
        return carry

    lax.fori_loop(1, n_groups // 2, body, 0)
    step_fn(None, None, n_groups - 1, 1)
    finish_fn(n_groups - 2, 0)
    finish_fn(n_groups - 1, 1)


def _attn_prompt_kernel(lam_ref, sg_ref, q_ref, k_ref, v_ref, o_ref, s_ref, ov_ref, *, lam_init):
    lam = _diff_lambda(lam_ref, lam_init)
    sg = sg_ref[...]
    n = k_ref.shape[2]
    ones = jnp.ones((n, MXU_N - HEAD), BF16)

    def step_fn(bs, ps, bv, pv):
        for hh in range(N_HEADS):
            if bs is not None:
                s_ref[ps * N_HEADS + hh] = lax.dot_general(
                    _stack_components(q_ref[bs, hh]), k_ref[bs, hh].astype(BF16),
                    (((1,), (1,)), ((), ())), preferred_element_type=F32)
        for hh in range(N_HEADS):
            if bv is not None:
                v_ext = jnp.concatenate([v_ref[bv, hh].astype(BF16), ones], axis=1)
                ov_ref[pv * N_HEADS + hh] = _softmax_pv(s_ref[pv * N_HEADS + hh], v_ext)

    def finish_fn(b, par):
        for hh in range(N_HEADS):
            o_ref[b, :, HEAD * hh:HEAD * (hh + 1)] = _normalise(ov_ref[par * N_HEADS + hh], lam, sg, lam_init)

    _chain_pipeline(q_ref.shape[0], step_fn, finish_fn)


def _attn_prompt_call(q, k, v, lam_params, subln_g, lam_init, nb):
    bsz, nh, n, hd = q.shape
    spec = pl.BlockSpec((nb, nh, n, hd), lambda b: (b, 0, 0, 0))
    return pl.pallas_call(
        functools.partial(_attn_prompt_kernel, lam_init=lam_init),
        out_shape=jax.ShapeDtypeStruct((bsz, n, nh * hd), BF16),
        grid=(bsz // nb,),
        in_specs=[_resident(lam_params.shape), _resident(subln_g.shape), spec, spec, spec],
        out_specs=pl.BlockSpec((nb, n, nh * hd), lambda b: (b, 0, 0)),
        scratch_shapes=[pltpu.VMEM((2 * nh, 2 * n, n), F32), pltpu.VMEM((2 * nh, 2 * n, MXU_N), F32)],
        compiler_params=_params(1),
        name="attn",
    )(lam_params, subln_g, q, k, v)


POST_BLOCK = 256
ONES_ROWS = 16
ATTN_WIDTH = 2
KEY_CHUNK = 256


def _attn_cache_kernel(*refs, lam_init, n_cast):
    lam_ref, sg_ref, q_ref, k_ref, vt_ref, ck_ref, cv_ref = refs[:7]
    cast_in = refs[7:7 + n_cast]
    o_ref = refs[7 + n_cast]
    cast_out = refs[8 + n_cast:8 + 2 * n_cast]
    kbuf_ref, vtbuf_ref, s_ref, ov_ref, m_ref, qt_ref = refs[8 + 2 * n_cast:]
    for src, dst in zip(cast_in, cast_out):
        dst[...] = src[...].astype(BF16)
    past = ck_ref.shape[0]
    lk = kbuf_ref.shape[0]
    kbuf_ref[0:past, :] = ck_ref[...].astype(BF16)
    kbuf_ref[past:, :] = k_ref[...]
    vtbuf_ref[0:HEAD, 0:past] = cv_ref[...].T.astype(BF16)
    vtbuf_ref[0:HEAD, past:] = vt_ref[...]
    vtbuf_ref[HEAD:, :] = jnp.ones((ONES_ROWS, lk), BF16)

    lam = _diff_lambda(lam_ref, lam_init)
    sg = sg_ref[...] * (1.0 - lam_init)

    def rows(g, u):
        start = (g * ATTN_WIDTH + u) * ATTN_SUB
        return pl.ds(pl.multiple_of(start, ATTN_SUB), ATTN_SUB)

    def step_fn(gs, ps, gv, pv):
        chains = range(ATTN_WIDTH)
        if gs is not None:
            for u in chains:
                qt_ref[u] = _stack_components(q_ref[rows(gs, u), :]).T
            qq = [qt_ref[u] for u in chains]
            col_max = [None] * ATTN_WIDTH
        if gv is not None:
            m_prev = [m_ref[pv * ATTN_WIDTH + u] for u in chains]
            acc = [None] * ATTN_WIDTH
        for c0 in range(0, lk, KEY_CHUNK):
            keys = slice(c0, c0 + KEY_CHUNK)
            for u in chains:
                if gs is not None:
                    s = _dot(kbuf_ref[keys, :], qq[u])
                    s_ref[ps * ATTN_WIDTH + u, keys, :] = s
                    cm = jnp.max(s, axis=0, keepdims=True)
                    col_max[u] = cm if col_max[u] is None else jnp.maximum(col_max[u], cm)
            for u in chains:
                if gv is not None:
                    e = jnp.exp2(s_ref[pv * ATTN_WIDTH + u, keys, :] - m_prev[u]).astype(BF16)
                    part = _dot(vtbuf_ref[:, keys], e)
                    acc[u] = part if acc[u] is None else acc[u] + part
        for u in chains:
            if gs is not None:
                m_ref[ps * ATTN_WIDTH + u] = col_max[u]
            if gv is not None:
                ov_ref[pv * ATTN_WIDTH + u] = acc[u]

    def finish_fn(g, par):
        for u in range(ATTN_WIDTH):
            ov = ov_ref[par * ATTN_WIDTH + u]
            o_t = (ov[0:HEAD, 0:ATTN_SUB] / ov[HEAD:HEAD + 1, 0:ATTN_SUB]
                   - lam * (ov[0:HEAD, ATTN_SUB:] / ov[HEAD:HEAD + 1, ATTN_SUB:]))
            ms = jnp.mean(o_t * o_t, axis=0, keepdims=True)
            o_ref[rows(g, u), :] = ((o_t * lax.rsqrt(ms + EPS)).T * sg).astype(BF16)

    _chain_pipeline(q_ref.shape[0] // (ATTN_SUB * ATTN_WIDTH), step_fn, finish_fn)


def _attn_cache_call(q, k, vt, lam_params, subln_g, cache_k, cache_v, layer, lam_init, to_cast):
    bsz, nh, n, hd = q.shape
    past = cache_k.shape[3]
    seq_spec = pl.BlockSpec((None, None, n, hd), lambda b, h: (b, h, 0, 0))
    vt_spec = pl.BlockSpec((None, None, hd, n), lambda b, h: (b, h, 0, 0))
    c_spec = pl.BlockSpec((None, None, None, past, hd), lambda b, h: (b, layer, h, 0, 0))
    flat = [w.reshape(-1, w.shape[-1]) for w in to_cast]
    slab_specs = [pl.BlockSpec((w.shape[0] // (bsz * nh), w.shape[1]), lambda b, h: (b * nh + h, 0))
                  for w in flat]
    outs = pl.pallas_call(
        functools.partial(_attn_cache_kernel, lam_init=lam_init, n_cast=len(flat)),
        out_shape=[jax.ShapeDtypeStruct((bsz, n, nh * hd), BF16)]
        + [jax.ShapeDtypeStruct(w.shape, BF16) for w in flat],
        grid=(bsz, nh),
        in_specs=[_resident(lam_params.shape), _resident(subln_g.shape),
                  seq_spec, seq_spec, vt_spec, c_spec, c_spec] + slab_specs,
        out_specs=[pl.BlockSpec((None, n, hd), lambda b, h: (b, 0, h))] + slab_specs,
        scratch_shapes=[pltpu.VMEM((past + n, hd), BF16), pltpu.VMEM((hd + ONES_ROWS, past + n), BF16),
                        pltpu.VMEM((2 * ATTN_WIDTH, past + n, 2 * ATTN_SUB), F32),
                        pltpu.VMEM((2 * ATTN_WIDTH, hd + ONES_ROWS, 2 * ATTN_SUB), F32),
                        pltpu.VMEM((2 * ATTN_WIDTH, 1, 2 * ATTN_SUB), F32),
                        pltpu.VMEM((ATTN_WIDTH, hd, 2 * ATTN_SUB), BF16)],
        compiler_params=_params(2),
        name="attn_cache",
    )(lam_params, subln_g, q, k, vt, cache_k, cache_v, *flat)
    return outs[0], [o.reshape(w.shape) for o, w in zip(outs[1:], to_cast)]


def _post_kernel(x_ref, a_ref, b_ref, mod_ref, g_ref, wo_ref, wg_ref, wu_ref, wd_ref, o_ref):
    nb = x_ref.shape[0] // POST_BLOCK
    blocks = [slice(i * POST_BLOCK, (i + 1) * POST_BLOCK) for i in range(nb)]

    def out_proj(rows):
        return _dot(jnp.concatenate([a_ref[rows, :], b_ref[rows, :]], axis=1), wo_ref[...])

    def norms(rows, y):
        x1 = x_ref[rows, :] + mod_ref[2:3, :] * _rms(y, g_ref[1:2, :])
        return x1, _modulate(x1, g_ref[2:3, :], mod_ref[3:4, :], mod_ref[4:5, :]).astype(BF16)

    def gate_up(h):
        return _dot(h, wg_ref[...]), _dot(h, wu_ref[...])

    def down(gu):
        return _dot((_silu(gu[0]) * gu[1]).astype(BF16), wd_ref[...])

    def finish(rows, x1, f):
        o_ref[rows, :] = x1 + mod_ref[5:6, :] * _rms(f, g_ref[3:4, :])

    y = {0: out_proj(blocks[0])}
    x1, gu = {}, {}
    for i in range(nb + 1):
        if i + 1 < nb:
            y[i + 1] = out_proj(blocks[i + 1])
        if i < nb:
            x1[i], h = norms(blocks[i], y.pop(i))
        if i >= 1:
            f = down(gu.pop(i - 1))
        if i < nb:
            gu[i] = gate_up(h)
        if i >= 1:
            finish(blocks[i - 1], x1.pop(i - 1), f)


def _post_call(x, a, b, mod, g, w_out, w_gate, w_up, w_down, layer, tm, mod_base, mod_stride):
    bsz, n, d = x.shape
    x_spec, _, _, mod_spec = _tile_specs(n, tm, mod_base, mod_stride)
    half_spec = pl.BlockSpec((None, tm, a.shape[2]), lambda b_, j: (b_, j, 0))

    def layer_resident(w):
        return pl.BlockSpec((None,) + w.shape[1:], lambda *_: (layer, 0, 0), pipeline_mode=pl.Buffered(1))

    return pl.pallas_call(
        _post_kernel,
        out_shape=jax.ShapeDtypeStruct(x.shape, F32),
        grid=(bsz, n // tm),
        in_specs=[x_spec, half_spec, half_spec, mod_spec, _resident(g.shape),
                  _resident(w_out.shape), layer_resident(w_gate), layer_resident(w_up),
                  layer_resident(w_down)],
        out_specs=x_spec,
        compiler_params=_params(2),
        name="post",
    )(x, a, b, mod, g, w_out, w_gate, w_up, w_down)


def _odd_in_kernel(*refs, n_seq, whole_seqs):
    if whole_seqs:
        (x_ref, mod_ref, g_ref, w_ref, wp_ref, ps_ref, cs_ref, dc_ref, ds_ref, wf_ref,
         pc_ref, fc_ref, us_ref, f2_ref, f4_ref) = refs
        nblk, blk = x_ref.shape[0], x_ref.shape[1]
    else:
        (x_ref, xp_ref, xn_ref, mod_ref, g_ref, w_ref, wp_ref, ps_ref, cs_ref,
         pc_ref, xc_ref, xs_ref, us_ref, f2_ref, f4_ref) = refs
        j = pl.program_id(1)
        nt = pl.num_programs(1)
        nblk, blk = 1, x_ref.shape[0]
    tm = nblk * blk
    g = g_ref[0:1, :]
    shift = mod_ref[0:1, :]
    scale = mod_ref[1:2, :]
    if whole_seqs:
        h = _modulate(x_ref[...].reshape(tm, x_ref.shape[2]), g, shift, scale).astype(BF16)
        up = _dot(h, w_ref[:, 0:POOL_W])
    else:
        xh = jnp.concatenate([x_ref[...], xp_ref[...], xn_ref[...]], axis=0)
        hz = _modulate(xh, g, shift, scale).astype(BF16)
        h = hz[0:tm]
        upz = _dot(hz, w_ref[:, 0:POOL_W])
        up, uph = upz[0:tm], upz[tm:]
    uf = _dot(h, w_ref[:, POOL_W:]).astype(BF16)

    def store(ref, lanes, val):
        if whole_seqs:
            for i in range(nblk):
                ref[i, :, lanes] = val[i * blk:(i + 1) * blk]
        else:
            ref[:, lanes] = val

    xc_groups, xs_groups = [], []
    for gi in range(FOURIER_W // GROUP):
        lanes = slice(GROUP * gi, GROUP * (gi + 1))
        cs = _dot(uf[:, lanes], cs_ref[...])
        xc_groups.append(cs[:, 0:GROUP].astype(BF16))
        xs_groups.append(cs[:, GROUP:].astype(BF16))
        if not whole_seqs:
            xc_ref[:, lanes] = xc_groups[gi]
            xs_ref[:, lanes] = xs_groups[gi]
    if whole_seqs:
        xc = jnp.concatenate(xc_groups, axis=1)
        xs = jnp.concatenate(xs_groups, axis=1)
        ortho = float(1.0 / math.sqrt(blk * GROUP))
        for i in range(nblk):
            seq = slice(i * blk, (i + 1) * blk)
            y = _dot(dc_ref[...], xc[seq]) - _dot(ds_ref[...], xs[seq])
            four = (y * ortho).astype(BF16)
            for gi in range(FOURIER_W // GROUP):
                lanes = slice(GROUP * gi, GROUP * (gi + 1))
                fc_ref[i, :, lanes] = _dot(four[:, lanes], wf_ref[gi]).astype(BF16)

    stride = blk + HALO

    def u0(i):
        return HALO + i * stride

    rows = nblk * stride + HALO
    zeros = jnp.zeros((HALO, POOL_W), F32)
    for i in range(nblk):
        us_ref[u0(i):u0(i) + blk, :] = up[i * blk:(i + 1) * blk]
        if whole_seqs:
            us_ref[u0(i) - HALO:u0(i), :] = zeros
    if whole_seqs:
        us_ref[rows - HALO:rows, :] = zeros
    else:
        us_ref[0:HALO, :] = jnp.where(j > 0, uph[0:HALO], 0.0)
        us_ref[rows - HALO:rows, :] = jnp.where(j < nt - 1, uph[HALO:], 0.0)
    us_ref[rows:, :] = zeros
    f2_ref[0:rows, :] = us_ref[0:rows, GROUP:] + us_ref[1:rows + 1, GROUP:]
    f2_ref[rows:, :] = jnp.zeros((HALO, POOL_W - GROUP), F32)
    f4_ref[0:rows, :] = f2_ref[0:rows, GROUP:] + f2_ref[2:rows + 2, GROUP:]
    f4_ref[rows:, :] = jnp.zeros((HALO, POOL_W - 2 * GROUP), F32)
    f8 = f4_ref[0:rows, GROUP:] + f4_ref[4:rows + 4, GROUP:]

    def centred(i):
        a = u0(i)
        return (us_ref[a - 1:a - 1 + blk, 0:GROUP] + us_ref[a:a + blk, 0:GROUP],
                f2_ref[a - 2:a - 2 + blk, 0:GROUP] + f2_ref[a:a + blk, 0:GROUP],
                f4_ref[a - 4:a - 4 + blk, 0:GROUP] + f4_ref[a:a + blk, 0:GROUP],
                f8[a - HALO:a - HALO + blk] + f8[a:a + blk])

    sums = [centred(i) for i in range(nblk)]
    pos = lax.broadcasted_iota(jnp.int32, (blk, 1), 0)
    t = (pos if whole_seqs else j * blk + pos).astype(F32)
    for gi, win in enumerate(POOL_WINDOWS):
        lanes = slice(GROUP * gi, GROUP * (gi + 1))
        cnt = jnp.minimum(t + float(win // 2), float(n_seq)) - jnp.maximum(t - float(win // 2), 0.0)
        pooled = jnp.concatenate([s[gi] / cnt for s in sums], axis=0)
        diff = (pooled - up[:, lanes]).astype(BF16)
        store(pc_ref, lanes, (_dot(diff, wp_ref[gi]) * ps_ref[0:1, lanes]).astype(BF16))


def _odd_in_call(x, mod, g, w_in, w_pool, pool_scale, cs_mat, tm, mod_base, mod_stride, seq_dft=None):
    bsz, n, d = x.shape
    whole_seqs = tm >= n
    weights = [_resident(g.shape), _resident(w_in.shape), _resident(w_pool.shape),
               _resident(pool_scale.shape), _resident(cs_mat.shape)]
    if whole_seqs:
        ns = tm // n
        in_specs = ([pl.BlockSpec((ns, n, d), lambda b: (b, 0, 0)),
                     pl.BlockSpec((None, 6, d), lambda b: (mod_base, 0, 0))] + weights
                    + [_resident(t.shape) for t in seq_dft])
        args = [x, mod, g, w_in, w_pool, pool_scale, cs_mat, *seq_dft]
        out_spec = pl.BlockSpec((ns, n, POOL_W), lambda b: (b, 0, 0))
        n_out = 2
        grid = (bsz // ns,)
        scratch_rows = ns * (n + HALO) + 2 * HALO
    else:
        x_spec, prev_spec, next_spec, mod_spec = _tile_specs(n, tm, mod_base, mod_stride)
        in_specs = [x_spec, prev_spec, next_spec, mod_spec] + weights
        args = [x, x, x, mod, g, w_in, w_pool, pool_scale, cs_mat]
        out_spec = pl.BlockSpec((None, tm, POOL_W), lambda b, j: (b, j, 0))
        n_out = 3
        grid = (bsz, n // tm)
        scratch_rows = tm + 3 * HALO
    out_sds = jax.ShapeDtypeStruct((bsz, n, POOL_W), BF16)
    return pl.pallas_call(
        functools.partial(_odd_in_kernel, n_seq=n, whole_seqs=whole_seqs),
        out_shape=(out_sds,) * n_out,
        grid=grid,
        in_specs=in_specs,
        out_specs=(out_spec,) * n_out,
        scratch_shapes=[pltpu.VMEM((scratch_rows, POOL_W), F32),
                        pltpu.VMEM((scratch_rows, POOL_W - GROUP), F32),
                        pltpu.VMEM((scratch_rows, POOL_W - 2 * GROUP), F32)],
        compiler_params=_params(len(grid)),
        name="odd_in",
    )(*args)


FLIP_BLOCK = 256


def _four_sym_kernel(c_ref, s_ref, pm_ref, pm0_ref, xc_ref, xs_ref, wf_ref, o_ref, *, scale):
    n = o_ref.shape[0]
    half = n // 2

    def folded(x_ref, sign):
        parts = []
        for b in range(half // FLIP_BLOCK):
            lo = n - FLIP_BLOCK * (b + 1)
            if b == 0:
                partner = _dot(pm0_ref[...], x_ref[lo:lo + FLIP_BLOCK, :])
            else:
                partner = _dot(pm_ref[...], x_ref[lo:lo + FLIP_BLOCK + HALO, :])
            direct = x_ref[FLIP_BLOCK * b:FLIP_BLOCK * (b + 1), :].astype(F32)
            parts.append((direct + sign * partner).astype(BF16))
        return jnp.concatenate(parts, axis=0)

    j = lax.broadcasted_iota(jnp.int32, (c_ref.shape[0], 1), 0)
    nyquist = (1 - 2 * (j & 1)).astype(F32) * xc_ref[half:half + 1, :].astype(F32)
    p = _dot(c_ref[...], folded(xc_ref, 1.0)) + nyquist
    q = _dot(s_ref[...], folded(xs_ref, -1.0))

    def project(rows, four):
        for gi in range(FOURIER_W // GROUP):
            lanes = slice(GROUP * gi, GROUP * (gi + 1))
            o_ref[rows, lanes] = _dot(four[:, lanes], wf_ref[gi]).astype(BF16)

    project(slice(0, half), ((p[0:half] - q[0:half]) * scale).astype(BF16))
    mirrored = ((p + q) * scale).astype(BF16)
    for b in range(half // FLIP_BLOCK):
        lo = half - FLIP_BLOCK * (b + 1)
        window = mirrored[lo:lo + FLIP_BLOCK + HALO, :]
        flipped = _dot(pm_ref[...], window).astype(BF16)
        project(slice(half + FLIP_BLOCK * b, half + FLIP_BLOCK * (b + 1)), flipped)


def _four_sym_call(c_quarter, s_quarter, perm, perm0, xc, xs, w_four):
    bsz, n, w = xc.shape
    seq_spec = pl.BlockSpec((None, n, w), lambda b: (b, 0, 0))
    return pl.pallas_call(
        functools.partial(_four_sym_kernel, scale=float(1.0 / math.sqrt(n * GROUP))),
        out_shape=jax.ShapeDtypeStruct((bsz, n, w), BF16),
        grid=(bsz,),
        in_specs=[_resident(c_quarter.shape), _resident(s_quarter.shape), _resident(perm.shape),
                  _resident(perm0.shape), seq_spec, seq_spec, _resident(w_four.shape)],
        out_specs=seq_spec,
        compiler_params=_params(1),
        name="fourier_sym",
    )(c_quarter, s_quarter, perm, perm0, xc, xs, w_four)


def _flip_perms():
    i = np.arange(FLIP_BLOCK)
    pm = np.zeros((FLIP_BLOCK, FLIP_BLOCK + HALO), np.float32)
    pm[i, FLIP_BLOCK - i] = 1.0
    pm0 = np.zeros((FLIP_BLOCK, FLIP_BLOCK), np.float32)
    pm0[i[1:], FLIP_BLOCK - i[1:]] = 1.0
    return pm, pm0


def _rope_tables(n_tok):
    rows = n_tok // GRID_W
    row = np.repeat(np.arange(rows), GRID_W).astype(np.float64)
    col = np.tile(np.arange(GRID_W), rows).astype(np.float64)
    inv = ROPE_BASE ** (-np.arange(0, ROPE_AXIS, 2, dtype=np.float64) / ROPE_AXIS)
    ang_r = row[:, None] * inv[None, :]
    ang_c = col[:, None] * inv[None, :]
    ang = np.concatenate([ang_r, ang_r, ang_c, ang_c], axis=-1)
    cos = np.concatenate([np.cos(ang)] * 2, axis=-1)
    sin = np.concatenate([np.sin(ang)] * 2, axis=-1)
    first_half = (np.arange(HEAD) % 32) < 16
    sin_signed = np.where(first_half[None, :], -sin, sin)
    return jnp.asarray(cos, F32), jnp.asarray(sin_signed, F32)


def _dft_mats(n):
    idx = np.arange(n, dtype=np.int64)
    ang = 2.0 * np.pi * ((idx[:, None] * idx[None, :]) % n).astype(np.float64) / n
    return np.cos(ang), np.sin(ang)


def kernel(x_prompt, x_sample, cache_k, cache_v, c, c_ctx, w_mod, b_mod, norm_g,
           w_in_even, lam_params, subln_g, conv_w, w_out_even,
           w_in_odd, w_pool, pool_scale, w_fourier, w_out_odd,
           w_gate, w_up, w_down):
    depth = w_mod.shape[0]
    n_dec = x_sample.shape[0]
    n_p, n_s = x_prompt.shape[1], x_sample.shape[1]

    pad_rows = 16 - 1 - n_dec
    cc = jnp.concatenate([c_ctx[None, :], c, jnp.zeros((pad_rows, D_MODEL), F32)], axis=0)
    mod_all = _mod_call(cc, w_mod, b_mod)[:, :1 + n_dec].reshape(depth, 1 + n_dec, 6, D_MODEL)

    rope = _rope_tables(n_s)
    cc_g, sc_g = _dft_mats(GROUP)
    cs_mat = jnp.asarray(np.concatenate([cc_g, sc_g], axis=1), F32).astype(BF16)
    dft_p = tuple(jnp.asarray(m, F32).astype(BF16) for m in _dft_mats(n_p))
    dft_s = tuple(jnp.asarray(m[:n_s // 2 + HALO, :n_s // 2], F32).astype(BF16) for m in _dft_mats(n_s))
    flips = tuple(jnp.asarray(m, F32).astype(BF16) for m in _flip_perms())

    late_weights = [w_gate, w_up, w_down, w_out_even, w_in_odd, w_out_odd]
    xp, xs = x_prompt, x_sample
    new_k, new_v = [], []
    for l in range(depth):
        mod = mod_all[l]
        g = norm_g[l]
        i = l // 2
        streams = []
        if l % 2 == 0:
            lam_init = 0.8 - 0.6 * math.exp(-0.3 * l)
            w_in = w_in_even[i].astype(BF16)
            sg = subln_g[i][None, :]
            qp, kp, vp, cbp = _even_in_call(xp, mod, g, w_in, conv_w[i], None, 4 * n_p, 0, 0)
            ap = _attn_prompt_call(qp, kp, vp, lam_params[i], sg, lam_init, 8)
            new_k.append(kp)
            new_v.append(vp)
            qs, ks, vts, cbs = _even_in_call(xs, mod, g, w_in, conv_w[i], rope, 2048, 1, 1)
            a_s, cast = _attn_cache_call(qs, ks, vts, lam_params[i], sg, cache_k, cache_v, i, lam_init,
                                         late_weights if l == 0 else [])
            if l == 0:
                wg, wu, wd, wo_even, wi_odd, wo_odd = cast
            w_out = wo_even[i]
            streams = [(ap, cbp), (a_s, cbs)]
        else:
            w_in = wi_odd[i]
            w_out = wo_odd[i]
            wp = w_pool[i].astype(BF16)
            wf = w_fourier[i].astype(BF16)
            ps = pool_scale[i][None, :]
            pcp, fcp = _odd_in_call(xp, mod, g, w_in, wp, ps, cs_mat, 4 * n_p, 0, 0, (*dft_p, wf))
            pcs, xcs, xss = _odd_in_call(xs, mod, g, w_in, wp, ps, cs_mat, 1024, 1, 1)
            fcs = _four_sym_call(*dft_s, *flips, xcs, xss, wf)
            streams = [(pcp, fcp), (pcs, fcs)]
        xp = _post_call(*(t.reshape(1, -1, t.shape[-1]) for t in (xp,) + streams[0]),
                        mod, g, w_out, wg, wu, wd, l, 512, 0, 0).reshape(x_prompt.shape)
        xs = _post_call(xs, streams[1][0], streams[1][1], mod, g, w_out, wg, wu, wd, l, 512, 1, 1)
    def stack_layers(parts):
        if len(parts) == 1:
            return parts[0][:, None]
        return jnp.stack(parts, axis=1)

    return xp, xs, stack_layers(new_k), stack_layers(new_v)
```

```python
import functools
import math

import numpy as np
import jax
import jax.numpy as jnp
from jax import lax
from jax.experimental import pallas as pl
from jax.experimental.pallas import tpu as pltpu

F32 = jnp.float32
BF16 = jnp.bfloat16

D_MODEL = 1024
GRID_W = 64
N_HEADS = 4
HEAD = 128
HALF_HEAD = 64
ROPE_AXIS = 32
ROPE_BASE = 10000.0
ATTN_W = 512
CONV_W = 512
POOL_W = 512
FOURIER_W = 512
GROUP = 128
POOL_WINDOWS = (2, 4, 8, 16)
D_FF = 2816
EPS = 1e-6
LOG2E = math.log2(math.e)
HALO = 8
MXU_N = 256
ATTN_SUB = 128
EVEN_BLOCK = 256
VMEM_LIMIT = 56 * 1024 * 1024


def _params(n_axes):
    return pltpu.CompilerParams(dimension_semantics=("arbitrary",) * n_axes,
                                vmem_limit_bytes=VMEM_LIMIT)


def _resident(shape):
    return pl.BlockSpec(shape, lambda *_: (0,) * len(shape), pipeline_mode=pl.Buffered(1))


def _rms(x, g):
    ms = jnp.mean(x * x, axis=-1, keepdims=True)
    return x * lax.rsqrt(ms + EPS) * g


def _modulate(x, g, shift, scale):
    return _rms(x, g) * (1.0 + scale) + shift


def _dot(a, b):
    return jnp.dot(a, b, preferred_element_type=F32)


def _silu(x):
    return x / (1.0 + jnp.exp(-x))


MOD_TN = 1024
MOD_BUFS = 3


def _mod_kernel(cc_ref, w_hbm, b_ref, o_ref, wbuf_ref, sem_ref):
    depth, _, n6 = w_hbm.shape
    tiles = [(l, j * MOD_TN) for l in range(depth) for j in range(n6 // MOD_TN)]

    def copy(t):
        l, c0 = tiles[t]
        slot = t % MOD_BUFS
        return pltpu.make_async_copy(w_hbm.at[l, :, pl.ds(c0, MOD_TN)], wbuf_ref.at[slot], sem_ref.at[slot])

    for t in range(min(MOD_BUFS, len(tiles))):
        copy(t).start()
    s = _silu(cc_ref[...]).astype(BF16)
    for t, (l, c0) in enumerate(tiles):
        copy(t).wait()
        w = wbuf_ref[t % MOD_BUFS].astype(BF16)
        o_ref[l, :, c0:c0 + MOD_TN] = _dot(s, w) + b_ref[l, :, c0:c0 + MOD_TN]
        if t + MOD_BUFS < len(tiles):
            copy(t + MOD_BUFS).start()


def _mod_call(cc, w_mod, b_mod):
    depth, d, n6 = w_mod.shape
    rows = cc.shape[0]
    vmem = pl.BlockSpec(memory_space=pltpu.VMEM)
    return pl.pallas_call(
        _mod_kernel,
        out_shape=jax.ShapeDtypeStruct((depth, rows, n6), F32),
        in_specs=[vmem, pl.BlockSpec(memory_space=pl.ANY), vmem],
        out_specs=vmem,
        scratch_shapes=[pltpu.VMEM((MOD_BUFS, d, MOD_TN), F32), pltpu.SemaphoreType.DMA((MOD_BUFS,))],
        compiler_params=pltpu.CompilerParams(vmem_limit_bytes=VMEM_LIMIT),
        name="mod",
    )(cc, w_mod, b_mod.reshape(depth, 1, n6))


def _tile_specs(n, tm, mod_base, mod_stride):
    nb8 = n // HALO
    t8 = tm // HALO
    x_spec = pl.BlockSpec((None, tm, D_MODEL), lambda b, j: (b, j, 0))
    prev_spec = pl.BlockSpec((None, HALO, D_MODEL),
                             lambda b, j: (b, jnp.maximum(j * t8 - 1, 0), 0))
    next_spec = pl.BlockSpec((None, HALO, D_MODEL),
                             lambda b, j: (b, jnp.minimum((j + 1) * t8, nb8 - 1), 0))
    mod_spec = pl.BlockSpec((None, 6, D_MODEL),
                            lambda b, j: (mod_base + mod_stride * b, 0, 0))
    return x_spec, prev_spec, next_spec, mod_spec


def _rope(t, cos, sin_signed, first_half):
    outs = []
    for hh in range(N_HEADS):
        th = t[:, HEAD * hh:HEAD * (hh + 1)]
        swapped = jnp.where(first_half, pltpu.roll(th, HEAD - 16, 1), pltpu.roll(th, 16, 1))
        outs.append(th * cos + swapped * sin_signed)
    return outs


def _even_in_kernel(*refs, use_rope):
    if use_rope:
        (x_ref, xp_ref, xn_ref, mod_ref, g_ref, w_ref, cw_ref, cos_ref, sin_ref,
         q_ref, k_ref, v_ref, cb_ref, zs_ref) = refs
        j = pl.program_id(1)
        nt = pl.num_programs(1)
        blk = EVEN_BLOCK
        nblk = x_ref.shape[0] // blk
        lane = lax.broadcasted_iota(jnp.int32, (1, HEAD), 1)
        first_half = (lane % 32) < 16
    else:
        x_ref, mod_ref, g_ref, w_ref, cw_ref, q_ref, k_ref, v_ref, cb_ref, zs_ref = refs
        nblk, blk = x_ref.shape[0], x_ref.shape[1]
    g = g_ref[0:1, :]
    shift = mod_ref[0:1, :]
    scale = mod_ref[1:2, :]
    stride = blk if use_rope else blk + HALO

    def z0(i):
        return HALO + i * stride

    def rows(i):
        return slice(i * blk, (i + 1) * blk)

    def proj(hh, lo):
        return _dot(hh, w_ref[:, lo:lo + 512])

    def modulated(i):
        xi = x_ref[rows(i), :] if use_rope else x_ref[i]
        return _modulate(xi, g, shift, scale).astype(BF16)

    def conv_inputs(i, h):
        outer = use_rope and i == 0
        if outer:
            xh = jnp.concatenate([xp_ref[...], xn_ref[...]], axis=0)
            hz = jnp.concatenate([h, _modulate(xh, g, shift, scale).astype(BF16)], axis=0)
        else:
            hz = h
        z = proj(hz, 2048) * proj(hz, 2560)
        if outer:
            zh = z[blk:]
            z = z[0:blk]
            zs_ref[0:HALO, :] = jnp.where(j > 0, zh[0:HALO], 0.0)
            zs_ref[z0(nblk):, :] = jnp.where(j < nt - 1, zh[HALO:], 0.0)
        zs_ref[z0(i):z0(i) + blk, :] = z
        return z, proj(h, 1536)

    def conv_out(i, z, gate_b):
        conv = (cw_ref[0:1, :] * zs_ref[z0(i) - 1:z0(i) - 1 + blk, :] + cw_ref[1:2, :] * z
                + cw_ref[2:3, :] * zs_ref[z0(i) + 1:z0(i) + 1 + blk, :])
        out = (gate_b * conv).astype(BF16)
        if use_rope:
            cb_ref[rows(i), :] = out
        else:
            cb_ref[i] = out

    def qkv(i, h):
        v = proj(h, 1024)
        for hh in range(N_HEADS):
            vh = v[:, HEAD * hh:HEAD * (hh + 1)]
            if use_rope:
                v_ref[hh, :, rows(i)] = vh.T.astype(v_ref.dtype)
            else:
                v_ref[i, hh] = vh.astype(v_ref.dtype)
        for ref, t in ((k_ref, proj(h, 512)), (q_ref, proj(h, 0) * (HALF_HEAD ** -0.5 * LOG2E))):
            if use_rope:
                heads = _rope(t, cos_ref[rows(i), :], sin_ref[rows(i), :], first_half)
            else:
                heads = [t[:, HEAD * hh:HEAD * (hh + 1)] for hh in range(N_HEADS)]
            for hh in range(N_HEADS):
                if use_rope:
                    ref[hh, rows(i), :] = heads[hh].astype(ref.dtype)
                else:
                    ref[i, hh] = heads[hh].astype(ref.dtype)

    if not use_rope:
        for i in range(nblk + 1):
            zs_ref[i * stride:i * stride + HALO, :] = jnp.zeros((HALO, CONV_W), F32)

    h = {0: modulated(0)}
    zg = {0: conv_inputs(0, h[0])}
    for i in range(nblk):
        if i + 1 < nblk:
            h[i + 1] = modulated(i + 1)
        else:
            conv_out(i, *zg.pop(i))
        qkv(i, h.pop(i))
        if i + 1 < nblk:
            zg[i + 1] = conv_inputs(i + 1, h[i + 1])
            conv_out(i, *zg.pop(i))


def _even_in_call(x, mod, g, w_in, conv_w, rope, tm, mod_base, mod_stride):
    bsz, n, d = x.shape
    use_rope = rope is not None
    head_shape = (bsz, N_HEADS, n, HEAD)
    if use_rope:
        x_spec, prev_spec, next_spec, mod_spec = _tile_specs(n, tm, mod_base, mod_stride)
        tab = pl.BlockSpec((tm, HEAD), lambda b, j: (j, 0))
        in_specs = [x_spec, prev_spec, next_spec, mod_spec, _resident(g.shape), _resident(w_in.shape),
                    _resident(conv_w.shape), tab, tab]
        args = [x, x, x, mod, g, w_in, conv_w, *rope]
        head_spec = pl.BlockSpec((None, N_HEADS, tm, HEAD), lambda b, j: (b, 0, j, 0))
        v_spec = pl.BlockSpec((None, N_HEADS, HEAD, tm), lambda b, j: (b, 0, 0, j))
        cb_spec = pl.BlockSpec((None, tm, CONV_W), lambda b, j: (b, j, 0))
        k_sds = jax.ShapeDtypeStruct(head_shape, BF16)
        v_sds = jax.ShapeDtypeStruct((bsz, N_HEADS, HEAD, n), BF16)
        grid = (bsz, n // tm)
        zs_rows = tm + 2 * HALO
    else:
        ns = tm // n
        in_specs = [pl.BlockSpec((ns, n, d), lambda b: (b, 0, 0)),
                    pl.BlockSpec((None, 6, d), lambda b: (mod_base, 0, 0)),
                    _resident(g.shape), _resident(w_in.shape), _resident(conv_w.shape)]
        args = [x, mod, g, w_in, conv_w]
        head_spec = v_spec = pl.BlockSpec((ns, N_HEADS, n, HEAD), lambda b: (b, 0, 0, 0))
        cb_spec = pl.BlockSpec((ns, n, CONV_W), lambda b: (b, 0, 0))
        k_sds = v_sds = jax.ShapeDtypeStruct(head_shape, F32)
        grid = (bsz // ns,)
        zs_rows = ns * (n + HALO) + HALO
    return pl.pallas_call(
        functools.partial(_even_in_kernel, use_rope=use_rope),
        out_shape=(jax.ShapeDtypeStruct(head_shape, BF16), k_sds, v_sds,
                   jax.ShapeDtypeStruct((bsz, n, CONV_W), BF16)),
        grid=grid,
        in_specs=in_specs,
        out_specs=(head_spec, head_spec, v_spec, cb_spec),
        scratch_shapes=[pltpu.VMEM((zs_rows, CONV_W), F32)],
        compiler_params=_params(len(grid)),
        name="even_in_rope" if use_rope else "even_in",
    )(*args)


def _diff_lambda(lam_ref, lam_init):
    lp = lam_ref[...]
    return (jnp.exp(jnp.sum(lp[0:1] * lp[1:2], axis=-1, keepdims=True))
            - jnp.exp(jnp.sum(lp[2:3] * lp[3:4], axis=-1, keepdims=True)) + lam_init)


def _stack_components(q):
    lane = lax.broadcasted_iota(jnp.int32, (1, HEAD), 1)
    zero = jnp.zeros_like(q)
    return jnp.concatenate([jnp.where(lane < HALF_HEAD, q, zero),
                            jnp.where(lane >= HALF_HEAD, q, zero)], axis=0)


def _softmax_pv(s, v_ext):
    e = jnp.exp2(s - jnp.max(s, axis=-1, keepdims=True)).astype(BF16)
    return _dot(e, v_ext)


def _normalise(ov, lam, sg, lam_init):
    t = ov.shape[0] // 2
    o = ov[:t, :HEAD] / ov[:t, HEAD:] - lam * (ov[t:, :HEAD] / ov[t:, HEAD:])
    return (_rms(o, sg) * (1.0 - lam_init)).astype(BF16)


def _chain_pipeline(n_groups, step_fn, finish_fn):
    assert n_groups % 2 == 0 and n_groups >= 2
    step_fn(0, 0, None, None)
    step_fn(1, 1, 0, 0)

    def body(t, carry):
        g = 2 * t
        step_fn(g, 0, g - 1, 1)
        finish_fn(g - 2, 0)
        step_fn(g + 1, 1, g, 0)
        finish_fn(g - 1, 1)
        return carry

    lax.fori_loop(1, n_groups // 2, body, 0)
    step_fn(None, None, n_groups - 1, 1)
    finish_fn(n_groups - 2, 0)
    finish_fn(n_groups - 1, 1)


def _attn_prompt_kernel(lam_ref, sg_ref, q_ref, k_ref, v_ref, o_ref, s_ref, ov_ref, *, lam_init):
    lam = _diff_lambda(lam_ref, lam_init)
    sg = sg_ref[...]
    n = k_ref.shape[2]
    ones = jnp.ones((n, MXU_N - HEAD), BF16)

    def step_fn(bs, ps, bv, pv):
        for hh in range(N_HEADS):
            if bs is not None:
                s_ref[ps * N_HEADS + hh] = lax.dot_general(
                    _stack_components(q_ref[bs, hh]), k_ref[bs, hh].astype(BF16),
                    (((1,), (1,)), ((), ())), preferred_element_type=F32)
        for hh in range(N_HEADS):
            if bv is not None:
                v_ext = jnp.concatenate([v_ref[bv, hh].astype(BF16), ones], axis=1)
                ov_ref[pv * N_HEADS + hh] = _softmax_pv(s_ref[pv * N_HEADS + hh], v_ext)

    def finish_fn(b, par):
        for hh in range(N_HEADS):
            o_ref[b, :, HEAD * hh:HEAD * (hh + 1)] = _normalise(ov_ref[par * N_HEADS + hh], lam, sg, lam_init)

    _chain_pipeline(q_ref.shape[0], step_fn, finish_fn)


def _attn_prompt_call(q, k, v, lam_params, subln_g, lam_init, nb):
    bsz, nh, n, hd = q.shape
    spec = pl.BlockSpec((nb, nh, n, hd), lambda b: (b, 0, 0, 0))
    return pl.pallas_call(
        functools.partial(_attn_prompt_kernel, lam_init=lam_init),
        out_shape=jax.ShapeDtypeStruct((bsz, n, nh * hd), BF16),
        grid=(bsz // nb,),
        in_specs=[_resident(lam_params.shape), _resident(subln_g.shape), spec, spec, spec],
        out_specs=pl.BlockSpec((nb, n, nh * hd), lambda b: (b, 0, 0)),
        scratch_shapes=[pltpu.VMEM((2 * nh, 2 * n, n), F32), pltpu.VMEM((2 * nh, 2 * n, MXU_N), F32)],
        compiler_params=_params(1),
        name="attn",
    )(lam_params, subln_g, q, k, v)


POST_BLOCK = 256
ONES_ROWS = 16
ATTN_WIDTH = 2
KEY_CHUNK = 256


def _attn_cache_kernel(*refs, lam_init, n_cast):
    lam_ref, sg_ref, q_ref, k_ref, vt_ref, ck_ref, cv_ref = refs[:7]
    cast_in = refs[7:7 + n_cast]
    o_ref = refs[7 + n_cast]
    cast_out = refs[8 + n_cast:8 + 2 * n_cast]
    kbuf_ref, vtbuf_ref, s_ref, ov_ref, m_ref, qt_ref = refs[8 + 2 * n_cast:]
    for src, dst in zip(cast_in, cast_out):
        dst[...] = src[...].astype(BF16)
    past = ck_ref.shape[0]
    lk = kbuf_ref.shape[0]
    kbuf_ref[0:past, :] = ck_ref[...].astype(BF16)
    kbuf_ref[past:, :] = k_ref[...]
    vtbuf_ref[0:HEAD, 0:past] = cv_ref[...].T.astype(BF16)
    vtbuf_ref[0:HEAD, past:] = vt_ref[...]
    vtbuf_ref[HEAD:, :] = jnp.ones((ONES_ROWS, lk), BF16)

    lam = _diff_lambda(lam_ref, lam_init)
    sg = sg_ref[...] * (1.0 - lam_init)

    def rows(g, u):
        start = (g * ATTN_WIDTH + u) * ATTN_SUB
        return pl.ds(pl.multiple_of(start, ATTN_SUB), ATTN_SUB)

    def step_fn(gs, ps, gv, pv):
        chains = range(ATTN_WIDTH)
        if gs is not None:
            for u in chains:
                qt_ref[u] = _stack_components(q_ref[rows(gs, u), :]).T
            qq = [qt_ref[u] for u in chains]
            col_max = [None] * ATTN_WIDTH
        if gv is not None:
            m_prev = [m_ref[pv * ATTN_WIDTH + u] for u in chains]
            acc = [None] * ATTN_WIDTH
        for c0 in range(0, lk, KEY_CHUNK):
            keys = slice(c0, c0 + KEY_CHUNK)
            for u in chains:
                if gs is not None:
                    s = _dot(kbuf_ref[keys, :], qq[u])
                    s_ref[ps * ATTN_WIDTH + u, keys, :] = s
                    cm = jnp.max(s, axis=0, keepdims=True)
                    col_max[u] = cm if col_max[u] is None else jnp.maximum(col_max[u], cm)
            for u in chains:
                if gv is not None:
                    e = jnp.exp2(s_ref[pv * ATTN_WIDTH + u, keys, :] - m_prev[u]).astype(BF16)
                    part = _dot(vtbuf_ref[:, keys], e)
                    acc[u] = part if acc[u] is None else acc[u] + part
        for u in chains:
            if gs is not None:
                m_ref[ps * ATTN_WIDTH + u] = col_max[u]
            if gv is not None:
                ov_ref[pv * ATTN_WIDTH + u] = acc[u]

    def finish_fn(g, par):
        for u in range(ATTN_WIDTH):
            ov = ov_ref[par * ATTN_WIDTH + u]
            o_t = (ov[0:HEAD, 0:ATTN_SUB] / ov[HEAD:HEAD + 1, 0:ATTN_SUB]
                   - lam * (ov[0:HEAD, ATTN_SUB:] / ov[HEAD:HEAD + 1, ATTN_SUB:]))
            ms = jnp.mean(o_t * o_t, axis=0, keepdims=True)
            o_ref[rows(g, u), :] = ((o_t * lax.rsqrt(ms + EPS)).T * sg).astype(BF16)

    _chain_pipeline(q_ref.shape[0] // (ATTN_SUB * ATTN_WIDTH), step_fn, finish_fn)


def _attn_cache_call(q, k, vt, lam_params, subln_g, cache_k, cache_v, layer, lam_init, to_cast):
    bsz, nh, n, hd = q.shape
    past = cache_k.shape[3]
    seq_spec = pl.BlockSpec((None, None, n, hd), lambda b, h: (b, h, 0, 0))
    vt_spec = pl.BlockSpec((None, None, hd, n), lambda b, h: (b, h, 0, 0))
    c_spec = pl.BlockSpec((None, None, None, past, hd), lambda b, h: (b, layer, h, 0, 0))
    flat = [w.reshape(-1, w.shape[-1]) for w in to_cast]
    slab_specs = [pl.BlockSpec((w.shape[0] // (bsz * nh), w.shape[1]), lambda b, h: (b * nh + h, 0))
                  for w in flat]
    outs = pl.pallas_call(
        functools.partial(_attn_cache_kernel, lam_init=lam_init, n_cast=len(flat)),
        out_shape=[jax.ShapeDtypeStruct((bsz, n, nh * hd), BF16)]
        + [jax.ShapeDtypeStruct(w.shape, BF16) for w in flat],
        grid=(bsz, nh),
        in_specs=[_resident(lam_params.shape), _resident(subln_g.shape),
                  seq_spec, seq_spec, vt_spec, c_spec, c_spec] + slab_specs,
        out_specs=[pl.BlockSpec((None, n, hd), lambda b, h: (b, 0, h))] + slab_specs,
        scratch_shapes=[pltpu.VMEM((past + n, hd), BF16), pltpu.VMEM((hd + ONES_ROWS, past + n), BF16),
                        pltpu.VMEM((2 * ATTN_WIDTH, past + n, 2 * ATTN_SUB), F32),
                        pltpu.VMEM((2 * ATTN_WIDTH, hd + ONES_ROWS, 2 * ATTN_SUB), F32),
                        pltpu.VMEM((2 * ATTN_WIDTH, 1, 2 * ATTN_SUB), F32),
                        pltpu.VMEM((ATTN_WIDTH, hd, 2 * ATTN_SUB), BF16)],
        compiler_params=_params(2),
        name="attn_cache",
    )(lam_params, subln_g, q, k, vt, cache_k, cache_v, *flat)
    return outs[0], [o.reshape(w.shape) for o, w in zip(outs[1:], to_cast)]


def _post_kernel(x_ref, a_ref, b_ref, mod_ref, g_ref, wo_ref, wg_ref, wu_ref, wd_ref, o_ref):
    nb = x_ref.shape[0] // POST_BLOCK
    blocks = [slice(i * POST_BLOCK, (i + 1) * POST_BLOCK) for i in range(nb)]

    def out_proj(rows):
        return _dot(jnp.concatenate([a_ref[rows, :], b_ref[rows, :]], axis=1), wo_ref[...])

    def norms(rows, y):
        x1 = x_ref[rows, :] + mod_ref[2:3, :] * _rms(y, g_ref[1:2, :])
        return x1, _modulate(x1, g_ref[2:3, :], mod_ref[3:4, :], mod_ref[4:5, :]).astype(BF16)

    def gate_up(h):
        return _dot(h, wg_ref[...]), _dot(h, wu_ref[...])

    def down(gu):
        return _dot((_silu(gu[0]) * gu[1]).astype(BF16), wd_ref[...])

    def finish(rows, x1, f):
        o_ref[rows, :] = x1 + mod_ref[5:6, :] * _rms(f, g_ref[3:4, :])

    y = {0: out_proj(blocks[0])}
    x1, gu = {}, {}
    for i in range(nb + 1):
        if i + 1 < nb:
            y[i + 1] = out_proj(blocks[i + 1])
        if i < nb:
            x1[i], h = norms(blocks[i], y.pop(i))
        if i >= 1:
            f = down(gu.pop(i - 1))
        if i < nb:
            gu[i] = gate_up(h)
        if i >= 1:
            finish(blocks[i - 1], x1.pop(i - 1), f)


def _post_call(x, a, b, mod, g, w_out, w_gate, w_up, w_down, layer, tm, mod_base, mod_stride):
    bsz, n, d = x.shape
    x_spec, _, _, mod_spec = _tile_specs(n, tm, mod_base, mod_stride)
    half_spec = pl.BlockSpec((None, tm, a.shape[2]), lambda b_, j: (b_, j, 0))

    def layer_resident(w):
        return pl.BlockSpec((None,) + w.shape[1:], lambda *_: (layer, 0, 0), pipeline_mode=pl.Buffered(1))

    return pl.pallas_call(
        _post_kernel,
        out_shape=jax.ShapeDtypeStruct(x.shape, F32),
        grid=(bsz, n // tm),
        in_specs=[x_spec, half_spec, half_spec, mod_spec, _resident(g.shape),
                  _resident(w_out.shape), layer_resident(w_gate), layer_resident(w_up),
                  layer_resident(w_down)],
        out_specs=x_spec,
        compiler_params=_params(2),
        name="post",
    )(x, a, b, mod, g, w_out, w_gate, w_up, w_down)


def _odd_in_kernel(*refs, n_seq, whole_seqs):
    if whole_seqs:
        (x_ref, mod_ref, g_ref, w_ref, wp_ref, ps_ref, cs_ref, dc_ref, ds_ref, wf_ref,
         pc_ref, fc_ref, us_ref, f2_ref, f4_ref) = refs
        nblk, blk = x_ref.shape[0], x_ref.shape[1]
    else:
        (x_ref, xp_ref, xn_ref, mod_ref, g_ref, w_ref, wp_ref, ps_ref, cs_ref,
         pc_ref, xc_ref, xs_ref, us_ref, f2_ref, f4_ref) = refs
        j = pl.program_id(1)
        nt = pl.num_programs(1)
        nblk, blk = 1, x_ref.shape[0]
    tm = nblk * blk
    g = g_ref[0:1, :]
    shift = mod_ref[0:1, :]
    scale = mod_ref[1:2, :]
    if whole_seqs:
        h = _modulate(x_ref[...].reshape(tm, x_ref.shape[2]), g, shift, scale).astype(BF16)
        up = _dot(h, w_ref[:, 0:POOL_W])
    else:
        xh = jnp.concatenate([x_ref[...], xp_ref[...], xn_ref[...]], axis=0)
        hz = _modulate(xh, g, shift, scale).astype(BF16)
        h = hz[0:tm]
        upz = _dot(hz, w_ref[:, 0:POOL_W])
        up, uph = upz[0:tm], upz[tm:]
    uf = _dot(h, w_ref[:, POOL_W:]).astype(BF16)

    def store(ref, lanes, val):
        if whole_seqs:
            for i in range(nblk):
                ref[i, :, lanes] = val[i * blk:(i + 1) * blk]
        else:
            ref[:, lanes] = val

    xc_groups, xs_groups = [], []
    for gi in range(FOURIER_W // GROUP):
        lanes = slice(GROUP * gi, GROUP * (gi + 1))
        cs = _dot(uf[:, lanes], cs_ref[...])
        xc_groups.append(cs[:, 0:GROUP].astype(BF16))
        xs_groups.append(cs[:, GROUP:].astype(BF16))
        if not whole_seqs:
            xc_ref[:, lanes] = xc_groups[gi]
            xs_ref[:, lanes] = xs_groups[gi]
    if whole_seqs:
        xc = jnp.concatenate(xc_groups, axis=1)
        xs = jnp.concatenate(xs_groups, axis=1)
        ortho = float(1.0 / math.sqrt(blk * GROUP))
        for i in range(nblk):
            seq = slice(i * blk, (i + 1) * blk)
            y = _dot(dc_ref[...], xc[seq]) - _dot(ds_ref[...], xs[seq])
            four = (y * ortho).astype(BF16)
            for gi in range(FOURIER_W // GROUP):
                lanes = slice(GROUP * gi, GROUP * (gi + 1))
                fc_ref[i, :, lanes] = _dot(four[:, lanes], wf_ref[gi]).astype(BF16)

    stride = blk + HALO

    def u0(i):
        return HALO + i * stride

    rows = nblk * stride + HALO
    zeros = jnp.zeros((HALO, POOL_W), F32)
    for i in range(nblk):
        us_ref[u0(i):u0(i) + blk, :] = up[i * blk:(i + 1) * blk]
        if whole_seqs:
            us_ref[u0(i) - HALO:u0(i), :] = zeros
    if whole_seqs:
        us_ref[rows - HALO:rows, :] = zeros
    else:
        us_ref[0:HALO, :] = jnp.where(j > 0, uph[0:HALO], 0.0)
        us_ref[rows - HALO:rows, :] = jnp.where(j < nt - 1, uph[HALO:], 0.0)
    us_ref[rows:, :] = zeros
    f2_ref[0:rows, :] = us_ref[0:rows, GROUP:] + us_ref[1:rows + 1, GROUP:]
    f2_ref[rows:, :] = jnp.zeros((HALO, POOL_W - GROUP), F32)
    f4_ref[0:rows, :] = f2_ref[0:rows, GROUP:] + f2_ref[2:rows + 2, GROUP:]
    f4_ref[rows:, :] = jnp.zeros((HALO, POOL_W - 2 * GROUP), F32)
    f8 = f4_ref[0:rows, GROUP:] + f4_ref[4:rows + 4, GROUP:]

    def centred(i):
        a = u0(i)
        return (us_ref[a - 1:a - 1 + blk, 0:GROUP] + us_ref[a:a + blk, 0:GROUP],
                f2_ref[a - 2:a - 2 + blk, 0:GROUP] + f2_ref[a:a + blk, 0:GROUP],
                f4_ref[a - 4:a - 4 + blk, 0:GROUP] + f4_ref[a:a + blk, 0:GROUP],
                f8[a - HALO:a - HALO + blk] + f8[a:a + blk])

    sums = [centred(i) for i in range(nblk)]
    pos = lax.broadcasted_iota(jnp.int32, (blk, 1), 0)
    t = (pos if whole_seqs else j * blk + pos).astype(F32)
    for gi, win in enumerate(POOL_WINDOWS):
        lanes = slice(GROUP * gi, GROUP * (gi + 1))
        cnt = jnp.minimum(t + float(win // 2), float(n_seq)) - jnp.maximum(t - float(win // 2), 0.0)
        pooled = jnp.concatenate([s[gi] / cnt for s in sums], axis=0)
        diff = (pooled - up[:, lanes]).astype(BF16)
        store(pc_ref, lanes, (_dot(diff, wp_ref[gi]) * ps_ref[0:1, lanes]).astype(BF16))


def _odd_in_call(x, mod, g, w_in, w_pool, pool_scale, cs_mat, tm, mod_base, mod_stride, seq_dft=None):
    bsz, n, d = x.shape
    whole_seqs = tm >= n
    weights = [_resident(g.shape), _resident(w_in.shape), _resident(w_pool.shape),
               _resident(pool_scale.shape), _resident(cs_mat.shape)]
    if whole_seqs:
        ns = tm // n
        in_specs = ([pl.BlockSpec((ns, n, d), lambda b: (b, 0, 0)),
                     pl.BlockSpec((None, 6, d), lambda b: (mod_base, 0, 0))] + weights
                    + [_resident(t.shape) for t in seq_dft])
        args = [x, mod, g, w_in, w_pool, pool_scale, cs_mat, *seq_dft]
        out_spec = pl.BlockSpec((ns, n, POOL_W), lambda b: (b, 0, 0))
        n_out = 2
        grid = (bsz // ns,)
        scratch_rows = ns * (n + HALO) + 2 * HALO
    else:
        x_spec, prev_spec, next_spec, mod_spec = _tile_specs(n, tm, mod_base, mod_stride)
        in_specs = [x_spec, prev_spec, next_spec, mod_spec] + weights
        args = [x, x, x, mod, g, w_in, w_pool, pool_scale, cs_mat]
        out_spec = pl.BlockSpec((None, tm, POOL_W), lambda b, j: (b, j, 0))
        n_out = 3
        grid = (bsz, n // tm)
        scratch_rows = tm + 3 * HALO
    out_sds = jax.ShapeDtypeStruct((bsz, n, POOL_W), BF16)
    return pl.pallas_call(
        functools.partial(_odd_in_kernel, n_seq=n, whole_seqs=whole_seqs),
        out_shape=(out_sds,) * n_out,
        grid=grid,
        in_specs=in_specs,
        out_specs=(out_spec,) * n_out,
        scratch_shapes=[pltpu.VMEM((scratch_rows, POOL_W), F32),
                        pltpu.VMEM((scratch_rows, POOL_W - GROUP), F32),
                        pltpu.VMEM((scratch_rows, POOL_W - 2 * GROUP), F32)],
        compiler_params=_params(len(grid)),
        name="odd_in",
    )(*args)


FLIP_BLOCK = 256


def _four_sym_kernel(c_ref, s_ref, pm_ref, pm0_ref, xc_ref, xs_ref, wf_ref, o_ref, *, scale):
    n = o_ref.shape[0]
    half = n // 2

    def folded(x_ref, sign):
        parts = []
        for b in range(half // FLIP_BLOCK):
            lo = n - FLIP_BLOCK * (b + 1)
            if b == 0:
                partner = _dot(pm0_ref[...], x_ref[lo:lo + FLIP_BLOCK, :])
            else:
                partner = _dot(pm_ref[...], x_ref[lo:lo + FLIP_BLOCK + HALO, :])
            direct = x_ref[FLIP_BLOCK * b:FLIP_BLOCK * (b + 1), :].astype(F32)
            parts.append((direct + sign * partner).astype(BF16))
        return jnp.concatenate(parts, axis=0)

    j = lax.broadcasted_iota(jnp.int32, (c_ref.shape[0], 1), 0)
    nyquist = (1 - 2 * (j & 1)).astype(F32) * xc_ref[half:half + 1, :].astype(F32)
    p = _dot(c_ref[...], folded(xc_ref, 1.0)) + nyquist
    q = _dot(s_ref[...], folded(xs_ref, -1.0))

    def project(rows, four):
        for gi in range(FOURIER_W // GROUP):
            lanes = slice(GROUP * gi, GROUP * (gi + 1))
            o_ref[rows, lanes] = _dot(four[:, lanes], wf_ref[gi]).astype(BF16)

    project(slice(0, half), ((p[0:half] - q[0:half]) * scale).astype(BF16))
    mirrored = ((p + q) * scale).astype(BF16)
    for b in range(half // FLIP_BLOCK):
        lo = half - FLIP_BLOCK * (b + 1)
        window = mirrored[lo:lo + FLIP_BLOCK + HALO, :]
        flipped = _dot(pm_ref[...], window).astype(BF16)
        project(slice(half + FLIP_BLOCK * b, half + FLIP_BLOCK * (b + 1)), flipped)


def _four_sym_call(c_quarter, s_quarter, perm, perm0, xc, xs, w_four):
    bsz, n, w = xc.shape
    seq_spec = pl.BlockSpec((None, n, w), lambda b: (b, 0, 0))
    return pl.pallas_call(
        functools.partial(_four_sym_kernel, scale=float(1.0 / math.sqrt(n * GROUP))),
        out_shape=jax.ShapeDtypeStruct((bsz, n, w), BF16),
        grid=(bsz,),
        in_specs=[_resident(c_quarter.shape), _resident(s_quarter.shape), _resident(perm.shape),
                  _resident(perm0.shape), seq_spec, seq_spec, _resident(w_four.shape)],
        out_specs=seq_spec,
        compiler_params=_params(1),
        name="fourier_sym",
    )(c_quarter, s_quarter, perm, perm0, xc, xs, w_four)


def _flip_perms():
    i = np.arange(FLIP_BLOCK)
    pm = np.zeros((FLIP_BLOCK, FLIP_BLOCK + HALO), np.float32)
    pm[i, FLIP_BLOCK - i] = 1.0
    pm0 = np.zeros((FLIP_BLOCK, FLIP_BLOCK), np.float32)
    pm0[i[1:], FLIP_BLOCK - i[1:]] = 1.0
    return pm, pm0


def _rope_tables(n_tok):
    rows = n_tok // GRID_W
    row = np.repeat(np.arange(rows), GRID_W).astype(np.float64)
    col = np.tile(np.arange(GRID_W), rows).astype(np.float64)
    inv = ROPE_BASE ** (-np.arange(0, ROPE_AXIS, 2, dtype=np.float64) / ROPE_AXIS)
    ang_r = row[:, None] * inv[None, :]
    ang_c = col[:, None] * inv[None, :]
    ang = np.concatenate([ang_r, ang_r, ang_c, ang_c], axis=-1)
    cos = np.concatenate([np.cos(ang)] * 2, axis=-1)
    sin = np.concatenate([np.sin(ang)] * 2, axis=-1)
    first_half = (np.arange(HEAD) % 32) < 16
    sin_signed = np.where(first_half[None, :], -sin, sin)
    return jnp.asarray(cos, F32), jnp.asarray(sin_signed, F32)


def _dft_mats(n):
    idx = np.arange(n, dtype=np.int64)
    ang = 2.0 * np.pi * ((idx[:, None] * idx[None, :]) % n).astype(np.float64) / n
    return np.cos(ang), np.sin(ang)


def kernel(x_prompt, x_sample, cache_k, cache_v, c, c_ctx, w_mod, b_mod, norm_g,
           w_in_even, lam_params, subln_g, conv_w, w_out_even,
           w_in_odd, w_pool, pool_scale, w_fourier, w_out_odd,
           w_gate, w_up, w_down):
    depth = w_mod.shape[0]
    n_dec = x_sample.shape[0]
    n_p, n_s = x_prompt.shape[1], x_sample.shape[1]

    pad_rows = 16 - 1 - n_dec
    cc = jnp.concatenate([c_ctx[None, :], c, jnp.zeros((pad_rows, D_MODEL), F32)], axis=0)
    mod_all = _mod_call(cc, w_mod, b_mod)[:, :1 + n_dec].reshape(depth, 1 + n_dec, 6, D_MODEL)

    rope = _rope_tables(n_s)
    cc_g, sc_g = _dft_mats(GROUP)
    cs_mat = jnp.asarray(np.concatenate([cc_g, sc_g], axis=1), F32).astype(BF16)
    dft_p = tuple(jnp.asarray(m, F32).astype(BF16) for m in _dft_mats(n_p))
    dft_s = tuple(jnp.asarray(m[:n_s // 2 + HALO, :n_s // 2], F32).astype(BF16) for m in _dft_mats(n_s))
    flips = tuple(jnp.asarray(m, F32).astype(BF16) for m in _flip_perms())

    late_weights = [w_gate, w_up, w_down, w_out_even, w_in_odd, w_out_odd]
    xp, xs = x_prompt, x_sample
    new_k, new_v = [], []
    for l in range(depth):
        mod = mod_all[l]
        g = norm_g[l]
        i = l // 2
        streams = []
        if l % 2 == 0:
            lam_init = 0.8 - 0.6 * math.exp(-0.3 * l)
            w_in = w_in_even[i].astype(BF16)
            sg = subln_g[i][None, :]
            qp, kp, vp, cbp = _even_in_call(xp, mod, g, w_in, conv_w[i], None, 4 * n_p, 0, 0)
            ap = _attn_prompt_call(qp, kp, vp, lam_params[i], sg, lam_init, 8)
            new_k.append(kp)
            new_v.append(vp)
            qs, ks, vts, cbs = _even_in_call(xs, mod, g, w_in, conv_w[i], rope, 2048, 1, 1)
            a_s, cast = _attn_cache_call(qs, ks, vts, lam_params[i], sg, cache_k, cache_v, i, lam_init,
                                         late_weights if l == 0 else [])
            if l == 0:
                wg, wu, wd, wo_even, wi_odd, wo_odd = cast
            w_out = wo_even[i]
            streams = [(ap, cbp), (a_s, cbs)]
        else:
            w_in = wi_odd[i]
            w_out = wo_odd[i]
            wp = w_pool[i].astype(BF16)
            wf = w_fourier[i].astype(BF16)
            ps = pool_scale[i][None, :]
            pcp, fcp = _odd_in_call(xp, mod, g, w_in, wp, ps, cs_mat, 4 * n_p, 0, 0, (*dft_p, wf))
            pcs, xcs, xss = _odd_in_call(xs, mod, g, w_in, wp, ps, cs_mat, 1024, 1, 1)
            fcs = _four_sym_call(*dft_s, *flips, xcs, xss, wf)
            streams = [(pcp, fcp), (pcs, fcs)]
        xp = _post_call(*(t.reshape(1, -1, t.shape[-1]) for t in (xp,) + streams[0]),
                        mod, g, w_out, wg, wu, wd, l, 512, 0, 0).reshape(x_prompt.shape)
        xs = _post_call(xs, streams[1][0], streams[1][1], mod, g, w_out, wg, wu, wd, l, 512, 1, 1)
    def stack_layers(parts):
        if len(parts) == 1:
            return parts[0][:, None]
        return jnp.stack(parts, axis=1)

    return xp, xs, stack_layers(new_k), stack_layers(new_v)
```

```python
import functools
import math

import numpy as np
import jax
import jax.numpy as jnp
from jax import lax
from jax.experimental import pallas as pl
from jax.experimental.pallas import tpu as pltpu

F32 = jnp.float32
BF16 = jnp.bfloat16

D_MODEL = 1024
GRID_W = 64
N_HEADS = 4
HEAD = 128
HALF_HEAD = 64
ROPE_AXIS = 32
ROPE_BASE = 10000.0
ATTN_W = 512
CONV_W = 512
POOL_W = 512
FOURIER_W = 512
GROUP = 128
POOL_WINDOWS = (2, 4, 8, 16)
D_FF = 2816
EPS = 1e-6
LOG2E = math.log2(math.e)
HALO = 8
MXU_N = 256
ATTN_SUB = 128
EVEN_BLOCK = 256
VMEM_LIMIT = 56 * 1024 * 1024


def _params(n_axes):
    return pltpu.CompilerParams(dimension_semantics=("arbitrary",) * n_axes,
                                vmem_limit_bytes=VMEM_LIMIT)


def _resident(shape):
    return pl.BlockSpec(shape, lambda *_: (0,) * len(shape), pipeline_mode=pl.Buffered(1))


def _rms(x, g):
    ms = jnp.mean(x * x, axis=-1, keepdims=True)
    return x * lax.rsqrt(ms + EPS) * g


def _modulate(x, g, shift, scale):
    return _rms(x, g) * (1.0 + scale) + shift


def _dot(a, b):
    return jnp.dot(a, b, preferred_element_type=F32)


def _silu(x):
    return x / (1.0 + jnp.exp(-x))


MOD_TN = 1024
MOD_BUFS = 3


def _mod_kernel(cc_ref, w_hbm, b_ref, o_ref, wbuf_ref, sem_ref):
    depth, _, n6 = w_hbm.shape
    tiles = [(l, j * MOD_TN) for l in range(depth) for j in range(n6 // MOD_TN)]

    def copy(t):
        l, c0 = tiles[t]
        slot = t % MOD_BUFS
        return pltpu.make_async_copy(w_hbm.at[l, :, pl.ds(c0, MOD_TN)], wbuf_ref.at[slot], sem_ref.at[slot])

    s = _silu(cc_ref[...]).astype(BF16)
    for t in range(min(MOD_BUFS, len(tiles))):
        copy(t).start()
    for t, (l, c0) in enumerate(tiles):
        copy(t).wait()
        w = wbuf_ref[t % MOD_BUFS].astype(BF16)
        o_ref[l, :, c0:c0 + MOD_TN] = _dot(s, w) + b_ref[l, :, c0:c0 + MOD_TN]
        if t + MOD_BUFS < len(tiles):
            copy(t + MOD_BUFS).start()


def _mod_call(cc, w_mod, b_mod):
    depth, d, n6 = w_mod.shape
    rows = cc.shape[0]
    vmem = pl.BlockSpec(memory_space=pltpu.VMEM)
    return pl.pallas_call(
        _mod_kernel,
        out_shape=jax.ShapeDtypeStruct((depth, rows, n6), F32),
        in_specs=[vmem, pl.BlockSpec(memory_space=pl.ANY), vmem],
        out_specs=vmem,
        scratch_shapes=[pltpu.VMEM((MOD_BUFS, d, MOD_TN), F32), pltpu.SemaphoreType.DMA((MOD_BUFS,))],
        compiler_params=pltpu.CompilerParams(vmem_limit_bytes=VMEM_LIMIT),
        name="mod",
    )(cc, w_mod, b_mod.reshape(depth, 1, n6))


def _tile_specs(n, tm, mod_base, mod_stride):
    nb8 = n // HALO
    t8 = tm // HALO
    x_spec = pl.BlockSpec((None, tm, D_MODEL), lambda b, j: (b, j, 0))
    prev_spec = pl.BlockSpec((None, HALO, D_MODEL),
                             lambda b, j: (b, jnp.maximum(j * t8 - 1, 0), 0))
    next_spec = pl.BlockSpec((None, HALO, D_MODEL),
                             lambda b, j: (b, jnp.minimum((j + 1) * t8, nb8 - 1), 0))
    mod_spec = pl.BlockSpec((None, 6, D_MODEL),
                            lambda b, j: (mod_base + mod_stride * b, 0, 0))
    return x_spec, prev_spec, next_spec, mod_spec


def _rope(t, cos, sin_signed, first_half):
    outs = []
    for hh in range(N_HEADS):
        th = t[:, HEAD * hh:HEAD * (hh + 1)]
        swapped = jnp.where(first_half, pltpu.roll(th, HEAD - 16, 1), pltpu.roll(th, 16, 1))
        outs.append(th * cos + swapped * sin_signed)
    return outs


def _even_in_kernel(*refs, use_rope):
    if use_rope:
        (x_ref, xp_ref, xn_ref, mod_ref, g_ref, w_ref, cw_ref, cos_ref, sin_ref,
         q_ref, k_ref, v_ref, cb_ref, zs_ref) = refs
        j = pl.program_id(1)
        nt = pl.num_programs(1)
        blk = EVEN_BLOCK
        nblk = x_ref.shape[0] // blk
        lane = lax.broadcasted_iota(jnp.int32, (1, HEAD), 1)
        first_half = (lane % 32) < 16
    else:
        x_ref, mod_ref, g_ref, w_ref, cw_ref, q_ref, k_ref, v_ref, cb_ref, zs_ref = refs
        nblk, blk = x_ref.shape[0], x_ref.shape[1]
    g = g_ref[0:1, :]
    shift = mod_ref[0:1, :]
    scale = mod_ref[1:2, :]
    stride = blk if use_rope else blk + HALO

    def z0(i):
        return HALO + i * stride

    def rows(i):
        return slice(i * blk, (i + 1) * blk)

    def proj(hh, lo):
        return _dot(hh, w_ref[:, lo:lo + 512])

    def modulated(i):
        xi = x_ref[rows(i), :] if use_rope else x_ref[i]
        return _modulate(xi, g, shift, scale).astype(BF16)

    def conv_inputs(i, h):
        outer = use_rope and i == 0
        if outer:
            xh = jnp.concatenate([xp_ref[...], xn_ref[...]], axis=0)
            hz = jnp.concatenate([h, _modulate(xh, g, shift, scale).astype(BF16)], axis=0)
        else:
            hz = h
        z = proj(hz, 2048) * proj(hz, 2560)
        if outer:
            zh = z[blk:]
            z = z[0:blk]
            zs_ref[0:HALO, :] = jnp.where(j > 0, zh[0:HALO], 0.0)
            zs_ref[z0(nblk):, :] = jnp.where(j < nt - 1, zh[HALO:], 0.0)
        zs_ref[z0(i):z0(i) + blk, :] = z
        return z, proj(h, 1536)

    def conv_out(i, z, gate_b):
        conv = (cw_ref[0:1, :] * zs_ref[z0(i) - 1:z0(i) - 1 + blk, :] + cw_ref[1:2, :] * z
                + cw_ref[2:3, :] * zs_ref[z0(i) + 1:z0(i) + 1 + blk, :])
        out = (gate_b * conv).astype(BF16)
        if use_rope:
            cb_ref[rows(i), :] = out
        else:
            cb_ref[i] = out

    def qkv(i, h):
        v = proj(h, 1024)
        for hh in range(N_HEADS):
            vh = v[:, HEAD * hh:HEAD * (hh + 1)]
            if use_rope:
                v_ref[hh, :, rows(i)] = vh.T.astype(v_ref.dtype)
            else:
                v_ref[i, hh] = vh.astype(v_ref.dtype)
        for ref, t in ((k_ref, proj(h, 512)), (q_ref, proj(h, 0) * (HALF_HEAD ** -0.5 * LOG2E))):
            if use_rope:
                heads = _rope(t, cos_ref[rows(i), :], sin_ref[rows(i), :], first_half)
            else:
                heads = [t[:, HEAD * hh:HEAD * (hh + 1)] for hh in range(N_HEADS)]
            for hh in range(N_HEADS):
                if use_rope:
                    ref[hh, rows(i), :] = heads[hh].astype(ref.dtype)
                else:
                    ref[i, hh] = heads[hh].astype(ref.dtype)

    if not use_rope:
        for i in range(nblk + 1):
            zs_ref[i * stride:i * stride + HALO, :] = jnp.zeros((HALO, CONV_W), F32)

    h = {0: modulated(0)}
    zg = {0: conv_inputs(0, h[0])}
    for i in range(nblk):
        if i + 1 < nblk:
            h[i + 1] = modulated(i + 1)
        else:
            conv_out(i, *zg.pop(i))
        qkv(i, h.pop(i))
        if i + 1 < nblk:
            zg[i + 1] = conv_inputs(i + 1, h[i + 1])
            conv_out(i, *zg.pop(i))


def _even_in_call(x, mod, g, w_in, conv_w, rope, tm, mod_base, mod_stride):
    bsz, n, d = x.shape
    use_rope = rope is not None
    head_shape = (bsz, N_HEADS, n, HEAD)
    if use_rope:
        x_spec, prev_spec, next_spec, mod_spec = _tile_specs(n, tm, mod_base, mod_stride)
        tab = pl.BlockSpec((tm, HEAD), lambda b, j: (j, 0))
        in_specs = [x_spec, prev_spec, next_spec, mod_spec, _resident(g.shape), _resident(w_in.shape),
                    _resident(conv_w.shape), tab, tab]
        args = [x, x, x, mod, g, w_in, conv_w, *rope]
        head_spec = pl.BlockSpec((None, N_HEADS, tm, HEAD), lambda b, j: (b, 0, j, 0))
        v_spec = pl.BlockSpec((None, N_HEADS, HEAD, tm), lambda b, j: (b, 0, 0, j))
        cb_spec = pl.BlockSpec((None, tm, CONV_W), lambda b, j: (b, j, 0))
        k_sds = jax.ShapeDtypeStruct(head_shape, BF16)
        v_sds = jax.ShapeDtypeStruct((bsz, N_HEADS, HEAD, n), BF16)
        grid = (bsz, n // tm)
        zs_rows = tm + 2 * HALO
    else:
        ns = tm // n
        in_specs = [pl.BlockSpec((ns, n, d), lambda b: (b, 0, 0)),
                    pl.BlockSpec((None, 6, d), lambda b: (mod_base, 0, 0)),
                    _resident(g.shape), _resident(w_in.shape), _resident(conv_w.shape)]
        args = [x, mod, g, w_in, conv_w]
        head_spec = v_spec = pl.BlockSpec((ns, N_HEADS, n, HEAD), lambda b: (b, 0, 0, 0))
        cb_spec = pl.BlockSpec((ns, n, CONV_W), lambda b: (b, 0, 0))
        k_sds = v_sds = jax.ShapeDtypeStruct(head_shape, F32)
        grid = (bsz // ns,)
        zs_rows = ns * (n + HALO) + HALO
    return pl.pallas_call(
        functools.partial(_even_in_kernel, use_rope=use_rope),
        out_shape=(jax.ShapeDtypeStruct(head_shape, BF16), k_sds, v_sds,
                   jax.ShapeDtypeStruct((bsz, n, CONV_W), BF16)),
        grid=grid,
        in_specs=in_specs,
        out_specs=(head_spec, head_spec, v_spec, cb_spec),
        scratch_shapes=[pltpu.VMEM((zs_rows, CONV_W), F32)],
        compiler_params=_params(len(grid)),
        name="even_in_rope" if use_rope else "even_in",
    )(*args)


def _diff_lambda(lam_ref, lam_init):
    lp = lam_ref[...]
    return (jnp.exp(jnp.sum(lp[0:1] * lp[1:2], axis=-1, keepdims=True))
            - jnp.exp(jnp.sum(lp[2:3] * lp[3:4], axis=-1, keepdims=True)) + lam_init)


def _stack_components(q):
    lane = lax.broadcasted_iota(jnp.int32, (1, HEAD), 1)
    zero = jnp.zeros_like(q)
    return jnp.concatenate([jnp.where(lane < HALF_HEAD, q, zero),
                            jnp.where(lane >= HALF_HEAD, q, zero)], axis=0)


def _softmax_pv(s, v_ext):
    e = jnp.exp2(s - jnp.max(s, axis=-1, keepdims=True)).astype(BF16)
    return _dot(e, v_ext)


def _normalise(ov, lam, sg, lam_init):
    t = ov.shape[0] // 2
    o = ov[:t, :HEAD] / ov[:t, HEAD:] - lam * (ov[t:, :HEAD] / ov[t:, HEAD:])
    return (_rms(o, sg) * (1.0 - lam_init)).astype(BF16)


def _chain_pipeline(n_groups, step_fn, finish_fn):
    assert n_groups % 2 == 0 and n_groups >= 2
    step_fn(0, 0, None, None)
    step_fn(1, 1, 0, 0)

    def body(t, carry):
        g = 2 * t
        step_fn(g, 0, g - 1, 1)
        finish_fn(g - 2, 0)
        step_fn(g + 1, 1, g, 0)
        finish_fn(g - 1, 1)
        return carry

    lax.fori_loop(1, n_groups // 2, body, 0)
    step_fn(None, None, n_groups - 1, 1)
    finish_fn(n_groups - 2, 0)
    finish_fn(n_groups - 1, 1)


def _attn_prompt_kernel(lam_ref, sg_ref, q_ref, k_ref, v_ref, o_ref, s_ref, ov_ref, *, lam_init):
    lam = _diff_lambda(lam_ref, lam_init)
    sg = sg_ref[...]
    n = k_ref.shape[2]
    ones = jnp.ones((n, MXU_N - HEAD), BF16)

    def step_fn(bs, ps, bv, pv):
        for hh in range(N_HEADS):
            if bs is not None:
                s_ref[ps * N_HEADS + hh] = lax.dot_general(
                    _stack_components(q_ref[bs, hh]), k_ref[bs, hh].astype(BF16),
                    (((1,), (1,)), ((), ())), preferred_element_type=F32)
        for hh in range(N_HEADS):
            if bv is not None:
                v_ext = jnp.concatenate([v_ref[bv, hh].astype(BF16), ones], axis=1)
                ov_ref[pv * N_HEADS + hh] = _softmax_pv(s_ref[pv * N_HEADS + hh], v_ext)

    def finish_fn(b, par):
        for hh in range(N_HEADS):
            o_ref[b, :, HEAD * hh:HEAD * (hh + 1)] = _normalise(ov_ref[par * N_HEADS + hh], lam, sg, lam_init)

    _chain_pipeline(q_ref.shape[0], step_fn, finish_fn)


def _attn_prompt_call(q, k, v, lam_params, subln_g, lam_init, nb):
    bsz, nh, n, hd = q.shape
    spec = pl.BlockSpec((nb, nh, n, hd), lambda b: (b, 0, 0, 0))
    return pl.pallas_call(
        functools.partial(_attn_prompt_kernel, lam_init=lam_init),
        out_shape=jax.ShapeDtypeStruct((bsz, n, nh * hd), BF16),
        grid=(bsz // nb,),
        in_specs=[_resident(lam_params.shape), _resident(subln_g.shape), spec, spec, spec],
        out_specs=pl.BlockSpec((nb, n, nh * hd), lambda b: (b, 0, 0)),
        scratch_shapes=[pltpu.VMEM((2 * nh, 2 * n, n), F32), pltpu.VMEM((2 * nh, 2 * n, MXU_N), F32)],
        compiler_params=_params(1),
        name="attn",
    )(lam_params, subln_g, q, k, v)


POST_BLOCK = 256
ONES_ROWS = 16
ATTN_WIDTH = 2
KEY_CHUNK = 256


def _attn_cache_kernel(*refs, lam_init, n_cast):
    lam_ref, sg_ref, q_ref, k_ref, vt_ref, ck_ref, cv_ref = refs[:7]
    cast_in = refs[7:7 + n_cast]
    o_ref = refs[7 + n_cast]
    cast_out = refs[8 + n_cast:8 + 2 * n_cast]
    kbuf_ref, vtbuf_ref, s_ref, ov_ref, m_ref, qt_ref = refs[8 + 2 * n_cast:]
    for src, dst in zip(cast_in, cast_out):
        dst[...] = src[...].astype(BF16)
    past = ck_ref.shape[0]
    lk = kbuf_ref.shape[0]
    kbuf_ref[0:past, :] = ck_ref[...].astype(BF16)
    kbuf_ref[past:, :] = k_ref[...]
    vtbuf_ref[0:HEAD, 0:past] = cv_ref[...].T.astype(BF16)
    vtbuf_ref[0:HEAD, past:] = vt_ref[...]
    vtbuf_ref[HEAD:, :] = jnp.ones((ONES_ROWS, lk), BF16)

    lam = _diff_lambda(lam_ref, lam_init)
    sg = sg_ref[...] * (1.0 - lam_init)

    def rows(g, u):
        start = (g * ATTN_WIDTH + u) * ATTN_SUB
        return pl.ds(pl.multiple_of(start, ATTN_SUB), ATTN_SUB)

    def step_fn(gs, ps, gv, pv):
        chains = range(ATTN_WIDTH)
        if gs is not None:
            for u in chains:
                qt_ref[u] = _stack_components(q_ref[rows(gs, u), :]).T
            qq = [qt_ref[u] for u in chains]
            col_max = [None] * ATTN_WIDTH
        if gv is not None:
            m_prev = [m_ref[pv * ATTN_WIDTH + u] for u in chains]
            acc = [None] * ATTN_WIDTH
        for c0 in range(0, lk, KEY_CHUNK):
            keys = slice(c0, c0 + KEY_CHUNK)
            for u in chains:
                if gs is not None:
                    s = _dot(kbuf_ref[keys, :], qq[u])
                    s_ref[ps * ATTN_WIDTH + u, keys, :] = s
                    cm = jnp.max(s, axis=0, keepdims=True)
                    col_max[u] = cm if col_max[u] is None else jnp.maximum(col_max[u], cm)
            for u in chains:
                if gv is not None:
                    e = jnp.exp2(s_ref[pv * ATTN_WIDTH + u, keys, :] - m_prev[u]).astype(BF16)
                    part = _dot(vtbuf_ref[:, keys], e)
                    acc[u] = part if acc[u] is None else acc[u] + part
        for u in chains:
            if gs is not None:
                m_ref[ps * ATTN_WIDTH + u] = col_max[u]
            if gv is not None:
                ov_ref[pv * ATTN_WIDTH + u] = acc[u]

    def finish_fn(g, par):
        for u in range(ATTN_WIDTH):
            ov = ov_ref[par * ATTN_WIDTH + u]
            o_t = (ov[0:HEAD, 0:ATTN_SUB] / ov[HEAD:HEAD + 1, 0:ATTN_SUB]
                   - lam * (ov[0:HEAD, ATTN_SUB:] / ov[HEAD:HEAD + 1, ATTN_SUB:]))
            ms = jnp.mean(o_t * o_t, axis=0, keepdims=True)
            o_ref[rows(g, u), :] = ((o_t * lax.rsqrt(ms + EPS)).T * sg).astype(BF16)

    _chain_pipeline(q_ref.shape[0] // (ATTN_SUB * ATTN_WIDTH), step_fn, finish_fn)


def _attn_cache_call(q, k, vt, lam_params, subln_g, cache_k, cache_v, layer, lam_init, to_cast):
    bsz, nh, n, hd = q.shape
    past = cache_k.shape[3]
    seq_spec = pl.BlockSpec((None, None, n, hd), lambda b, h: (b, h, 0, 0))
    vt_spec = pl.BlockSpec((None, None, hd, n), lambda b, h: (b, h, 0, 0))
    c_spec = pl.BlockSpec((None, None, None, past, hd), lambda b, h: (b, layer, h, 0, 0))
    flat = [w.reshape(-1, w.shape[-1]) for w in to_cast]
    slab_specs = [pl.BlockSpec((w.shape[0] // (bsz * nh), w.shape[1]), lambda b, h: (b * nh + h, 0))
                  for w in flat]
    outs = pl.pallas_call(
        functools.partial(_attn_cache_kernel, lam_init=lam_init, n_cast=len(flat)),
        out_shape=[jax.ShapeDtypeStruct((bsz, n, nh * hd), BF16)]
        + [jax.ShapeDtypeStruct(w.shape, BF16) for w in flat],
        grid=(bsz, nh),
        in_specs=[_resident(lam_params.shape), _resident(subln_g.shape),
                  seq_spec, seq_spec, vt_spec, c_spec, c_spec] + slab_specs,
        out_specs=[pl.BlockSpec((None, n, hd), lambda b, h: (b, 0, h))] + slab_specs,
        scratch_shapes=[pltpu.VMEM((past + n, hd), BF16), pltpu.VMEM((hd + ONES_ROWS, past + n), BF16),
                        pltpu.VMEM((2 * ATTN_WIDTH, past + n, 2 * ATTN_SUB), F32),
                        pltpu.VMEM((2 * ATTN_WIDTH, hd + ONES_ROWS, 2 * ATTN_SUB), F32),
                        pltpu.VMEM((2 * ATTN_WIDTH, 1, 2 * ATTN_SUB), F32),
                        pltpu.VMEM((ATTN_WIDTH, hd, 2 * ATTN_SUB), BF16)],
        compiler_params=_params(2),
        name="attn_cache",
    )(lam_params, subln_g, q, k, vt, cache_k, cache_v, *flat)
    return outs[0], [o.reshape(w.shape) for o, w in zip(outs[1:], to_cast)]


def _post_kernel(x_ref, a_ref, b_ref, mod_ref, g_ref, wo_ref, wg_ref, wu_ref, wd_ref, o_ref):
    nb = x_ref.shape[0] // POST_BLOCK
    blocks = [slice(i * POST_BLOCK, (i + 1) * POST_BLOCK) for i in range(nb)]

    def out_proj(rows):
        return _dot(jnp.concatenate([a_ref[rows, :], b_ref[rows, :]], axis=1), wo_ref[...])

    def norms(rows, y):
        x1 = x_ref[rows, :] + mod_ref[2:3, :] * _rms(y, g_ref[1:2, :])
        return x1, _modulate(x1, g_ref[2:3, :], mod_ref[3:4, :], mod_ref[4:5, :]).astype(BF16)

    def gate_up(h):
        return _dot(h, wg_ref[...]), _dot(h, wu_ref[...])

    def down(gu):
        return _dot((_silu(gu[0]) * gu[1]).astype(BF16), wd_ref[...])

    def finish(rows, x1, f):
        o_ref[rows, :] = x1 + mod_ref[5:6, :] * _rms(f, g_ref[3:4, :])

    y = {0: out_proj(blocks[0])}
    x1, gu = {}, {}
    for i in range(nb + 1):
        if i + 1 < nb:
            y[i + 1] = out_proj(blocks[i + 1])
        if i < nb:
            x1[i], h = norms(blocks[i], y.pop(i))
        if i >= 1:
            f = down(gu.pop(i - 1))
        if i < nb:
            gu[i] = gate_up(h)
        if i >= 1:
            finish(blocks[i - 1], x1.pop(i - 1), f)


def _post_call(x, a, b, mod, g, w_out, w_gate, w_up, w_down, layer, tm, mod_base, mod_stride):
    bsz, n, d = x.shape
    x_spec, _, _, mod_spec = _tile_specs(n, tm, mod_base, mod_stride)
    half_spec = pl.BlockSpec((None, tm, a.shape[2]), lambda b_, j: (b_, j, 0))

    def layer_resident(w):
        return pl.BlockSpec((None,) + w.shape[1:], lambda *_: (layer, 0, 0), pipeline_mode=pl.Buffered(1))

    return pl.pallas_call(
        _post_kernel,
        out_shape=jax.ShapeDtypeStruct(x.shape, F32),
        grid=(bsz, n // tm),
        in_specs=[x_spec, half_spec, half_spec, mod_spec, _resident(g.shape),
                  _resident(w_out.shape), layer_resident(w_gate), layer_resident(w_up),
                  layer_resident(w_down)],
        out_specs=x_spec,
        compiler_params=_params(2),
        name="post",
    )(x, a, b, mod, g, w_out, w_gate, w_up, w_down)


def _odd_in_kernel(*refs, n_seq, whole_seqs):
    if whole_seqs:
        (x_ref, mod_ref, g_ref, w_ref, wp_ref, ps_ref, cs_ref, dc_ref, ds_ref, wf_ref,
         pc_ref, fc_ref, us_ref, f2_ref, f4_ref) = refs
        nblk, blk = x_ref.shape[0], x_ref.shape[1]
    else:
        (x_ref, xp_ref, xn_ref, mod_ref, g_ref, w_ref, wp_ref, ps_ref, cs_ref,
         pc_ref, xc_ref, xs_ref, us_ref, f2_ref, f4_ref) = refs
        j = pl.program_id(1)
        nt = pl.num_programs(1)
        nblk, blk = 1, x_ref.shape[0]
    tm = nblk * blk
    g = g_ref[0:1, :]
    shift = mod_ref[0:1, :]
    scale = mod_ref[1:2, :]
    if whole_seqs:
        h = _modulate(x_ref[...].reshape(tm, x_ref.shape[2]), g, shift, scale).astype(BF16)
        up = _dot(h, w_ref[:, 0:POOL_W])
    else:
        xh = jnp.concatenate([x_ref[...], xp_ref[...], xn_ref[...]], axis=0)
        hz = _modulate(xh, g, shift, scale).astype(BF16)
        h = hz[0:tm]
        upz = _dot(hz, w_ref[:, 0:POOL_W])
        up, uph = upz[0:tm], upz[tm:]
    uf = _dot(h, w_ref[:, POOL_W:]).astype(BF16)

    def store(ref, lanes, val):
        if whole_seqs:
            for i in range(nblk):
                ref[i, :, lanes] = val[i * blk:(i + 1) * blk]
        else:
            ref[:, lanes] = val

    xc_groups, xs_groups = [], []
    for gi in range(FOURIER_W // GROUP):
        lanes = slice(GROUP * gi, GROUP * (gi + 1))
        cs = _dot(uf[:, lanes], cs_ref[...])
        xc_groups.append(cs[:, 0:GROUP].astype(BF16))
        xs_groups.append(cs[:, GROUP:].astype(BF16))
        if not whole_seqs:
            xc_ref[:, lanes] = xc_groups[gi]
            xs_ref[:, lanes] = xs_groups[gi]
    if whole_seqs:
        xc = jnp.concatenate(xc_groups, axis=1)
        xs = jnp.concatenate(xs_groups, axis=1)
        ortho = float(1.0 / math.sqrt(blk * GROUP))
        for i in range(nblk):
            seq = slice(i * blk, (i + 1) * blk)
            y = _dot(dc_ref[...], xc[seq]) - _dot(ds_ref[...], xs[seq])
            four = (y * ortho).astype(BF16)
            for gi in range(FOURIER_W // GROUP):
                lanes = slice(GROUP * gi, GROUP * (gi + 1))
                fc_ref[i, :, lanes] = _dot(four[:, lanes], wf_ref[gi]).astype(BF16)

    stride = blk + HALO

    def u0(i):
        return HALO + i * stride

    rows = nblk * stride + HALO
    zeros = jnp.zeros((HALO, POOL_W), F32)
    for i in range(nblk):
        us_ref[u0(i):u0(i) + blk, :] = up[i * blk:(i + 1) * blk]
        if whole_seqs:
            us_ref[u0(i) - HALO:u0(i), :] = zeros
    if whole_seqs:
        us_ref[rows - HALO:rows, :] = zeros
    else:
        us_ref[0:HALO, :] = jnp.where(j > 0, uph[0:HALO], 0.0)
        us_ref[rows - HALO:rows, :] = jnp.where(j < nt - 1, uph[HALO:], 0.0)
    us_ref[rows:, :] = zeros
    f2_ref[0:rows, :] = us_ref[0:rows, GROUP:] + us_ref[1:rows + 1, GROUP:]
    f2_ref[rows:, :] = jnp.zeros((HALO, POOL_W - GROUP), F32)
    f4_ref[0:rows, :] = f2_ref[0:rows, GROUP:] + f2_ref[2:rows + 2, GROUP:]
    f4_ref[rows:, :] = jnp.zeros((HALO, POOL_W - 2 * GROUP), F32)
    f8 = f4_ref[0:rows, GROUP:] + f4_ref[4:rows + 4, GROUP:]

    def centred(i):
        a = u0(i)
        return (us_ref[a - 1:a - 1 + blk, 0:GROUP] + us_ref[a:a + blk, 0:GROUP],
                f2_ref[a - 2:a - 2 + blk, 0:GROUP] + f2_ref[a:a + blk, 0:GROUP],
                f4_ref[a - 4:a - 4 + blk, 0:GROUP] + f4_ref[a:a + blk, 0:GROUP],
                f8[a - HALO:a - HALO + blk] + f8[a:a + blk])

    sums = [centred(i) for i in range(nblk)]
    pos = lax.broadcasted_iota(jnp.int32, (blk, 1), 0)
    t = (pos if whole_seqs else j * blk + pos).astype(F32)
    for gi, win in enumerate(POOL_WINDOWS):
        lanes = slice(GROUP * gi, GROUP * (gi + 1))
        cnt = jnp.minimum(t + float(win // 2), float(n_seq)) - jnp.maximum(t - float(win // 2), 0.0)
        pooled = jnp.concatenate([s[gi] / cnt for s in sums], axis=0)
        diff = (pooled - up[:, lanes]).astype(BF16)
        store(pc_ref, lanes, (_dot(diff, wp_ref[gi]) * ps_ref[0:1, lanes]).astype(BF16))


def _odd_in_call(x, mod, g, w_in, w_pool, pool_scale, cs_mat, tm, mod_base, mod_stride, seq_dft=None):
    bsz, n, d = x.shape
    whole_seqs = tm >= n
    weights = [_resident(g.shape), _resident(w_in.shape), _resident(w_pool.shape),
               _resident(pool_scale.shape), _resident(cs_mat.shape)]
    if whole_seqs:
        ns = tm // n
        in_specs = ([pl.BlockSpec((ns, n, d), lambda b: (b, 0, 0)),
                     pl.BlockSpec((None, 6, d), lambda b: (mod_base, 0, 0))] + weights
                    + [_resident(t.shape) for t in seq_dft])
        args = [x, mod, g, w_in, w_pool, pool_scale, cs_mat, *seq_dft]
        out_spec = pl.BlockSpec((ns, n, POOL_W), lambda b: (b, 0, 0))
        n_out = 2
        grid = (bsz // ns,)
        scratch_rows = ns * (n + HALO) + 2 * HALO
    else:
        x_spec, prev_spec, next_spec, mod_spec = _tile_specs(n, tm, mod_base, mod_stride)
        in_specs = [x_spec, prev_spec, next_spec, mod_spec] + weights
        args = [x, x, x, mod, g, w_in, w_pool, pool_scale, cs_mat]
        out_spec = pl.BlockSpec((None, tm, POOL_W), lambda b, j: (b, j, 0))
        n_out = 3
        grid = (bsz, n // tm)
        scratch_rows = tm + 3 * HALO
    out_sds = jax.ShapeDtypeStruct((bsz, n, POOL_W), BF16)
    return pl.pallas_call(
        functools.partial(_odd_in_kernel, n_seq=n, whole_seqs=whole_seqs),
        out_shape=(out_sds,) * n_out,
        grid=grid,
        in_specs=in_specs,
        out_specs=(out_spec,) * n_out,
        scratch_shapes=[pltpu.VMEM((scratch_rows, POOL_W), F32),
                        pltpu.VMEM((scratch_rows, POOL_W - GROUP), F32),
                        pltpu.VMEM((scratch_rows, POOL_W - 2 * GROUP), F32)],
        compiler_params=_params(len(grid)),
        name="odd_in",
    )(*args)


FLIP_BLOCK = 256


def _four_sym_kernel(c_ref, s_ref, pm_ref, pm0_ref, xc_ref, xs_ref, wf_ref, o_ref, *, scale):
    n = o_ref.shape[0]
    half = n // 2

    def folded(x_ref, sign):
        parts = []
        for b in range(half // FLIP_BLOCK):
            lo = n - FLIP_BLOCK * (b + 1)
            if b == 0:
                partner = _dot(pm0_ref[...], x_ref[lo:lo + FLIP_BLOCK, :])
            else:
                partner = _dot(pm_ref[...], x_ref[lo:lo + FLIP_BLOCK + HALO, :])
            direct = x_ref[FLIP_BLOCK * b:FLIP_BLOCK * (b + 1), :].astype(F32)
            parts.append((direct + sign * partner).astype(BF16))
        return jnp.concatenate(parts, axis=0)

    j = lax.broadcasted_iota(jnp.int32, (c_ref.shape[0], 1), 0)
    nyquist = (1 - 2 * (j & 1)).astype(F32) * xc_ref[half:half + 1, :].astype(F32)
    p = _dot(c_ref[...], folded(xc_ref, 1.0)) + nyquist
    q = _dot(s_ref[...], folded(xs_ref, -1.0))

    def project(rows, four):
        for gi in range(FOURIER_W // GROUP):
            lanes = slice(GROUP * gi, GROUP * (gi + 1))
            o_ref[rows, lanes] = _dot(four[:, lanes], wf_ref[gi]).astype(BF16)

    project(slice(0, half), ((p[0:half] - q[0:half]) * scale).astype(BF16))
    mirrored = ((p + q) * scale).astype(BF16)
    for b in range(half // FLIP_BLOCK):
        lo = half - FLIP_BLOCK * (b + 1)
        window = mirrored[lo:lo + FLIP_BLOCK + HALO, :]
        flipped = _dot(pm_ref[...], window).astype(BF16)
        project(slice(half + FLIP_BLOCK * b, half + FLIP_BLOCK * (b + 1)), flipped)


def _four_sym_call(c_quarter, s_quarter, perm, perm0, xc, xs, w_four):
    bsz, n, w = xc.shape
    seq_spec = pl.BlockSpec((None, n, w), lambda b: (b, 0, 0))
    return pl.pallas_call(
        functools.partial(_four_sym_kernel, scale=float(1.0 / math.sqrt(n * GROUP))),
        out_shape=jax.ShapeDtypeStruct((bsz, n, w), BF16),
        grid=(bsz,),
        in_specs=[_resident(c_quarter.shape), _resident(s_quarter.shape), _resident(perm.shape),
                  _resident(perm0.shape), seq_spec, seq_spec, _resident(w_four.shape)],
        out_specs=seq_spec,
        compiler_params=_params(1),
        name="fourier_sym",
    )(c_quarter, s_quarter, perm, perm0, xc, xs, w_four)


def _flip_perms():
    i = np.arange(FLIP_BLOCK)
    pm = np.zeros((FLIP_BLOCK, FLIP_BLOCK + HALO), np.float32)
    pm[i, FLIP_BLOCK - i] = 1.0
    pm0 = np.zeros((FLIP_BLOCK, FLIP_BLOCK), np.float32)
    pm0[i[1:], FLIP_BLOCK - i[1:]] = 1.0
    return pm, pm0


def _rope_tables(n_tok):
    rows = n_tok // GRID_W
    row = np.repeat(np.arange(rows), GRID_W).astype(np.float64)
    col = np.tile(np.arange(GRID_W), rows).astype(np.float64)
    inv = ROPE_BASE ** (-np.arange(0, ROPE_AXIS, 2, dtype=np.float64) / ROPE_AXIS)
    ang_r = row[:, None] * inv[None, :]
    ang_c = col[:, None] * inv[None, :]
    ang = np.concatenate([ang_r, ang_r, ang_c, ang_c], axis=-1)
    cos = np.concatenate([np.cos(ang)] * 2, axis=-1)
    sin = np.concatenate([np.sin(ang)] * 2, axis=-1)
    first_half = (np.arange(HEAD) % 32) < 16
    sin_signed = np.where(first_half[None, :], -sin, sin)
    return jnp.asarray(cos, F32), jnp.asarray(sin_signed, F32)


def _dft_mats(n):
    idx = np.arange(n, dtype=np.int64)
    ang = 2.0 * np.pi * ((idx[:, None] * idx[None, :]) % n).astype(np.float64) / n
    return np.cos(ang), np.sin(ang)


def kernel(x_prompt, x_sample, cache_k, cache_v, c, c_ctx, w_mod, b_mod, norm_g,
           w_in_even, lam_params, subln_g, conv_w, w_out_even,
           w_in_odd, w_pool, pool_scale, w_fourier, w_out_odd,
           w_gate, w_up, w_down):
    depth = w_mod.shape[0]
    n_dec = x_sample.shape[0]
    n_p, n_s = x_prompt.shape[1], x_sample.shape[1]

    pad_rows = 16 - 1 - n_dec
    cc = jnp.concatenate([c_ctx[None, :], c, jnp.zeros((pad_rows, D_MODEL), F32)], axis=0)
    mod_all = _mod_call(cc, w_mod, b_mod)[:, :1 + n_dec].reshape(depth, 1 + n_dec, 6, D_MODEL)

    rope = _rope_tables(n_s)
    cc_g, sc_g = _dft_mats(GROUP)
    cs_mat = jnp.asarray(np.concatenate([cc_g, sc_g], axis=1), F32).astype(BF16)
    dft_p = tuple(jnp.asarray(m, F32).astype(BF16) for m in _dft_mats(n_p))
    dft_s = tuple(jnp.asarray(m[:n_s // 2 + HALO, :n_s // 2], F32).astype(BF16) for m in _dft_mats(n_s))
    flips = tuple(jnp.asarray(m, F32).astype(BF16) for m in _flip_perms())

    late_weights = [w_gate, w_up, w_down, w_out_even, w_in_odd, w_out_odd]
    xp, xs = x_prompt, x_sample
    new_k, new_v = [], []
    for l in range(depth):
        mod = mod_all[l]
        g = norm_g[l]
        i = l // 2
        streams = []
        if l % 2 == 0:
            lam_init = 0.8 - 0.6 * math.exp(-0.3 * l)
            w_in = w_in_even[i].astype(BF16)
            sg = subln_g[i][None, :]
            qp, kp, vp, cbp = _even_in_call(xp, mod, g, w_in, conv_w[i], None, 4 * n_p, 0, 0)
            ap = _attn_prompt_call(qp, kp, vp, lam_params[i], sg, lam_init, 8)
            new_k.append(kp)
            new_v.append(vp)
            qs, ks, vts, cbs = _even_in_call(xs, mod, g, w_in, conv_w[i], rope, 2048, 1, 1)
            a_s, cast = _attn_cache_call(qs, ks, vts, lam_params[i], sg, cache_k, cache_v, i, lam_init,
                                         late_weights if l == 0 else [])
            if l == 0:
                wg, wu, wd, wo_even, wi_odd, wo_odd = cast
            w_out = wo_even[i]
            streams = [(ap, cbp), (a_s, cbs)]
        else:
            w_in = wi_odd[i]
            w_out = wo_odd[i]
            wp = w_pool[i].astype(BF16)
            wf = w_fourier[i].astype(BF16)
            ps = pool_scale[i][None, :]
            pcp, fcp = _odd_in_call(xp, mod, g, w_in, wp, ps, cs_mat, 4 * n_p, 0, 0, (*dft_p, wf))
            pcs, xcs, xss = _odd_in_call(xs, mod, g, w_in, wp, ps, cs_mat, 1024, 1, 1)
            fcs = _four_sym_call(*dft_s, *flips, xcs, xss, wf)
            streams = [(pcp, fcp), (pcs, fcs)]
        xp = _post_call(*(t.reshape(1, -1, t.shape[-1]) for t in (xp,) + streams[0]),
                        mod, g, w_out, wg, wu, wd, l, 512, 0, 0).reshape(x_prompt.shape)
        xs = _post_call(xs, streams[1][0], streams[1][1], mod, g, w_out, wg, wu, wd, l, 512, 1, 1)
    def stack_layers(parts):
        if len(parts) == 1:
            return parts[0][:, None]
        return jnp.stack(parts, axis=1)

    return xp, xs, stack_layers(new_k), stack_layers(new_v)
```
